```python
import math
import jax, jax.numpy as jnp
from jax import lax
import numpy as np

D_MODEL = 4096
BATCH = 1
SEQ = 8192
DEPTH = 2

GRID_W = 64
CTX_LEN = 256
MIX_W = D_MODEL
BR_W = MIX_W // 4
HY_W = BR_W
RET_W = BR_W
POOL_W = BR_W
SSM_W = BR_W
RET_HEADS = 8
RET_DH = RET_W // RET_HEADS
ROPE_BASE = 10000.0
SSM_HEADDIM = 64
SSM_HEADS = SSM_W // SSM_HEADDIM
SSM_GROUPS = 4
SSM_STATE = 128
SSM_GN = SSM_GROUPS * SSM_STATE
SSM_CONV_CH = SSM_W + 2 * SSM_GN
CHUNK = 128
SHORT_CONV = 3
POOL_WINDOWS = (2, 4, 8, 16)
POOL_GROUPS = len(POOL_WINDOWS)
POOL_GROUP = POOL_W // POOL_GROUPS
HY_BANDS = 16
HY_EMB = 1 + 2 * HY_BANDS
HY_FFN = 64
HY_TARGET = 1e-2
HY_FAST = 0.3
HY_SLOW = 1.5
HY_MIN_DECAY = math.log(HY_TARGET) / HY_SLOW
HY_MAX_DECAY = math.log(HY_TARGET) / HY_FAST
ALPHA = (2.0 * DEPTH) ** 0.25
BETA = (8.0 * DEPTH) ** -0.25
LN_EPS = 1e-5
RET_K = 0
RET_V = RET_K + RET_W
SSM_DT = RET_V + RET_W
SSM_X = SSM_DT + 2 * SSM_HEADS
SSM_B = SSM_X + SSM_W
N_KV = SSM_B + SSM_GN
RET_Q = N_KV
SSM_C = RET_Q + RET_W
HY = SSM_C + SSM_GN
POOL = HY + 3 * HY_W
GATE = POOL + POOL_W
N_IN = GATE + MIX_W

kernel_name = 'hyena_retnet_pool_ssd_parallel_hybrid'


def _layernorm(z):
    z = z.astype(jnp.float32)
    mu = jnp.mean(z, -1, keepdims=True)
    var = jnp.mean(jnp.square(z - mu), -1, keepdims=True)
    return (z - mu) * lax.rsqrt(var + LN_EPS)


def _modulate(h, shift, scale):
    return (_layernorm(h) * (1.0 + scale[:, None, :]) + shift[:, None, :]).astype(h.dtype)


def _post(h, gate, out, g, b):
    z = ALPHA * h.astype(jnp.float32) + gate[:, None, :] * out.astype(jnp.float32)
    return (_layernorm(z) * g + b).astype(h.dtype)


def _short_conv(u, w, b):
    L = u.shape[1]
    up = jnp.pad(u, ((0, 0), (1, 1), (0, 0)))
    return up[:, :L] * w[0] + up[:, 1:L + 1] * w[1] + up[:, 2:] * w[2] + b


def _rope2d(t):
    L = t.shape[1]
    rows = L // GRID_W
    f32 = jnp.float32
    row = jnp.repeat(jnp.arange(rows), GRID_W).astype(f32)
    col = jnp.tile(jnp.arange(GRID_W), rows).astype(f32)
    nq = RET_DH // 4
    inv = ROPE_BASE ** (-jnp.arange(nq, dtype=f32) / nq)
    ang = jnp.concatenate([row[:, None] * inv, col[:, None] * inv], -1)
    cos = jnp.cos(ang)[None, :, None, :]
    sin = jnp.sin(ang)[None, :, None, :]
    t1, t2 = t[..., :RET_DH // 2], t[..., RET_DH // 2:]
    return jnp.concatenate([t1 * cos - t2 * sin, t1 * sin + t2 * cos], -1).astype(t.dtype)


def _group_to_heads(t):
    return jnp.repeat(t, SSM_HEADS // SSM_GROUPS, axis=2)


def _chunk_states(k, v, a, init):
    b, L, H, dk = k.shape
    dv = v.shape[-1]
    nc = L // CHUNK
    kc = k.reshape(b, nc, CHUNK, H, dk)
    vc = v.reshape(b, nc, CHUNK, H, dv)
    cum = jnp.cumsum(a.astype(jnp.float32).reshape(b, nc, CHUNK, H), axis=2)
    last = cum[:, :, -1]
    local = jnp.einsum('bnchk,bnchv->bnhkv', kc * jnp.exp(last[:, :, None] - cum)[..., None], vc)
    s0 = jnp.zeros((b, H, dk, dv), jnp.float32) if init is None else init

    def step(s, inp):
        dec, loc = inp
        return dec[..., None, None] * s + loc, s

    final, starts = lax.scan(step, s0, (jnp.moveaxis(jnp.exp(last), 1, 0), jnp.moveaxis(local, 1, 0)))
    return jnp.moveaxis(starts, 0, 1), final


def _chunk_outputs(q, k, v, a, starts):
    b, L, H, dk = q.shape
    dv = v.shape[-1]
    nc = L // CHUNK
    qc = q.reshape(b, nc, CHUNK, H, dk)
    kc = k.reshape(b, nc, CHUNK, H, dk)
    vc = v.reshape(b, nc, CHUNK, H, dv)
    cum = jnp.cumsum(a.astype(jnp.float32).reshape(b, nc, CHUNK, H), axis=2)
    diff = cum[:, :, :, None, :] - cum[:, :, None, :, :]
    lower = jnp.tril(jnp.ones((CHUNK, CHUNK), bool))[None, None, :, :, None]
    dec = jnp.exp(jnp.where(lower, diff, -jnp.inf))
    scores = jnp.einsum('bnihk,bnjhk->bnijh', qc, kc) * dec
    y = jnp.einsum('bnijh,bnjhv->bnihv', scores, vc)
    y = y + jnp.einsum('bnihk,bnhkv->bnihv', qc * jnp.exp(cum)[..., None], starts)
    return y.reshape(b, L, H, dv)


def _flip(t):
    return jnp.flip(t, axis=1)


def _bidir_scan(q, k_f, k_b, v, a_f, a_b, init_f, init_b):
    starts_f, fin_f = _chunk_states(k_f, v, a_f, init_f)
    y_f = _chunk_outputs(q, k_f, v, a_f, starts_f)
    starts_b, fin_b = _chunk_states(_flip(k_b), _flip(v), _flip(a_b), init_b)
    y_b = _flip(_chunk_outputs(_flip(q), _flip(k_b), _flip(v), _flip(a_b), starts_b))
    return y_f + y_b, fin_f, fin_b


def _bidir_final(k_f, k_b, v, a_f, a_b):
    _, fin_f = _chunk_states(k_f, v, a_f, None)
    _, fin_b = _chunk_states(_flip(k_b), _flip(v), _flip(a_b), None)
    return fin_f, fin_b


def _kv_features(pk, lp, latent):
    b, L, _ = pk.shape
    f32 = jnp.float32
    rk = pk[..., RET_K:RET_K + RET_W].reshape(b, L, RET_HEADS, RET_DH) * (RET_DH ** -0.5)
    if latent:
        rk = _rope2d(rk)
    rv = pk[..., RET_V:RET_V + RET_W].reshape(b, L, RET_HEADS, RET_DH)
    log_gamma = jax.nn.log_sigmoid(lp['ret_decay_logit'].astype(f32))
    ra_f = jnp.broadcast_to(log_gamma[0], (b, L, RET_HEADS))
    ra_b = jnp.broadcast_to(log_gamma[1], (b, L, RET_HEADS))
    dt = jax.nn.softplus(pk[..., SSM_DT:SSM_X].astype(f32).reshape(b, L, 2, SSM_HEADS)
                         + lp['ssm_dt_bias'].astype(f32))
    A = -jnp.exp(lp['ssm_A_log'].astype(f32))
    xb = jax.nn.silu(_short_conv(pk[..., SSM_X:N_KV], lp['conv_ssm_w'][:, :SSM_W + SSM_GN],
                                 lp['conv_ssm_b'][:SSM_W + SSM_GN]))
    sv = xb[..., :SSM_W].reshape(b, L, SSM_HEADS, SSM_HEADDIM)
    bh = _group_to_heads(xb[..., SSM_W:].reshape(b, L, SSM_GROUPS, SSM_STATE))
    sk_f = bh * dt[:, :, 0, :, None]
    sk_b = bh * dt[:, :, 1, :, None]
    sa_f = dt[:, :, 0] * A[0]
    sa_b = dt[:, :, 1] * A[1]
    return rk, rv, ra_f, ra_b, sk_f, sk_b, sv, sa_f, sa_b


def _hyena_filter_fft(L, lp):
    f32 = jnp.float32
    t = jnp.linspace(0.0, 1.0, L, dtype=f32)[:, None]
    w = 2.0 * math.pi * jnp.arange(L, dtype=f32)[:, None] / L
    bands = jnp.linspace(1e-4, HY_BANDS - 1, HY_BANDS, dtype=f32)[None, :]
    z = jnp.concatenate([t, jnp.cos(bands * w), -jnp.sin(bands * w)], axis=-1)
    freq = lp['hy_freq'].astype(f32)
    hdn = jnp.sin(freq * (z @ lp['hy_w1'].astype(f32) + lp['hy_b1'].astype(f32)))
    hdn = jnp.sin(freq * (hdn @ lp['hy_w2'].astype(f32) + lp['hy_b2'].astype(f32)))
    filt = hdn @ lp['hy_w3'].astype(f32)
    deltas = jnp.abs(jnp.linspace(HY_MIN_DECAY, HY_MAX_DECAY, HY_W, dtype=f32))
    window = jnp.exp(-t * deltas[None, :])
    h_fwd = filt[:, :HY_W] * window
    h_bwd = filt[:, HY_W:] * window
    buf = jnp.concatenate([h_fwd, jnp.zeros((1, HY_W), f32), jnp.flip(h_bwd[1:], axis=0)], axis=0)
    return jnp.fft.rfft(buf, axis=0)


def _long_conv(w, hf, bias):
    L = w.shape[1]
    wf = jnp.fft.rfft(w.astype(jnp.float32), n=2 * L, axis=1)
    y = jnp.fft.irfft(wf * hf[None], n=2 * L, axis=1)[:, :L]
    return y + w.astype(jnp.float32) * bias.astype(jnp.float32)


def _pool_mix(u, pool_w, pool_scale):
    b, L, _ = u.shape
    f32 = jnp.float32
    ug = u.astype(f32).reshape(b, L, POOL_GROUPS, POOL_GROUP)
    csum = jnp.concatenate([jnp.zeros_like(ug[:, :1]), jnp.cumsum(ug, axis=1)], axis=1)
    pos = jnp.arange(L)
    diffs = []
    for g, win in enumerate(POOL_WINDOWS):
        lo = jnp.clip(pos - win // 2, 0, L)
        hi = jnp.clip(pos - win // 2 + win, 0, L)
        mean = (csum[:, hi, g] - csum[:, lo, g]) / (hi - lo).astype(f32)[None, :, None]
        diffs.append(mean - ug[:, :, g])
    d = jnp.stack(diffs, axis=2)
    y = jnp.einsum('blgi,gio->blgo', d, pool_w)
    return y.reshape(b, L, POOL_W) * pool_scale


def _grouped_rmsnorm(y, wgt):
    b, L, _ = y.shape
    yg = y.astype(jnp.float32).reshape(b, L, SSM_GROUPS, SSM_W // SSM_GROUPS)
    yg = yg * lax.rsqrt(jnp.mean(jnp.square(yg), -1, keepdims=True) + LN_EPS)
    return yg.reshape(b, L, SSM_W) * wgt


def _mix(h, shift, scale, lp, states, latent):
    b, L, _ = h.shape
    u = _modulate(h, shift, scale)
    proj = u @ lp['w_in']
    rk, rv, ra_f, ra_b, sk_f, sk_b, sv, sa_f, sa_b = _kv_features(proj[..., :N_KV], lp, latent)
    init = (None, None, None, None) if states is None else states
    rq = proj[..., RET_Q:RET_Q + RET_W].reshape(b, L, RET_HEADS, RET_DH)
    if latent:
        rq = _rope2d(rq)
    y_ret, ret_f, ret_b = _bidir_scan(rq, rk, rk, rv, ra_f, ra_b, init[0], init[1])
    y_ret = _layernorm(y_ret).reshape(b, L, RET_W)
    cg = jax.nn.silu(_short_conv(proj[..., SSM_C:SSM_C + SSM_GN], lp['conv_ssm_w'][:, SSM_W + SSM_GN:],
                                 lp['conv_ssm_b'][SSM_W + SSM_GN:]))
    ch = _group_to_heads(cg.reshape(b, L, SSM_GROUPS, SSM_STATE))
    y_ssm, ssm_f, ssm_b = _bidir_scan(ch, sk_f, sk_b, sv, sa_f, sa_b, init[2], init[3])
    y_ssm = (y_ssm + lp['ssm_D'][:, None] * sv).reshape(b, L, SSM_W)
    hy = _short_conv(proj[..., HY:HY + 3 * HY_W], lp['conv_hy_w'], lp['conv_hy_b'])
    hv, hx0, hx1 = jnp.split(hy, 3, axis=-1)
    y_hy = hx0 * _long_conv(hx1 * hv, _hyena_filter_fft(L, lp), lp['hy_bias'])
    y_pool = _pool_mix(proj[..., POOL:POOL + POOL_W], lp['pool_w'], lp['pool_scale'])
    g = jax.nn.silu(proj[..., GATE:GATE + MIX_W].astype(jnp.float32))
    g_hy, g_ret, g_pool, g_ssm = jnp.split(g, 4, axis=-1)
    y_ssm = _grouped_rmsnorm(y_ssm * g_ssm, lp['ssm_norm_w'])
    y = jnp.concatenate([y_hy * g_hy, y_ret * g_ret, y_pool * g_pool, y_ssm], axis=-1).astype(h.dtype)
    return y @ lp['w_out'], (ret_f, ret_b, ssm_f, ssm_b)


def _context_states(hc, shift, scale, lp):
    u = _modulate(hc, shift, scale)
    pk = u @ lp['w_in'][:, :N_KV]
    rk, rv, ra_f, ra_b, sk_f, sk_b, sv, sa_f, sa_b = _kv_features(pk, lp, False)
    ret_f, ret_b = _bidir_final(rk, rk, rv, ra_f, ra_b)
    ssm_f, ssm_b = _bidir_final(sk_f, sk_b, sv, sa_f, sa_b)
    return (ret_f, ret_b, ssm_f, ssm_b)


def setup_inputs(seed: int = 0) -> dict:
    key = jax.random.key(seed)
    ks = jax.random.split(key, 28)
    f32 = jnp.float32
    d = D_MODEL

    def nrm(k, shape, s):
        return jax.random.normal(k, shape, f32) * s

    kexp = 5.0 + np.arange(RET_HEADS)
    ret_logit0 = jnp.asarray(np.log(2.0 ** kexp - 1.0), f32)
    dt0 = jnp.exp(jax.random.uniform(ks[13], (DEPTH, 2, SSM_HEADS), f32, math.log(1e-3), math.log(1e-1)))
    return {
        'x': nrm(ks[0], (BATCH, SEQ, d), 1.0),
        'c': nrm(ks[1], (BATCH, d), 1.0),
        'ctx': nrm(ks[2], (BATCH, CTX_LEN, d), 1.0),
        'c_ctx': nrm(ks[3], (d,), 1.0),
        'w_mod': nrm(ks[4], (DEPTH, d, 3 * d), 0.5 * d ** -0.5),
        'b_mod': nrm(ks[5], (DEPTH, 3 * d), 0.01),
        'w_in': nrm(ks[6], (DEPTH, d, N_IN), d ** -0.5),
        'conv_ssm_w': nrm(ks[7], (DEPTH, SHORT_CONV, SSM_CONV_CH), SHORT_CONV ** -0.5),
        'conv_ssm_b': nrm(ks[8], (DEPTH, SSM_CONV_CH), 0.01),
        'conv_hy_w': nrm(ks[9], (DEPTH, SHORT_CONV, 3 * HY_W), SHORT_CONV ** -0.5),
        'conv_hy_b': nrm(ks[10], (DEPTH, 3 * HY_W), 0.01),
        'ret_decay_logit': ret_logit0[None, None, :] + nrm(ks[11], (DEPTH, 2, RET_HEADS), 0.01),
        'ssm_A_log': jnp.log(jax.random.uniform(ks[12], (DEPTH, 2, SSM_HEADS), f32, 1.0, 16.0)),
        'ssm_dt_bias': dt0 + jnp.log(-jnp.expm1(-dt0)),
        'ssm_D': 1.0 + nrm(ks[14], (DEPTH, SSM_HEADS), 0.01),
        'ssm_norm_w': 1.0 + nrm(ks[15], (DEPTH, SSM_W), 0.01),
        'hy_w1': nrm(ks[16], (DEPTH, HY_EMB, HY_FFN), HY_EMB ** -0.5),
        'hy_b1': nrm(ks[17], (DEPTH, HY_FFN), 0.1),
        'hy_w2': nrm(ks[18], (DEPTH, HY_FFN, HY_FFN), HY_FFN ** -0.5),
        'hy_b2': nrm(ks[19], (DEPTH, HY_FFN), 0.1),
        'hy_w3': nrm(ks[20], (DEPTH, HY_FFN, 2 * HY_W), 0.1 * HY_FFN ** -0.5),
        'hy_freq': 1.0 + nrm(ks[21], (DEPTH, HY_FFN), 0.01),
        'hy_bias': nrm(ks[22], (DEPTH, HY_W), 1.0),
        'pool_w': nrm(ks[23], (DEPTH, POOL_GROUPS, POOL_GROUP, POOL_GROUP), POOL_GROUP ** -0.5),
        'pool_scale': 1.0 + nrm(ks[24], (DEPTH, POOL_W), 0.01),
        'w_out': nrm(ks[25], (DEPTH, MIX_W, d), BETA * MIX_W ** -0.5),
        'ln_g': 1.0 + nrm(ks[26], (DEPTH, d), 0.01),
        'ln_b': nrm(ks[27], (DEPTH, d), 0.01),
    }


def reference(x, c, ctx, c_ctx, w_mod, b_mod, w_in, conv_ssm_w, conv_ssm_b, conv_hy_w, conv_hy_b,
              ret_decay_logit, ssm_A_log, ssm_dt_bias, ssm_D, ssm_norm_w, hy_w1, hy_b1, hy_w2, hy_b2,
              hy_w3, hy_freq, hy_bias, pool_w, pool_scale, w_out, ln_g, ln_b):
    f32 = jnp.float32
    h, hc = x, ctx
    for l in range(DEPTH):
        lp = {
            'w_in': w_in[l], 'conv_ssm_w': conv_ssm_w[l], 'conv_ssm_b': conv_ssm_b[l],
            'conv_hy_w': conv_hy_w[l], 'conv_hy_b': conv_hy_b[l], 'ret_decay_logit': ret_decay_logit[l],
            'ssm_A_log': ssm_A_log[l], 'ssm_dt_bias': ssm_dt_bias[l], 'ssm_D': ssm_D[l],
            'ssm_norm_w': ssm_norm_w[l], 'hy_w1': hy_w1[l], 'hy_b1': hy_b1[l], 'hy_w2': hy_w2[l],
            'hy_b2': hy_b2[l], 'hy_w3': hy_w3[l], 'hy_freq': hy_freq[l], 'hy_bias': hy_bias[l],
            'pool_w': pool_w[l], 'pool_scale': pool_scale[l], 'w_out': w_out[l],
        }
        mod = jax.nn.silu(c.astype(f32)) @ w_mod[l] + b_mod[l]
        shift, scale, gate = jnp.split(mod, 3, axis=-1)
        mod_c = jax.nn.silu(c_ctx.astype(f32))[None] @ w_mod[l] + b_mod[l]
        shift_c, scale_c, gate_c = jnp.split(mod_c, 3, axis=-1)
        if l < DEPTH - 1:
            out_c, states = _mix(hc, shift_c, scale_c, lp, None, False)
            hc_next = _post(hc, gate_c, out_c, ln_g[l], ln_b[l])
        else:
            states = _context_states(hc, shift_c, scale_c, lp)
            hc_next = hc
        out, _ = _mix(h, shift, scale, lp, states, True)
        h = _post(h, gate, out, ln_g[l], ln_b[l])
        hc = hc_next
    return h
```

```python
import functools
import math

import jax
import jax.numpy as jnp
import numpy as np
from jax import lax
from jax.experimental import pallas as pl
from jax.experimental.pallas import tpu as pltpu

D_MODEL = 4096
DEPTH = 2
GRID_W = 64
MIX_W = D_MODEL
BR_W = MIX_W // 4
HY_W = RET_W = POOL_W = SSM_W = BR_W
RET_HEADS = 8
RET_DH = RET_W // RET_HEADS
ROPE_BASE = 10000.0
SSM_HEADDIM = 64
SSM_HEADS = SSM_W // SSM_HEADDIM
SSM_GROUPS = 4
SSM_STATE = 128
SSM_GN = SSM_GROUPS * SSM_STATE
CHUNK = 128
POOL_WINDOWS = (2, 4, 8, 16)
POOL_GROUPS = len(POOL_WINDOWS)
POOL_GROUP = POOL_W // POOL_GROUPS
HY_BANDS = 16
HY_TARGET = 1e-2
HY_FAST = 0.3
HY_SLOW = 1.5
HY_MIN_DECAY = math.log(HY_TARGET) / HY_SLOW
HY_MAX_DECAY = math.log(HY_TARGET) / HY_FAST
ALPHA = (2.0 * DEPTH) ** 0.25
LN_EPS = 1e-5
RET_K = 0
RET_V = RET_K + RET_W
SSM_DT = RET_V + RET_W
SSM_X = SSM_DT + 2 * SSM_HEADS
SSM_B = SSM_X + SSM_W
N_KV = SSM_B + SSM_GN
RET_Q = N_KV
SSM_C = RET_Q + RET_W
HY = SSM_C + SSM_GN
POOL = HY + 3 * HY_W
GATE = POOL + POOL_W
N_IN = GATE + MIX_W

VMEM_LIMIT_BYTES = 56 * 1024 * 1024


def _matmul_kernel(a_ref, b_ref, o_ref):
    o_ref[...] = jnp.dot(a_ref[...], b_ref[...], preferred_element_type=jnp.float32)


def _matmul(a, b, tm, tn):
    m, k = a.shape
    _, n = b.shape
    assert m % tm == 0 and n % tn == 0
    return pl.pallas_call(
        _matmul_kernel,
        grid=(m // tm, n // tn),
        in_specs=[pl.BlockSpec((tm, k), lambda i, j: (i, 0)),
                  pl.BlockSpec((k, tn), lambda i, j: (0, j))],
        out_specs=pl.BlockSpec((tm, tn), lambda i, j: (i, j)),
        out_shape=jax.ShapeDtypeStruct((m, n), jnp.float32),
        compiler_params=pltpu.CompilerParams(
            dimension_semantics=("parallel", "parallel"),
            vmem_limit_bytes=VMEM_LIMIT_BYTES),
        name="matmul",
    )(a, b)


def _layernorm(z):
    z = z.astype(jnp.float32)
    mu = jnp.mean(z, -1, keepdims=True)
    var = jnp.mean(jnp.square(z - mu), -1, keepdims=True)
    return (z - mu) * lax.rsqrt(var + LN_EPS)


def _modulate(h, shift, scale):
    return (_layernorm(h) * (1.0 + scale[:, None, :]) + shift[:, None, :]).astype(h.dtype)


def _post(h, gate, out, g, b):
    z = ALPHA * h.astype(jnp.float32) + gate[:, None, :] * out.astype(jnp.float32)
    return (_layernorm(z) * g + b).astype(h.dtype)


def _short_conv(u, w, b):
    L = u.shape[1]
    up = jnp.pad(u, ((0, 0), (1, 1), (0, 0)))
    return up[:, :L] * w[0] + up[:, 1:L + 1] * w[1] + up[:, 2:] * w[2] + b


def _rope2d(t):
    L = t.shape[1]
    rows = L // GRID_W
    f32 = jnp.float32
    row = jnp.repeat(jnp.arange(rows), GRID_W).astype(f32)
    col = jnp.tile(jnp.arange(GRID_W), rows).astype(f32)
    nq = RET_DH // 4
    inv = ROPE_BASE ** (-jnp.arange(nq, dtype=f32) / nq)
    ang = jnp.concatenate([row[:, None] * inv, col[:, None] * inv], -1)
    cos = jnp.cos(ang)[None, :, None, :]
    sin = jnp.sin(ang)[None, :, None, :]
    t1, t2 = t[..., :RET_DH // 2], t[..., RET_DH // 2:]
    return jnp.concatenate([t1 * cos - t2 * sin, t1 * sin + t2 * cos], -1).astype(t.dtype)


def _group_to_heads(t):
    return jnp.repeat(t, SSM_HEADS // SSM_GROUPS, axis=2)


def _chunk_states(k, v, a, init):
    b, L, H, dk = k.shape
    dv = v.shape[-1]
    nc = L // CHUNK
    kc = k.reshape(b, nc, CHUNK, H, dk)
    vc = v.reshape(b, nc, CHUNK, H, dv)
    cum = jnp.cumsum(a.astype(jnp.float32).reshape(b, nc, CHUNK, H), axis=2)
    last = cum[:, :, -1]
    local = jnp.einsum('bnchk,bnchv->bnhkv', kc * jnp.exp(last[:, :, None] - cum)[..., None], vc)
    s0 = jnp.zeros((b, H, dk, dv), jnp.float32) if init is None else init

    def step(s, inp):
        dec, loc = inp
        return dec[..., None, None] * s + loc, s

    final, starts = lax.scan(step, s0, (jnp.moveaxis(jnp.exp(last), 1, 0), jnp.moveaxis(local, 1, 0)))
    return jnp.moveaxis(starts, 0, 1), final


def _chunk_outputs(q, k, v, a, starts):
    b, L, H, dk = q.shape
    dv = v.shape[-1]
    nc = L // CHUNK
    qc = q.reshape(b, nc, CHUNK, H, dk)
    kc = k.reshape(b, nc, CHUNK, H, dk)
    vc = v.reshape(b, nc, CHUNK, H, dv)
    cum = jnp.cumsum(a.astype(jnp.float32).reshape(b, nc, CHUNK, H), axis=2)
    diff = cum[:, :, :, None, :] - cum[:, :, None, :, :]
    lower = jnp.tril(jnp.ones((CHUNK, CHUNK), bool))[None, None, :, :, None]
    dec = jnp.exp(jnp.where(lower, diff, -jnp.inf))
    scores = jnp.einsum('bnihk,bnjhk->bnijh', qc, kc) * dec
    y = jnp.einsum('bnijh,bnjhv->bnihv', scores, vc)
    y = y + jnp.einsum('bnihk,bnhkv->bnihv', qc * jnp.exp(cum)[..., None], starts)
    return y.reshape(b, L, H, dv)


def _flip(t):
    return jnp.flip(t, axis=1)


def _bidir_scan(q, k_f, k_b, v, a_f, a_b, init_f, init_b):
    starts_f, fin_f = _chunk_states(k_f, v, a_f, init_f)
    y_f = _chunk_outputs(q, k_f, v, a_f, starts_f)
    starts_b, fin_b = _chunk_states(_flip(k_b), _flip(v), _flip(a_b), init_b)
    y_b = _flip(_chunk_outputs(_flip(q), _flip(k_b), _flip(v), _flip(a_b), starts_b))
    return y_f + y_b, fin_f, fin_b


def _bidir_final(k_f, k_b, v, a_f, a_b):
    _, fin_f = _chunk_states(k_f, v, a_f, None)
    _, fin_b = _chunk_states(_flip(k_b), _flip(v), _flip(a_b), None)
    return fin_f, fin_b


def _kv_features(pk, lp, latent):
    b, L, _ = pk.shape
    f32 = jnp.float32
    rk = pk[..., RET_K:RET_K + RET_W].reshape(b, L, RET_HEADS, RET_DH) * (RET_DH ** -0.5)
    if latent:
        rk = _rope2d(rk)
    rv = pk[..., RET_V:RET_V + RET_W].reshape(b, L, RET_HEADS, RET_DH)
    log_gamma = jax.nn.log_sigmoid(lp['ret_decay_logit'].astype(f32))
    ra_f = jnp.broadcast_to(log_gamma[0], (b, L, RET_HEADS))
    ra_b = jnp.broadcast_to(log_gamma[1], (b, L, RET_HEADS))
    dt = jax.nn.softplus(pk[..., SSM_DT:SSM_X].astype(f32).reshape(b, L, 2, SSM_HEADS)
                         + lp['ssm_dt_bias'].astype(f32))
    A = -jnp.exp(lp['ssm_A_log'].astype(f32))
    xb = jax.nn.silu(_short_conv(pk[..., SSM_X:N_KV], lp['conv_ssm_w'][:, :SSM_W + SSM_GN],
                                 lp['conv_ssm_b'][:SSM_W + SSM_GN]))
    sv = xb[..., :SSM_W].reshape(b, L, SSM_HEADS, SSM_HEADDIM)
    bh = _group_to_heads(xb[..., SSM_W:].reshape(b, L, SSM_GROUPS, SSM_STATE))
    sk_f = bh * dt[:, :, 0, :, None]
    sk_b = bh * dt[:, :, 1, :, None]
    sa_f = dt[:, :, 0] * A[0]
    sa_b = dt[:, :, 1] * A[1]
    return rk, rv, ra_f, ra_b, sk_f, sk_b, sv, sa_f, sa_b


def _hyena_filter_fft(L, lp):
    f32 = jnp.float32
    t = jnp.linspace(0.0, 1.0, L, dtype=f32)[:, None]
    w = 2.0 * math.pi * jnp.arange(L, dtype=f32)[:, None] / L
    bands = jnp.linspace(1e-4, HY_BANDS - 1, HY_BANDS, dtype=f32)[None, :]
    z = jnp.concatenate([t, jnp.cos(bands * w), -jnp.sin(bands * w)], axis=-1)
    freq = lp['hy_freq'].astype(f32)
    hdn = jnp.sin(freq * (z @ lp['hy_w1'].astype(f32) + lp['hy_b1'].astype(f32)))
    hdn = jnp.sin(freq * (hdn @ lp['hy_w2'].astype(f32) + lp['hy_b2'].astype(f32)))
    filt = hdn @ lp['hy_w3'].astype(f32)
    deltas = jnp.abs(jnp.linspace(HY_MIN_DECAY, HY_MAX_DECAY, HY_W, dtype=f32))
    window = jnp.exp(-t * deltas[None, :])
    h_fwd = filt[:, :HY_W] * window
    h_bwd = filt[:, HY_W:] * window
    buf = jnp.concatenate([h_fwd, jnp.zeros((1, HY_W), f32), jnp.flip(h_bwd[1:], axis=0)], axis=0)
    return jnp.fft.rfft(buf, axis=0)


def _long_conv(w, hf, bias):
    L = w.shape[1]
    wf = jnp.fft.rfft(w.astype(jnp.float32), n=2 * L, axis=1)
    y = jnp.fft.irfft(wf * hf[None], n=2 * L, axis=1)[:, :L]
    return y + w.astype(jnp.float32) * bias.astype(jnp.float32)


def _pool_mix(u, pool_w, pool_scale):
    b, L, _ = u.shape
    f32 = jnp.float32
    ug = u.astype(f32).reshape(b, L, POOL_GROUPS, POOL_GROUP)
    csum = jnp.concatenate([jnp.zeros_like(ug[:, :1]), jnp.cumsum(ug, axis=1)], axis=1)
    pos = jnp.arange(L)
    diffs = []
    for g, win in enumerate(POOL_WINDOWS):
        lo = jnp.clip(pos - win // 2, 0, L)
        hi = jnp.clip(pos - win // 2 + win, 0, L)
        mean = (csum[:, hi, g] - csum[:, lo, g]) / (hi - lo).astype(f32)[None, :, None]
        diffs.append(mean - ug[:, :, g])
    d = jnp.stack(diffs, axis=2)
    y = jnp.einsum('blgi,gio->blgo', d, pool_w)
    return y.reshape(b, L, POOL_W) * pool_scale


def _grouped_rmsnorm(y, wgt):
    b, L, _ = y.shape
    yg = y.astype(jnp.float32).reshape(b, L, SSM_GROUPS, SSM_W // SSM_GROUPS)
    yg = yg * lax.rsqrt(jnp.mean(jnp.square(yg), -1, keepdims=True) + LN_EPS)
    return yg.reshape(b, L, SSM_W) * wgt


def _in_proj(u, w_bf16):
    b, L, d = u.shape
    tm = 512 if L % 512 == 0 else 256
    n = w_bf16.shape[1]
    return _matmul(u.reshape(b * L, d).astype(jnp.bfloat16), w_bf16, tm, 640).reshape(b, L, n)


def _mix(h, shift, scale, lp, states, latent):
    b, L, _ = h.shape
    u = _modulate(h, shift, scale)
    proj = _in_proj(u, lp['w_in'])
    rk, rv, ra_f, ra_b, sk_f, sk_b, sv, sa_f, sa_b = _kv_features(proj[..., :N_KV], lp, latent)
    init = (None, None, None, None) if states is None else states
    rq = proj[..., RET_Q:RET_Q + RET_W].reshape(b, L, RET_HEADS, RET_DH)
    if latent:
        rq = _rope2d(rq)
    y_ret, ret_f, ret_b = _bidir_scan(rq, rk, rk, rv, ra_f, ra_b, init[0], init[1])
    y_ret = _layernorm(y_ret).reshape(b, L, RET_W)
    cg = jax.nn.silu(_short_conv(proj[..., SSM_C:SSM_C + SSM_GN], lp['conv_ssm_w'][:, SSM_W + SSM_GN:],
                                 lp['conv_ssm_b'][SSM_W + SSM_GN:]))
    ch = _group_to_heads(cg.reshape(b, L, SSM_GROUPS, SSM_STATE))
    y_ssm, ssm_f, ssm_b = _bidir_scan(ch, sk_f, sk_b, sv, sa_f, sa_b, init[2], init[3])
    y_ssm = (y_ssm + lp['ssm_D'][:, None] * sv).reshape(b, L, SSM_W)
    hy = _short_conv(proj[..., HY:HY + 3 * HY_W], lp['conv_hy_w'], lp['conv_hy_b'])
    hv, hx0, hx1 = jnp.split(hy, 3, axis=-1)
    y_hy = hx0 * _long_conv(hx1 * hv, _hyena_filter_fft(L, lp), lp['hy_bias'])
    y_pool = _pool_mix(proj[..., POOL:POOL + POOL_W], lp['pool_w'], lp['pool_scale'])
    g = jax.nn.silu(proj[..., GATE:GATE + MIX_W].astype(jnp.float32))
    g_hy, g_ret, g_pool, g_ssm = jnp.split(g, 4, axis=-1)
    y_ssm = _grouped_rmsnorm(y_ssm * g_ssm, lp['ssm_norm_w'])
    y = jnp.concatenate([y_hy * g_hy, y_ret * g_ret, y_pool * g_pool, y_ssm], axis=-1).astype(h.dtype)
    tm = 512 if L % 512 == 0 else 256
    out = _matmul(y.reshape(b * L, MIX_W).astype(jnp.bfloat16), lp['w_out'], tm, 1024)
    return out.reshape(b, L, D_MODEL), (ret_f, ret_b, ssm_f, ssm_b)


def _context_states(hc, shift, scale, lp):
    u = _modulate(hc, shift, scale)
    pk = _in_proj(u, lp['w_in'])[..., :N_KV]
    rk, rv, ra_f, ra_b, sk_f, sk_b, sv, sa_f, sa_b = _kv_features(pk, lp, False)
    ret_f, ret_b = _bidir_final(rk, rk, rv, ra_f, ra_b)
    ssm_f, ssm_b = _bidir_final(sk_f, sk_b, sv, sa_f, sa_b)
    return (ret_f, ret_b, ssm_f, ssm_b)


def kernel(x, c, ctx, c_ctx, w_mod, b_mod, w_in, conv_ssm_w, conv_ssm_b, conv_hy_w, conv_hy_b,
           ret_decay_logit, ssm_A_log, ssm_dt_bias, ssm_D, ssm_norm_w, hy_w1, hy_b1, hy_w2, hy_b2,
           hy_w3, hy_freq, hy_bias, pool_w, pool_scale, w_out, ln_g, ln_b):
    f32 = jnp.float32
    h, hc = x, ctx
    for l in range(DEPTH):
        lp = {
            'w_in': jnp.pad(w_in[l].astype(jnp.bfloat16), ((0, 0), (0, 13440 - N_IN))), 'conv_ssm_w': conv_ssm_w[l], 'conv_ssm_b': conv_ssm_b[l],
            'conv_hy_w': conv_hy_w[l], 'conv_hy_b': conv_hy_b[l], 'ret_decay_logit': ret_decay_logit[l],
            'ssm_A_log': ssm_A_log[l], 'ssm_dt_bias': ssm_dt_bias[l], 'ssm_D': ssm_D[l],
            'ssm_norm_w': ssm_norm_w[l], 'hy_w1': hy_w1[l], 'hy_b1': hy_b1[l], 'hy_w2': hy_w2[l],
            'hy_b2': hy_b2[l], 'hy_w3': hy_w3[l], 'hy_freq': hy_freq[l], 'hy_bias': hy_bias[l],
            'pool_w': pool_w[l], 'pool_scale': pool_scale[l], 'w_out': w_out[l].astype(jnp.bfloat16),
        }
        mod = jax.nn.silu(c.astype(f32)) @ w_mod[l] + b_mod[l]
        shift, scale, gate = jnp.split(mod, 3, axis=-1)
        mod_c = jax.nn.silu(c_ctx.astype(f32))[None] @ w_mod[l] + b_mod[l]
        shift_c, scale_c, gate_c = jnp.split(mod_c, 3, axis=-1)
        if l < DEPTH - 1:
            out_c, states = _mix(hc, shift_c, scale_c, lp, None, False)
            hc_next = _post(hc, gate_c, out_c, ln_g[l], ln_b[l])
        else:
            states = _context_states(hc, shift_c, scale_c, lp)
            hc_next = hc
        out, _ = _mix(h, shift, scale, lp, states, True)
        h = _post(h, gate, out, ln_g[l], ln_b[l])
        hc = hc_next
    return h
```

```python
import functools
import math

import jax
import jax.numpy as jnp
import numpy as np
from jax import lax
from jax.experimental import pallas as pl
from jax.experimental.pallas import tpu as pltpu

D_MODEL = 4096
DEPTH = 2
GRID_W = 64
MIX_W = D_MODEL
BR_W = MIX_W // 4
HY_W = RET_W = POOL_W = SSM_W = BR_W
RET_HEADS = 8
RET_DH = RET_W // RET_HEADS
ROPE_BASE = 10000.0
SSM_HEADDIM = 64
SSM_HEADS = SSM_W // SSM_HEADDIM
SSM_GROUPS = 4
SSM_HPG = SSM_HEADS // SSM_GROUPS
SSM_STATE = 128
SSM_GN = SSM_GROUPS * SSM_STATE
CHUNK = 128
POOL_WINDOWS = (2, 4, 8, 16)
POOL_GROUPS = len(POOL_WINDOWS)
POOL_GROUP = POOL_W // POOL_GROUPS
HY_BANDS = 16
HY_TARGET = 1e-2
HY_FAST = 0.3
HY_SLOW = 1.5
HY_MIN_DECAY = math.log(HY_TARGET) / HY_SLOW
HY_MAX_DECAY = math.log(HY_TARGET) / HY_FAST
ALPHA = (2.0 * DEPTH) ** 0.25
LN_EPS = 1e-5

O_RET_K = 0
O_RET_V = O_RET_K + RET_W
O_SSM_DT = O_RET_V + RET_W
O_SSM_X = O_SSM_DT + 2 * SSM_HEADS
O_SSM_B = O_SSM_X + SSM_W
O_RET_Q = O_SSM_B + SSM_GN
O_SSM_C = O_RET_Q + RET_W
O_HY = O_SSM_C + SSM_GN
O_POOL = O_HY + 3 * HY_W
O_GATE = O_POOL + POOL_W
N_IN = O_GATE + MIX_W

LANES = 128
SUBLANES = 8
C_RQ = 0
C_RK = 1024
C_RV = 2048
C_HV = 3072
C_SC = 4096
C_SB = 4608
C_SX = 5120
C_HX0 = 6144
C_HX1 = 7168
C_POOL = 8192
C_GATE = 9216
C_DT = 13312
N_PROJ = C_DT + LANES

VMEM_LIMIT_BYTES = 56 * 1024 * 1024
MXU_DTYPE = jnp.bfloat16
HIGHEST = lax.Precision.HIGHEST

_f32 = jnp.float32


def _cparams(*sem):
    return pltpu.CompilerParams(dimension_semantics=sem, vmem_limit_bytes=VMEM_LIMIT_BYTES)


def _matmul_kernel(a_ref, b_ref, o_ref):
    o_ref[...] = jnp.dot(a_ref[...], b_ref[...], preferred_element_type=_f32)


def _matmul(a, b, tm, tn):
    m, k = a.shape
    _, n = b.shape
    assert m % tm == 0 and n % tn == 0
    return pl.pallas_call(
        _matmul_kernel,
        grid=(m // tm, n // tn),
        in_specs=[pl.BlockSpec((tm, k), lambda i, j: (i, 0)),
                  pl.BlockSpec((k, tn), lambda i, j: (0, j))],
        out_specs=pl.BlockSpec((tm, tn), lambda i, j: (i, j)),
        out_shape=jax.ShapeDtypeStruct((m, n), _f32),
        compiler_params=_cparams("parallel", "parallel"),
        name="matmul",
    )(a, b)


def _permute_in_proj_weight(w):
    sec = lambda a, n: w[:, a:a + n]
    pad = jnp.zeros((w.shape[0], LANES - 2 * SSM_HEADS), w.dtype)
    cols = [sec(O_RET_Q, RET_W), sec(O_RET_K, RET_W), sec(O_RET_V, RET_W), sec(O_HY, HY_W),
            sec(O_SSM_C, SSM_GN), sec(O_SSM_B, SSM_GN), sec(O_SSM_X, SSM_W),
            sec(O_HY + HY_W, HY_W), sec(O_HY + 2 * HY_W, HY_W), sec(O_POOL, POOL_W),
            sec(O_GATE, MIX_W), sec(O_SSM_DT, 2 * SSM_HEADS), pad]
    return jnp.concatenate(cols, axis=1).astype(MXU_DTYPE)


def _rope_tables(L):
    rows = L // GRID_W
    row = jnp.repeat(jnp.arange(rows), GRID_W).astype(_f32)
    col = jnp.tile(jnp.arange(GRID_W), rows).astype(_f32)
    nq = RET_DH // 4
    inv = ROPE_BASE ** (-jnp.arange(nq, dtype=_f32) / nq)
    ang = jnp.concatenate([row[:, None] * inv, col[:, None] * inv], -1)
    cos, sin = jnp.cos(ang), jnp.sin(ang)
    return jnp.concatenate([cos, cos], -1), jnp.concatenate([-sin, sin], -1)


def _prep_ret_kernel(q_ref, k_ref, v_ref, cos_ref, sin_ref, qo_ref, ko_ref, vo_ref, *, rope):
    def rot(t):
        if not rope:
            return t
        return t * cos_ref[...] + pltpu.roll(t, RET_DH // 2, axis=1) * sin_ref[...]

    for h in range(RET_HEADS):
        sl = slice(h * RET_DH, (h + 1) * RET_DH)
        qo_ref[:, sl] = rot(q_ref[:, sl]).astype(qo_ref.dtype)
        ko_ref[:, sl] = rot(k_ref[:, sl] * (RET_DH ** -0.5)).astype(ko_ref.dtype)
    vo_ref[...] = v_ref[...].astype(vo_ref.dtype)


def _prep_ret(proj, L, rope):
    tr = 256
    cos, sin = _rope_tables(L) if rope else (jnp.ones((L, LANES), _f32), jnp.zeros((L, LANES), _f32))
    sec = lambda c: pl.BlockSpec((tr, RET_W), lambda i, c=c: (i, c // RET_W))
    tab = pl.BlockSpec((tr, LANES), lambda i: (i, 0))
    out = pl.BlockSpec((tr, RET_W), lambda i: (i, 0))
    shp = jax.ShapeDtypeStruct((L, RET_W), MXU_DTYPE)
    return pl.pallas_call(
        functools.partial(_prep_ret_kernel, rope=rope),
        grid=(L // tr,),
        in_specs=[sec(C_RQ), sec(C_RK), sec(C_RV), tab, tab],
        out_specs=[out, out, out],
        out_shape=[shp, shp, shp],
        compiler_params=_cparams("parallel"),
        name="prep_ret",
    )(proj, proj, proj, cos, sin)


def _scan_ret_kernel(logit_ref, qi_ref, ki_ref, vi_ref, qj_ref, kj_ref, vj_ref, s0f_ref, s0b_ref,
                     ya_ref, yb_ref, finf_ref, finb_ref,
                     sf, sb, dmask, f_out, f_upd, f_all, b_out, b_upd, b_all):
    i = pl.program_id(0)
    c = CHUNK

    @pl.when(i == 0)
    def _():
        sf[...] = s0f_ref[...]
        sb[...] = s0b_ref[...]
        ii = lax.broadcasted_iota(jnp.int32, (c, c), 0).astype(_f32)
        jj = lax.broadcasted_iota(jnp.int32, (c, c), 1).astype(_f32)
        for h in range(RET_HEADS):
            def lg(d):
                x = logit_ref[d, h]
                v = -jnp.log1p(jnp.exp(-x))
                return jnp.broadcast_to(v[0:1, :], (c, c))
            lf, lb = lg(0), lg(1)
            dmask[h] = jnp.where(ii > jj, jnp.exp(lf * (ii - jj)),
                                 jnp.where(jj > ii, jnp.exp(lb * (jj - ii)), 2.0))
            f_out[h] = jnp.exp(lf * (ii + 1.0))
            f_upd[h] = jnp.exp(lf * (c - 1.0 - ii))
            f_all[h] = jnp.exp(lf * float(c))
            b_out[h] = jnp.exp(lb * (c - ii))
            b_upd[h] = jnp.exp(lb * ii)
            b_all[h] = jnp.exp(lb * float(c))

    tn = (((0,), (0,)), ((), ()))
    nt = (((1,), (1,)), ((), ()))
    for h in range(RET_HEADS):
        sl = slice(h * RET_DH, (h + 1) * RET_DH)
        q, k, v = qi_ref[:, sl], ki_ref[:, sl], vi_ref[:, sl]
        s = lax.dot_general(q, k, nt, preferred_element_type=_f32) * dmask[h]
        y = jnp.dot(s.astype(MXU_DTYPE), v, preferred_element_type=_f32)
        y += jnp.dot((q.astype(_f32) * f_out[h]).astype(MXU_DTYPE), sf[h].astype(MXU_DTYPE),
                     preferred_element_type=_f32)
        ya_ref[:, sl] = y
        sf[h] = f_all[h] * sf[h] + lax.dot_general(
            (k.astype(_f32) * f_upd[h]).astype(MXU_DTYPE), v, tn, preferred_element_type=_f32)
        q, k, v = qj_ref[:, sl], kj_ref[:, sl], vj_ref[:, sl]
        yb_ref[:, sl] = jnp.dot((q.astype(_f32) * b_out[h]).astype(MXU_DTYPE), sb[h].astype(MXU_DTYPE),
                                preferred_element_type=_f32)
        sb[h] = b_all[h] * sb[h] + lax.dot_general(
            (k.astype(_f32) * b_upd[h]).astype(MXU_DTYPE), v, tn, preferred_element_type=_f32)

    @pl.when(i == pl.num_programs(0) - 1)
    def _():
        finf_ref[...] = sf[...]
        finb_ref[...] = sb[...]


def _scan_ret(q, k, v, logit, s0f, s0b):
    L = q.shape[0]
    nc = L // CHUNK
    logit_b = jnp.broadcast_to(logit.astype(_f32)[:, :, None, None], (2, RET_HEADS, SUBLANES, LANES))
    fw = pl.BlockSpec((CHUNK, RET_W), lambda i: (i, 0))
    bw = pl.BlockSpec((CHUNK, RET_W), lambda i: (nc - 1 - i, 0))
    st = pl.BlockSpec((RET_HEADS, RET_DH, RET_DH), lambda i: (0, 0, 0))
    yshape = jax.ShapeDtypeStruct((L, RET_W), _f32)
    sshape = jax.ShapeDtypeStruct((RET_HEADS, RET_DH, RET_DH), _f32)
    tile = pltpu.VMEM((RET_HEADS, CHUNK, CHUNK), _f32)
    return pl.pallas_call(
        _scan_ret_kernel,
        grid=(nc,),
        in_specs=[pl.BlockSpec((2, RET_HEADS, SUBLANES, LANES), lambda i: (0, 0, 0, 0)),
                  fw, fw, fw, bw, bw, bw, st, st],
        out_specs=[fw, bw, st, st],
        out_shape=[yshape, yshape, sshape, sshape],
        scratch_shapes=[pltpu.VMEM((RET_HEADS, RET_DH, RET_DH), _f32)] * 2 + [tile] * 7,
        compiler_params=_cparams("arbitrary"),
        name="scan_ret",
    )(logit_b, q, k, v, q, k, v, s0f, s0b)


def _shift_rows(x, prev_row, next_row):
    r = x.shape[0]
    rid = lax.broadcasted_iota(jnp.int32, x.shape, 0)
    up = jnp.where(rid == 0, prev_row, pltpu.roll(x, 1, axis=0))
    dn = jnp.where(rid == r - 1, next_row, pltpu.roll(x, r - 1, axis=0))
    return up, dn


def _conv3(x_ref, prev_ref, next_ref, w_ref, b_ref, has_prev, has_next):
    x = x_ref[...]
    prev_row = prev_ref[SUBLANES - 1:SUBLANES, :] * has_prev
    next_row = next_ref[0:1, :] * has_next
    up, dn = _shift_rows(x, prev_row, next_row)
    return up * w_ref[0:1, :] + x * w_ref[1:2, :] + dn * w_ref[2:3, :] + b_ref[...]


def _prep_ssd_kernel(x_ref, prev_ref, next_ref, dt_ref, w_ref, b_ref, dtb_ref, alog_ref,
                     co_ref, bo_ref, xo_ref, pack_ref):
    i = pl.program_id(0)
    has_prev = (i > 0).astype(_f32)
    has_next = (i < pl.num_programs(0) - 1).astype(_f32)
    y = _conv3(x_ref, prev_ref, next_ref, w_ref, b_ref, has_prev, has_next)
    y = y * jax.nn.sigmoid(y)
    co_ref[...] = y[:, :SSM_GN].astype(co_ref.dtype)
    bo_ref[...] = y[:, SSM_GN:2 * SSM_GN].astype(bo_ref.dtype)
    xo_ref[...] = y[:, 2 * SSM_GN:].astype(xo_ref.dtype)
    z = dt_ref[...] + dtb_ref[...]
    dt = jnp.maximum(z, 0.0) + jnp.log1p(jnp.exp(-jnp.abs(z)))
    a = dt * (-jnp.exp(alog_ref[...]))
    c = CHUNK
    ii = lax.broadcasted_iota(jnp.int32, (c, c), 0)
    jj = lax.broadcasted_iota(jnp.int32, (c, c), 1)
    lower = (jj <= ii).astype(_f32)
    upper = (jj >= ii).astype(_f32)
    lane = lax.broadcasted_iota(jnp.int32, (c, LANES), 1)
    dt_sh = pltpu.roll(dt, 2 * SSM_HEADS, axis=1)
    for n in range(x_ref.shape[0] // c):
        rs = slice(n * c, (n + 1) * c)
        pre = jnp.dot(lower, a[rs], precision=HIGHEST, preferred_element_type=_f32)
        suf = jnp.dot(upper, a[rs], precision=HIGHEST, preferred_element_type=_f32)
        pack_ref[rs, :] = jnp.where(lane < SSM_HEADS, pre,
                                    jnp.where(lane < 2 * SSM_HEADS, suf, dt_sh[rs]))


def _prep_ssd(proj, L, conv_w, conv_b, dt_bias, a_log):
    tr = 256
    w = jnp.concatenate([conv_w[:, SSM_W + SSM_GN:], conv_w[:, SSM_W:SSM_W + SSM_GN], conv_w[:, :SSM_W]], 1)
    b = jnp.concatenate([conv_b[SSM_W + SSM_GN:], conv_b[SSM_W:SSM_W + SSM_GN], conv_b[:SSM_W]])[None]
    w = jnp.pad(w.astype(_f32), ((0, SUBLANES - 3), (0, 0)))
    lanes = lambda t: jnp.pad(t.astype(_f32).reshape(1, 2 * SSM_HEADS), ((0, 0), (0, LANES - 2 * SSM_HEADS)))
    wd = SSM_W + 2 * SSM_GN
    nb = tr // SUBLANES
    last = L // SUBLANES - 1
    return pl.pallas_call(
        _prep_ssd_kernel,
        grid=(L // tr,),
        in_specs=[pl.BlockSpec((tr, wd), lambda i: (i, C_SC // wd)),
                  pl.BlockSpec((SUBLANES, wd), lambda i: (jnp.maximum(i * nb - 1, 0), C_SC // wd)),
                  pl.BlockSpec((SUBLANES, wd), lambda i: (jnp.minimum((i + 1) * nb, last), C_SC // wd)),
                  pl.BlockSpec((tr, LANES), lambda i: (i, C_DT // LANES)),
                  pl.BlockSpec((SUBLANES, wd), lambda i: (0, 0)),
                  pl.BlockSpec((1, wd), lambda i: (0, 0)),
                  pl.BlockSpec((1, LANES), lambda i: (0, 0)),
                  pl.BlockSpec((1, LANES), lambda i: (0, 0))],
        out_specs=[pl.BlockSpec((tr, SSM_GN), lambda i: (i, 0)),
                   pl.BlockSpec((tr, SSM_GN), lambda i: (i, 0)),
                   pl.BlockSpec((tr, SSM_W), lambda i: (i, 0)),
                   pl.BlockSpec((tr, LANES), lambda i: (i, 0))],
        out_shape=[jax.ShapeDtypeStruct((L, SSM_GN), MXU_DTYPE),
                   jax.ShapeDtypeStruct((L, SSM_GN), MXU_DTYPE),
                   jax.ShapeDtypeStruct((L, SSM_W), MXU_DTYPE),
                   jax.ShapeDtypeStruct((L, LANES), _f32)],
        compiler_params=_cparams("parallel"),
        name="prep_ssd",
    )(proj, proj, proj, proj, w, b.astype(_f32), lanes(dt_bias), lanes(a_log))


def _scan_ssd_kernel(ci_ref, bi_ref, xi_ref, pi_ref, cj_ref, bj_ref, xj_ref, pj_ref, dskip_ref,
                     s0f_ref, s0b_ref, ya_ref, yb_ref, finf_ref, finb_ref, sf, sb):
    i = pl.program_id(0)
    c = CHUNK
    H = SSM_HEADS

    @pl.when(i == 0)
    def _():
        sf[...] = s0f_ref[...]
        sb[...] = s0b_ref[...]

    tn = (((0,), (0,)), ((), ()))
    nt = (((1,), (1,)), ((), ()))
    ii = lax.broadcasted_iota(jnp.int32, (c, c), 0)
    jj = lax.broadcasted_iota(jnp.int32, (c, c), 1)
    pi = pi_ref[...]
    pit = pi.T
    pj = pj_ref[...]
    for g in range(SSM_GROUPS):
        gs = slice(g * SSM_STATE, (g + 1) * SSM_STATE)
        ci, bi = ci_ref[:, gs], bi_ref[:, gs]
        cj, bj = cj_ref[:, gs], bj_ref[:, gs]
        cb = lax.dot_general(ci, bi, nt, preferred_element_type=_f32)
        ci32, bi32, cj32, bj32 = (t.astype(_f32) for t in (ci, bi, cj, bj))
        for hh in range(SSM_HPG):
            h = g * SSM_HPG + hh
            hs = slice(h * SSM_HEADDIM, (h + 1) * SSM_HEADDIM)
            col = lambda p, o: p[:, o + h:o + h + 1]
            row = lambda o: pit[o + h:o + h + 1, :]
            pre_c, suf_c = col(pi, 0), col(pi, H)
            mf = jnp.where(ii >= jj, jnp.exp(jnp.minimum(pre_c - row(0), 0.0)), 0.0) * row(2 * H)
            mb = jnp.where(jj >= ii, jnp.exp(jnp.minimum(suf_c - row(H), 0.0)), 0.0) * row(3 * H)
            x = xi_ref[:, hs]
            y = jnp.dot((cb * (mf + mb)).astype(MXU_DTYPE), x, preferred_element_type=_f32)
            y += jnp.dot((ci32 * jnp.exp(pre_c)).astype(MXU_DTYPE), sf[h].astype(MXU_DTYPE),
                         preferred_element_type=_f32)
            ya_ref[:, hs] = y + dskip_ref[:, hs] * x.astype(_f32)
            tot = pi[c - 1:c, h:h + 1]
            wgt = jnp.exp(tot - pre_c) * col(pi, 2 * H)
            sf[h] = jnp.exp(tot) * sf[h] + lax.dot_general(
                (bi32 * wgt).astype(MXU_DTYPE), x, tn, preferred_element_type=_f32)
            suf_c = col(pj, H)
            x = xj_ref[:, hs]
            yb_ref[:, hs] = jnp.dot((cj32 * jnp.exp(suf_c)).astype(MXU_DTYPE), sb[h].astype(MXU_DTYPE),
                                    preferred_element_type=_f32)
            tot = pj[0:1, H + h:H + h + 1]
            wgt = jnp.exp(tot - suf_c) * col(pj, 3 * H)
            sb[h] = jnp.exp(tot) * sb[h] + lax.dot_general(
                (bj32 * wgt).astype(MXU_DTYPE), x, tn, preferred_element_type=_f32)

    @pl.when(i == pl.num_programs(0) - 1)
    def _():
        finf_ref[...] = sf[...]
        finb_ref[...] = sb[...]


def _scan_ssd(cs, bs, xs, pack, d_skip, s0f, s0b):
    L = xs.shape[0]
    nc = L // CHUNK
    dvec = jnp.repeat(d_skip.astype(_f32), SSM_HEADDIM)[None]
    fw = lambda w: pl.BlockSpec((CHUNK, w), lambda i: (i, 0))
    bw = lambda w: pl.BlockSpec((CHUNK, w), lambda i: (nc - 1 - i, 0))
    st = pl.BlockSpec((SSM_HEADS, SSM_STATE, SSM_HEADDIM), lambda i: (0, 0, 0))
    yshape = jax.ShapeDtypeStruct((L, SSM_W), _f32)
    sshape = jax.ShapeDtypeStruct((SSM_HEADS, SSM_STATE, SSM_HEADDIM), _f32)
    return pl.pallas_call(
        _scan_ssd_kernel,
        grid=(nc,),
        in_specs=[fw(SSM_GN), fw(SSM_GN), fw(SSM_W), fw(LANES), bw(SSM_GN), bw(SSM_GN), bw(SSM_W), bw(LANES),
                  pl.BlockSpec((1, SSM_W), lambda i: (0, 0)), st, st],
        out_specs=[fw(SSM_W), bw(SSM_W), st, st],
        out_shape=[yshape, yshape, sshape, sshape],
        scratch_shapes=[pltpu.VMEM((SSM_HEADS, SSM_STATE, SSM_HEADDIM), _f32)] * 2,
        compiler_params=_cparams("arbitrary"),
        name="scan_ssd",
    )(cs, bs, xs, pack, cs, bs, xs, pack, dvec, s0f, s0b)


def _layernorm(z):
    z = z.astype(jnp.float32)
    mu = jnp.mean(z, -1, keepdims=True)
    var = jnp.mean(jnp.square(z - mu), -1, keepdims=True)
    return (z - mu) * lax.rsqrt(var + LN_EPS)


def _modulate(h, shift, scale):
    return (_layernorm(h) * (1.0 + scale) + shift).astype(h.dtype)


def _post(h, gate, out, g, b):
    z = ALPHA * h.astype(jnp.float32) + gate * out.astype(jnp.float32)
    return (_layernorm(z) * g + b).astype(h.dtype)


def _short_conv(u, w, b):
    L = u.shape[0]
    up = jnp.pad(u, ((1, 1), (0, 0)))
    return up[:L] * w[0] + up[1:L + 1] * w[1] + up[2:] * w[2] + b


def _hyena_filter_fft(L, lp):
    f32 = jnp.float32
    t = jnp.linspace(0.0, 1.0, L, dtype=f32)[:, None]
    w = 2.0 * math.pi * jnp.arange(L, dtype=f32)[:, None] / L
    bands = jnp.linspace(1e-4, HY_BANDS - 1, HY_BANDS, dtype=f32)[None, :]
    z = jnp.concatenate([t, jnp.cos(bands * w), -jnp.sin(bands * w)], axis=-1)
    freq = lp['hy_freq'].astype(f32)
    hdn = jnp.sin(freq * (z @ lp['hy_w1'].astype(f32) + lp['hy_b1'].astype(f32)))
    hdn = jnp.sin(freq * (hdn @ lp['hy_w2'].astype(f32) + lp['hy_b2'].astype(f32)))
    filt = hdn @ lp['hy_w3'].astype(f32)
    deltas = jnp.abs(jnp.linspace(HY_MIN_DECAY, HY_MAX_DECAY, HY_W, dtype=f32))
    window = jnp.exp(-t * deltas[None, :])
    h_fwd = filt[:, :HY_W] * window
    h_bwd = filt[:, HY_W:] * window
    buf = jnp.concatenate([h_fwd, jnp.zeros((1, HY_W), f32), jnp.flip(h_bwd[1:], axis=0)], axis=0)
    return jnp.fft.rfft(buf, axis=0)


def _long_conv(w, hf, bias):
    L = w.shape[0]
    wf = jnp.fft.rfft(w.astype(jnp.float32), n=2 * L, axis=0)
    y = jnp.fft.irfft(wf * hf, n=2 * L, axis=0)[:L]
    return y + w.astype(jnp.float32) * bias.astype(jnp.float32)


def _pool_mix(u, pool_w, pool_scale):
    L = u.shape[0]
    f32 = jnp.float32
    ug = u.astype(f32).reshape(L, POOL_GROUPS, POOL_GROUP)
    csum = jnp.concatenate([jnp.zeros_like(ug[:1]), jnp.cumsum(ug, axis=0)], axis=0)
    pos = jnp.arange(L)
    diffs = []
    for g, win in enumerate(POOL_WINDOWS):
        lo = jnp.clip(pos - win // 2, 0, L)
        hi = jnp.clip(pos - win // 2 + win, 0, L)
        mean = (csum[hi, g] - csum[lo, g]) / (hi - lo).astype(f32)[:, None]
        diffs.append(mean - ug[:, g])
    d = jnp.stack(diffs, axis=1)
    y = jnp.einsum('lgi,gio->lgo', d, pool_w)
    return y.reshape(L, POOL_W) * pool_scale


def _grouped_rmsnorm(y, wgt):
    L = y.shape[0]
    yg = y.astype(jnp.float32).reshape(L, SSM_GROUPS, SSM_W // SSM_GROUPS)
    yg = yg * lax.rsqrt(jnp.mean(jnp.square(yg), -1, keepdims=True) + LN_EPS)
    return yg.reshape(L, SSM_W) * wgt


def _zero_states():
    return (jnp.zeros((RET_HEADS, RET_DH, RET_DH), _f32), jnp.zeros((RET_HEADS, RET_DH, RET_DH), _f32),
            jnp.zeros((SSM_HEADS, SSM_STATE, SSM_HEADDIM), _f32),
            jnp.zeros((SSM_HEADS, SSM_STATE, SSM_HEADDIM), _f32))


def _recurrent(proj, L, lp, states, latent):
    q, k, v = _prep_ret(proj, L, latent)
    ra, rb, ret_f, ret_b = _scan_ret(q, k, v, lp['ret_decay_logit'], states[0], states[1])
    cs, bs, xs, pack = _prep_ssd(proj, L, lp['conv_ssm_w'], lp['conv_ssm_b'], lp['ssm_dt_bias'], lp['ssm_A_log'])
    sa, sb_, ssm_f, ssm_b = _scan_ssd(cs, bs, xs, pack, lp['ssm_D'], states[2], states[3])
    return (ra, rb, sa, sb_), (ret_f, ret_b, ssm_f, ssm_b)


def _mix(h, shift, scale, lp, states, latent):
    L = h.shape[0]
    u = _modulate(h, shift, scale)
    tm = 512 if L % 512 == 0 else 256
    proj = _matmul(u.astype(MXU_DTYPE), lp['w_in'], tm, 640)
    (ra, rb, sa, sb_), fin = _recurrent(proj, L, lp, states, latent)
    y_ret = _layernorm((ra + rb).reshape(L, RET_HEADS, RET_DH)).reshape(L, RET_W)
    hy = [_short_conv(proj[:, c:c + HY_W], lp['conv_hy_w'][:, n * HY_W:(n + 1) * HY_W],
                      lp['conv_hy_b'][n * HY_W:(n + 1) * HY_W]) for n, c in enumerate((C_HV, C_HX0, C_HX1))]
    hv, hx0, hx1 = hy
    y_hy = hx0 * _long_conv(hx1 * hv, _hyena_filter_fft(L, lp), lp['hy_bias'])
    y_pool = _pool_mix(proj[:, C_POOL:C_POOL + POOL_W], lp['pool_w'], lp['pool_scale'])
    g = jax.nn.silu(proj[:, C_GATE:C_GATE + MIX_W])
    g_hy, g_ret, g_pool, g_ssm = jnp.split(g, 4, axis=-1)
    y_ssm = _grouped_rmsnorm((sa + sb_) * g_ssm, lp['ssm_norm_w'])
    y = jnp.concatenate([y_hy * g_hy, y_ret * g_ret, y_pool * g_pool, y_ssm], axis=-1)
    out = _matmul(y.astype(MXU_DTYPE), lp['w_out'], tm, 1024)
    return out, fin


def _context_states(hc, shift, scale, lp):
    L = hc.shape[0]
    u = _modulate(hc, shift, scale)
    proj = _matmul(u.astype(MXU_DTYPE), lp['w_in'], 256, 640)
    _, fin = _recurrent(proj, L, lp, _zero_states(), False)
    return fin


def kernel(x, c, ctx, c_ctx, w_mod, b_mod, w_in, conv_ssm_w, conv_ssm_b, conv_hy_w, conv_hy_b,
           ret_decay_logit, ssm_A_log, ssm_dt_bias, ssm_D, ssm_norm_w, hy_w1, hy_b1, hy_w2, hy_b2,
           hy_w3, hy_freq, hy_bias, pool_w, pool_scale, w_out, ln_g, ln_b):
    f32 = jnp.float32
    assert x.shape[0] == 1
    h, hc = x[0], ctx[0]
    for l in range(DEPTH):
        lp = {
            'w_in': _permute_in_proj_weight(w_in[l]), 'conv_ssm_w': conv_ssm_w[l], 'conv_ssm_b': conv_ssm_b[l],
            'conv_hy_w': conv_hy_w[l], 'conv_hy_b': conv_hy_b[l], 'ret_decay_logit': ret_decay_logit[l],
            'ssm_A_log': ssm_A_log[l], 'ssm_dt_bias': ssm_dt_bias[l], 'ssm_D': ssm_D[l],
            'ssm_norm_w': ssm_norm_w[l], 'hy_w1': hy_w1[l], 'hy_b1': hy_b1[l], 'hy_w2': hy_w2[l],
            'hy_b2': hy_b2[l], 'hy_w3': hy_w3[l], 'hy_freq': hy_freq[l], 'hy_bias': hy_bias[l],
            'pool_w': pool_w[l], 'pool_scale': pool_scale[l], 'w_out': w_out[l].astype(MXU_DTYPE),
        }
        mod = jax.nn.silu(c.astype(f32)) @ w_mod[l] + b_mod[l]
        shift, scale, gate = jnp.split(mod, 3, axis=-1)
        mod_c = jax.nn.silu(c_ctx.astype(f32))[None] @ w_mod[l] + b_mod[l]
        shift_c, scale_c, gate_c = jnp.split(mod_c, 3, axis=-1)
        if l < DEPTH - 1:
            out_c, states = _mix(hc, shift_c, scale_c, lp, _zero_states(), False)
            hc_next = _post(hc, gate_c, out_c, ln_g[l], ln_b[l])
        else:
            states = _context_states(hc, shift_c, scale_c, lp)
            hc_next = hc
        out, _ = _mix(h, shift, scale, lp, states, True)
        h = _post(h, gate, out, ln_g[l], ln_b[l])
        hc = hc_next
    return h[None]
```

```python
import functools
import math

import jax
import jax.numpy as jnp
import numpy as np
from jax import lax
from jax.experimental import pallas as pl
from jax.experimental.pallas import tpu as pltpu

D_MODEL = 4096
DEPTH = 2
GRID_W = 64
MIX_W = D_MODEL
BR_W = MIX_W // 4
HY_W = RET_W = POOL_W = SSM_W = BR_W
RET_HEADS = 8
RET_DH = RET_W // RET_HEADS
ROPE_BASE = 10000.0
SSM_HEADDIM = 64
SSM_HEADS = SSM_W // SSM_HEADDIM
SSM_GROUPS = 4
SSM_HPG = SSM_HEADS // SSM_GROUPS
SSM_STATE = 128
SSM_GN = SSM_GROUPS * SSM_STATE
CHUNK = 128
POOL_WINDOWS = (2, 4, 8, 16)
POOL_GROUPS = len(POOL_WINDOWS)
POOL_GROUP = POOL_W // POOL_GROUPS
HY_BANDS = 16
HY_TARGET = 1e-2
HY_FAST = 0.3
HY_SLOW = 1.5
HY_MIN_DECAY = math.log(HY_TARGET) / HY_SLOW
HY_MAX_DECAY = math.log(HY_TARGET) / HY_FAST
ALPHA = (2.0 * DEPTH) ** 0.25
LN_EPS = 1e-5

O_RET_K = 0
O_RET_V = O_RET_K + RET_W
O_SSM_DT = O_RET_V + RET_W
O_SSM_X = O_SSM_DT + 2 * SSM_HEADS
O_SSM_B = O_SSM_X + SSM_W
O_RET_Q = O_SSM_B + SSM_GN
O_SSM_C = O_RET_Q + RET_W
O_HY = O_SSM_C + SSM_GN
O_POOL = O_HY + 3 * HY_W
O_GATE = O_POOL + POOL_W
N_IN = O_GATE + MIX_W

LANES = 128
SUBLANES = 8
C_RQ = 0
C_RK = 1024
C_RV = 2048
C_HV = 3072
C_SC = 4096
C_SB = 4608
C_SX = 5120
C_HX0 = 6144
C_HX1 = 7168
C_POOL = 8192
C_GATE = 9216
C_DT = 13312
N_PROJ = C_DT + LANES

VMEM_LIMIT_BYTES = 56 * 1024 * 1024
MXU_DTYPE = jnp.bfloat16
HIGHEST = lax.Precision.HIGHEST

_f32 = jnp.float32


def _cparams(*sem):
    return pltpu.CompilerParams(dimension_semantics=sem, vmem_limit_bytes=VMEM_LIMIT_BYTES)


def _matmul_kernel(a_ref, b_ref, o_ref):
    o_ref[...] = jnp.dot(a_ref[...], b_ref[...], preferred_element_type=_f32)


def _matmul(a, b, tm, tn):
    m, k = a.shape
    _, n = b.shape
    assert m % tm == 0 and n % tn == 0
    return pl.pallas_call(
        _matmul_kernel,
        grid=(m // tm, n // tn),
        in_specs=[pl.BlockSpec((tm, k), lambda i, j: (i, 0)),
                  pl.BlockSpec((k, tn), lambda i, j: (0, j))],
        out_specs=pl.BlockSpec((tm, tn), lambda i, j: (i, j)),
        out_shape=jax.ShapeDtypeStruct((m, n), _f32),
        compiler_params=_cparams("parallel", "parallel"),
        name="matmul",
    )(a, b)


def _permute_in_proj_weight(w):
    sec = lambda a, n: w[:, a:a + n]
    pad = jnp.zeros((w.shape[0], LANES - 2 * SSM_HEADS), w.dtype)
    cols = [sec(O_RET_Q, RET_W), sec(O_RET_K, RET_W), sec(O_RET_V, RET_W), sec(O_HY, HY_W),
            sec(O_SSM_C, SSM_GN), sec(O_SSM_B, SSM_GN), sec(O_SSM_X, SSM_W),
            sec(O_HY + HY_W, HY_W), sec(O_HY + 2 * HY_W, HY_W), sec(O_POOL, POOL_W),
            sec(O_GATE, MIX_W), sec(O_SSM_DT, 2 * SSM_HEADS), pad]
    return jnp.concatenate(cols, axis=1).astype(MXU_DTYPE)


def _rope_tables(L):
    rows = L // GRID_W
    row = jnp.repeat(jnp.arange(rows), GRID_W).astype(_f32)
    col = jnp.tile(jnp.arange(GRID_W), rows).astype(_f32)
    nq = RET_DH // 4
    inv = ROPE_BASE ** (-jnp.arange(nq, dtype=_f32) / nq)
    ang = jnp.concatenate([row[:, None] * inv, col[:, None] * inv], -1)
    cos, sin = jnp.cos(ang), jnp.sin(ang)
    return jnp.concatenate([cos, cos], -1), jnp.concatenate([-sin, sin], -1)


def _prep_ret_kernel(q_ref, k_ref, v_ref, cos_ref, sin_ref, qo_ref, ko_ref, vo_ref, *, rope):
    def rot(t):
        if not rope:
            return t
        return t * cos_ref[...] + pltpu.roll(t, RET_DH // 2, axis=1) * sin_ref[...]

    for h in range(RET_HEADS):
        sl = slice(h * RET_DH, (h + 1) * RET_DH)
        qo_ref[:, sl] = rot(q_ref[:, sl]).astype(qo_ref.dtype)
        ko_ref[:, sl] = rot(k_ref[:, sl] * (RET_DH ** -0.5)).astype(ko_ref.dtype)
    vo_ref[...] = v_ref[...].astype(vo_ref.dtype)


def _prep_ret(proj, L, rope):
    tr = 256
    cos, sin = _rope_tables(L) if rope else (jnp.ones((L, LANES), _f32), jnp.zeros((L, LANES), _f32))
    sec = lambda c: pl.BlockSpec((tr, RET_W), lambda i, c=c: (i, c // RET_W))
    tab = pl.BlockSpec((tr, LANES), lambda i: (i, 0))
    out = pl.BlockSpec((tr, RET_W), lambda i: (i, 0))
    shp = jax.ShapeDtypeStruct((L, RET_W), MXU_DTYPE)
    return pl.pallas_call(
        functools.partial(_prep_ret_kernel, rope=rope),
        grid=(L // tr,),
        in_specs=[sec(C_RQ), sec(C_RK), sec(C_RV), tab, tab],
        out_specs=[out, out, out],
        out_shape=[shp, shp, shp],
        compiler_params=_cparams("parallel"),
        name="prep_ret",
    )(proj, proj, proj, cos, sin)


def _scan_ret_kernel(logit_ref, qi_ref, ki_ref, vi_ref, qj_ref, kj_ref, vj_ref, s0f_ref, s0b_ref,
                     ya_ref, yb_ref, finf_ref, finb_ref,
                     sf, sb, dmask, f_out, f_upd, f_all, b_out, b_upd, b_all):
    i = pl.program_id(0)
    c = CHUNK

    @pl.when(i == 0)
    def _():
        sf[...] = s0f_ref[...]
        sb[...] = s0b_ref[...]
        ii = lax.broadcasted_iota(jnp.int32, (c, c), 0).astype(_f32)
        jj = lax.broadcasted_iota(jnp.int32, (c, c), 1).astype(_f32)
        for h in range(RET_HEADS):
            def lg(d):
                x = logit_ref[d, h]
                v = -jnp.log1p(jnp.exp(-x))
                return jnp.broadcast_to(v[0:1, :], (c, c))
            lf, lb = lg(0), lg(1)
            dmask[h] = jnp.where(ii > jj, jnp.exp(lf * (ii - jj)),
                                 jnp.where(jj > ii, jnp.exp(lb * (jj - ii)), 2.0))
            f_out[h] = jnp.exp(lf * (ii + 1.0))
            f_upd[h] = jnp.exp(lf * (c - 1.0 - ii))
            f_all[h] = jnp.exp(lf * float(c))
            b_out[h] = jnp.exp(lb * (c - ii))
            b_upd[h] = jnp.exp(lb * ii)
            b_all[h] = jnp.exp(lb * float(c))

    tn = (((0,), (0,)), ((), ()))
    nt = (((1,), (1,)), ((), ()))
    for h in range(RET_HEADS):
        sl = slice(h * RET_DH, (h + 1) * RET_DH)
        q, k, v = qi_ref[:, sl], ki_ref[:, sl], vi_ref[:, sl]
        s = lax.dot_general(q, k, nt, preferred_element_type=_f32) * dmask[h]
        y = jnp.dot(s.astype(MXU_DTYPE), v, preferred_element_type=_f32)
        y += jnp.dot((q.astype(_f32) * f_out[h]).astype(MXU_DTYPE), sf[h].astype(MXU_DTYPE),
                     preferred_element_type=_f32)
        ya_ref[:, sl] = y
        sf[h] = f_all[h] * sf[h] + lax.dot_general(
            (k.astype(_f32) * f_upd[h]).astype(MXU_DTYPE), v, tn, preferred_element_type=_f32)
        q, k, v = qj_ref[:, sl], kj_ref[:, sl], vj_ref[:, sl]
        yb_ref[:, sl] = jnp.dot((q.astype(_f32) * b_out[h]).astype(MXU_DTYPE), sb[h].astype(MXU_DTYPE),
                                preferred_element_type=_f32)
        sb[h] = b_all[h] * sb[h] + lax.dot_general(
            (k.astype(_f32) * b_upd[h]).astype(MXU_DTYPE), v, tn, preferred_element_type=_f32)

    @pl.when(i == pl.num_programs(0) - 1)
    def _():
        finf_ref[...] = sf[...]
        finb_ref[...] = sb[...]


def _scan_ret(q, k, v, logit, s0f, s0b):
    L = q.shape[0]
    nc = L // CHUNK
    logit_b = jnp.broadcast_to(logit.astype(_f32)[:, :, None, None], (2, RET_HEADS, SUBLANES, LANES))
    fw = pl.BlockSpec((CHUNK, RET_W), lambda i: (i, 0))
    bw = pl.BlockSpec((CHUNK, RET_W), lambda i: (nc - 1 - i, 0))
    st = pl.BlockSpec((RET_HEADS, RET_DH, RET_DH), lambda i: (0, 0, 0))
    yshape = jax.ShapeDtypeStruct((L, RET_W), _f32)
    sshape = jax.ShapeDtypeStruct((RET_HEADS, RET_DH, RET_DH), _f32)
    tile = pltpu.VMEM((RET_HEADS, CHUNK, CHUNK), _f32)
    return pl.pallas_call(
        _scan_ret_kernel,
        grid=(nc,),
        in_specs=[pl.BlockSpec((2, RET_HEADS, SUBLANES, LANES), lambda i: (0, 0, 0, 0)),
                  fw, fw, fw, bw, bw, bw, st, st],
        out_specs=[fw, bw, st, st],
        out_shape=[yshape, yshape, sshape, sshape],
        scratch_shapes=[pltpu.VMEM((RET_HEADS, RET_DH, RET_DH), _f32)] * 2 + [tile] * 7,
        compiler_params=_cparams("arbitrary"),
        name="scan_ret",
    )(logit_b, q, k, v, q, k, v, s0f, s0b)


def _shift_rows(x, prev_row, next_row):
    r = x.shape[0]
    rid = lax.broadcasted_iota(jnp.int32, x.shape, 0)
    up = jnp.where(rid == 0, prev_row, pltpu.roll(x, 1, axis=0))
    dn = jnp.where(rid == r - 1, next_row, pltpu.roll(x, r - 1, axis=0))
    return up, dn


def _conv3(x_ref, prev_ref, next_ref, w_ref, b_ref, has_prev, has_next):
    x = x_ref[...]
    prev_row = prev_ref[SUBLANES - 1:SUBLANES, :] * has_prev
    next_row = next_ref[0:1, :] * has_next
    up, dn = _shift_rows(x, prev_row, next_row)
    return up * w_ref[0:1, :] + x * w_ref[1:2, :] + dn * w_ref[2:3, :] + b_ref[...]


def _prep_ssd_kernel(x_ref, prev_ref, next_ref, dt_ref, w_ref, b_ref, dtb_ref, alog_ref,
                     co_ref, bo_ref, xo_ref, pack_ref):
    i = pl.program_id(0)
    has_prev = (i > 0).astype(_f32)
    has_next = (i < pl.num_programs(0) - 1).astype(_f32)
    y = _conv3(x_ref, prev_ref, next_ref, w_ref, b_ref, has_prev, has_next)
    y = y * jax.nn.sigmoid(y)
    co_ref[...] = y[:, :SSM_GN].astype(co_ref.dtype)
    bo_ref[...] = y[:, SSM_GN:2 * SSM_GN].astype(bo_ref.dtype)
    xo_ref[...] = y[:, 2 * SSM_GN:].astype(xo_ref.dtype)
    z = dt_ref[...] + dtb_ref[...]
    dt = jnp.maximum(z, 0.0) + jnp.log1p(jnp.exp(-jnp.abs(z)))
    a = dt * (-jnp.exp(alog_ref[...]))
    c = CHUNK
    ii = lax.broadcasted_iota(jnp.int32, (c, c), 0)
    jj = lax.broadcasted_iota(jnp.int32, (c, c), 1)
    lower = (jj <= ii).astype(_f32)
    upper = (jj >= ii).astype(_f32)
    lane = lax.broadcasted_iota(jnp.int32, (c, LANES), 1)
    dt_sh = pltpu.roll(dt, 2 * SSM_HEADS, axis=1)
    for n in range(x_ref.shape[0] // c):
        rs = slice(n * c, (n + 1) * c)
        pre = jnp.dot(lower, a[rs], precision=HIGHEST, preferred_element_type=_f32)
        suf = jnp.dot(upper, a[rs], precision=HIGHEST, preferred_element_type=_f32)
        pack_ref[rs, :] = jnp.where(lane < SSM_HEADS, pre,
                                    jnp.where(lane < 2 * SSM_HEADS, suf, dt_sh[rs]))


def _prep_ssd(proj, L, conv_w, conv_b, dt_bias, a_log):
    tr = 256
    w = jnp.concatenate([conv_w[:, SSM_W + SSM_GN:], conv_w[:, SSM_W:SSM_W + SSM_GN], conv_w[:, :SSM_W]], 1)
    b = jnp.concatenate([conv_b[SSM_W + SSM_GN:], conv_b[SSM_W:SSM_W + SSM_GN], conv_b[:SSM_W]])[None]
    w = jnp.pad(w.astype(_f32), ((0, SUBLANES - 3), (0, 0)))
    lanes = lambda t: jnp.pad(t.astype(_f32).reshape(1, 2 * SSM_HEADS), ((0, 0), (0, LANES - 2 * SSM_HEADS)))
    wd = SSM_W + 2 * SSM_GN
    nb = tr // SUBLANES
    last = L // SUBLANES - 1
    return pl.pallas_call(
        _prep_ssd_kernel,
        grid=(L // tr,),
        in_specs=[pl.BlockSpec((tr, wd), lambda i: (i, C_SC // wd)),
                  pl.BlockSpec((SUBLANES, wd), lambda i: (jnp.maximum(i * nb - 1, 0), C_SC // wd)),
                  pl.BlockSpec((SUBLANES, wd), lambda i: (jnp.minimum((i + 1) * nb, last), C_SC // wd)),
                  pl.BlockSpec((tr, LANES), lambda i: (i, C_DT // LANES)),
                  pl.BlockSpec((SUBLANES, wd), lambda i: (0, 0)),
                  pl.BlockSpec((1, wd), lambda i: (0, 0)),
                  pl.BlockSpec((1, LANES), lambda i: (0, 0)),
                  pl.BlockSpec((1, LANES), lambda i: (0, 0))],
        out_specs=[pl.BlockSpec((tr, SSM_GN), lambda i: (i, 0)),
                   pl.BlockSpec((tr, SSM_GN), lambda i: (i, 0)),
                   pl.BlockSpec((tr, SSM_W), lambda i: (i, 0)),
                   pl.BlockSpec((tr, LANES), lambda i: (i, 0))],
        out_shape=[jax.ShapeDtypeStruct((L, SSM_GN), MXU_DTYPE),
                   jax.ShapeDtypeStruct((L, SSM_GN), MXU_DTYPE),
                   jax.ShapeDtypeStruct((L, SSM_W), MXU_DTYPE),
                   jax.ShapeDtypeStruct((L, LANES), _f32)],
        compiler_params=_cparams("parallel"),
        name="prep_ssd",
    )(proj, proj, proj, proj, w, b.astype(_f32), lanes(dt_bias), lanes(a_log))


def _scan_ssd_kernel(ci_ref, bi_ref, xi_ref, pi_ref, cj_ref, bj_ref, xj_ref, pj_ref, dskip_ref,
                     s0f_ref, s0b_ref, ya_ref, yb_ref, finf_ref, finb_ref, sf, sb):
    i = pl.program_id(0)
    c = CHUNK
    H = SSM_HEADS

    @pl.when(i == 0)
    def _():
        sf[...] = s0f_ref[...]
        sb[...] = s0b_ref[...]

    tn = (((0,), (0,)), ((), ()))
    nt = (((1,), (1,)), ((), ()))
    ii = lax.broadcasted_iota(jnp.int32, (c, c), 0)
    jj = lax.broadcasted_iota(jnp.int32, (c, c), 1)
    pi = pi_ref[...]
    pit = pi.T
    pj = pj_ref[...]
    for g in range(SSM_GROUPS):
        gs = slice(g * SSM_STATE, (g + 1) * SSM_STATE)
        ci, bi = ci_ref[:, gs], bi_ref[:, gs]
        cj, bj = cj_ref[:, gs], bj_ref[:, gs]
        cb = lax.dot_general(ci, bi, nt, preferred_element_type=_f32)
        ci32, bi32, cj32, bj32 = (t.astype(_f32) for t in (ci, bi, cj, bj))
        for hh in range(SSM_HPG):
            h = g * SSM_HPG + hh
            hs = slice(h * SSM_HEADDIM, (h + 1) * SSM_HEADDIM)
            col = lambda p, o: p[:, o + h:o + h + 1]
            row = lambda o: pit[o + h:o + h + 1, :]
            pre_c, suf_c = col(pi, 0), col(pi, H)
            mf = jnp.where(ii >= jj, jnp.exp(jnp.minimum(pre_c - row(0), 0.0)), 0.0) * row(2 * H)
            mb = jnp.where(jj >= ii, jnp.exp(jnp.minimum(suf_c - row(H), 0.0)), 0.0) * row(3 * H)
            x = xi_ref[:, hs]
            y = jnp.dot((cb * (mf + mb)).astype(MXU_DTYPE), x, preferred_element_type=_f32)
            y += jnp.dot((ci32 * jnp.exp(pre_c)).astype(MXU_DTYPE), sf[h].astype(MXU_DTYPE),
                         preferred_element_type=_f32)
            ya_ref[:, hs] = y + dskip_ref[:, hs] * x.astype(_f32)
            tot = pi[c - 1:c, h:h + 1]
            wgt = jnp.exp(tot - pre_c) * col(pi, 2 * H)
            sf[h] = jnp.exp(tot) * sf[h] + lax.dot_general(
                (bi32 * wgt).astype(MXU_DTYPE), x, tn, preferred_element_type=_f32)
            suf_c = col(pj, H)
            x = xj_ref[:, hs]
            yb_ref[:, hs] = jnp.dot((cj32 * jnp.exp(suf_c)).astype(MXU_DTYPE), sb[h].astype(MXU_DTYPE),
                                    preferred_element_type=_f32)
            tot = pj[0:1, H + h:H + h + 1]
            wgt = jnp.exp(tot - suf_c) * col(pj, 3 * H)
            sb[h] = jnp.exp(tot) * sb[h] + lax.dot_general(
                (bj32 * wgt).astype(MXU_DTYPE), x, tn, preferred_element_type=_f32)

    @pl.when(i == pl.num_programs(0) - 1)
    def _():
        finf_ref[...] = sf[...]
        finb_ref[...] = sb[...]


def _scan_ssd(cs, bs, xs, pack, d_skip, s0f, s0b):
    L = xs.shape[0]
    nc = L // CHUNK
    dvec = jnp.repeat(d_skip.astype(_f32), SSM_HEADDIM)[None]
    fw = lambda w: pl.BlockSpec((CHUNK, w), lambda i: (i, 0))
    bw = lambda w: pl.BlockSpec((CHUNK, w), lambda i: (nc - 1 - i, 0))
    st = pl.BlockSpec((SSM_HEADS, SSM_STATE, SSM_HEADDIM), lambda i: (0, 0, 0))
    yshape = jax.ShapeDtypeStruct((L, SSM_W), _f32)
    sshape = jax.ShapeDtypeStruct((SSM_HEADS, SSM_STATE, SSM_HEADDIM), _f32)
    return pl.pallas_call(
        _scan_ssd_kernel,
        grid=(nc,),
        in_specs=[fw(SSM_GN), fw(SSM_GN), fw(SSM_W), fw(LANES), bw(SSM_GN), bw(SSM_GN), bw(SSM_W), bw(LANES),
                  pl.BlockSpec((1, SSM_W), lambda i: (0, 0)), st, st],
        out_specs=[fw(SSM_W), bw(SSM_W), st, st],
        out_shape=[yshape, yshape, sshape, sshape],
        scratch_shapes=[pltpu.VMEM((SSM_HEADS, SSM_STATE, SSM_HEADDIM), _f32)] * 2,
        compiler_params=_cparams("arbitrary"),
        name="scan_ssd",
    )(cs, bs, xs, pack, cs, bs, xs, pack, dvec, s0f, s0b)


def _ln_rows(z):
    mu = jnp.mean(z, -1, keepdims=True)
    zc = z - mu
    var = jnp.mean(zc * zc, -1, keepdims=True)
    return zc * lax.rsqrt(var + LN_EPS)


def _modulate_kernel(h_ref, shift_ref, scale_ref, o_ref):
    o_ref[...] = (_ln_rows(h_ref[...]) * (1.0 + scale_ref[...]) + shift_ref[...]).astype(o_ref.dtype)


def _modulate(h, shift, scale):
    L, d = h.shape
    tr = 256
    vec = pl.BlockSpec((1, d), lambda i: (0, 0))
    return pl.pallas_call(
        _modulate_kernel,
        grid=(L // tr,),
        in_specs=[pl.BlockSpec((tr, d), lambda i: (i, 0)), vec, vec],
        out_specs=pl.BlockSpec((tr, d), lambda i: (i, 0)),
        out_shape=jax.ShapeDtypeStruct((L, d), MXU_DTYPE),
        compiler_params=_cparams("parallel"),
        name="modulate",
    )(h, shift, scale)


def _post_kernel(h_ref, out_ref, gate_ref, g_ref, b_ref, o_ref):
    z = ALPHA * h_ref[...] + gate_ref[...] * out_ref[...]
    o_ref[...] = _ln_rows(z) * g_ref[...] + b_ref[...]


def _post(h, gate, out, g, b):
    L, d = h.shape
    tr = 256
    vec = pl.BlockSpec((1, d), lambda i: (0, 0))
    row = pl.BlockSpec((tr, d), lambda i: (i, 0))
    return pl.pallas_call(
        _post_kernel,
        grid=(L // tr,),
        in_specs=[row, row, vec, vec, vec],
        out_specs=row,
        out_shape=jax.ShapeDtypeStruct((L, d), _f32),
        compiler_params=_cparams("parallel"),
        name="post",
    )(h, out, gate, g[None], b[None])


def _silu(x):
    return x * jax.nn.sigmoid(x)


def _merge_kernel(ra_ref, rb_ref, sa_ref, sb_ref, gr_ref, gs_ref, nw_ref, yr_ref, ys_ref):
    for h in range(RET_HEADS):
        sl = slice(h * RET_DH, (h + 1) * RET_DH)
        y = _ln_rows(ra_ref[:, sl] + rb_ref[:, sl])
        yr_ref[:, sl] = (y * _silu(gr_ref[:, sl])).astype(yr_ref.dtype)
    gw = SSM_W // SSM_GROUPS
    for g in range(SSM_GROUPS):
        sl = slice(g * gw, (g + 1) * gw)
        y = (sa_ref[:, sl] + sb_ref[:, sl]) * _silu(gs_ref[:, sl])
        y = y * lax.rsqrt(jnp.mean(y * y, -1, keepdims=True) + LN_EPS)
        ys_ref[:, sl] = (y * nw_ref[:, sl]).astype(ys_ref.dtype)


def _merge(ra, rb, sa, sb_, proj, norm_w):
    L = ra.shape[0]
    tr = 256
    row = pl.BlockSpec((tr, BR_W), lambda i: (i, 0))
    gate = lambda n: pl.BlockSpec((tr, BR_W), lambda i, n=n: (i, C_GATE // BR_W + n))
    shp = jax.ShapeDtypeStruct((L, BR_W), MXU_DTYPE)
    return pl.pallas_call(
        _merge_kernel,
        grid=(L // tr,),
        in_specs=[row, row, row, row, gate(1), gate(3), pl.BlockSpec((1, BR_W), lambda i: (0, 0))],
        out_specs=[row, row],
        out_shape=[shp, shp],
        compiler_params=_cparams("parallel"),
        name="merge",
    )(ra, rb, sa, sb_, proj, proj, norm_w.astype(_f32)[None])


def _pool_kernel(x_ref, prev_ref, next_ref, g_ref, pw_ref, ps_ref, o_ref, *, L):
    i = pl.program_id(0)
    t = x_ref.shape[0]
    halo = SUBLANES
    has_prev = (i > 0).astype(_f32)
    has_next = (i < pl.num_programs(0) - 1).astype(_f32)
    pos = i * t + lax.broadcasted_iota(jnp.int32, (t, 1), 0)
    for g, win in enumerate(POOL_WINDOWS):
        sl = slice(g * POOL_GROUP, (g + 1) * POOL_GROUP)
        x = x_ref[:, sl]
        s = jnp.concatenate([prev_ref[:, sl] * has_prev, x, next_ref[:, sl] * has_next], axis=0)
        rows = t + 2 * halo
        width = 1
        while width < win:
            s = s + pltpu.roll(s, rows - width, axis=0)
            width *= 2
        off = halo - win // 2
        if off:
            s = pltpu.roll(s, rows - off, axis=0)
        cnt = jnp.minimum(pos + win // 2, L) - jnp.maximum(pos - win // 2, 0)
        d = s[:t] / cnt.astype(_f32) - x
        y = jnp.dot(d.astype(MXU_DTYPE), pw_ref[g], preferred_element_type=_f32)
        o_ref[:, sl] = (y * ps_ref[:, sl] * _silu(g_ref[:, sl])).astype(o_ref.dtype)


def _pool(proj, L, pool_w, pool_scale):
    tr = 256
    nb = tr // SUBLANES
    last = L // SUBLANES - 1
    cb = C_POOL // POOL_W
    return pl.pallas_call(
        functools.partial(_pool_kernel, L=L),
        grid=(L // tr,),
        in_specs=[pl.BlockSpec((tr, POOL_W), lambda i: (i, cb)),
                  pl.BlockSpec((SUBLANES, POOL_W), lambda i: (jnp.maximum(i * nb - 1, 0), cb)),
                  pl.BlockSpec((SUBLANES, POOL_W), lambda i: (jnp.minimum((i + 1) * nb, last), cb)),
                  pl.BlockSpec((tr, POOL_W), lambda i: (i, C_GATE // BR_W + 2)),
                  pl.BlockSpec((POOL_GROUPS, POOL_GROUP, POOL_GROUP), lambda i: (0, 0, 0)),
                  pl.BlockSpec((1, POOL_W), lambda i: (0, 0))],
        out_specs=pl.BlockSpec((tr, POOL_W), lambda i: (i, 0)),
        out_shape=jax.ShapeDtypeStruct((L, POOL_W), MXU_DTYPE),
        compiler_params=_cparams("parallel"),
        name="pool",
    )(proj, proj, proj, proj, pool_w.astype(MXU_DTYPE), pool_scale.astype(_f32)[None])


def _prep_hy_kernel(v_ref, vp_ref, vn_ref, x0_ref, x0p_ref, x0n_ref, x1_ref, x1p_ref, x1n_ref,
                    g_ref, w_ref, b_ref, wo_ref, x0g_ref):
    i = pl.program_id(0)
    has_prev = (i > 0).astype(_f32)
    has_next = (i < pl.num_programs(0) - 1).astype(_f32)

    def conv(n, x_ref, p_ref, n_ref):
        sl = slice(n * HY_W, (n + 1) * HY_W)
        return _conv3(x_ref, p_ref, n_ref, w_ref.at[:, sl], b_ref.at[:, sl], has_prev, has_next)

    hv = conv(0, v_ref, vp_ref, vn_ref)
    hx0 = conv(1, x0_ref, x0p_ref, x0n_ref)
    hx1 = conv(2, x1_ref, x1p_ref, x1n_ref)
    wo_ref[...] = hx1 * hv
    x0g_ref[...] = hx0 * _silu(g_ref[...])


def _prep_hy(proj, L, conv_w, conv_b):
    tr = 256
    nb = tr // SUBLANES
    last = L // SUBLANES - 1
    w = jnp.pad(conv_w.astype(_f32), ((0, SUBLANES - 3), (0, 0)))

    def sec(c):
        cb = c // HY_W
        return [pl.BlockSpec((tr, HY_W), lambda i: (i, cb)),
                pl.BlockSpec((SUBLANES, HY_W), lambda i: (jnp.maximum(i * nb - 1, 0), cb)),
                pl.BlockSpec((SUBLANES, HY_W), lambda i: (jnp.minimum((i + 1) * nb, last), cb))]

    row = pl.BlockSpec((tr, HY_W), lambda i: (i, 0))
    shp = jax.ShapeDtypeStruct((L, HY_W), _f32)
    return pl.pallas_call(
        _prep_hy_kernel,
        grid=(L // tr,),
        in_specs=sec(C_HV) + sec(C_HX0) + sec(C_HX1) + [
            pl.BlockSpec((tr, HY_W), lambda i: (i, C_GATE // BR_W)),
            pl.BlockSpec((SUBLANES, 3 * HY_W), lambda i: (0, 0)),
            pl.BlockSpec((1, 3 * HY_W), lambda i: (0, 0))],
        out_specs=[row, row],
        out_shape=[shp, shp],
        compiler_params=_cparams("parallel"),
        name="prep_hy",
    )(*([proj] * 10), w, conv_b.astype(_f32)[None])


def _filter_kernel(z_ref, w1_ref, b1_ref, w2_ref, b2_ref, w3_ref, freq_ref, delta_ref, o_ref, *, L):
    t = z_ref.shape[0]
    z = z_ref[...]
    dot = functools.partial(jnp.dot, precision=HIGHEST, preferred_element_type=_f32)
    freq = freq_ref[...]
    hdn = jnp.sin(freq * (dot(z, w1_ref[...]) + b1_ref[...]))
    hdn = jnp.sin(freq * (dot(hdn, w2_ref[...]) + b2_ref[...]))
    filt = dot(hdn, w3_ref[...])
    n = pl.program_id(0) * t + lax.broadcasted_iota(jnp.int32, (t, 1), 0)
    taps = jnp.where(n < L, filt[:, :HY_W], jnp.where(n > L, filt[:, HY_W:], 0.0))
    o_ref[...] = taps * jnp.exp(-z[:, 0:1] * delta_ref[...])


def _hy_filter(L, lp):
    n = jnp.arange(2 * L)
    lag = jnp.minimum(jnp.where(n < L, n, 2 * L - n), L - 1)
    t = jnp.linspace(0.0, 1.0, L, dtype=_f32)[lag][:, None]
    w = (2.0 * math.pi * jnp.arange(L, dtype=_f32) / L)[lag][:, None]
    bands = jnp.linspace(1e-4, HY_BANDS - 1, HY_BANDS, dtype=_f32)[None, :]
    z = jnp.concatenate([t, jnp.cos(bands * w), -jnp.sin(bands * w)], axis=-1)
    emb = z.shape[1]
    z = jnp.pad(z, ((0, 0), (0, LANES - emb)))
    w1 = jnp.pad(lp['hy_w1'].astype(_f32), ((0, LANES - emb), (0, 0)))
    deltas = jnp.abs(jnp.linspace(HY_MIN_DECAY, HY_MAX_DECAY, HY_W, dtype=_f32))[None]
    tr = 512
    full = lambda a: pl.BlockSpec(a.shape, lambda i: (0,) * a.ndim)
    args = [w1, lp['hy_b1'].astype(_f32)[None], lp['hy_w2'].astype(_f32), lp['hy_b2'].astype(_f32)[None],
            lp['hy_w3'].astype(_f32), lp['hy_freq'].astype(_f32)[None], deltas]
    return pl.pallas_call(
        functools.partial(_filter_kernel, L=L),
        grid=(2 * L // tr,),
        in_specs=[pl.BlockSpec((tr, LANES), lambda i: (i, 0))] + [full(a) for a in args],
        out_specs=pl.BlockSpec((tr, HY_W), lambda i: (i, 0)),
        out_shape=jax.ShapeDtypeStruct((2 * L, HY_W), _f32),
        compiler_params=_cparams("parallel"),
        name="hy_filter",
    )(z, *args)


def _split(x):
    hi = x.astype(MXU_DTYPE)
    return hi, (x - hi.astype(_f32)).astype(MXU_DTYPE)


def _dot3(a_hi, a_lo, b):
    b_hi, b_lo = _split(b)
    d = lambda p, q: jnp.dot(p, q, preferred_element_type=_f32)
    return d(a_hi, b_hi) + (d(a_hi, b_lo) + d(a_lo, b_hi))


def _const_split(m):
    return _split(jnp.asarray(m, _f32))


def _cs(num, den):
    ang = 2.0 * np.pi * (np.asarray(num, np.int64) % den) / den
    return np.cos(ang), np.sin(ang)


FFT_N2 = LANES


def _fft_first_kernel(x_ref, mhi_ref, mlo_ref, o_ref):
    o_ref[...] = _dot3(mhi_ref[...], mlo_ref[...], x_ref[...])


def _fft_first(x2, n1):
    rows, cols = x2.shape
    c, s = _cs(np.outer(np.arange(n1), np.arange(rows)), n1)
    mhi, mlo = _const_split(np.concatenate([c, -s], 0))
    tcol = 4096
    return pl.pallas_call(
        _fft_first_kernel,
        grid=(cols // tcol,),
        in_specs=[pl.BlockSpec((rows, tcol), lambda j: (0, j)),
                  pl.BlockSpec((2 * n1, rows), lambda j: (0, 0)),
                  pl.BlockSpec((2 * n1, rows), lambda j: (0, 0))],
        out_specs=pl.BlockSpec((2 * n1, tcol), lambda j: (0, j)),
        out_shape=jax.ShapeDtypeStruct((2 * n1, cols), _f32),
        compiler_params=_cparams("parallel"),
        name="fft_first",
    )(x2, mhi, mlo)


def _fft_mid_kernel(*refs, conv):
    if conv:
        a_ref, twr_ref, twi_ref, fhi_ref, flo_ref, h_ref, ghi_ref, glo_ref, o_ref = refs
    else:
        a_ref, twr_ref, twi_ref, fhi_ref, flo_ref, o_ref = refs
    n2 = FFT_N2
    reps = a_ref.shape[-1] // LANES
    twr = jnp.concatenate([twr_ref[0]] * reps, axis=1)
    twi = jnp.concatenate([twi_ref[0]] * reps, axis=1)
    ar, ai = a_ref[0, 0], a_ref[1, 0]
    x = _dot3(fhi_ref[...], flo_ref[...],
              jnp.concatenate([ar * twr - ai * twi, ar * twi + ai * twr], axis=0))
    if not conv:
        o_ref[0, 0] = x[:n2]
        o_ref[1, 0] = x[n2:]
        return
    xr, xi = x[:n2], x[n2:]
    hr, hi = h_ref[0, 0], h_ref[1, 0]
    b = _dot3(ghi_ref[...], glo_ref[...],
              jnp.concatenate([xr * hr - xi * hi, xr * hi + xi * hr], axis=0))
    br, bi = b[:n2], b[n2:]
    o_ref[0, 0] = br * twr + bi * twi
    o_ref[1, 0] = bi * twr - br * twi


def _fft_mid(a, n1, hf=None):
    n2 = FFT_N2
    ch = a.shape[-1]
    n = n1 * n2
    idx = jnp.arange(n1)[:, None] * jnp.arange(n2)[None, :]
    ang = (2.0 * math.pi / n) * (idx % n).astype(_f32)
    twr = jnp.broadcast_to(jnp.cos(ang)[:, :, None], (n1, n2, LANES))
    twi = jnp.broadcast_to(-jnp.sin(ang)[:, :, None], (n1, n2, LANES))
    c, s = _cs(np.outer(np.arange(n2), np.arange(n2)), n2)
    fhi, flo = _const_split(np.block([[c, s], [-s, c]]))
    blk = pl.BlockSpec((2, 1, n2, ch), lambda k: (0, k, 0, 0))
    tw = pl.BlockSpec((1, n2, LANES), lambda k: (k, 0, 0))
    mat = pl.BlockSpec((2 * n2, 2 * n2), lambda k: (0, 0))
    args, specs = [a, twr, twi, fhi, flo], [blk, tw, tw, mat, mat]
    if hf is not None:
        ghi, glo = _const_split(np.block([[c, -s], [s, c]]))
        args += [hf, ghi, glo]
        specs += [blk, mat, mat]
    return pl.pallas_call(
        functools.partial(_fft_mid_kernel, conv=hf is not None),
        grid=(n1,),
        in_specs=specs,
        out_specs=blk,
        out_shape=jax.ShapeDtypeStruct(a.shape, _f32),
        compiler_params=_cparams("parallel"),
        name="fft_mid",
    )(*args)


def _fft_last_kernel(c_ref, mhi_ref, mlo_ref, w_ref, x0g_ref, bias_ref, o_ref):
    y = _dot3(mhi_ref[...], mlo_ref[...], c_ref[...])
    o_ref[...] = (x0g_ref[...] * (y + w_ref[...] * bias_ref[...])).astype(o_ref.dtype)


def _fft_last(c2, n1, w2, x0g2, bias):
    rows, cols = w2.shape
    n = n1 * FFT_N2
    c, s = _cs(np.outer(np.arange(rows), np.arange(n1)), n1)
    mhi, mlo = _const_split(np.concatenate([c, -s], 1) / n)
    tcol = 4096
    bias_t = jnp.tile(bias.astype(_f32), tcol // bias.shape[0])[None]
    blk = pl.BlockSpec((rows, tcol), lambda j: (0, j))
    return pl.pallas_call(
        _fft_last_kernel,
        grid=(cols // tcol,),
        in_specs=[pl.BlockSpec((2 * n1, tcol), lambda j: (0, j)),
                  pl.BlockSpec((rows, 2 * n1), lambda j: (0, 0)),
                  pl.BlockSpec((rows, 2 * n1), lambda j: (0, 0)),
                  blk, blk, pl.BlockSpec((1, tcol), lambda j: (0, 0))],
        out_specs=blk,
        out_shape=jax.ShapeDtypeStruct((rows, cols), MXU_DTYPE),
        compiler_params=_cparams("parallel"),
        name="fft_last",
    )(c2, mhi, mlo, w2, x0g2, bias_t)


def _hy_small_kernel(w_ref, buf_ref, x0g_ref, bias_ref, fwhi_ref, fwlo_ref, fbhi_ref, fblo_ref,
                     ihi_ref, ilo_ref, o_ref):
    n = buf_ref.shape[0]
    w = w_ref[...]
    wf = _dot3(fwhi_ref[...], fwlo_ref[...], w)
    hf = _dot3(fbhi_ref[...], fblo_ref[...], buf_ref[...])
    wr, wi, hr, hi = wf[:n], wf[n:], hf[:n], hf[n:]
    y = _dot3(ihi_ref[...], ilo_ref[...], jnp.concatenate([wr * hr - wi * hi, wr * hi + wi * hr], axis=0))
    o_ref[...] = (x0g_ref[...] * (y + w * bias_ref[...])).astype(o_ref.dtype)


def _hy_conv_small(w, buf, x0g, bias):
    L, ch = w.shape
    n = 2 * L
    tc = 256
    c, s = _cs(np.outer(np.arange(n), np.arange(n)), n)
    fb = np.concatenate([c, -s], 0)
    mats = [*_const_split(fb[:, :L]), *_const_split(fb),
            *_const_split(np.concatenate([c[:L], -s[:L]], 1) / n)]
    col = lambda r: pl.BlockSpec((r, tc), lambda j: (0, j))
    return pl.pallas_call(
        _hy_small_kernel,
        grid=(ch // tc,),
        in_specs=[col(L), col(n), col(L), col(1)] + [pl.BlockSpec(m.shape, lambda j: (0, 0)) for m in mats],
        out_specs=col(L),
        out_shape=jax.ShapeDtypeStruct((L, ch), MXU_DTYPE),
        compiler_params=_cparams("parallel"),
        name="hy_conv_small",
    )(w, buf, x0g, bias.astype(_f32)[None], *mats)


def _hy_conv(w, buf, x0g, bias):
    L, ch = w.shape
    if L < 512:
        return _hy_conv_small(w, buf, x0g, bias)
    n1 = 2 * L // FFT_N2
    cols = FFT_N2 * ch
    hf = _fft_mid(_fft_first(buf.reshape(n1, cols), n1).reshape(2, n1, FFT_N2, ch), n1)
    a = _fft_first(w.reshape(n1 // 2, cols), n1).reshape(2, n1, FFT_N2, ch)
    cc = _fft_mid(a, n1, hf)
    y = _fft_last(cc.reshape(2 * n1, cols), n1, w.reshape(n1 // 2, cols), x0g.reshape(n1 // 2, cols), bias)
    return y.reshape(L, ch)


def _zero_states():
    return (jnp.zeros((RET_HEADS, RET_DH, RET_DH), _f32), jnp.zeros((RET_HEADS, RET_DH, RET_DH), _f32),
            jnp.zeros((SSM_HEADS, SSM_STATE, SSM_HEADDIM), _f32),
            jnp.zeros((SSM_HEADS, SSM_STATE, SSM_HEADDIM), _f32))


def _recurrent(proj, L, lp, states, latent):
    q, k, v = _prep_ret(proj, L, latent)
    ra, rb, ret_f, ret_b = _scan_ret(q, k, v, lp['ret_decay_logit'], states[0], states[1])
    cs, bs, xs, pack = _prep_ssd(proj, L, lp['conv_ssm_w'], lp['conv_ssm_b'], lp['ssm_dt_bias'], lp['ssm_A_log'])
    sa, sb_, ssm_f, ssm_b = _scan_ssd(cs, bs, xs, pack, lp['ssm_D'], states[2], states[3])
    return (ra, rb, sa, sb_), (ret_f, ret_b, ssm_f, ssm_b)


def _in_proj(h, shift, scale, lp):
    L = h.shape[0]
    tm = 512 if L % 512 == 0 else 256
    return _matmul(_modulate(h, shift, scale), lp['w_in'], tm, 640)


def _mix(h, shift, scale, lp, states, latent):
    L = h.shape[0]
    proj = _in_proj(h, shift, scale, lp)
    (ra, rb, sa, sb_), fin = _recurrent(proj, L, lp, states, latent)
    y_ret, y_ssm = _merge(ra, rb, sa, sb_, proj, lp['ssm_norm_w'])
    w, x0g = _prep_hy(proj, L, lp['conv_hy_w'], lp['conv_hy_b'])
    y_hy = _hy_conv(w, _hy_filter(L, lp), x0g, lp['hy_bias'])
    y_pool = _pool(proj, L, lp['pool_w'], lp['pool_scale'])
    y = jnp.concatenate([y_hy, y_ret, y_pool, y_ssm], axis=-1)
    tm = 512 if L % 512 == 0 else 256
    out = _matmul(y, lp['w_out'], tm, 1024)
    return out, fin


def _context_states(hc, shift, scale, lp):
    proj = _in_proj(hc, shift, scale, lp)
    _, fin = _recurrent(proj, hc.shape[0], lp, _zero_states(), False)
    return fin


def kernel(x, c, ctx, c_ctx, w_mod, b_mod, w_in, conv_ssm_w, conv_ssm_b, conv_hy_w, conv_hy_b,
           ret_decay_logit, ssm_A_log, ssm_dt_bias, ssm_D, ssm_norm_w, hy_w1, hy_b1, hy_w2, hy_b2,
           hy_w3, hy_freq, hy_bias, pool_w, pool_scale, w_out, ln_g, ln_b):
    f32 = jnp.float32
    assert x.shape[0] == 1
    h, hc = x[0], ctx[0]
    for l in range(DEPTH):
        lp = {
            'w_in': _permute_in_proj_weight(w_in[l]), 'conv_ssm_w': conv_ssm_w[l], 'conv_ssm_b': conv_ssm_b[l],
            'conv_hy_w': conv_hy_w[l], 'conv_hy_b': conv_hy_b[l], 'ret_decay_logit': ret_decay_logit[l],
            'ssm_A_log': ssm_A_log[l], 'ssm_dt_bias': ssm_dt_bias[l], 'ssm_D': ssm_D[l],
            'ssm_norm_w': ssm_norm_w[l], 'hy_w1': hy_w1[l], 'hy_b1': hy_b1[l], 'hy_w2': hy_w2[l],
            'hy_b2': hy_b2[l], 'hy_w3': hy_w3[l], 'hy_freq': hy_freq[l], 'hy_bias': hy_bias[l],
            'pool_w': pool_w[l], 'pool_scale': pool_scale[l], 'w_out': w_out[l].astype(MXU_DTYPE),
        }
        mod = jax.nn.silu(c.astype(f32)) @ w_mod[l] + b_mod[l]
        shift, scale, gate = jnp.split(mod, 3, axis=-1)
        mod_c = jax.nn.silu(c_ctx.astype(f32))[None] @ w_mod[l] + b_mod[l]
        shift_c, scale_c, gate_c = jnp.split(mod_c, 3, axis=-1)
        if l < DEPTH - 1:
            out_c, states = _mix(hc, shift_c, scale_c, lp, _zero_states(), False)
            hc_next = _post(hc, gate_c, out_c, ln_g[l], ln_b[l])
        else:
            states = _context_states(hc, shift_c, scale_c, lp)
            hc_next = hc
        out, _ = _mix(h, shift, scale, lp, states, True)
        h = _post(h, gate, out, ln_g[l], ln_b[l])
        hc = hc_next
    return h[None]
```

```python
import functools
import math

import jax
import jax.numpy as jnp
import numpy as np
from jax import lax
from jax.experimental import pallas as pl
from jax.experimental.pallas import tpu as pltpu

D_MODEL = 4096
DEPTH = 2
GRID_W = 64
MIX_W = D_MODEL
BR_W = MIX_W // 4
HY_W = RET_W = POOL_W = SSM_W = BR_W
RET_HEADS = 8
RET_DH = RET_W // RET_HEADS
ROPE_BASE = 10000.0
SSM_HEADDIM = 64
SSM_HEADS = SSM_W // SSM_HEADDIM
SSM_GROUPS = 4
SSM_HPG = SSM_HEADS // SSM_GROUPS
SSM_STATE = 128
SSM_GN = SSM_GROUPS * SSM_STATE
CHUNK = 128
POOL_WINDOWS = (2, 4, 8, 16)
POOL_GROUPS = len(POOL_WINDOWS)
POOL_GROUP = POOL_W // POOL_GROUPS
HY_BANDS = 16
HY_TARGET = 1e-2
HY_FAST = 0.3
HY_SLOW = 1.5
HY_MIN_DECAY = math.log(HY_TARGET) / HY_SLOW
HY_MAX_DECAY = math.log(HY_TARGET) / HY_FAST
ALPHA = (2.0 * DEPTH) ** 0.25
LN_EPS = 1e-5

O_RET_K = 0
O_RET_V = O_RET_K + RET_W
O_SSM_DT = O_RET_V + RET_W
O_SSM_X = O_SSM_DT + 2 * SSM_HEADS
O_SSM_B = O_SSM_X + SSM_W
O_RET_Q = O_SSM_B + SSM_GN
O_SSM_C = O_RET_Q + RET_W
O_HY = O_SSM_C + SSM_GN
O_POOL = O_HY + 3 * HY_W
O_GATE = O_POOL + POOL_W
N_IN = O_GATE + MIX_W

LANES = 128
SUBLANES = 8
N_A = O_SSM_DT
B_SX = 0
B_SB = O_SSM_B - O_SSM_X
B_RQ = O_RET_Q - O_SSM_X
B_SC = O_SSM_C - O_SSM_X
B_HY = O_HY - O_SSM_X
B_POOL = O_POOL - O_SSM_X
B_GATE = O_GATE - O_SSM_X
N_B = N_IN - O_SSM_X

VMEM_LIMIT_BYTES = 56 * 1024 * 1024
MXU_DTYPE = jnp.bfloat16
HIGHEST = lax.Precision.HIGHEST

_f32 = jnp.float32


def _cparams(*sem):
    return pltpu.CompilerParams(dimension_semantics=sem, vmem_limit_bytes=VMEM_LIMIT_BYTES)


def _silu(x):
    return x * jax.nn.sigmoid(x)


def _ln_rows(z):
    mu = jnp.mean(z, -1, keepdims=True)
    zc = z - mu
    var = jnp.mean(zc * zc, -1, keepdims=True)
    return zc * lax.rsqrt(var + LN_EPS)


def _prep_w_in_kernel(w_ref, a_ref, b_ref, d_ref):
    a_ref[0] = w_ref[0, :, 0:N_A].astype(a_ref.dtype)
    b_ref[0] = w_ref[0, :, O_SSM_X:N_IN].astype(b_ref.dtype)
    pad = jnp.zeros((w_ref.shape[1], LANES - 2 * SSM_HEADS), _f32)
    d_ref[0] = jnp.concatenate([w_ref[0, :, O_SSM_DT:O_SSM_X], pad], axis=1).astype(d_ref.dtype)


def _prep_w_in(w):
    dep, k, n = w.shape
    tr = 128
    blk = lambda c: pl.BlockSpec((1, tr, c), lambda l, i: (l, i, 0))
    shp = lambda c: jax.ShapeDtypeStruct((dep, k, c), MXU_DTYPE)
    return pl.pallas_call(
        _prep_w_in_kernel,
        grid=(dep, k // tr),
        in_specs=[blk(n)],
        out_specs=[blk(N_A), blk(N_B), blk(LANES)],
        out_shape=[shp(N_A), shp(N_B), shp(LANES)],
        compiler_params=_cparams("parallel", "parallel"),
        name="prep_w_in",
    )(w)


def _cast_kernel(w_ref, o_ref):
    o_ref[...] = w_ref[...].astype(o_ref.dtype)


def _cast_w_out(w):
    dep, k, n = w.shape
    tr = 512
    blk = pl.BlockSpec((1, tr, n), lambda l, i: (l, i, 0))
    return pl.pallas_call(
        _cast_kernel,
        grid=(dep, k // tr),
        in_specs=[blk],
        out_specs=blk,
        out_shape=jax.ShapeDtypeStruct(w.shape, MXU_DTYPE),
        compiler_params=_cparams("parallel", "parallel"),
        name="cast_w_out",
    )(w)


def _adaln_kernel(c_ref, w_ref, b_ref, o_ref):
    tn = w_ref.shape[-1]
    outs = []
    for m in range(c_ref.shape[0]):
        x = _silu(c_ref[m])
        cols = [jnp.sum(x * w_ref[0, :, j * LANES:(j + 1) * LANES], axis=0, keepdims=True)
                for j in range(tn // LANES)]
        outs.append(jnp.concatenate(cols, axis=1))
    outs.append(jnp.zeros((SUBLANES - len(outs), tn), _f32))
    o_ref[0] = jnp.concatenate(outs, axis=0) + b_ref[0]


def _adaln(c_rows, w_mod, b_mod):
    dep, k, n = w_mod.shape
    r = c_rows.shape[0]
    tn = 512
    cb = jnp.broadcast_to(c_rows.astype(_f32)[:, :, None], (r, k, LANES))
    return pl.pallas_call(
        _adaln_kernel,
        grid=(dep, n // tn),
        in_specs=[pl.BlockSpec((r, k, LANES), lambda l, j: (0, 0, 0)),
                  pl.BlockSpec((1, k, tn), lambda l, j: (l, 0, j)),
                  pl.BlockSpec((1, 1, tn), lambda l, j: (l, 0, j))],
        out_specs=pl.BlockSpec((1, SUBLANES, tn), lambda l, j: (l, 0, j)),
        out_shape=jax.ShapeDtypeStruct((dep, SUBLANES, n), _f32),
        compiler_params=_cparams("parallel", "parallel"),
        name="adaln",
    )(cb, w_mod, b_mod[:, None, :])


def _matmul_kernel(a_ref, b_ref, o_ref):
    o_ref[...] = jnp.dot(a_ref[...], b_ref[...], preferred_element_type=_f32)


def _matmul(a, b, tm, tn):
    m, k = a.shape
    _, n = b.shape
    assert m % tm == 0 and n % tn == 0
    return pl.pallas_call(
        _matmul_kernel,
        grid=(m // tm, n // tn),
        in_specs=[pl.BlockSpec((tm, k), lambda i, j: (i, 0)),
                  pl.BlockSpec((k, tn), lambda i, j: (0, j))],
        out_specs=pl.BlockSpec((tm, tn), lambda i, j: (i, j)),
        out_shape=jax.ShapeDtypeStruct((m, n), _f32),
        compiler_params=_cparams("parallel", "parallel"),
        name="matmul",
    )(a, b)


def _modulate_kernel(h_ref, shift_ref, scale_ref, o_ref):
    o_ref[...] = (_ln_rows(h_ref[...]) * (1.0 + scale_ref[...]) + shift_ref[...]).astype(o_ref.dtype)


def _modulate(h, shift, scale):
    L, d = h.shape
    tr = 256
    vec = pl.BlockSpec((1, d), lambda i: (0, 0))
    return pl.pallas_call(
        _modulate_kernel,
        grid=(L // tr,),
        in_specs=[pl.BlockSpec((tr, d), lambda i: (i, 0)), vec, vec],
        out_specs=pl.BlockSpec((tr, d), lambda i: (i, 0)),
        out_shape=jax.ShapeDtypeStruct((L, d), MXU_DTYPE),
        compiler_params=_cparams("parallel"),
        name="modulate",
    )(h, shift, scale)


def _in_proj(h, shift, scale, lp):
    L = h.shape[0]
    tm = 512 if L % 512 == 0 else 256
    u = _modulate(h, shift, scale)
    return (_matmul(u, lp['w_a'], tm, 1024), _matmul(u, lp['w_b'], tm, 1024), _matmul(u, lp['w_dt'], tm, LANES))


def _out_proj_kernel(y0_ref, y1_ref, y2_ref, y3_ref, w_ref, h_ref, gate_ref, g_ref, b_ref, o_ref):
    k = pl.program_id(1)
    for n, y_ref in enumerate((y0_ref, y1_ref, y2_ref, y3_ref)):
        @pl.when(k == n)
        def _():
            d = jnp.dot(y_ref[...], w_ref[...], preferred_element_type=_f32)
            if n == 0:
                o_ref[...] = d
            else:
                o_ref[...] += d

    @pl.when(k == pl.num_programs(1) - 1)
    def _():
        z = ALPHA * h_ref[...] + gate_ref[...] * o_ref[...]
        o_ref[...] = _ln_rows(z) * g_ref[...] + b_ref[...]


def _out_proj(ys, w, h, gate, g, b):
    L, d = h.shape
    tm = 512 if L % 512 == 0 else 256
    lhs = pl.BlockSpec((tm, BR_W), lambda i, k: (i, 0))
    vec = pl.BlockSpec((1, d), lambda i, k: (0, 0))
    return pl.pallas_call(
        _out_proj_kernel,
        grid=(L // tm, len(ys)),
        in_specs=[lhs] * len(ys) + [
            pl.BlockSpec((BR_W, d), lambda i, k: (k, 0)),
            pl.BlockSpec((tm, d), lambda i, k: (i, 0), pipeline_mode=pl.Buffered(1)), vec, vec, vec],
        out_specs=pl.BlockSpec((tm, d), lambda i, k: (i, 0), pipeline_mode=pl.Buffered(1)),
        out_shape=jax.ShapeDtypeStruct((L, d), _f32),
        compiler_params=_cparams("parallel", "arbitrary"),
        name="out_proj",
    )(*ys, w, h, gate, g[None], b[None])


def _rope_tables(L):
    rows = L // GRID_W
    row = jnp.repeat(jnp.arange(rows), GRID_W).astype(_f32)
    col = jnp.tile(jnp.arange(GRID_W), rows).astype(_f32)
    nq = RET_DH // 4
    inv = ROPE_BASE ** (-jnp.arange(nq, dtype=_f32) / nq)
    ang = jnp.concatenate([row[:, None] * inv, col[:, None] * inv], -1)
    cos, sin = jnp.cos(ang), jnp.sin(ang)
    return jnp.concatenate([cos, cos], -1), jnp.concatenate([-sin, sin], -1)


def _prep_ret_kernel(qlo_ref, qhi_ref, k_ref, v_ref, cos_ref, sin_ref, qo_ref, ko_ref, vo_ref, *, rope):
    def rot(t):
        if not rope:
            return t
        return t * cos_ref[...] + pltpu.roll(t, RET_DH // 2, axis=1) * sin_ref[...]

    half = RET_HEADS // 2
    for h in range(RET_HEADS):
        sl = slice(h * RET_DH, (h + 1) * RET_DH)
        q_ref, qs = (qlo_ref, sl) if h < half else (qhi_ref, slice((h - half) * RET_DH, (h - half + 1) * RET_DH))
        qo_ref[:, sl] = rot(q_ref[:, qs]).astype(qo_ref.dtype)
        ko_ref[:, sl] = rot(k_ref[:, sl] * (RET_DH ** -0.5)).astype(ko_ref.dtype)
    vo_ref[...] = v_ref[...].astype(vo_ref.dtype)


def _prep_ret(pa, pb, L, rope):
    tr = 256
    cos, sin = _rope_tables(L) if rope else (jnp.ones((L, LANES), _f32), jnp.zeros((L, LANES), _f32))
    hw = RET_W // 2
    qsp = lambda n: pl.BlockSpec((tr, hw), lambda i: (i, B_RQ // hw + n))
    sec = lambda c: pl.BlockSpec((tr, RET_W), lambda i: (i, c // RET_W))
    tab = pl.BlockSpec((tr, LANES), lambda i: (i, 0))
    out = pl.BlockSpec((tr, RET_W), lambda i: (i, 0))
    shp = jax.ShapeDtypeStruct((L, RET_W), MXU_DTYPE)
    return pl.pallas_call(
        functools.partial(_prep_ret_kernel, rope=rope),
        grid=(L // tr,),
        in_specs=[qsp(0), qsp(1), sec(O_RET_K), sec(O_RET_V), tab, tab],
        out_specs=[out, out, out],
        out_shape=[shp, shp, shp],
        compiler_params=_cparams("parallel"),
        name="prep_ret",
    )(pb, pb, pa, pa, cos, sin)


def _scan_ret_kernel(logit_ref, qi_ref, ki_ref, vi_ref, qj_ref, kj_ref, vj_ref, s0f_ref, s0b_ref,
                     ya_ref, yb_ref, finf_ref, finb_ref,
                     sf, sb, dmask, f_out, f_upd, f_all, b_out, b_upd, b_all):
    i = pl.program_id(0)
    c = CHUNK

    @pl.when(i == 0)
    def _():
        sf[...] = s0f_ref[...]
        sb[...] = s0b_ref[...]
        ii = lax.broadcasted_iota(jnp.int32, (c, c), 0).astype(_f32)
        jj = lax.broadcasted_iota(jnp.int32, (c, c), 1).astype(_f32)
        for h in range(RET_HEADS):
            def lg(d):
                x = logit_ref[d, h]
                v = -jnp.log1p(jnp.exp(-x))
                return jnp.broadcast_to(v[0:1, :], (c, c))
            lf, lb = lg(0), lg(1)
            dmask[h] = jnp.where(ii > jj, jnp.exp(lf * (ii - jj)),
                                 jnp.where(jj > ii, jnp.exp(lb * (jj - ii)), 2.0))
            f_out[h] = jnp.exp(lf * (ii + 1.0))
            f_upd[h] = jnp.exp(lf * (c - 1.0 - ii))
            f_all[h] = jnp.exp(lf * float(c))
            b_out[h] = jnp.exp(lb * (c - ii))
            b_upd[h] = jnp.exp(lb * ii)
            b_all[h] = jnp.exp(lb * float(c))

    tn = (((0,), (0,)), ((), ()))
    nt = (((1,), (1,)), ((), ()))
    for h in range(RET_HEADS):
        sl = slice(h * RET_DH, (h + 1) * RET_DH)
        q, k, v = qi_ref[:, sl], ki_ref[:, sl], vi_ref[:, sl]
        s = lax.dot_general(q, k, nt, preferred_element_type=_f32) * dmask[h]
        y = jnp.dot(s.astype(MXU_DTYPE), v, preferred_element_type=_f32)
        y += jnp.dot((q.astype(_f32) * f_out[h]).astype(MXU_DTYPE), sf[h].astype(MXU_DTYPE),
                     preferred_element_type=_f32)
        ya_ref[:, sl] = y
        sf[h] = f_all[h] * sf[h] + lax.dot_general(
            (k.astype(_f32) * f_upd[h]).astype(MXU_DTYPE), v, tn, preferred_element_type=_f32)
        q, k, v = qj_ref[:, sl], kj_ref[:, sl], vj_ref[:, sl]
        yb_ref[:, sl] = jnp.dot((q.astype(_f32) * b_out[h]).astype(MXU_DTYPE), sb[h].astype(MXU_DTYPE),
                                preferred_element_type=_f32)
        sb[h] = b_all[h] * sb[h] + lax.dot_general(
            (k.astype(_f32) * b_upd[h]).astype(MXU_DTYPE), v, tn, preferred_element_type=_f32)

    @pl.when(i == pl.num_programs(0) - 1)
    def _():
        finf_ref[...] = sf[...]
        finb_ref[...] = sb[...]


def _scan_ret(q, k, v, logit, s0f, s0b):
    L = q.shape[0]
    nc = L // CHUNK
    logit_b = jnp.broadcast_to(logit.astype(_f32)[:, :, None, None], (2, RET_HEADS, SUBLANES, LANES))
    fw = pl.BlockSpec((CHUNK, RET_W), lambda i: (i, 0))
    bw = pl.BlockSpec((CHUNK, RET_W), lambda i: (nc - 1 - i, 0))
    st = pl.BlockSpec((RET_HEADS, RET_DH, RET_DH), lambda i: (0, 0, 0))
    yshape = jax.ShapeDtypeStruct((L, RET_W), _f32)
    sshape = jax.ShapeDtypeStruct((RET_HEADS, RET_DH, RET_DH), _f32)
    tile = pltpu.VMEM((RET_HEADS, CHUNK, CHUNK), _f32)
    return pl.pallas_call(
        _scan_ret_kernel,
        grid=(nc,),
        in_specs=[pl.BlockSpec((2, RET_HEADS, SUBLANES, LANES), lambda i: (0, 0, 0, 0)),
                  fw, fw, fw, bw, bw, bw, st, st],
        out_specs=[fw, bw, st, st],
        out_shape=[yshape, yshape, sshape, sshape],
        scratch_shapes=[pltpu.VMEM((RET_HEADS, RET_DH, RET_DH), _f32)] * 2 + [tile] * 7,
        compiler_params=_cparams("arbitrary"),
        name="scan_ret",
    )(logit_b, q, k, v, q, k, v, s0f, s0b)


def _shift_rows(x, prev_row, next_row):
    r = x.shape[0]
    rid = lax.broadcasted_iota(jnp.int32, x.shape, 0)
    up = jnp.where(rid == 0, prev_row, pltpu.roll(x, 1, axis=0))
    dn = jnp.where(rid == r - 1, next_row, pltpu.roll(x, r - 1, axis=0))
    return up, dn


def _conv3(x_ref, prev_ref, next_ref, w_ref, b_ref, has_prev, has_next):
    x = x_ref[...]
    prev_row = prev_ref[SUBLANES - 1:SUBLANES, :] * has_prev
    next_row = next_ref[0:1, :] * has_next
    up, dn = _shift_rows(x, prev_row, next_row)
    return up * w_ref[0:1, :] + x * w_ref[1:2, :] + dn * w_ref[2:3, :] + b_ref[...]


def _halo_specs(tr, L, width, col):
    nb = tr // SUBLANES
    last = L // SUBLANES - 1
    cb = col // width
    return [pl.BlockSpec((tr, width), lambda i: (i, cb)),
            pl.BlockSpec((SUBLANES, width), lambda i: (jnp.maximum(i * nb - 1, 0), cb)),
            pl.BlockSpec((SUBLANES, width), lambda i: (jnp.minimum((i + 1) * nb, last), cb))]


def _prep_ssd_kernel(x_ref, xp_ref, xn_ref, b_ref, bp_ref, bn_ref, c_ref, cp_ref, cn_ref, dt_ref,
                     w_ref, cb_ref, dtb_ref, alog_ref, co_ref, bo_ref, xo_ref, pack_ref):
    i = pl.program_id(0)
    has_prev = (i > 0).astype(_f32)
    has_next = (i < pl.num_programs(0) - 1).astype(_f32)

    def conv(lo, hi, t_ref, p_ref, n_ref, o_ref):
        y = _conv3(t_ref, p_ref, n_ref, w_ref.at[:, lo:hi], cb_ref.at[:, lo:hi], has_prev, has_next)
        o_ref[...] = _silu(y).astype(o_ref.dtype)

    conv(0, SSM_W, x_ref, xp_ref, xn_ref, xo_ref)
    conv(SSM_W, SSM_W + SSM_GN, b_ref, bp_ref, bn_ref, bo_ref)
    conv(SSM_W + SSM_GN, SSM_W + 2 * SSM_GN, c_ref, cp_ref, cn_ref, co_ref)
    z = dt_ref[...] + dtb_ref[...]
    dt = jnp.maximum(z, 0.0) + jnp.log1p(jnp.exp(-jnp.abs(z)))
    a = dt * (-jnp.exp(alog_ref[...]))
    c = CHUNK
    ii = lax.broadcasted_iota(jnp.int32, (c, c), 0)
    jj = lax.broadcasted_iota(jnp.int32, (c, c), 1)
    lower = (jj <= ii).astype(_f32)
    upper = (jj >= ii).astype(_f32)
    lane = lax.broadcasted_iota(jnp.int32, (c, LANES), 1)
    dt_sh = pltpu.roll(dt, 2 * SSM_HEADS, axis=1)
    for n in range(x_ref.shape[0] // c):
        rs = slice(n * c, (n + 1) * c)
        pre = jnp.dot(lower, a[rs], precision=HIGHEST, preferred_element_type=_f32)
        suf = jnp.dot(upper, a[rs], precision=HIGHEST, preferred_element_type=_f32)
        pack_ref[rs, :] = jnp.where(lane < SSM_HEADS, pre,
                                    jnp.where(lane < 2 * SSM_HEADS, suf, dt_sh[rs]))


def _prep_ssd(pb, pdt, L, conv_w, conv_b, dt_bias, a_log):
    tr = 256
    w = jnp.pad(conv_w.astype(_f32), ((0, SUBLANES - 3), (0, 0)))
    lanes = lambda t: jnp.pad(t.astype(_f32).reshape(1, 2 * SSM_HEADS), ((0, 0), (0, LANES - 2 * SSM_HEADS)))
    wd = SSM_W + 2 * SSM_GN
    row = lambda c: pl.BlockSpec((tr, c), lambda i: (i, 0))
    return pl.pallas_call(
        _prep_ssd_kernel,
        grid=(L // tr,),
        in_specs=_halo_specs(tr, L, SSM_W, B_SX) + _halo_specs(tr, L, SSM_GN, B_SB)
        + _halo_specs(tr, L, SSM_GN, B_SC) + [
            row(LANES),
            pl.BlockSpec((SUBLANES, wd), lambda i: (0, 0)),
            pl.BlockSpec((1, wd), lambda i: (0, 0)),
            pl.BlockSpec((1, LANES), lambda i: (0, 0)),
            pl.BlockSpec((1, LANES), lambda i: (0, 0))],
        out_specs=[row(SSM_GN), row(SSM_GN), row(SSM_W), row(LANES)],
        out_shape=[jax.ShapeDtypeStruct((L, SSM_GN), MXU_DTYPE),
                   jax.ShapeDtypeStruct((L, SSM_GN), MXU_DTYPE),
                   jax.ShapeDtypeStruct((L, SSM_W), MXU_DTYPE),
                   jax.ShapeDtypeStruct((L, LANES), _f32)],
        compiler_params=_cparams("parallel"),
        name="prep_ssd",
    )(*([pb] * 9), pdt, w, conv_b.astype(_f32)[None], lanes(dt_bias), lanes(a_log))


def _scan_ssd_kernel(ci_ref, bi_ref, xi_ref, pi_ref, cj_ref, bj_ref, xj_ref, pj_ref, dskip_ref,
                     s0f_ref, s0b_ref, ya_ref, yb_ref, finf_ref, finb_ref, sf, sb):
    i = pl.program_id(0)
    c = CHUNK
    H = SSM_HEADS

    @pl.when(i == 0)
    def _():
        sf[...] = s0f_ref[...]
        sb[...] = s0b_ref[...]

    tn = (((0,), (0,)), ((), ()))
    nt = (((1,), (1,)), ((), ()))
    ii = lax.broadcasted_iota(jnp.int32, (c, c), 0)
    jj = lax.broadcasted_iota(jnp.int32, (c, c), 1)
    pi = pi_ref[...]
    pit = pi.T
    pj = pj_ref[...]
    for g in range(SSM_GROUPS):
        gs = slice(g * SSM_STATE, (g + 1) * SSM_STATE)
        ci, bi = ci_ref[:, gs], bi_ref[:, gs]
        cj, bj = cj_ref[:, gs], bj_ref[:, gs]
        cb = lax.dot_general(ci, bi, nt, preferred_element_type=_f32)
        ci32, bi32, cj32, bj32 = (t.astype(_f32) for t in (ci, bi, cj, bj))
        for hh in range(SSM_HPG):
            h = g * SSM_HPG + hh
            hs = slice(h * SSM_HEADDIM, (h + 1) * SSM_HEADDIM)
            col = lambda p, o: p[:, o + h:o + h + 1]
            row = lambda o: pit[o + h:o + h + 1, :]
            pre_c, suf_c = col(pi, 0), col(pi, H)
            mf = jnp.where(ii >= jj, jnp.exp(jnp.minimum(pre_c - row(0), 0.0)), 0.0) * row(2 * H)
            mb = jnp.where(jj >= ii, jnp.exp(jnp.minimum(suf_c - row(H), 0.0)), 0.0) * row(3 * H)
            x = xi_ref[:, hs]
            y = jnp.dot((cb * (mf + mb)).astype(MXU_DTYPE), x, preferred_element_type=_f32)
            y += jnp.dot((ci32 * jnp.exp(pre_c)).astype(MXU_DTYPE), sf[h].astype(MXU_DTYPE),
                         preferred_element_type=_f32)
            ya_ref[:, hs] = y + dskip_ref[:, hs] * x.astype(_f32)
            tot = pi[c - 1:c, h:h + 1]
            wgt = jnp.exp(tot - pre_c) * col(pi, 2 * H)
            sf[h] = jnp.exp(tot) * sf[h] + lax.dot_general(
                (bi32 * wgt).astype(MXU_DTYPE), x, tn, preferred_element_type=_f32)
            suf_c = col(pj, H)
            x = xj_ref[:, hs]
            yb_ref[:, hs] = jnp.dot((cj32 * jnp.exp(suf_c)).astype(MXU_DTYPE), sb[h].astype(MXU_DTYPE),
                                    preferred_element_type=_f32)
            tot = pj[0:1, H + h:H + h + 1]
            wgt = jnp.exp(tot - suf_c) * col(pj, 3 * H)
            sb[h] = jnp.exp(tot) * sb[h] + lax.dot_general(
                (bj32 * wgt).astype(MXU_DTYPE), x, tn, preferred_element_type=_f32)

    @pl.when(i == pl.num_programs(0) - 1)
    def _():
        finf_ref[...] = sf[...]
        finb_ref[...] = sb[...]


def _scan_ssd(cs, bs, xs, pack, d_skip, s0f, s0b):
    L = xs.shape[0]
    nc = L // CHUNK
    dvec = jnp.repeat(d_skip.astype(_f32), SSM_HEADDIM)[None]
    fw = lambda w: pl.BlockSpec((CHUNK, w), lambda i: (i, 0))
    bw = lambda w: pl.BlockSpec((CHUNK, w), lambda i: (nc - 1 - i, 0))
    st = pl.BlockSpec((SSM_HEADS, SSM_STATE, SSM_HEADDIM), lambda i: (0, 0, 0))
    yshape = jax.ShapeDtypeStruct((L, SSM_W), _f32)
    sshape = jax.ShapeDtypeStruct((SSM_HEADS, SSM_STATE, SSM_HEADDIM), _f32)
    return pl.pallas_call(
        _scan_ssd_kernel,
        grid=(nc,),
        in_specs=[fw(SSM_GN), fw(SSM_GN), fw(SSM_W), fw(LANES), bw(SSM_GN), bw(SSM_GN), bw(SSM_W), bw(LANES),
                  pl.BlockSpec((1, SSM_W), lambda i: (0, 0)), st, st],
        out_specs=[fw(SSM_W), bw(SSM_W), st, st],
        out_shape=[yshape, yshape, sshape, sshape],
        scratch_shapes=[pltpu.VMEM((SSM_HEADS, SSM_STATE, SSM_HEADDIM), _f32)] * 2,
        compiler_params=_cparams("arbitrary"),
        name="scan_ssd",
    )(cs, bs, xs, pack, cs, bs, xs, pack, dvec, s0f, s0b)


def _merge_kernel(ra_ref, rb_ref, sa_ref, sb_ref, gr_ref, gs_ref, nw_ref, yr_ref, ys_ref):
    for h in range(RET_HEADS):
        sl = slice(h * RET_DH, (h + 1) * RET_DH)
        y = _ln_rows(ra_ref[:, sl] + rb_ref[:, sl])
        yr_ref[:, sl] = (y * _silu(gr_ref[:, sl])).astype(yr_ref.dtype)
    gw = SSM_W // SSM_GROUPS
    for g in range(SSM_GROUPS):
        sl = slice(g * gw, (g + 1) * gw)
        y = (sa_ref[:, sl] + sb_ref[:, sl]) * _silu(gs_ref[:, sl])
        y = y * lax.rsqrt(jnp.mean(y * y, -1, keepdims=True) + LN_EPS)
        ys_ref[:, sl] = (y * nw_ref[:, sl]).astype(ys_ref.dtype)


def _gate_spec(tr, n):
    return pl.BlockSpec((tr, BR_W), lambda i: (i, B_GATE // BR_W + n))


def _merge(ra, rb, sa, sb_, pb, norm_w):
    L = ra.shape[0]
    tr = 256
    row = pl.BlockSpec((tr, BR_W), lambda i: (i, 0))
    shp = jax.ShapeDtypeStruct((L, BR_W), MXU_DTYPE)
    return pl.pallas_call(
        _merge_kernel,
        grid=(L // tr,),
        in_specs=[row, row, row, row, _gate_spec(tr, 1), _gate_spec(tr, 3),
                  pl.BlockSpec((1, BR_W), lambda i: (0, 0))],
        out_specs=[row, row],
        out_shape=[shp, shp],
        compiler_params=_cparams("parallel"),
        name="merge",
    )(ra, rb, sa, sb_, pb, pb, norm_w.astype(_f32)[None])


def _pool_kernel(x_ref, prev_ref, next_ref, g_ref, pw_ref, ps_ref, o_ref, *, L):
    i = pl.program_id(0)
    t = x_ref.shape[0]
    halo = SUBLANES
    has_prev = (i > 0).astype(_f32)
    has_next = (i < pl.num_programs(0) - 1).astype(_f32)
    pos = i * t + lax.broadcasted_iota(jnp.int32, (t, 1), 0)
    for g, win in enumerate(POOL_WINDOWS):
        sl = slice(g * POOL_GROUP, (g + 1) * POOL_GROUP)
        x = x_ref[:, sl]
        s = jnp.concatenate([prev_ref[:, sl] * has_prev, x, next_ref[:, sl] * has_next], axis=0)
        rows = t + 2 * halo
        width = 1
        while width < win:
            s = s + pltpu.roll(s, rows - width, axis=0)
            width *= 2
        off = halo - win // 2
        if off:
            s = pltpu.roll(s, rows - off, axis=0)
        cnt = jnp.minimum(pos + win // 2, L) - jnp.maximum(pos - win // 2, 0)
        d = s[:t] / cnt.astype(_f32) - x
        y = jnp.dot(d.astype(MXU_DTYPE), pw_ref[g], preferred_element_type=_f32)
        o_ref[:, sl] = (y * ps_ref[:, sl] * _silu(g_ref[:, sl])).astype(o_ref.dtype)


def _pool(pb, L, pool_w, pool_scale):
    tr = 256
    return pl.pallas_call(
        functools.partial(_pool_kernel, L=L),
        grid=(L // tr,),
        in_specs=_halo_specs(tr, L, POOL_W, B_POOL) + [
            _gate_spec(tr, 2),
            pl.BlockSpec((POOL_GROUPS, POOL_GROUP, POOL_GROUP), lambda i: (0, 0, 0)),
            pl.BlockSpec((1, POOL_W), lambda i: (0, 0))],
        out_specs=pl.BlockSpec((tr, POOL_W), lambda i: (i, 0)),
        out_shape=jax.ShapeDtypeStruct((L, POOL_W), MXU_DTYPE),
        compiler_params=_cparams("parallel"),
        name="pool",
    )(pb, pb, pb, pb, pool_w.astype(MXU_DTYPE), pool_scale.astype(_f32)[None])


def _prep_hy_kernel(v_ref, vp_ref, vn_ref, x0_ref, x0p_ref, x0n_ref, x1_ref, x1p_ref, x1n_ref,
                    g_ref, w_ref, b_ref, wo_ref, x0g_ref):
    i = pl.program_id(0)
    has_prev = (i > 0).astype(_f32)
    has_next = (i < pl.num_programs(0) - 1).astype(_f32)

    def conv(n, x_ref, p_ref, n_ref):
        sl = slice(n * HY_W, (n + 1) * HY_W)
        return _conv3(x_ref, p_ref, n_ref, w_ref.at[:, sl], b_ref.at[:, sl], has_prev, has_next)

    hv = conv(0, v_ref, vp_ref, vn_ref)
    hx0 = conv(1, x0_ref, x0p_ref, x0n_ref)
    hx1 = conv(2, x1_ref, x1p_ref, x1n_ref)
    wo_ref[...] = hx1 * hv
    x0g_ref[...] = hx0 * _silu(g_ref[...])


def _prep_hy(pb, L, conv_w, conv_b):
    tr = 256
    w = jnp.pad(conv_w.astype(_f32), ((0, SUBLANES - 3), (0, 0)))
    row = pl.BlockSpec((tr, HY_W), lambda i: (i, 0))
    shp = jax.ShapeDtypeStruct((L, HY_W), _f32)
    secs = sum((_halo_specs(tr, L, HY_W, B_HY + n * HY_W) for n in range(3)), [])
    return pl.pallas_call(
        _prep_hy_kernel,
        grid=(L // tr,),
        in_specs=secs + [_gate_spec(tr, 0),
                         pl.BlockSpec((SUBLANES, 3 * HY_W), lambda i: (0, 0)),
                         pl.BlockSpec((1, 3 * HY_W), lambda i: (0, 0))],
        out_specs=[row, row],
        out_shape=[shp, shp],
        compiler_params=_cparams("parallel"),
        name="prep_hy",
    )(*([pb] * 10), w, conv_b.astype(_f32)[None])


def _split(x):
    hi = x.astype(MXU_DTYPE)
    return hi, (x - hi.astype(_f32)).astype(MXU_DTYPE)


def _dot3(a_hi, a_lo, b):
    b_hi, b_lo = _split(b)
    d = lambda p, q: jnp.dot(p, q, preferred_element_type=_f32)
    return d(a_hi, b_hi) + (d(a_hi, b_lo) + d(a_lo, b_hi))


def _const_split(m):
    return _split(jnp.asarray(m, _f32))


def _filter_kernel(z_ref, w1_ref, b1_ref, w2_ref, b2_ref, w3hi_ref, w3lo_ref, freq_ref, delta_ref, o_ref, *, L):
    t = z_ref.shape[0]
    z = z_ref[...]
    dot = functools.partial(jnp.dot, precision=HIGHEST, preferred_element_type=_f32)
    freq = freq_ref[...]
    hdn = jnp.sin(freq * (dot(z, w1_ref[...]) + b1_ref[...]))
    hdn = jnp.sin(freq * (dot(hdn, w2_ref[...]) + b2_ref[...]))
    h_hi, h_lo = _split(hdn)
    d = lambda p, q: jnp.dot(p, q, preferred_element_type=_f32)
    filt = d(h_hi, w3hi_ref[...]) + (d(h_hi, w3lo_ref[...]) + d(h_lo, w3hi_ref[...]))
    n = pl.program_id(0) * t + lax.broadcasted_iota(jnp.int32, (t, 1), 0)
    o_ref[...] = jnp.where(n == L, 0.0, filt) * jnp.exp(-z[:, 0:1] * delta_ref[...])


def _hy_filter(L, lp):
    n = jnp.arange(2 * L)
    lag = jnp.minimum(jnp.where(n < L, n, 2 * L - n), L - 1).astype(_f32)[:, None]
    t = lag / (L - 1)
    w = 2.0 * math.pi * lag / L
    bands = jnp.linspace(1e-4, HY_BANDS - 1, HY_BANDS, dtype=_f32)[None, :]
    z = jnp.concatenate([t, jnp.cos(bands * w), -jnp.sin(bands * w)], axis=-1)
    emb = z.shape[1]
    z = jnp.pad(z, ((0, 0), (0, LANES - emb)))
    w1 = jnp.pad(lp['hy_w1'].astype(_f32), ((0, LANES - emb), (0, 0)))
    deltas = jnp.abs(jnp.linspace(HY_MIN_DECAY, HY_MAX_DECAY, HY_W, dtype=_f32))[None]
    tr = min(512, L)
    w3hi, w3lo = _split(lp['hy_w3'].astype(_f32))
    full = lambda a: pl.BlockSpec(a.shape, lambda i: (0,) * a.ndim)
    half = pl.BlockSpec((w3hi.shape[0], HY_W), lambda i: (0, i // (L // tr)))
    pre = [w1, lp['hy_b1'].astype(_f32)[None], lp['hy_w2'].astype(_f32), lp['hy_b2'].astype(_f32)[None]]
    post = [lp['hy_freq'].astype(_f32)[None], deltas]
    return pl.pallas_call(
        functools.partial(_filter_kernel, L=L),
        grid=(2 * L // tr,),
        in_specs=[pl.BlockSpec((tr, LANES), lambda i: (i, 0))] + [full(a) for a in pre] + [half, half]
        + [full(a) for a in post],
        out_specs=pl.BlockSpec((tr, HY_W), lambda i: (i, 0)),
        out_shape=jax.ShapeDtypeStruct((2 * L, HY_W), _f32),
        compiler_params=_cparams("parallel"),
        name="hy_filter",
    )(z, *pre, w3hi, w3lo, *post)


def _cs(num, den):
    ang = 2.0 * np.pi * (np.asarray(num, np.int64) % den) / den
    return np.cos(ang), np.sin(ang)


FFT_N2 = LANES


def _fft_rows(n1):
    return -(-(n1 // 2 + 1) // SUBLANES) * SUBLANES


def _fft_first_kernel(x_ref, mhi_ref, mlo_ref, o_ref):
    o_ref[...] = _dot3(mhi_ref[...], mlo_ref[...], x_ref[...])


def _fft_first(x2, n1):
    rows, cols = x2.shape
    kp = _fft_rows(n1)
    c, s = _cs(np.outer(np.arange(kp), np.arange(rows)), n1)
    mhi, mlo = _const_split(np.concatenate([c, -s], 0))
    tcol = 4096
    return pl.pallas_call(
        _fft_first_kernel,
        grid=(cols // tcol,),
        in_specs=[pl.BlockSpec((rows, tcol), lambda j: (0, j)),
                  pl.BlockSpec((2 * kp, rows), lambda j: (0, 0)),
                  pl.BlockSpec((2 * kp, rows), lambda j: (0, 0))],
        out_specs=pl.BlockSpec((2 * kp, tcol), lambda j: (0, j)),
        out_shape=jax.ShapeDtypeStruct((2 * kp, cols), _f32),
        compiler_params=_cparams("parallel"),
        name="fft_first",
    )(x2, mhi, mlo)


def _fft_mid_kernel(*refs, conv):
    if conv:
        a_ref, twr_ref, twi_ref, fhi_ref, flo_ref, h_ref, ghi_ref, glo_ref, o_ref = refs
    else:
        a_ref, twr_ref, twi_ref, fhi_ref, flo_ref, o_ref = refs
    n2 = FFT_N2
    reps = a_ref.shape[-1] // LANES
    twr = jnp.concatenate([twr_ref[0]] * reps, axis=1)
    twi = jnp.concatenate([twi_ref[0]] * reps, axis=1)
    ar, ai = a_ref[0, 0], a_ref[1, 0]
    x = _dot3(fhi_ref[...], flo_ref[...],
              jnp.concatenate([ar * twr - ai * twi, ar * twi + ai * twr], axis=0))
    if not conv:
        o_ref[0, 0] = x[:n2]
        o_ref[1, 0] = x[n2:]
        return
    xr, xi = x[:n2], x[n2:]
    hr, hi = h_ref[0, 0], h_ref[1, 0]
    b = _dot3(ghi_ref[...], glo_ref[...],
              jnp.concatenate([xr * hr - xi * hi, xr * hi + xi * hr], axis=0))
    br, bi = b[:n2], b[n2:]
    o_ref[0, 0] = br * twr + bi * twi
    o_ref[1, 0] = bi * twr - br * twi


def _fft_mid(a, n1, hf=None):
    n2 = FFT_N2
    kp, ch = a.shape[1], a.shape[-1]
    n = n1 * n2
    idx = jnp.arange(kp)[:, None] * jnp.arange(n2)[None, :]
    ang = (2.0 * math.pi / n) * (idx % n).astype(_f32)
    twr = jnp.broadcast_to(jnp.cos(ang)[:, :, None], (kp, n2, LANES))
    twi = jnp.broadcast_to(-jnp.sin(ang)[:, :, None], (kp, n2, LANES))
    c, s = _cs(np.outer(np.arange(n2), np.arange(n2)), n2)
    fhi, flo = _const_split(np.block([[c, s], [-s, c]]))
    blk = pl.BlockSpec((2, 1, n2, ch), lambda k: (0, k, 0, 0))
    tw = pl.BlockSpec((1, n2, LANES), lambda k: (k, 0, 0))
    mat = pl.BlockSpec((2 * n2, 2 * n2), lambda k: (0, 0))
    args, specs = [a, twr, twi, fhi, flo], [blk, tw, tw, mat, mat]
    if hf is not None:
        ghi, glo = _const_split(np.block([[c, -s], [s, c]]))
        args += [hf, ghi, glo]
        specs += [blk, mat, mat]
    return pl.pallas_call(
        functools.partial(_fft_mid_kernel, conv=hf is not None),
        grid=(kp,),
        in_specs=specs,
        out_specs=blk,
        out_shape=jax.ShapeDtypeStruct(a.shape, _f32),
        compiler_params=_cparams("parallel"),
        name="fft_mid",
    )(*args)


def _fft_last_kernel(c_ref, mhi_ref, mlo_ref, w_ref, x0g_ref, bias_ref, o_ref):
    y = _dot3(mhi_ref[...], mlo_ref[...], c_ref[...])
    o_ref[...] = (x0g_ref[...] * (y + w_ref[...] * bias_ref[...])).astype(o_ref.dtype)


def _fft_last(c2, n1, w2, x0g2, bias):
    rows, cols = w2.shape
    kp = _fft_rows(n1)
    n = n1 * FFT_N2
    c, s = _cs(np.outer(np.arange(rows), np.arange(kp)), n1)
    k1 = np.arange(kp)
    mult = np.where((k1 == 0) | (k1 == n1 // 2), 1.0, np.where(k1 < n1 // 2, 2.0, 0.0))
    mhi, mlo = _const_split(np.concatenate([c * mult, -s * mult], 1) / n)
    tcol = 4096
    bias_t = jnp.tile(bias.astype(_f32), tcol // bias.shape[0])[None]
    blk = pl.BlockSpec((rows, tcol), lambda j: (0, j))
    return pl.pallas_call(
        _fft_last_kernel,
        grid=(cols // tcol,),
        in_specs=[pl.BlockSpec((2 * kp, tcol), lambda j: (0, j)),
                  pl.BlockSpec((rows, 2 * kp), lambda j: (0, 0)),
                  pl.BlockSpec((rows, 2 * kp), lambda j: (0, 0)),
                  blk, blk, pl.BlockSpec((1, tcol), lambda j: (0, 0))],
        out_specs=blk,
        out_shape=jax.ShapeDtypeStruct((rows, cols), MXU_DTYPE),
        compiler_params=_cparams("parallel"),
        name="fft_last",
    )(c2, mhi, mlo, w2, x0g2, bias_t)


def _hy_small_kernel(w_ref, buf_ref, x0g_ref, bias_ref, fwhi_ref, fwlo_ref, fbhi_ref, fblo_ref,
                     ihi_ref, ilo_ref, o_ref):
    n = buf_ref.shape[0]
    w = w_ref[...]
    wf = _dot3(fwhi_ref[...], fwlo_ref[...], w)
    hf = _dot3(fbhi_ref[...], fblo_ref[...], buf_ref[...])
    wr, wi, hr, hi = wf[:n], wf[n:], hf[:n], hf[n:]
    y = _dot3(ihi_ref[...], ilo_ref[...], jnp.concatenate([wr * hr - wi * hi, wr * hi + wi * hr], axis=0))
    o_ref[...] = (x0g_ref[...] * (y + w * bias_ref[...])).astype(o_ref.dtype)


def _hy_conv_small(w, buf, x0g, bias):
    L, ch = w.shape
    n = 2 * L
    tc = 256
    c, s = _cs(np.outer(np.arange(n), np.arange(n)), n)
    fb = np.concatenate([c, -s], 0)
    mats = [*_const_split(fb[:, :L]), *_const_split(fb),
            *_const_split(np.concatenate([c[:L], -s[:L]], 1) / n)]
    col = lambda r: pl.BlockSpec((r, tc), lambda j: (0, j))
    return pl.pallas_call(
        _hy_small_kernel,
        grid=(ch // tc,),
        in_specs=[col(L), col(n), col(L), col(1)] + [pl.BlockSpec(m.shape, lambda j: (0, 0)) for m in mats],
        out_specs=col(L),
        out_shape=jax.ShapeDtypeStruct((L, ch), MXU_DTYPE),
        compiler_params=_cparams("parallel"),
        name="hy_conv_small",
    )(w, buf, x0g, bias.astype(_f32)[None], *mats)


def _hy_conv(w, buf, x0g, bias):
    L, ch = w.shape
    if L < 512:
        return _hy_conv_small(w, buf, x0g, bias)
    n1 = 2 * L // FFT_N2
    kp = _fft_rows(n1)
    cols = FFT_N2 * ch
    hf = _fft_mid(_fft_first(buf.reshape(n1, cols), n1).reshape(2, kp, FFT_N2, ch), n1)
    a = _fft_first(w.reshape(n1 // 2, cols), n1).reshape(2, kp, FFT_N2, ch)
    cc = _fft_mid(a, n1, hf)
    y = _fft_last(cc.reshape(2 * kp, cols), n1, w.reshape(n1 // 2, cols), x0g.reshape(n1 // 2, cols), bias)
    return y.reshape(L, ch)


def _zero_states():
    return (jnp.zeros((RET_HEADS, RET_DH, RET_DH), _f32), jnp.zeros((RET_HEADS, RET_DH, RET_DH), _f32),
            jnp.zeros((SSM_HEADS, SSM_STATE, SSM_HEADDIM), _f32),
            jnp.zeros((SSM_HEADS, SSM_STATE, SSM_HEADDIM), _f32))


def _recurrent(proj, L, lp, states, latent):
    pa, pb, pdt = proj
    q, k, v = _prep_ret(pa, pb, L, latent)
    ra, rb, ret_f, ret_b = _scan_ret(q, k, v, lp['ret_decay_logit'], states[0], states[1])
    cs, bs, xs, pack = _prep_ssd(pb, pdt, L, lp['conv_ssm_w'], lp['conv_ssm_b'], lp['ssm_dt_bias'],
                                 lp['ssm_A_log'])
    sa, sb_, ssm_f, ssm_b = _scan_ssd(cs, bs, xs, pack, lp['ssm_D'], states[2], states[3])
    return (ra, rb, sa, sb_), (ret_f, ret_b, ssm_f, ssm_b)


def _mix(h, mod, lp, states, latent):
    L = h.shape[0]
    proj = _in_proj(h, mod[0], mod[1], lp)
    pb = proj[1]
    (ra, rb, sa, sb_), fin = _recurrent(proj, L, lp, states, latent)
    y_ret, y_ssm = _merge(ra, rb, sa, sb_, pb, lp['ssm_norm_w'])
    w, x0g = _prep_hy(pb, L, lp['conv_hy_w'], lp['conv_hy_b'])
    y_hy = _hy_conv(w, _hy_filter(L, lp), x0g, lp['hy_bias'])
    y_pool = _pool(pb, L, lp['pool_w'], lp['pool_scale'])
    out = _out_proj([y_hy, y_ret, y_pool, y_ssm], lp['w_out'], h, mod[2], lp['ln_g'], lp['ln_b'])
    return out, fin


def _context_states(hc, mod, lp):
    proj = _in_proj(hc, mod[0], mod[1], lp)
    _, fin = _recurrent(proj, hc.shape[0], lp, _zero_states(), False)
    return fin


def kernel(x, c, ctx, c_ctx, w_mod, b_mod, w_in, conv_ssm_w, conv_ssm_b, conv_hy_w, conv_hy_b,
           ret_decay_logit, ssm_A_log, ssm_dt_bias, ssm_D, ssm_norm_w, hy_w1, hy_b1, hy_w2, hy_b2,
           hy_w3, hy_freq, hy_bias, pool_w, pool_scale, w_out, ln_g, ln_b):
    assert x.shape[0] == 1
    h, hc = x[0], ctx[0]
    w_a, w_b, w_dt = _prep_w_in(w_in)
    w_o = _cast_w_out(w_out)
    mods = _adaln(jnp.concatenate([c, c_ctx[None]], axis=0), w_mod, b_mod)
    for l in range(DEPTH):
        lp = {
            'w_a': w_a[l], 'w_b': w_b[l], 'w_dt': w_dt[l], 'w_out': w_o[l],
            'conv_ssm_w': conv_ssm_w[l], 'conv_ssm_b': conv_ssm_b[l],
            'conv_hy_w': conv_hy_w[l], 'conv_hy_b': conv_hy_b[l], 'ret_decay_logit': ret_decay_logit[l],
            'ssm_A_log': ssm_A_log[l], 'ssm_dt_bias': ssm_dt_bias[l], 'ssm_D': ssm_D[l],
            'ssm_norm_w': ssm_norm_w[l], 'hy_w1': hy_w1[l], 'hy_b1': hy_b1[l], 'hy_w2': hy_w2[l],
            'hy_b2': hy_b2[l], 'hy_w3': hy_w3[l], 'hy_freq': hy_freq[l], 'hy_bias': hy_bias[l],
            'pool_w': pool_w[l], 'pool_scale': pool_scale[l], 'ln_g': ln_g[l], 'ln_b': ln_b[l],
        }
        mod = lambda r: tuple(mods[l, r:r + 1, n * D_MODEL:(n + 1) * D_MODEL] for n in range(3))
        if l < DEPTH - 1:
            hc_next, states = _mix(hc, mod(1), lp, _zero_states(), False)
        else:
            states = _context_states(hc, mod(1), lp)
            hc_next = hc
        h, _ = _mix(h, mod(0), lp, states, True)
        hc = hc_next
    return h[None]
```

```python
import functools
import math

import jax
import jax.numpy as jnp
import numpy as np
from jax import lax
from jax.experimental import pallas as pl
from jax.experimental.pallas import tpu as pltpu

D_MODEL = 4096
DEPTH = 2
GRID_W = 64
MIX_W = D_MODEL
BR_W = MIX_W // 4
HY_W = RET_W = POOL_W = SSM_W = BR_W
RET_HEADS = 8
RET_DH = RET_W // RET_HEADS
ROPE_BASE = 10000.0
SSM_HEADDIM = 64
SSM_HEADS = SSM_W // SSM_HEADDIM
SSM_GROUPS = 4
SSM_HPG = SSM_HEADS // SSM_GROUPS
SSM_STATE = 128
SSM_GN = SSM_GROUPS * SSM_STATE
CHUNK = 128
POOL_WINDOWS = (2, 4, 8, 16)
POOL_GROUPS = len(POOL_WINDOWS)
POOL_GROUP = POOL_W // POOL_GROUPS
HY_BANDS = 16
HY_TARGET = 1e-2
HY_FAST = 0.3
HY_SLOW = 1.5
HY_MIN_DECAY = math.log(HY_TARGET) / HY_SLOW
HY_MAX_DECAY = math.log(HY_TARGET) / HY_FAST
ALPHA = (2.0 * DEPTH) ** 0.25
LN_EPS = 1e-5

O_RET_K = 0
O_RET_V = O_RET_K + RET_W
O_SSM_DT = O_RET_V + RET_W
O_SSM_X = O_SSM_DT + 2 * SSM_HEADS
O_SSM_B = O_SSM_X + SSM_W
O_RET_Q = O_SSM_B + SSM_GN
O_SSM_C = O_RET_Q + RET_W
O_HY = O_SSM_C + SSM_GN
O_POOL = O_HY + 3 * HY_W
O_GATE = O_POOL + POOL_W
N_IN = O_GATE + MIX_W

LANES = 128
SUBLANES = 8
N_A = O_SSM_DT + LANES
B_SX = 0
B_SB = O_SSM_B - O_SSM_X
B_RQ = O_RET_Q - O_SSM_X
B_SC = O_SSM_C - O_SSM_X
B_HY = O_HY - O_SSM_X
B_POOL = O_POOL - O_SSM_X
B_GATE = O_GATE - O_SSM_X
N_B = N_IN - O_SSM_X

VMEM_LIMIT_BYTES = 56 * 1024 * 1024
MXU_DTYPE = jnp.bfloat16
HIGHEST = lax.Precision.HIGHEST

_f32 = jnp.float32


def _cparams(*sem, vmem=VMEM_LIMIT_BYTES):
    return pltpu.CompilerParams(dimension_semantics=sem, vmem_limit_bytes=vmem)


def _silu(x):
    return x * jax.nn.sigmoid(x)


def _ln_rows(z):
    mu = jnp.mean(z, -1, keepdims=True)
    zc = z - mu
    var = jnp.mean(zc * zc, -1, keepdims=True)
    return zc * lax.rsqrt(var + LN_EPS)


def _cast_kernel(w_ref, o_ref):
    o_ref[...] = w_ref[...].astype(o_ref.dtype)


def _cast_layer(w, l, tr):
    _, r, c = w.shape
    return pl.pallas_call(
        _cast_kernel,
        grid=(r // tr,),
        in_specs=[pl.BlockSpec((1, tr, c), lambda i: (l, i, 0))],
        out_specs=pl.BlockSpec((1, tr, c), lambda i: (0, i, 0)),
        out_shape=jax.ShapeDtypeStruct((1, r, c), MXU_DTYPE),
        compiler_params=_cparams("parallel"),
        name="cast_layer",
    )(w)[0]


def _adaln_kernel(c_ref, w_ref, b_ref, o_ref):
    tn = w_ref.shape[-1]
    outs = []
    for m in range(c_ref.shape[0]):
        x = _silu(c_ref[m])
        cols = [jnp.sum(x * w_ref[0, :, j * LANES:(j + 1) * LANES], axis=0, keepdims=True)
                for j in range(tn // LANES)]
        outs.append(jnp.concatenate(cols, axis=1))
    outs.append(jnp.zeros((SUBLANES - len(outs), tn), _f32))
    o_ref[0] = jnp.concatenate(outs, axis=0) + b_ref[0]


def _adaln(c_rows, w_mod, b_mod):
    dep, k, n = w_mod.shape
    r = c_rows.shape[0]
    tn = 512
    cb = jnp.broadcast_to(c_rows.astype(_f32)[:, :, None], (r, k, LANES))
    return pl.pallas_call(
        _adaln_kernel,
        grid=(dep, n // tn),
        in_specs=[pl.BlockSpec((r, k, LANES), lambda l, j: (0, 0, 0)),
                  pl.BlockSpec((1, k, tn), lambda l, j: (l, 0, j)),
                  pl.BlockSpec((1, 1, tn), lambda l, j: (l, 0, j))],
        out_specs=pl.BlockSpec((1, SUBLANES, tn), lambda l, j: (l, 0, j)),
        out_shape=jax.ShapeDtypeStruct((dep, SUBLANES, n), _f32),
        compiler_params=_cparams("parallel", "parallel"),
        name="adaln",
    )(cb, w_mod, b_mod[:, None, :])


def _matmul_nt_kernel(a_ref, b_ref, o_ref):
    o_ref[...] = lax.dot_general(a_ref[...], b_ref[...], (((1,), (1,)), ((), ())), preferred_element_type=_f32)


def _matmul_nt(a, wt, row0, n, tm, tn):
    m, k = a.shape
    assert m % tm == 0 and n % tn == 0 and row0 % 32 == 0 and tn % 32 == 0
    return pl.pallas_call(
        _matmul_nt_kernel,
        grid=(m // tm, n // tn),
        in_specs=[pl.BlockSpec((tm, k), lambda i, j: (i, 0)),
                  pl.BlockSpec((pl.Element(tn), pl.Element(k)),
                               lambda i, j: (pl.multiple_of(row0 + j * tn, 32), 0))],
        out_specs=pl.BlockSpec((tm, tn), lambda i, j: (i, j)),
        out_shape=jax.ShapeDtypeStruct((m, n), _f32),
        compiler_params=_cparams("parallel", "parallel"),
        name="matmul_nt",
    )(a, wt)


def _modulate_kernel(h_ref, shift_ref, scale_ref, o_ref):
    o_ref[...] = (_ln_rows(h_ref[...]) * (1.0 + scale_ref[...]) + shift_ref[...]).astype(o_ref.dtype)


def _modulate(h, shift, scale):
    L, d = h.shape
    tr = 256
    vec = pl.BlockSpec((1, d), lambda i: (0, 0))
    return pl.pallas_call(
        _modulate_kernel,
        grid=(L // tr,),
        in_specs=[pl.BlockSpec((tr, d), lambda i: (i, 0)), vec, vec],
        out_specs=pl.BlockSpec((tr, d), lambda i: (i, 0)),
        out_shape=jax.ShapeDtypeStruct((L, d), MXU_DTYPE),
        compiler_params=_cparams("parallel"),
        name="modulate",
    )(h, shift, scale)


def _in_proj(h, shift, scale, lp):
    L = h.shape[0]
    tm = 512 if L % 512 == 0 else 256
    u = _modulate(h, shift, scale)
    wt = lp['w_in_t']
    return _matmul_nt(u, wt, 0, N_A, tm, N_A), _matmul_nt(u, wt, O_SSM_X, N_B, tm, 1024)


def _out_proj_kernel(y0_ref, y1_ref, y2_ref, y3_ref, w_ref, h_ref, gate_ref, g_ref, b_ref, o_ref):
    out = None
    for n, y_ref in enumerate((y0_ref, y1_ref, y2_ref, y3_ref)):
        d = jnp.dot(y_ref[...], w_ref[n * BR_W:(n + 1) * BR_W, :], preferred_element_type=_f32)
        out = d if out is None else out + d
    z = ALPHA * h_ref[...] + gate_ref[...] * out
    o_ref[...] = _ln_rows(z) * g_ref[...] + b_ref[...]


OUT_PROJ_VMEM_BYTES = 60 * 1024 * 1024


def _out_proj(ys, w, h, gate, g, b):
    L, d = h.shape
    tm = 256
    lhs = pl.BlockSpec((tm, BR_W), lambda i: (i, 0))
    vec = pl.BlockSpec((1, d), lambda i: (0, 0))
    row = pl.BlockSpec((tm, d), lambda i: (i, 0))
    return pl.pallas_call(
        _out_proj_kernel,
        grid=(L // tm,),
        in_specs=[lhs] * len(ys) + [
            pl.BlockSpec(w.shape, lambda i: (0, 0), pipeline_mode=pl.Buffered(1)), row, vec, vec, vec],
        out_specs=row,
        out_shape=jax.ShapeDtypeStruct((L, d), _f32),
        compiler_params=_cparams("parallel", vmem=OUT_PROJ_VMEM_BYTES),
        name="out_proj",
    )(*ys, w, h, gate, g[None], b[None])


def _rope_tables(L):
    rows = L // GRID_W
    row = jnp.repeat(jnp.arange(rows), GRID_W).astype(_f32)
    col = jnp.tile(jnp.arange(GRID_W), rows).astype(_f32)
    nq = RET_DH // 4
    inv = ROPE_BASE ** (-jnp.arange(nq, dtype=_f32) / nq)
    ang = jnp.concatenate([row[:, None] * inv, col[:, None] * inv], -1)
    cos, sin = jnp.cos(ang), jnp.sin(ang)
    return jnp.concatenate([cos, cos], -1), jnp.concatenate([-sin, sin], -1)


def _prep_ret_kernel(qlo_ref, qhi_ref, k_ref, v_ref, cos_ref, sin_ref, qo_ref, ko_ref, vo_ref, *, rope):
    def rot(t):
        if not rope:
            return t
        return t * cos_ref[...] + pltpu.roll(t, RET_DH // 2, axis=1) * sin_ref[...]

    half = RET_HEADS // 2
    for h in range(RET_HEADS):
        sl = slice(h * RET_DH, (h + 1) * RET_DH)
        q_ref, qs = (qlo_ref, sl) if h < half else (qhi_ref, slice((h - half) * RET_DH, (h - half + 1) * RET_DH))
        qo_ref[:, sl] = rot(q_ref[:, qs]).astype(qo_ref.dtype)
        ko_ref[:, sl] = rot(k_ref[:, sl] * (RET_DH ** -0.5)).astype(ko_ref.dtype)
    vo_ref[...] = v_ref[...].astype(vo_ref.dtype)


def _prep_ret(pa, pb, L, rope):
    tr = 256
    cos, sin = _rope_tables(L) if rope else (jnp.ones((L, LANES), _f32), jnp.zeros((L, LANES), _f32))
    hw = RET_W // 2
    qsp = lambda n: pl.BlockSpec((tr, hw), lambda i: (i, B_RQ // hw + n))
    sec = lambda c: pl.BlockSpec((tr, RET_W), lambda i: (i, c // RET_W))
    tab = pl.BlockSpec((tr, LANES), lambda i: (i, 0))
    out = pl.BlockSpec((tr, RET_W), lambda i: (i, 0))
    shp = jax.ShapeDtypeStruct((L, RET_W), MXU_DTYPE)
    return pl.pallas_call(
        functools.partial(_prep_ret_kernel, rope=rope),
        grid=(L // tr,),
        in_specs=[qsp(0), qsp(1), sec(O_RET_K), sec(O_RET_V), tab, tab],
        out_specs=[out, out, out],
        out_shape=[shp, shp, shp],
        compiler_params=_cparams("parallel"),
        name="prep_ret",
    )(pb, pb, pa, pa, cos, sin)


def _scan_ret_kernel(logit_ref, qi_ref, ki_ref, vi_ref, qj_ref, kj_ref, vj_ref, s0f_ref, s0b_ref,
                     ya_ref, yb_ref, finf_ref, finb_ref,
                     sf, sb, dmask, f_out, f_upd, f_all, b_out, b_upd, b_all):
    i = pl.program_id(0)
    c = CHUNK

    @pl.when(i == 0)
    def _():
        sf[...] = s0f_ref[...]
        sb[...] = s0b_ref[...]
        ii = lax.broadcasted_iota(jnp.int32, (c, c), 0).astype(_f32)
        jj = lax.broadcasted_iota(jnp.int32, (c, c), 1).astype(_f32)
        for h in range(RET_HEADS):
            def lg(d):
                x = logit_ref[d, h]
                v = -jnp.log1p(jnp.exp(-x))
                return jnp.broadcast_to(v[0:1, :], (c, c))
            lf, lb = lg(0), lg(1)
            dmask[h] = jnp.where(ii > jj, jnp.exp(lf * (ii - jj)),
                                 jnp.where(jj > ii, jnp.exp(lb * (jj - ii)), 2.0))
            f_out[h] = jnp.exp(lf * (ii + 1.0))
            f_upd[h] = jnp.exp(lf * (c - 1.0 - ii))
            f_all[h] = jnp.exp(lf * float(c))
            b_out[h] = jnp.exp(lb * (c - ii))
            b_upd[h] = jnp.exp(lb * ii)
            b_all[h] = jnp.exp(lb * float(c))

    tn = (((0,), (0,)), ((), ()))
    nt = (((1,), (1,)), ((), ()))
    for h in range(RET_HEADS):
        sl = slice(h * RET_DH, (h + 1) * RET_DH)
        q, k, v = qi_ref[:, sl], ki_ref[:, sl], vi_ref[:, sl]
        s = lax.dot_general(q, k, nt, preferred_element_type=_f32) * dmask[h]
        y = jnp.dot(s.astype(MXU_DTYPE), v, preferred_element_type=_f32)
        y += jnp.dot((q.astype(_f32) * f_out[h]).astype(MXU_DTYPE), sf[h].astype(MXU_DTYPE),
                     preferred_element_type=_f32)
        ya_ref[:, sl] = y
        sf[h] = f_all[h] * sf[h] + lax.dot_general(
            (k.astype(_f32) * f_upd[h]).astype(MXU_DTYPE), v, tn, preferred_element_type=_f32)
        q, k, v = qj_ref[:, sl], kj_ref[:, sl], vj_ref[:, sl]
        yb_ref[:, sl] = jnp.dot((q.astype(_f32) * b_out[h]).astype(MXU_DTYPE), sb[h].astype(MXU_DTYPE),
                                preferred_element_type=_f32)
        sb[h] = b_all[h] * sb[h] + lax.dot_general(
            (k.astype(_f32) * b_upd[h]).astype(MXU_DTYPE), v, tn, preferred_element_type=_f32)

    @pl.when(i == pl.num_programs(0) - 1)
    def _():
        finf_ref[...] = sf[...]
        finb_ref[...] = sb[...]


def _scan_ret(q, k, v, logit, s0f, s0b):
    L = q.shape[0]
    nc = L // CHUNK
    logit_b = jnp.broadcast_to(logit.astype(_f32)[:, :, None, None], (2, RET_HEADS, SUBLANES, LANES))
    fw = pl.BlockSpec((CHUNK, RET_W), lambda i: (i, 0))
    bw = pl.BlockSpec((CHUNK, RET_W), lambda i: (nc - 1 - i, 0))
    st = pl.BlockSpec((RET_HEADS, RET_DH, RET_DH), lambda i: (0, 0, 0))
    yshape = jax.ShapeDtypeStruct((L, RET_W), _f32)
    sshape = jax.ShapeDtypeStruct((RET_HEADS, RET_DH, RET_DH), _f32)
    tile = pltpu.VMEM((RET_HEADS, CHUNK, CHUNK), _f32)
    return pl.pallas_call(
        _scan_ret_kernel,
        grid=(nc,),
        in_specs=[pl.BlockSpec((2, RET_HEADS, SUBLANES, LANES), lambda i: (0, 0, 0, 0)),
                  fw, fw, fw, bw, bw, bw, st, st],
        out_specs=[fw, bw, st, st],
        out_shape=[yshape, yshape, sshape, sshape],
        scratch_shapes=[pltpu.VMEM((RET_HEADS, RET_DH, RET_DH), _f32)] * 2 + [tile] * 7,
        compiler_params=_cparams("arbitrary"),
        name="scan_ret",
    )(logit_b, q, k, v, q, k, v, s0f, s0b)


def _shift_rows(x, prev_row, next_row):
    r = x.shape[0]
    rid = lax.broadcasted_iota(jnp.int32, x.shape, 0)
    up = jnp.where(rid == 0, prev_row, pltpu.roll(x, 1, axis=0))
    dn = jnp.where(rid == r - 1, next_row, pltpu.roll(x, r - 1, axis=0))
    return up, dn


def _conv3(x_ref, prev_ref, next_ref, w_ref, b_ref, has_prev, has_next):
    x = x_ref[...]
    prev_row = prev_ref[SUBLANES - 1:SUBLANES, :] * has_prev
    next_row = next_ref[0:1, :] * has_next
    up, dn = _shift_rows(x, prev_row, next_row)
    return up * w_ref[0:1, :] + x * w_ref[1:2, :] + dn * w_ref[2:3, :] + b_ref[...]


def _halo_specs(tr, L, width, col):
    nb = tr // SUBLANES
    last = L // SUBLANES - 1
    cb = col // width
    return [pl.BlockSpec((tr, width), lambda i: (i, cb)),
            pl.BlockSpec((SUBLANES, width), lambda i: (jnp.maximum(i * nb - 1, 0), cb)),
            pl.BlockSpec((SUBLANES, width), lambda i: (jnp.minimum((i + 1) * nb, last), cb))]


def _prep_ssd_kernel(x_ref, xp_ref, xn_ref, b_ref, bp_ref, bn_ref, c_ref, cp_ref, cn_ref, dt_ref,
                     w_ref, cb_ref, dtb_ref, alog_ref, co_ref, bo_ref, xo_ref, pack_ref):
    i = pl.program_id(0)
    has_prev = (i > 0).astype(_f32)
    has_next = (i < pl.num_programs(0) - 1).astype(_f32)

    def conv(lo, hi, t_ref, p_ref, n_ref, o_ref):
        y = _conv3(t_ref, p_ref, n_ref, w_ref.at[:, lo:hi], cb_ref.at[:, lo:hi], has_prev, has_next)
        o_ref[...] = _silu(y).astype(o_ref.dtype)

    conv(0, SSM_W, x_ref, xp_ref, xn_ref, xo_ref)
    conv(SSM_W, SSM_W + SSM_GN, b_ref, bp_ref, bn_ref, bo_ref)
    conv(SSM_W + SSM_GN, SSM_W + 2 * SSM_GN, c_ref, cp_ref, cn_ref, co_ref)
    z = dt_ref[...] + dtb_ref[...]
    dt = jnp.maximum(z, 0.0) + jnp.log1p(jnp.exp(-jnp.abs(z)))
    a = dt * (-jnp.exp(alog_ref[...]))
    c = CHUNK
    ii = lax.broadcasted_iota(jnp.int32, (c, c), 0)
    jj = lax.broadcasted_iota(jnp.int32, (c, c), 1)
    lower = (jj <= ii).astype(_f32)
    upper = (jj >= ii).astype(_f32)
    lane = lax.broadcasted_iota(jnp.int32, (c, LANES), 1)
    dt_sh = pltpu.roll(dt, 2 * SSM_HEADS, axis=1)
    for n in range(x_ref.shape[0] // c):
        rs = slice(n * c, (n + 1) * c)
        pre = jnp.dot(lower, a[rs], precision=HIGHEST, preferred_element_type=_f32)
        suf = jnp.dot(upper, a[rs], precision=HIGHEST, preferred_element_type=_f32)
        pack_ref[rs, :] = jnp.where(lane < SSM_HEADS, pre,
                                    jnp.where(lane < 2 * SSM_HEADS, suf, dt_sh[rs]))


def _prep_ssd(pa, pb, L, conv_w, conv_b, dt_bias, a_log):
    tr = 256
    w = jnp.pad(conv_w.astype(_f32), ((0, SUBLANES - 3), (0, 0)))
    lanes = lambda t: jnp.pad(t.astype(_f32).reshape(1, 2 * SSM_HEADS), ((0, 0), (0, LANES - 2 * SSM_HEADS)))
    wd = SSM_W + 2 * SSM_GN
    row = lambda c: pl.BlockSpec((tr, c), lambda i: (i, 0))
    return pl.pallas_call(
        _prep_ssd_kernel,
        grid=(L // tr,),
        in_specs=_halo_specs(tr, L, SSM_W, B_SX) + _halo_specs(tr, L, SSM_GN, B_SB)
        + _halo_specs(tr, L, SSM_GN, B_SC) + [
            pl.BlockSpec((tr, LANES), lambda i: (i, O_SSM_DT // LANES)),
            pl.BlockSpec((SUBLANES, wd), lambda i: (0, 0)),
            pl.BlockSpec((1, wd), lambda i: (0, 0)),
            pl.BlockSpec((1, LANES), lambda i: (0, 0)),
            pl.BlockSpec((1, LANES), lambda i: (0, 0))],
        out_specs=[row(SSM_GN), row(SSM_GN), row(SSM_W), row(LANES)],
        out_shape=[jax.ShapeDtypeStruct((L, SSM_GN), MXU_DTYPE),
                   jax.ShapeDtypeStruct((L, SSM_GN), MXU_DTYPE),
                   jax.ShapeDtypeStruct((L, SSM_W), MXU_DTYPE),
                   jax.ShapeDtypeStruct((L, LANES), _f32)],
        compiler_params=_cparams("parallel"),
        name="prep_ssd",
    )(*([pb] * 9), pa, w, conv_b.astype(_f32)[None], lanes(dt_bias), lanes(a_log))


def _scan_ssd_kernel(ci_ref, bi_ref, xi_ref, pi_ref, cj_ref, bj_ref, xj_ref, pj_ref, dskip_ref,
                     s0f_ref, s0b_ref, ya_ref, yb_ref, finf_ref, finb_ref, sf, sb):
    i = pl.program_id(0)
    c = CHUNK
    H = SSM_HEADS

    @pl.when(i == 0)
    def _():
        sf[...] = s0f_ref[...]
        sb[...] = s0b_ref[...]

    tn = (((0,), (0,)), ((), ()))
    nt = (((1,), (1,)), ((), ()))
    ii = lax.broadcasted_iota(jnp.int32, (c, c), 0)
    jj = lax.broadcasted_iota(jnp.int32, (c, c), 1)
    low = lax.broadcasted_iota(jnp.int32, (c, LANES), 1) < SSM_HEADDIM
    low2 = lax.broadcasted_iota(jnp.int32, (2 * SSM_STATE, LANES), 1) < SSM_HEADDIM
    diag = (lax.broadcasted_iota(jnp.int32, (2 * SSM_STATE, LANES), 0) < SSM_STATE) == low2
    pi = pi_ref[...]
    pit = pi.T
    pj = pj_ref[...]
    ei = jnp.exp(jnp.minimum(pi, 0.0))
    ej = jnp.exp(jnp.minimum(pj, 0.0))
    tot_i, tot_j = pi[c - 1:c, :], pj[0:1, :]
    wi = jnp.exp(jnp.minimum(tot_i - pi, 0.0)) * pltpu.roll(pi, LANES - 2 * H, axis=1)
    wj = jnp.exp(jnp.minimum(tot_j - pj, 0.0)) * pltpu.roll(pj, LANES - 2 * H, axis=1)
    eti, etj = jnp.exp(jnp.minimum(tot_i, 0.0)), jnp.exp(jnp.minimum(tot_j, 0.0))
    colb = lambda t, k: jnp.broadcast_to(t[:, k:k + 1], (c, LANES))
    for g in range(SSM_GROUPS):
        gs = slice(g * SSM_STATE, (g + 1) * SSM_STATE)
        ci, bi = ci_ref[:, gs], bi_ref[:, gs]
        cj, bj = cj_ref[:, gs], bj_ref[:, gs]
        cb = lax.dot_general(ci, bi, nt, preferred_element_type=_f32)
        ci32, bi32, cj32, bj32 = (t.astype(_f32) for t in (ci, bi, cj, bj))
        for pp in range(SSM_HPG // 2):
            q = g * (SSM_HPG // 2) + pp
            heads = (2 * q, 2 * q + 1)
            xs = slice(q * LANES, (q + 1) * LANES)
            x = xi_ref[:, xs]
            x32 = x.astype(_f32)
            scores, cw, bw = [], [], []
            for h in heads:
                row = lambda o: pit[o + h:o + h + 1, :]
                mf = jnp.where(ii >= jj, jnp.exp(jnp.minimum(colb(pi, h) - row(0), 0.0)), 0.0) * row(2 * H)
                mb = jnp.where(jj >= ii, jnp.exp(jnp.minimum(colb(pi, H + h) - row(H), 0.0)), 0.0) * row(3 * H)
                scores.append((cb * (mf + mb)).astype(MXU_DTYPE))
                cw.append((ci32 * colb(ei, h)).astype(MXU_DTYPE))
                bw.append((bi32 * colb(wi, h)).astype(MXU_DTYPE))
            xa = jnp.where(low, x32, 0.0).astype(MXU_DTYPE)
            xb = jnp.where(low, 0.0, x32).astype(MXU_DTYPE)
            lhs = jnp.concatenate(scores + cw, axis=1)
            rhs = jnp.concatenate([xa, xb, sf[q].astype(MXU_DTYPE)], axis=0)
            y = jnp.dot(lhs, rhs, preferred_element_type=_f32)
            ya_ref[:, xs] = y + dskip_ref[:, xs] * x32
            upd = lax.dot_general(jnp.concatenate(bw, axis=1), x, tn, preferred_element_type=_f32)
            dec = jnp.where(low2, eti[0:1, heads[0]:heads[0] + 1], eti[0:1, heads[1]:heads[1] + 1])
            sf[q] = dec * sf[q] + jnp.where(diag, upd, 0.0)
            x = xj_ref[:, xs]
            cw = [(cj32 * colb(ej, H + h)).astype(MXU_DTYPE) for h in heads]
            bw = [(bj32 * colb(wj, H + h)).astype(MXU_DTYPE) for h in heads]
            yb_ref[:, xs] = jnp.dot(jnp.concatenate(cw, axis=1), sb[q].astype(MXU_DTYPE),
                                    preferred_element_type=_f32)
            upd = lax.dot_general(jnp.concatenate(bw, axis=1), x, tn, preferred_element_type=_f32)
            dec = jnp.where(low2, etj[0:1, H + heads[0]:H + heads[0] + 1], etj[0:1, H + heads[1]:H + heads[1] + 1])
            sb[q] = dec * sb[q] + jnp.where(diag, upd, 0.0)

    @pl.when(i == pl.num_programs(0) - 1)
    def _():
        finf_ref[...] = sf[...]
        finb_ref[...] = sb[...]


def _pair_states(s):
    s = s.reshape(SSM_HEADS // 2, 2, SSM_STATE, SSM_HEADDIM)
    z = jnp.zeros_like(s[:, 0])
    return jnp.concatenate([jnp.concatenate([s[:, 0], z], -1), jnp.concatenate([z, s[:, 1]], -1)], 1)


def _unpair_states(s):
    top, bot = s[:, :SSM_STATE, :SSM_HEADDIM], s[:, SSM_STATE:, SSM_HEADDIM:]
    return jnp.stack([top, bot], 1).reshape(SSM_HEADS, SSM_STATE, SSM_HEADDIM)


def _scan_ssd(cs, bs, xs, pack, d_skip, s0f, s0b):
    L = xs.shape[0]
    nc = L // CHUNK
    dvec = jnp.repeat(d_skip.astype(_f32), SSM_HEADDIM)[None]
    fw = lambda w: pl.BlockSpec((CHUNK, w), lambda i: (i, 0))
    bw = lambda w: pl.BlockSpec((CHUNK, w), lambda i: (nc - 1 - i, 0))
    pshape = (SSM_HEADS // 2, 2 * SSM_STATE, 2 * SSM_HEADDIM)
    st = pl.BlockSpec(pshape, lambda i: (0, 0, 0))
    yshape = jax.ShapeDtypeStruct((L, SSM_W), _f32)
    sshape = jax.ShapeDtypeStruct(pshape, _f32)
    ya, yb, fin_f, fin_b = pl.pallas_call(
        _scan_ssd_kernel,
        grid=(nc,),
        in_specs=[fw(SSM_GN), fw(SSM_GN), fw(SSM_W), fw(LANES), bw(SSM_GN), bw(SSM_GN), bw(SSM_W), bw(LANES),
                  pl.BlockSpec((1, SSM_W), lambda i: (0, 0)), st, st],
        out_specs=[fw(SSM_W), bw(SSM_W), st, st],
        out_shape=[yshape, yshape, sshape, sshape],
        scratch_shapes=[pltpu.VMEM(pshape, _f32)] * 2,
        compiler_params=_cparams("arbitrary"),
        name="scan_ssd",
    )(cs, bs, xs, pack, cs, bs, xs, pack, dvec, _pair_states(s0f), _pair_states(s0b))
    return ya, yb, _unpair_states(fin_f), _unpair_states(fin_b)


def _merge_kernel(ra_ref, rb_ref, sa_ref, sb_ref, gr_ref, gs_ref, nw_ref, yr_ref, ys_ref):
    for h in range(RET_HEADS):
        sl = slice(h * RET_DH, (h + 1) * RET_DH)
        y = _ln_rows(ra_ref[:, sl] + rb_ref[:, sl])
        yr_ref[:, sl] = (y * _silu(gr_ref[:, sl])).astype(yr_ref.dtype)
    gw = SSM_W // SSM_GROUPS
    for g in range(SSM_GROUPS):
        sl = slice(g * gw, (g + 1) * gw)
        y = (sa_ref[:, sl] + sb_ref[:, sl]) * _silu(gs_ref[:, sl])
        y = y * lax.rsqrt(jnp.mean(y * y, -1, keepdims=True) + LN_EPS)
        ys_ref[:, sl] = (y * nw_ref[:, sl]).astype(ys_ref.dtype)


def _gate_spec(tr, n):
    return pl.BlockSpec((tr, BR_W), lambda i: (i, B_GATE // BR_W + n))


def _merge(ra, rb, sa, sb_, pb, norm_w):
    L = ra.shape[0]
    tr = 256
    row = pl.BlockSpec((tr, BR_W), lambda i: (i, 0))
    shp = jax.ShapeDtypeStruct((L, BR_W), MXU_DTYPE)
    return pl.pallas_call(
        _merge_kernel,
        grid=(L // tr,),
        in_specs=[row, row, row, row, _gate_spec(tr, 1), _gate_spec(tr, 3),
                  pl.BlockSpec((1, BR_W), lambda i: (0, 0))],
        out_specs=[row, row],
        out_shape=[shp, shp],
        compiler_params=_cparams("parallel"),
        name="merge",
    )(ra, rb, sa, sb_, pb, pb, norm_w.astype(_f32)[None])


def _pool_kernel(x_ref, prev_ref, next_ref, g_ref, pw_ref, ps_ref, o_ref, *, L):
    i = pl.program_id(0)
    t = x_ref.shape[0]
    halo = SUBLANES
    has_prev = (i > 0).astype(_f32)
    has_next = (i < pl.num_programs(0) - 1).astype(_f32)
    pos = i * t + lax.broadcasted_iota(jnp.int32, (t, 1), 0)
    for g, win in enumerate(POOL_WINDOWS):
        sl = slice(g * POOL_GROUP, (g + 1) * POOL_GROUP)
        x = x_ref[:, sl]
        s = jnp.concatenate([prev_ref[:, sl] * has_prev, x, next_ref[:, sl] * has_next], axis=0)
        rows = t + 2 * halo
        width = 1
        while width < win:
            s = s + pltpu.roll(s, rows - width, axis=0)
            width *= 2
        off = halo - win // 2
        if off:
            s = pltpu.roll(s, rows - off, axis=0)
        cnt = jnp.minimum(pos + win // 2, L) - jnp.maximum(pos - win // 2, 0)
        d = s[:t] / cnt.astype(_f32) - x
        y = jnp.dot(d.astype(MXU_DTYPE), pw_ref[g], preferred_element_type=_f32)
        o_ref[:, sl] = (y * ps_ref[:, sl] * _silu(g_ref[:, sl])).astype(o_ref.dtype)


def _pool(pb, L, pool_w, pool_scale):
    tr = 256
    return pl.pallas_call(
        functools.partial(_pool_kernel, L=L),
        grid=(L // tr,),
        in_specs=_halo_specs(tr, L, POOL_W, B_POOL) + [
            _gate_spec(tr, 2),
            pl.BlockSpec((POOL_GROUPS, POOL_GROUP, POOL_GROUP), lambda i: (0, 0, 0)),
            pl.BlockSpec((1, POOL_W), lambda i: (0, 0))],
        out_specs=pl.BlockSpec((tr, POOL_W), lambda i: (i, 0)),
        out_shape=jax.ShapeDtypeStruct((L, POOL_W), MXU_DTYPE),
        compiler_params=_cparams("parallel"),
        name="pool",
    )(pb, pb, pb, pb, pool_w.astype(MXU_DTYPE), pool_scale.astype(_f32)[None])


def _prep_hy_kernel(v_ref, vp_ref, vn_ref, x0_ref, x0p_ref, x0n_ref, x1_ref, x1p_ref, x1n_ref,
                    g_ref, w_ref, b_ref, wo_ref, x0g_ref):
    i = pl.program_id(0)
    has_prev = (i > 0).astype(_f32)
    has_next = (i < pl.num_programs(0) - 1).astype(_f32)

    def conv(n, x_ref, p_ref, n_ref):
        sl = slice(n * HY_W, (n + 1) * HY_W)
        return _conv3(x_ref, p_ref, n_ref, w_ref.at[:, sl], b_ref.at[:, sl], has_prev, has_next)

    hv = conv(0, v_ref, vp_ref, vn_ref)
    hx0 = conv(1, x0_ref, x0p_ref, x0n_ref)
    hx1 = conv(2, x1_ref, x1p_ref, x1n_ref)
    wo_ref[...] = hx1 * hv
    x0g_ref[...] = hx0 * _silu(g_ref[...])


def _prep_hy(pb, L, conv_w, conv_b):
    tr = 256
    w = jnp.pad(conv_w.astype(_f32), ((0, SUBLANES - 3), (0, 0)))
    row = pl.BlockSpec((tr, HY_W), lambda i: (i, 0))
    shp = jax.ShapeDtypeStruct((L, HY_W), _f32)
    secs = sum((_halo_specs(tr, L, HY_W, B_HY + n * HY_W) for n in range(3)), [])
    return pl.pallas_call(
        _prep_hy_kernel,
        grid=(L // tr,),
        in_specs=secs + [_gate_spec(tr, 0),
                         pl.BlockSpec((SUBLANES, 3 * HY_W), lambda i: (0, 0)),
                         pl.BlockSpec((1, 3 * HY_W), lambda i: (0, 0))],
        out_specs=[row, row],
        out_shape=[shp, shp],
        compiler_params=_cparams("parallel"),
        name="prep_hy",
    )(*([pb] * 10), w, conv_b.astype(_f32)[None])


def _split(x):
    hi = x.astype(MXU_DTYPE)
    return hi, (x - hi.astype(_f32)).astype(MXU_DTYPE)


def _dot3(a_hi, a_lo, b):
    b_hi, b_lo = _split(b)
    d = lambda p, q: jnp.dot(p, q, preferred_element_type=_f32)
    return d(a_hi, b_hi) + (d(a_hi, b_lo) + d(a_lo, b_hi))


def _const_split(m):
    return _split(jnp.asarray(m, _f32))


def _filter_kernel(z_ref, w1_ref, b1_ref, w2_ref, b2_ref, w3hi_ref, w3lo_ref, freq_ref, delta_ref, o_ref, *, L):
    t = z_ref.shape[0]
    z = z_ref[...]
    dot = functools.partial(jnp.dot, precision=HIGHEST, preferred_element_type=_f32)
    freq = freq_ref[...]
    hdn = jnp.sin(freq * (dot(z, w1_ref[...]) + b1_ref[...]))
    hdn = jnp.sin(freq * (dot(hdn, w2_ref[...]) + b2_ref[...]))
    h_hi, h_lo = _split(hdn)
    d = lambda p, q: jnp.dot(p, q, preferred_element_type=_f32)
    filt = d(h_hi, w3hi_ref[...]) + (d(h_hi, w3lo_ref[...]) + d(h_lo, w3hi_ref[...]))
    n = pl.program_id(0) * t + lax.broadcasted_iota(jnp.int32, (t, 1), 0)
    o_ref[...] = jnp.where(n == L, 0.0, filt) * jnp.exp(-z[:, 0:1] * delta_ref[...])


def _hy_filter(L, lp):
    n = jnp.arange(2 * L)
    lag = jnp.minimum(jnp.where(n < L, n, 2 * L - n), L - 1).astype(_f32)[:, None]
    t = lag / (L - 1)
    w = 2.0 * math.pi * lag / L
    bands = jnp.linspace(1e-4, HY_BANDS - 1, HY_BANDS, dtype=_f32)[None, :]
    z = jnp.concatenate([t, jnp.cos(bands * w), -jnp.sin(bands * w)], axis=-1)
    emb = z.shape[1]
    z = jnp.pad(z, ((0, 0), (0, LANES - emb)))
    w1 = jnp.pad(lp['hy_w1'].astype(_f32), ((0, LANES - emb), (0, 0)))
    deltas = jnp.abs(jnp.linspace(HY_MIN_DECAY, HY_MAX_DECAY, HY_W, dtype=_f32))[None]
    tr = min(512, L)
    w3hi, w3lo = _split(lp['hy_w3'].astype(_f32))
    full = lambda a: pl.BlockSpec(a.shape, lambda i: (0,) * a.ndim)
    half = pl.BlockSpec((w3hi.shape[0], HY_W), lambda i: (0, i // (L // tr)))
    pre = [w1, lp['hy_b1'].astype(_f32)[None], lp['hy_w2'].astype(_f32), lp['hy_b2'].astype(_f32)[None]]
    post = [lp['hy_freq'].astype(_f32)[None], deltas]
    return pl.pallas_call(
        functools.partial(_filter_kernel, L=L),
        grid=(2 * L // tr,),
        in_specs=[pl.BlockSpec((tr, LANES), lambda i: (i, 0))] + [full(a) for a in pre] + [half, half]
        + [full(a) for a in post],
        out_specs=pl.BlockSpec((tr, HY_W), lambda i: (i, 0)),
        out_shape=jax.ShapeDtypeStruct((2 * L, HY_W), _f32),
        compiler_params=_cparams("parallel"),
        name="hy_filter",
    )(z, *pre, w3hi, w3lo, *post)


def _cs(num, den):
    ang = 2.0 * np.pi * (np.asarray(num, np.int64) % den) / den
    return np.cos(ang), np.sin(ang)


FFT_N2 = LANES


def _fft_rows(n1):
    return -(-(n1 // 2 + 1) // SUBLANES) * SUBLANES


def _fft_first_kernel(x_ref, mhi_ref, mlo_ref, o_ref):
    o_ref[...] = _dot3(mhi_ref[...], mlo_ref[...], x_ref[...])


def _fft_first(x2, n1):
    rows, cols = x2.shape
    kp = _fft_rows(n1)
    c, s = _cs(np.outer(np.arange(kp), np.arange(rows)), n1)
    mhi, mlo = _const_split(np.concatenate([c, -s], 0))
    tcol = 4096
    return pl.pallas_call(
        _fft_first_kernel,
        grid=(cols // tcol,),
        in_specs=[pl.BlockSpec((rows, tcol), lambda j: (0, j)),
                  pl.BlockSpec((2 * kp, rows), lambda j: (0, 0)),
                  pl.BlockSpec((2 * kp, rows), lambda j: (0, 0))],
        out_specs=pl.BlockSpec((2 * kp, tcol), lambda j: (0, j)),
        out_shape=jax.ShapeDtypeStruct((2 * kp, cols), _f32),
        compiler_params=_cparams("parallel"),
        name="fft_first",
    )(x2, mhi, mlo)


def _fft_mid_kernel(*refs, conv):
    if conv:
        a_ref, twr_ref, twi_ref, fhi_ref, flo_ref, h_ref, ghi_ref, glo_ref, o_ref = refs
    else:
        a_ref, twr_ref, twi_ref, fhi_ref, flo_ref, o_ref = refs
    n2 = FFT_N2
    reps = a_ref.shape[-1] // LANES
    twr = jnp.concatenate([twr_ref[0]] * reps, axis=1)
    twi = jnp.concatenate([twi_ref[0]] * reps, axis=1)
    ar, ai = a_ref[0, 0], a_ref[1, 0]
    x = _dot3(fhi_ref[...], flo_ref[...],
              jnp.concatenate([ar * twr - ai * twi, ar * twi + ai * twr], axis=0))
    if not conv:
        o_ref[0, 0] = x[:n2]
        o_ref[1, 0] = x[n2:]
        return
    xr, xi = x[:n2], x[n2:]
    hr, hi = h_ref[0, 0], h_ref[1, 0]
    b = _dot3(ghi_ref[...], glo_ref[...],
              jnp.concatenate([xr * hr - xi * hi, xr * hi + xi * hr], axis=0))
    br, bi = b[:n2], b[n2:]
    o_ref[0, 0] = br * twr + bi * twi
    o_ref[1, 0] = bi * twr - br * twi


def _fft_mid(a, n1, hf=None):
    n2 = FFT_N2
    kp, ch = a.shape[1], a.shape[-1]
    n = n1 * n2
    idx = jnp.arange(kp)[:, None] * jnp.arange(n2)[None, :]
    ang = (2.0 * math.pi / n) * (idx % n).astype(_f32)
    twr = jnp.broadcast_to(jnp.cos(ang)[:, :, None], (kp, n2, LANES))
    twi = jnp.broadcast_to(-jnp.sin(ang)[:, :, None], (kp, n2, LANES))
    c, s = _cs(np.outer(np.arange(n2), np.arange(n2)), n2)
    fhi, flo = _const_split(np.block([[c, s], [-s, c]]))
    blk = pl.BlockSpec((2, 1, n2, ch), lambda k: (0, k, 0, 0))
    tw = pl.BlockSpec((1, n2, LANES), lambda k: (k, 0, 0))
    mat = pl.BlockSpec((2 * n2, 2 * n2), lambda k: (0, 0))
    args, specs = [a, twr, twi, fhi, flo], [blk, tw, tw, mat, mat]
    if hf is not None:
        ghi, glo = _const_split(np.block([[c, -s], [s, c]]))
        args += [hf, ghi, glo]
        specs += [blk, mat, mat]
    return pl.pallas_call(
        functools.partial(_fft_mid_kernel, conv=hf is not None),
        grid=(kp,),
        in_specs=specs,
        out_specs=blk,
        out_shape=jax.ShapeDtypeStruct(a.shape, _f32),
        compiler_params=_cparams("parallel"),
        name="fft_mid",
    )(*args)


def _fft_last_kernel(c_ref, mhi_ref, mlo_ref, w_ref, x0g_ref, bias_ref, o_ref):
    y = _dot3(mhi_ref[...], mlo_ref[...], c_ref[...])
    o_ref[...] = (x0g_ref[...] * (y + w_ref[...] * bias_ref[...])).astype(o_ref.dtype)


def _fft_last(c2, n1, w2, x0g2, bias):
    rows, cols = w2.shape
    kp = _fft_rows(n1)
    n = n1 * FFT_N2
    c, s = _cs(np.outer(np.arange(rows), np.arange(kp)), n1)
    k1 = np.arange(kp)
    mult = np.where((k1 == 0) | (k1 == n1 // 2), 1.0, np.where(k1 < n1 // 2, 2.0, 0.0))
    mhi, mlo = _const_split(np.concatenate([c * mult, -s * mult], 1) / n)
    tcol = 4096
    bias_t = jnp.tile(bias.astype(_f32), tcol // bias.shape[0])[None]
    blk = pl.BlockSpec((rows, tcol), lambda j: (0, j))
    return pl.pallas_call(
        _fft_last_kernel,
        grid=(cols // tcol,),
        in_specs=[pl.BlockSpec((2 * kp, tcol), lambda j: (0, j)),
                  pl.BlockSpec((rows, 2 * kp), lambda j: (0, 0)),
                  pl.BlockSpec((rows, 2 * kp), lambda j: (0, 0)),
                  blk, blk, pl.BlockSpec((1, tcol), lambda j: (0, 0))],
        out_specs=blk,
        out_shape=jax.ShapeDtypeStruct((rows, cols), MXU_DTYPE),
        compiler_params=_cparams("parallel"),
        name="fft_last",
    )(c2, mhi, mlo, w2, x0g2, bias_t)


def _hy_small_kernel(w_ref, buf_ref, x0g_ref, bias_ref, fwhi_ref, fwlo_ref, fbhi_ref, fblo_ref,
                     ihi_ref, ilo_ref, o_ref):
    n = buf_ref.shape[0]
    w = w_ref[...]
    wf = _dot3(fwhi_ref[...], fwlo_ref[...], w)
    hf = _dot3(fbhi_ref[...], fblo_ref[...], buf_ref[...])
    wr, wi, hr, hi = wf[:n], wf[n:], hf[:n], hf[n:]
    y = _dot3(ihi_ref[...], ilo_ref[...], jnp.concatenate([wr * hr - wi * hi, wr * hi + wi * hr], axis=0))
    o_ref[...] = (x0g_ref[...] * (y + w * bias_ref[...])).astype(o_ref.dtype)


def _hy_conv_small(w, buf, x0g, bias):
    L, ch = w.shape
    n = 2 * L
    tc = 256
    c, s = _cs(np.outer(np.arange(n), np.arange(n)), n)
    fb = np.concatenate([c, -s], 0)
    mats = [*_const_split(fb[:, :L]), *_const_split(fb),
            *_const_split(np.concatenate([c[:L], -s[:L]], 1) / n)]
    col = lambda r: pl.BlockSpec((r, tc), lambda j: (0, j))
    return pl.pallas_call(
        _hy_small_kernel,
        grid=(ch // tc,),
        in_specs=[col(L), col(n), col(L), col(1)] + [pl.BlockSpec(m.shape, lambda j: (0, 0)) for m in mats],
        out_specs=col(L),
        out_shape=jax.ShapeDtypeStruct((L, ch), MXU_DTYPE),
        compiler_params=_cparams("parallel"),
        name="hy_conv_small",
    )(w, buf, x0g, bias.astype(_f32)[None], *mats)


def _hy_conv(w, buf, x0g, bias):
    L, ch = w.shape
    if L < 512:
        return _hy_conv_small(w, buf, x0g, bias)
    n1 = 2 * L // FFT_N2
    kp = _fft_rows(n1)
    cols = FFT_N2 * ch
    hf = _fft_mid(_fft_first(buf.reshape(n1, cols), n1).reshape(2, kp, FFT_N2, ch), n1)
    a = _fft_first(w.reshape(n1 // 2, cols), n1).reshape(2, kp, FFT_N2, ch)
    cc = _fft_mid(a, n1, hf)
    y = _fft_last(cc.reshape(2 * kp, cols), n1, w.reshape(n1 // 2, cols), x0g.reshape(n1 // 2, cols), bias)
    return y.reshape(L, ch)


def _zero_states():
    return (jnp.zeros((RET_HEADS, RET_DH, RET_DH), _f32), jnp.zeros((RET_HEADS, RET_DH, RET_DH), _f32),
            jnp.zeros((SSM_HEADS, SSM_STATE, SSM_HEADDIM), _f32),
            jnp.zeros((SSM_HEADS, SSM_STATE, SSM_HEADDIM), _f32))


def _recurrent(proj, L, lp, states, latent):
    pa, pb = proj
    q, k, v = _prep_ret(pa, pb, L, latent)
    ra, rb, ret_f, ret_b = _scan_ret(q, k, v, lp['ret_decay_logit'], states[0], states[1])
    cs, bs, xs, pack = _prep_ssd(pa, pb, L, lp['conv_ssm_w'], lp['conv_ssm_b'], lp['ssm_dt_bias'],
                                 lp['ssm_A_log'])
    sa, sb_, ssm_f, ssm_b = _scan_ssd(cs, bs, xs, pack, lp['ssm_D'], states[2], states[3])
    return (ra, rb, sa, sb_), (ret_f, ret_b, ssm_f, ssm_b)


def _mix(h, mod, lp, states, latent):
    L = h.shape[0]
    proj = _in_proj(h, mod[0], mod[1], lp)
    pb = proj[1]
    (ra, rb, sa, sb_), fin = _recurrent(proj, L, lp, states, latent)
    y_ret, y_ssm = _merge(ra, rb, sa, sb_, pb, lp['ssm_norm_w'])
    w, x0g = _prep_hy(pb, L, lp['conv_hy_w'], lp['conv_hy_b'])
    y_hy = _hy_conv(w, _hy_filter(L, lp), x0g, lp['hy_bias'])
    y_pool = _pool(pb, L, lp['pool_w'], lp['pool_scale'])
    out = _out_proj([y_hy, y_ret, y_pool, y_ssm], lp['w_out'], h, mod[2], lp['ln_g'], lp['ln_b'])
    return out, fin


def _context_states(hc, mod, lp):
    proj = _in_proj(hc, mod[0], mod[1], lp)
    _, fin = _recurrent(proj, hc.shape[0], lp, _zero_states(), False)
    return fin


def kernel(x, c, ctx, c_ctx, w_mod, b_mod, w_in, conv_ssm_w, conv_ssm_b, conv_hy_w, conv_hy_b,
           ret_decay_logit, ssm_A_log, ssm_dt_bias, ssm_D, ssm_norm_w, hy_w1, hy_b1, hy_w2, hy_b2,
           hy_w3, hy_freq, hy_bias, pool_w, pool_scale, w_out, ln_g, ln_b):
    assert x.shape[0] == 1
    h, hc = x[0], ctx[0]
    w_in_t = jnp.swapaxes(w_in, 1, 2)
    mods = _adaln(jnp.concatenate([c, c_ctx[None]], axis=0), w_mod, b_mod)
    for l in range(DEPTH):
        lp = {
            'w_in_t': _cast_layer(w_in_t, l, 96), 'w_out': _cast_layer(w_out, l, 512),
            'conv_ssm_w': conv_ssm_w[l], 'conv_ssm_b': conv_ssm_b[l],
            'conv_hy_w': conv_hy_w[l], 'conv_hy_b': conv_hy_b[l], 'ret_decay_logit': ret_decay_logit[l],
            'ssm_A_log': ssm_A_log[l], 'ssm_dt_bias': ssm_dt_bias[l], 'ssm_D': ssm_D[l],
            'ssm_norm_w': ssm_norm_w[l], 'hy_w1': hy_w1[l], 'hy_b1': hy_b1[l], 'hy_w2': hy_w2[l],
            'hy_b2': hy_b2[l], 'hy_w3': hy_w3[l], 'hy_freq': hy_freq[l], 'hy_bias': hy_bias[l],
            'pool_w': pool_w[l], 'pool_scale': pool_scale[l], 'ln_g': ln_g[l], 'ln_b': ln_b[l],
        }
        mod = lambda r: tuple(mods[l, r:r + 1, n * D_MODEL:(n + 1) * D_MODEL] for n in range(3))
        if l < DEPTH - 1:
            hc_next, states = _mix(hc, mod(1), lp, _zero_states(), False)
        else:
            states = _context_states(hc, mod(1), lp)
            hc_next = hc
        h, _ = _mix(h, mod(0), lp, states, True)
        hc = hc_next
    return h[None]
```

```python
import functools
import math

import jax
import jax.numpy as jnp
import numpy as np
from jax import lax
from jax.experimental import pallas as pl
from jax.experimental.pallas import tpu as pltpu

D_MODEL = 4096
DEPTH = 2
GRID_W = 64
MIX_W = D_MODEL
BR_W = MIX_W // 4
HY_W = RET_W = POOL_W = SSM_W = BR_W
RET_HEADS = 8
RET_DH = RET_W // RET_HEADS
ROPE_BASE = 10000.0
SSM_HEADDIM = 64
SSM_HEADS = SSM_W // SSM_HEADDIM
SSM_GROUPS = 4
SSM_HPG = SSM_HEADS // SSM_GROUPS
SSM_STATE = 128
SSM_GN = SSM_GROUPS * SSM_STATE
CHUNK = 128
POOL_WINDOWS = (2, 4, 8, 16)
POOL_GROUPS = len(POOL_WINDOWS)
POOL_GROUP = POOL_W // POOL_GROUPS
HY_BANDS = 16
HY_TARGET = 1e-2
HY_FAST = 0.3
HY_SLOW = 1.5
HY_MIN_DECAY = math.log(HY_TARGET) / HY_SLOW
HY_MAX_DECAY = math.log(HY_TARGET) / HY_FAST
ALPHA = (2.0 * DEPTH) ** 0.25
LN_EPS = 1e-5

O_RET_K = 0
O_RET_V = O_RET_K + RET_W
O_SSM_DT = O_RET_V + RET_W
O_SSM_X = O_SSM_DT + 2 * SSM_HEADS
O_SSM_B = O_SSM_X + SSM_W
O_RET_Q = O_SSM_B + SSM_GN
O_SSM_C = O_RET_Q + RET_W
O_HY = O_SSM_C + SSM_GN
O_POOL = O_HY + 3 * HY_W
O_GATE = O_POOL + POOL_W
N_IN = O_GATE + MIX_W

LANES = 128
SUBLANES = 8
N_A = O_SSM_DT + LANES
B_SX = 0
B_SB = O_SSM_B - O_SSM_X
B_RQ = O_RET_Q - O_SSM_X
B_SC = O_SSM_C - O_SSM_X
B_HY = O_HY - O_SSM_X
B_POOL = O_POOL - O_SSM_X
B_GATE = O_GATE - O_SSM_X
N_B = N_IN - O_SSM_X

VMEM_LIMIT_BYTES = 56 * 1024 * 1024
MXU_DTYPE = jnp.bfloat16
HIGHEST = lax.Precision.HIGHEST

_f32 = jnp.float32


def _cparams(*sem, vmem=VMEM_LIMIT_BYTES):
    return pltpu.CompilerParams(dimension_semantics=sem, vmem_limit_bytes=vmem)


def _silu(x):
    return x * jax.nn.sigmoid(x)


def _ln_rows(z):
    mu = jnp.mean(z, -1, keepdims=True)
    zc = z - mu
    var = jnp.mean(zc * zc, -1, keepdims=True)
    return zc * lax.rsqrt(var + LN_EPS)


def _cast_kernel(w_ref, o_ref):
    o_ref[...] = w_ref[...].astype(o_ref.dtype)


def _cast_layer(w, l, tr, tc):
    _, r, c = w.shape
    assert r % tr == 0 and c % tc == 0
    return pl.pallas_call(
        _cast_kernel,
        grid=(r // tr, c // tc),
        in_specs=[pl.BlockSpec((1, tr, tc), lambda i, j: (l, i, j))],
        out_specs=pl.BlockSpec((1, tr, tc), lambda i, j: (0, i, j)),
        out_shape=jax.ShapeDtypeStruct((1, r, c), MXU_DTYPE),
        compiler_params=_cparams("parallel", "parallel"),
        name="cast_layer",
    )(w)[0]


def _adaln_kernel(c_ref, w_ref, b_ref, o_ref):
    tn = w_ref.shape[-1]
    outs = []
    for m in range(c_ref.shape[0]):
        x = _silu(c_ref[m])
        cols = [jnp.sum(x * w_ref[0, :, j * LANES:(j + 1) * LANES], axis=0, keepdims=True)
                for j in range(tn // LANES)]
        outs.append(jnp.concatenate(cols, axis=1))
    outs.append(jnp.zeros((SUBLANES - len(outs), tn), _f32))
    o_ref[0] = jnp.concatenate(outs, axis=0) + b_ref[0]


def _adaln(c_rows, w_mod, b_mod):
    dep, k, n = w_mod.shape
    r = c_rows.shape[0]
    tn = 512
    cb = jnp.broadcast_to(c_rows.astype(_f32)[:, :, None], (r, k, LANES))
    return pl.pallas_call(
        _adaln_kernel,
        grid=(dep, n // tn),
        in_specs=[pl.BlockSpec((r, k, LANES), lambda l, j: (0, 0, 0)),
                  pl.BlockSpec((1, k, tn), lambda l, j: (l, 0, j)),
                  pl.BlockSpec((1, 1, tn), lambda l, j: (l, 0, j))],
        out_specs=pl.BlockSpec((1, SUBLANES, tn), lambda l, j: (l, 0, j)),
        out_shape=jax.ShapeDtypeStruct((dep, SUBLANES, n), _f32),
        compiler_params=_cparams("parallel", "parallel"),
        name="adaln",
    )(cb, w_mod, b_mod[:, None, :])


def _matmul_nt_kernel(a_ref, b_ref, o_ref):
    o_ref[...] = lax.dot_general(a_ref[...], b_ref[...], (((1,), (1,)), ((), ())),
                                 preferred_element_type=_f32).astype(o_ref.dtype)


def _matmul_nt(a, wt, row0, n, tm, tn, out_dtype):
    m, k = a.shape
    assert m % tm == 0 and n % tn == 0 and row0 % 32 == 0 and tn % 32 == 0
    return pl.pallas_call(
        _matmul_nt_kernel,
        grid=(m // tm, n // tn),
        in_specs=[pl.BlockSpec((tm, k), lambda i, j: (i, 0)),
                  pl.BlockSpec((pl.Element(tn), pl.Element(k)),
                               lambda i, j: (pl.multiple_of(row0 + j * tn, 32), 0))],
        out_specs=pl.BlockSpec((tm, tn), lambda i, j: (i, j)),
        out_shape=jax.ShapeDtypeStruct((m, n), out_dtype),
        compiler_params=_cparams("parallel", "parallel"),
        name="matmul_nt",
    )(a, wt)


def _modulate_kernel(h_ref, shift_ref, scale_ref, o_ref):
    o_ref[...] = (_ln_rows(h_ref[...]) * (1.0 + scale_ref[...]) + shift_ref[...]).astype(o_ref.dtype)


def _modulate(h, shift, scale):
    L, d = h.shape
    tr = 256
    vec = pl.BlockSpec((1, d), lambda i: (0, 0))
    return pl.pallas_call(
        _modulate_kernel,
        grid=(L // tr,),
        in_specs=[pl.BlockSpec((tr, d), lambda i: (i, 0)), vec, vec],
        out_specs=pl.BlockSpec((tr, d), lambda i: (i, 0)),
        out_shape=jax.ShapeDtypeStruct((L, d), MXU_DTYPE),
        compiler_params=_cparams("parallel"),
        name="modulate",
    )(h, shift, scale)


def _in_proj(h, shift, scale, lp):
    L = h.shape[0]
    tm = 512 if L % 512 == 0 else 256
    u = _modulate(h, shift, scale)
    wt = lp['w_in_t']
    return _matmul_nt(u, wt, 0, N_A, tm, N_A, _f32), _matmul_nt(u, wt, O_SSM_X, N_B, tm, 1024, MXU_DTYPE)


def _out_proj_kernel(y0_ref, y1_ref, y2_ref, y3_ref, w_ref, h_ref, gate_ref, g_ref, b_ref, o_ref):
    out = None
    for n, y_ref in enumerate((y0_ref, y1_ref, y2_ref, y3_ref)):
        d = jnp.dot(y_ref[...], w_ref[n * BR_W:(n + 1) * BR_W, :], preferred_element_type=_f32)
        out = d if out is None else out + d
    z = ALPHA * h_ref[...] + gate_ref[...] * out
    o_ref[...] = _ln_rows(z) * g_ref[...] + b_ref[...]


OUT_PROJ_VMEM_BYTES = 60 * 1024 * 1024


def _out_proj(ys, w, h, gate, g, b):
    L, d = h.shape
    tm = 256
    lhs = pl.BlockSpec((tm, BR_W), lambda i: (i, 0))
    vec = pl.BlockSpec((1, d), lambda i: (0, 0))
    row = pl.BlockSpec((tm, d), lambda i: (i, 0))
    return pl.pallas_call(
        _out_proj_kernel,
        grid=(L // tm,),
        in_specs=[lhs] * len(ys) + [
            pl.BlockSpec(w.shape, lambda i: (0, 0), pipeline_mode=pl.Buffered(1)), row, vec, vec, vec],
        out_specs=row,
        out_shape=jax.ShapeDtypeStruct((L, d), _f32),
        compiler_params=_cparams("parallel", vmem=OUT_PROJ_VMEM_BYTES),
        name="out_proj",
    )(*ys, w, h, gate, g[None], b[None])


def _rope_tables(L):
    rows = L // GRID_W
    row = jnp.repeat(jnp.arange(rows), GRID_W).astype(_f32)
    col = jnp.tile(jnp.arange(GRID_W), rows).astype(_f32)
    nq = RET_DH // 4
    inv = ROPE_BASE ** (-jnp.arange(nq, dtype=_f32) / nq)
    ang = jnp.concatenate([row[:, None] * inv, col[:, None] * inv], -1)
    cos, sin = jnp.cos(ang), jnp.sin(ang)
    return jnp.concatenate([cos, cos], -1), jnp.concatenate([-sin, sin], -1)


def _prep_ret_kernel(qlo_ref, qhi_ref, k_ref, v_ref, cos_ref, sin_ref, qo_ref, ko_ref, vo_ref, *, rope):
    def rot(t):
        if not rope:
            return t
        return t * cos_ref[...] + pltpu.roll(t, RET_DH // 2, axis=1) * sin_ref[...]

    half = RET_HEADS // 2
    for h in range(RET_HEADS):
        sl = slice(h * RET_DH, (h + 1) * RET_DH)
        q_ref, qs = (qlo_ref, sl) if h < half else (qhi_ref, slice((h - half) * RET_DH, (h - half + 1) * RET_DH))
        qo_ref[:, sl] = rot(q_ref[:, qs].astype(_f32)).astype(qo_ref.dtype)
        ko_ref[:, sl] = rot(k_ref[:, sl] * (RET_DH ** -0.5)).astype(ko_ref.dtype)
    vo_ref[...] = v_ref[...].astype(vo_ref.dtype)


def _prep_ret(pa, pb, L, rope):
    tr = 256
    cos, sin = _rope_tables(L) if rope else (jnp.ones((L, LANES), _f32), jnp.zeros((L, LANES), _f32))
    hw = RET_W // 2
    qsp = lambda n: pl.BlockSpec((tr, hw), lambda i: (i, B_RQ // hw + n))
    sec = lambda c: pl.BlockSpec((tr, RET_W), lambda i: (i, c // RET_W))
    tab = pl.BlockSpec((tr, LANES), lambda i: (i, 0))
    out = pl.BlockSpec((tr, RET_W), lambda i: (i, 0))
    shp = jax.ShapeDtypeStruct((L, RET_W), MXU_DTYPE)
    return pl.pallas_call(
        functools.partial(_prep_ret_kernel, rope=rope),
        grid=(L // tr,),
        in_specs=[qsp(0), qsp(1), sec(O_RET_K), sec(O_RET_V), tab, tab],
        out_specs=[out, out, out],
        out_shape=[shp, shp, shp],
        compiler_params=_cparams("parallel"),
        name="prep_ret",
    )(pb, pb, pa, pa, cos, sin)


def _scan_ret_kernel(logit_ref, qi_ref, ki_ref, vi_ref, qj_ref, kj_ref, vj_ref, s0f_ref, s0b_ref,
                     ya_ref, yb_ref, finf_ref, finb_ref,
                     sf, sb, dmask, f_out, f_upd, f_all, b_out, b_upd, b_all):
    i = pl.program_id(0)
    c = CHUNK

    @pl.when(i == 0)
    def _():
        sf[...] = s0f_ref[...]
        sb[...] = s0b_ref[...]
        ii = lax.broadcasted_iota(jnp.int32, (c, c), 0).astype(_f32)
        jj = lax.broadcasted_iota(jnp.int32, (c, c), 1).astype(_f32)
        for h in range(RET_HEADS):
            def lg(d):
                x = logit_ref[d, h]
                v = -jnp.log1p(jnp.exp(-x))
                return jnp.broadcast_to(v[0:1, :], (c, c))
            lf, lb = lg(0), lg(1)
            dmask[h] = jnp.where(ii > jj, jnp.exp(lf * (ii - jj)),
                                 jnp.where(jj > ii, jnp.exp(lb * (jj - ii)), 2.0))
            f_out[h] = jnp.exp(lf * (ii + 1.0))
            f_upd[h] = jnp.exp(lf * (c - 1.0 - ii))
            f_all[h] = jnp.exp(lf * float(c))
            b_out[h] = jnp.exp(lb * (c - ii))
            b_upd[h] = jnp.exp(lb * ii)
            b_all[h] = jnp.exp(lb * float(c))

    tn = (((0,), (0,)), ((), ()))
    nt = (((1,), (1,)), ((), ()))
    for h in range(RET_HEADS):
        sl = slice(h * RET_DH, (h + 1) * RET_DH)
        q, k, v = qi_ref[:, sl], ki_ref[:, sl], vi_ref[:, sl]
        s = lax.dot_general(q, k, nt, preferred_element_type=_f32) * dmask[h]
        lhs = jnp.concatenate([s.astype(MXU_DTYPE), (q.astype(_f32) * f_out[h]).astype(MXU_DTYPE)], axis=1)
        rhs = jnp.concatenate([v, sf[h].astype(MXU_DTYPE)], axis=0)
        ya_ref[:, sl] = jnp.dot(lhs, rhs, preferred_element_type=_f32)
        sf[h] = f_all[h] * sf[h] + lax.dot_general(
            (k.astype(_f32) * f_upd[h]).astype(MXU_DTYPE), v, tn, preferred_element_type=_f32)
        q, k, v = qj_ref[:, sl], kj_ref[:, sl], vj_ref[:, sl]
        yb_ref[:, sl] = jnp.dot((q.astype(_f32) * b_out[h]).astype(MXU_DTYPE), sb[h].astype(MXU_DTYPE),
                                preferred_element_type=_f32)
        sb[h] = b_all[h] * sb[h] + lax.dot_general(
            (k.astype(_f32) * b_upd[h]).astype(MXU_DTYPE), v, tn, preferred_element_type=_f32)

    @pl.when(i == pl.num_programs(0) - 1)
    def _():
        finf_ref[...] = sf[...]
        finb_ref[...] = sb[...]


def _scan_ret(q, k, v, logit, s0f, s0b):
    L = q.shape[0]
    nc = L // CHUNK
    logit_b = jnp.broadcast_to(logit.astype(_f32)[:, :, None, None], (2, RET_HEADS, SUBLANES, LANES))
    fw = pl.BlockSpec((CHUNK, RET_W), lambda i: (i, 0))
    bw = pl.BlockSpec((CHUNK, RET_W), lambda i: (nc - 1 - i, 0))
    st = pl.BlockSpec((RET_HEADS, RET_DH, RET_DH), lambda i: (0, 0, 0))
    yshape = jax.ShapeDtypeStruct((L, RET_W), _f32)
    sshape = jax.ShapeDtypeStruct((RET_HEADS, RET_DH, RET_DH), _f32)
    tile = pltpu.VMEM((RET_HEADS, CHUNK, CHUNK), _f32)
    return pl.pallas_call(
        _scan_ret_kernel,
        grid=(nc,),
        in_specs=[pl.BlockSpec((2, RET_HEADS, SUBLANES, LANES), lambda i: (0, 0, 0, 0)),
                  fw, fw, fw, bw, bw, bw, st, st],
        out_specs=[fw, bw, st, st],
        out_shape=[yshape, yshape, sshape, sshape],
        scratch_shapes=[pltpu.VMEM((RET_HEADS, RET_DH, RET_DH), _f32)] * 2 + [tile] * 7,
        compiler_params=_cparams("arbitrary"),
        name="scan_ret",
    )(logit_b, q, k, v, q, k, v, s0f, s0b)


def _shift_rows(x, prev_row, next_row):
    r = x.shape[0]
    rid = lax.broadcasted_iota(jnp.int32, x.shape, 0)
    up = jnp.where(rid == 0, prev_row, pltpu.roll(x, 1, axis=0))
    dn = jnp.where(rid == r - 1, next_row, pltpu.roll(x, r - 1, axis=0))
    return up, dn


HALO = 16


def _conv3(x_ref, prev_ref, next_ref, w_ref, b_ref, has_prev, has_next):
    x = x_ref[...].astype(_f32)
    prev_row = prev_ref[...].astype(_f32)[HALO - 1:HALO, :] * has_prev
    next_row = next_ref[...].astype(_f32)[0:1, :] * has_next
    up, dn = _shift_rows(x, prev_row, next_row)
    return up * w_ref[0:1, :] + x * w_ref[1:2, :] + dn * w_ref[2:3, :] + b_ref[...]


def _halo_specs(tr, L, width, col):
    nb = tr // HALO
    last = L // HALO - 1
    cb = col // width
    return [pl.BlockSpec((tr, width), lambda i: (i, cb)),
            pl.BlockSpec((HALO, width), lambda i: (jnp.maximum(i * nb - 1, 0), cb)),
            pl.BlockSpec((HALO, width), lambda i: (jnp.minimum((i + 1) * nb, last), cb))]


def _prep_ssd_kernel(x_ref, xp_ref, xn_ref, b_ref, bp_ref, bn_ref, c_ref, cp_ref, cn_ref, dt_ref,
                     w_ref, cb_ref, dtb_ref, alog_ref, co_ref, bo_ref, xo_ref, pack_ref):
    i = pl.program_id(0)
    has_prev = (i > 0).astype(_f32)
    has_next = (i < pl.num_programs(0) - 1).astype(_f32)

    def conv(lo, hi, t_ref, p_ref, n_ref, o_ref):
        y = _conv3(t_ref, p_ref, n_ref, w_ref.at[:, lo:hi], cb_ref.at[:, lo:hi], has_prev, has_next)
        o_ref[...] = _silu(y).astype(o_ref.dtype)

    conv(0, SSM_W, x_ref, xp_ref, xn_ref, xo_ref)
    conv(SSM_W, SSM_W + SSM_GN, b_ref, bp_ref, bn_ref, bo_ref)
    conv(SSM_W + SSM_GN, SSM_W + 2 * SSM_GN, c_ref, cp_ref, cn_ref, co_ref)
    z = dt_ref[...] + dtb_ref[...]
    dt = jnp.maximum(z, 0.0) + jnp.log1p(jnp.exp(-jnp.abs(z)))
    a = dt * (-jnp.exp(alog_ref[...]))
    c = CHUNK
    ii = lax.broadcasted_iota(jnp.int32, (c, c), 0)
    jj = lax.broadcasted_iota(jnp.int32, (c, c), 1)
    lower = (jj <= ii).astype(_f32)
    upper = (jj >= ii).astype(_f32)
    lane = lax.broadcasted_iota(jnp.int32, (c, LANES), 1)
    dt_sh = pltpu.roll(dt, 2 * SSM_HEADS, axis=1)
    for n in range(x_ref.shape[0] // c):
        rs = slice(n * c, (n + 1) * c)
        pre = jnp.dot(lower, a[rs], precision=HIGHEST, preferred_element_type=_f32)
        suf = jnp.dot(upper, a[rs], precision=HIGHEST, preferred_element_type=_f32)
        pack_ref[rs, :] = jnp.where(lane < SSM_HEADS, pre,
                                    jnp.where(lane < 2 * SSM_HEADS, suf, dt_sh[rs]))


def _prep_ssd(pa, pb, L, conv_w, conv_b, dt_bias, a_log):
    tr = 256
    w = jnp.pad(conv_w.astype(_f32), ((0, SUBLANES - 3), (0, 0)))
    lanes = lambda t: jnp.pad(t.astype(_f32).reshape(1, 2 * SSM_HEADS), ((0, 0), (0, LANES - 2 * SSM_HEADS)))
    wd = SSM_W + 2 * SSM_GN
    row = lambda c: pl.BlockSpec((tr, c), lambda i: (i, 0))
    return pl.pallas_call(
        _prep_ssd_kernel,
        grid=(L // tr,),
        in_specs=_halo_specs(tr, L, SSM_W, B_SX) + _halo_specs(tr, L, SSM_GN, B_SB)
        + _halo_specs(tr, L, SSM_GN, B_SC) + [
            pl.BlockSpec((tr, LANES), lambda i: (i, O_SSM_DT // LANES)),
            pl.BlockSpec((SUBLANES, wd), lambda i: (0, 0)),
            pl.BlockSpec((1, wd), lambda i: (0, 0)),
            pl.BlockSpec((1, LANES), lambda i: (0, 0)),
            pl.BlockSpec((1, LANES), lambda i: (0, 0))],
        out_specs=[row(SSM_GN), row(SSM_GN), row(SSM_W), row(LANES)],
        out_shape=[jax.ShapeDtypeStruct((L, SSM_GN), MXU_DTYPE),
                   jax.ShapeDtypeStruct((L, SSM_GN), MXU_DTYPE),
                   jax.ShapeDtypeStruct((L, SSM_W), MXU_DTYPE),
                   jax.ShapeDtypeStruct((L, LANES), _f32)],
        compiler_params=_cparams("parallel"),
        name="prep_ssd",
    )(*([pb] * 9), pa, w, conv_b.astype(_f32)[None], lanes(dt_bias), lanes(a_log))


def _scan_ssd_kernel(ci_ref, bi_ref, xi_ref, pi_ref, cj_ref, bj_ref, xj_ref, pj_ref, dskip_ref,
                     s0f_ref, s0b_ref, ya_ref, yb_ref, finf_ref, finb_ref, sf, sb):
    i = pl.program_id(0)
    c = CHUNK
    H = SSM_HEADS

    @pl.when(i == 0)
    def _():
        sf[...] = s0f_ref[...]
        sb[...] = s0b_ref[...]

    tn = (((0,), (0,)), ((), ()))
    nt = (((1,), (1,)), ((), ()))
    ii = lax.broadcasted_iota(jnp.int32, (c, c), 0)
    jj = lax.broadcasted_iota(jnp.int32, (c, c), 1)
    low = lax.broadcasted_iota(jnp.int32, (c, LANES), 1) < SSM_HEADDIM
    low2 = lax.broadcasted_iota(jnp.int32, (2 * SSM_STATE, LANES), 1) < SSM_HEADDIM
    diag = (lax.broadcasted_iota(jnp.int32, (2 * SSM_STATE, LANES), 0) < SSM_STATE) == low2
    pi = pi_ref[...]
    pit = pi.T
    pj = pj_ref[...]
    ei = jnp.exp(jnp.minimum(pi, 0.0))
    ej = jnp.exp(jnp.minimum(pj, 0.0))
    tot_i, tot_j = pi[c - 1:c, :], pj[0:1, :]
    wi = jnp.exp(jnp.minimum(tot_i - pi, 0.0)) * pltpu.roll(pi, LANES - 2 * H, axis=1)
    wj = jnp.exp(jnp.minimum(tot_j - pj, 0.0)) * pltpu.roll(pj, LANES - 2 * H, axis=1)
    eti, etj = jnp.exp(jnp.minimum(tot_i, 0.0)), jnp.exp(jnp.minimum(tot_j, 0.0))
    colb = lambda t, k: jnp.broadcast_to(t[:, k:k + 1], (c, LANES))
    for g in range(SSM_GROUPS):
        gs = slice(g * SSM_STATE, (g + 1) * SSM_STATE)
        ci, bi = ci_ref[:, gs], bi_ref[:, gs]
        cj, bj = cj_ref[:, gs], bj_ref[:, gs]
        cb = lax.dot_general(ci, bi, nt, preferred_element_type=_f32)
        ci32, bi32, cj32, bj32 = (t.astype(_f32) for t in (ci, bi, cj, bj))
        for pp in range(SSM_HPG // 2):
            q = g * (SSM_HPG // 2) + pp
            heads = (2 * q, 2 * q + 1)
            xs = slice(q * LANES, (q + 1) * LANES)
            x = xi_ref[:, xs]
            x32 = x.astype(_f32)
            scores, cw, bw = [], [], []
            for h in heads:
                row = lambda o: pit[o + h:o + h + 1, :]
                mf = jnp.where(ii >= jj, jnp.exp(jnp.minimum(colb(pi, h) - row(0), 0.0)), 0.0) * row(2 * H)
                mb = jnp.where(jj >= ii, jnp.exp(jnp.minimum(colb(pi, H + h) - row(H), 0.0)), 0.0) * row(3 * H)
                scores.append((cb * (mf + mb)).astype(MXU_DTYPE))
                cw.append((ci32 * colb(ei, h)).astype(MXU_DTYPE))
                bw.append((bi32 * colb(wi, h)).astype(MXU_DTYPE))
            xa = jnp.where(low, x32, 0.0).astype(MXU_DTYPE)
            xb = jnp.where(low, 0.0, x32).astype(MXU_DTYPE)
            lhs = jnp.concatenate(scores + cw, axis=1)
            rhs = jnp.concatenate([xa, xb, sf[q].astype(MXU_DTYPE)], axis=0)
            y = jnp.dot(lhs, rhs, preferred_element_type=_f32)
            ya_ref[:, xs] = y + dskip_ref[:, xs] * x32
            upd = lax.dot_general(jnp.concatenate(bw, axis=1), x, tn, preferred_element_type=_f32)
            dec = jnp.where(low2, eti[0:1, heads[0]:heads[0] + 1], eti[0:1, heads[1]:heads[1] + 1])
            sf[q] = dec * sf[q] + jnp.where(diag, upd, 0.0)
            x = xj_ref[:, xs]
            cw = [(cj32 * colb(ej, H + h)).astype(MXU_DTYPE) for h in heads]
            bw = [(bj32 * colb(wj, H + h)).astype(MXU_DTYPE) for h in heads]
            yb_ref[:, xs] = jnp.dot(jnp.concatenate(cw, axis=1), sb[q].astype(MXU_DTYPE),
                                    preferred_element_type=_f32)
            upd = lax.dot_general(jnp.concatenate(bw, axis=1), x, tn, preferred_element_type=_f32)
            dec = jnp.where(low2, etj[0:1, H + heads[0]:H + heads[0] + 1], etj[0:1, H + heads[1]:H + heads[1] + 1])
            sb[q] = dec * sb[q] + jnp.where(diag, upd, 0.0)

    @pl.when(i == pl.num_programs(0) - 1)
    def _():
        finf_ref[...] = sf[...]
        finb_ref[...] = sb[...]


def _pair_states(s):
    s = s.reshape(SSM_HEADS // 2, 2, SSM_STATE, SSM_HEADDIM)
    z = jnp.zeros_like(s[:, 0])
    return jnp.concatenate([jnp.concatenate([s[:, 0], z], -1), jnp.concatenate([z, s[:, 1]], -1)], 1)


def _unpair_states(s):
    top, bot = s[:, :SSM_STATE, :SSM_HEADDIM], s[:, SSM_STATE:, SSM_HEADDIM:]
    return jnp.stack([top, bot], 1).reshape(SSM_HEADS, SSM_STATE, SSM_HEADDIM)


def _scan_ssd(cs, bs, xs, pack, d_skip, s0f, s0b):
    L = xs.shape[0]
    nc = L // CHUNK
    dvec = jnp.repeat(d_skip.astype(_f32), SSM_HEADDIM)[None]
    fw = lambda w: pl.BlockSpec((CHUNK, w), lambda i: (i, 0))
    bw = lambda w: pl.BlockSpec((CHUNK, w), lambda i: (nc - 1 - i, 0))
    pshape = (SSM_HEADS // 2, 2 * SSM_STATE, 2 * SSM_HEADDIM)
    st = pl.BlockSpec(pshape, lambda i: (0, 0, 0))
    yshape = jax.ShapeDtypeStruct((L, SSM_W), _f32)
    sshape = jax.ShapeDtypeStruct(pshape, _f32)
    ya, yb, fin_f, fin_b = pl.pallas_call(
        _scan_ssd_kernel,
        grid=(nc,),
        in_specs=[fw(SSM_GN), fw(SSM_GN), fw(SSM_W), fw(LANES), bw(SSM_GN), bw(SSM_GN), bw(SSM_W), bw(LANES),
                  pl.BlockSpec((1, SSM_W), lambda i: (0, 0)), st, st],
        out_specs=[fw(SSM_W), bw(SSM_W), st, st],
        out_shape=[yshape, yshape, sshape, sshape],
        scratch_shapes=[pltpu.VMEM(pshape, _f32)] * 2,
        compiler_params=_cparams("arbitrary"),
        name="scan_ssd",
    )(cs, bs, xs, pack, cs, bs, xs, pack, dvec, _pair_states(s0f), _pair_states(s0b))
    return ya, yb, _unpair_states(fin_f), _unpair_states(fin_b)


def _merge_kernel(ra_ref, rb_ref, sa_ref, sb_ref, gr_ref, gs_ref, nw_ref, yr_ref, ys_ref):
    for h in range(RET_HEADS):
        sl = slice(h * RET_DH, (h + 1) * RET_DH)
        y = _ln_rows(ra_ref[:, sl] + rb_ref[:, sl])
        yr_ref[:, sl] = (y * _silu(gr_ref[:, sl].astype(_f32))).astype(yr_ref.dtype)
    gw = SSM_W // SSM_GROUPS
    for g in range(SSM_GROUPS):
        sl = slice(g * gw, (g + 1) * gw)
        y = (sa_ref[:, sl] + sb_ref[:, sl]) * _silu(gs_ref[:, sl].astype(_f32))
        y = y * lax.rsqrt(jnp.mean(y * y, -1, keepdims=True) + LN_EPS)
        ys_ref[:, sl] = (y * nw_ref[:, sl]).astype(ys_ref.dtype)


def _gate_spec(tr, n):
    return pl.BlockSpec((tr, BR_W), lambda i: (i, B_GATE // BR_W + n))


def _merge(ra, rb, sa, sb_, pb, norm_w):
    L = ra.shape[0]
    tr = 256
    row = pl.BlockSpec((tr, BR_W), lambda i: (i, 0))
    shp = jax.ShapeDtypeStruct((L, BR_W), MXU_DTYPE)
    return pl.pallas_call(
        _merge_kernel,
        grid=(L // tr,),
        in_specs=[row, row, row, row, _gate_spec(tr, 1), _gate_spec(tr, 3),
                  pl.BlockSpec((1, BR_W), lambda i: (0, 0))],
        out_specs=[row, row],
        out_shape=[shp, shp],
        compiler_params=_cparams("parallel"),
        name="merge",
    )(ra, rb, sa, sb_, pb, pb, norm_w.astype(_f32)[None])


def _pool_kernel(x_ref, prev_ref, next_ref, g_ref, pw_ref, ps_ref, o_ref, *, L):
    i = pl.program_id(0)
    t = x_ref.shape[0]
    halo = HALO
    has_prev = (i > 0).astype(_f32)
    has_next = (i < pl.num_programs(0) - 1).astype(_f32)
    pos = i * t + lax.broadcasted_iota(jnp.int32, (t, 1), 0)
    for g, win in enumerate(POOL_WINDOWS):
        sl = slice(g * POOL_GROUP, (g + 1) * POOL_GROUP)
        x = x_ref[:, sl].astype(_f32)
        s = jnp.concatenate([prev_ref[:, sl].astype(_f32) * has_prev, x,
                             next_ref[:, sl].astype(_f32) * has_next], axis=0)
        rows = t + 2 * halo
        width = 1
        while width < win:
            s = s + pltpu.roll(s, rows - width, axis=0)
            width *= 2
        off = halo - win // 2
        if off:
            s = pltpu.roll(s, rows - off, axis=0)
        cnt = jnp.minimum(pos + win // 2, L) - jnp.maximum(pos - win // 2, 0)
        d = s[:t] / cnt.astype(_f32) - x
        y = jnp.dot(d.astype(MXU_DTYPE), pw_ref[g], preferred_element_type=_f32)
        o_ref[:, sl] = (y * ps_ref[:, sl] * _silu(g_ref[:, sl].astype(_f32))).astype(o_ref.dtype)


def _pool(pb, L, pool_w, pool_scale):
    tr = 256
    return pl.pallas_call(
        functools.partial(_pool_kernel, L=L),
        grid=(L // tr,),
        in_specs=_halo_specs(tr, L, POOL_W, B_POOL) + [
            _gate_spec(tr, 2),
            pl.BlockSpec((POOL_GROUPS, POOL_GROUP, POOL_GROUP), lambda i: (0, 0, 0)),
            pl.BlockSpec((1, POOL_W), lambda i: (0, 0))],
        out_specs=pl.BlockSpec((tr, POOL_W), lambda i: (i, 0)),
        out_shape=jax.ShapeDtypeStruct((L, POOL_W), MXU_DTYPE),
        compiler_params=_cparams("parallel"),
        name="pool",
    )(pb, pb, pb, pb, pool_w.astype(MXU_DTYPE), pool_scale.astype(_f32)[None])


def _prep_hy_kernel(v_ref, vp_ref, vn_ref, x0_ref, x0p_ref, x0n_ref, x1_ref, x1p_ref, x1n_ref,
                    g_ref, w_ref, b_ref, wo_ref, x0g_ref):
    i = pl.program_id(0)
    has_prev = (i > 0).astype(_f32)
    has_next = (i < pl.num_programs(0) - 1).astype(_f32)

    def conv(n, x_ref, p_ref, n_ref):
        sl = slice(n * HY_W, (n + 1) * HY_W)
        return _conv3(x_ref, p_ref, n_ref, w_ref.at[:, sl], b_ref.at[:, sl], has_prev, has_next)

    hv = conv(0, v_ref, vp_ref, vn_ref)
    hx0 = conv(1, x0_ref, x0p_ref, x0n_ref)
    hx1 = conv(2, x1_ref, x1p_ref, x1n_ref)
    wo_ref[...] = hx1 * hv
    x0g_ref[...] = hx0 * _silu(g_ref[...].astype(_f32))


def _prep_hy(pb, L, conv_w, conv_b):
    tr = 256
    w = jnp.pad(conv_w.astype(_f32), ((0, SUBLANES - 3), (0, 0)))
    row = pl.BlockSpec((tr, HY_W), lambda i: (i, 0))
    shp = jax.ShapeDtypeStruct((L, HY_W), _f32)
    secs = sum((_halo_specs(tr, L, HY_W, B_HY + n * HY_W) for n in range(3)), [])
    return pl.pallas_call(
        _prep_hy_kernel,
        grid=(L // tr,),
        in_specs=secs + [_gate_spec(tr, 0),
                         pl.BlockSpec((SUBLANES, 3 * HY_W), lambda i: (0, 0)),
                         pl.BlockSpec((1, 3 * HY_W), lambda i: (0, 0))],
        out_specs=[row, row],
        out_shape=[shp, shp],
        compiler_params=_cparams("parallel"),
        name="prep_hy",
    )(*([pb] * 10), w, conv_b.astype(_f32)[None])


def _split(x):
    hi = x.astype(MXU_DTYPE)
    return hi, (x - hi.astype(_f32)).astype(MXU_DTYPE)


def _dot3(a_hi, a_lo, b):
    b_hi, b_lo = _split(b)
    d = lambda p, q: jnp.dot(p, q, preferred_element_type=_f32)
    return d(a_hi, b_hi) + (d(a_hi, b_lo) + d(a_lo, b_hi))


def _const_split(m):
    return _split(jnp.asarray(m, _f32))


def _filter_kernel(z_ref, w1_ref, b1_ref, w2_ref, b2_ref, w3hi_ref, w3lo_ref, freq_ref, delta_ref, o_ref, *, L):
    t = z_ref.shape[0]
    z = z_ref[...]
    dot = functools.partial(jnp.dot, precision=HIGHEST, preferred_element_type=_f32)
    freq = freq_ref[...]
    hdn = jnp.sin(freq * (dot(z, w1_ref[...]) + b1_ref[...]))
    hdn = jnp.sin(freq * (dot(hdn, w2_ref[...]) + b2_ref[...]))
    h_hi, h_lo = _split(hdn)
    d = lambda p, q: jnp.dot(p, q, preferred_element_type=_f32)
    filt = d(h_hi, w3hi_ref[...]) + (d(h_hi, w3lo_ref[...]) + d(h_lo, w3hi_ref[...]))
    n = pl.program_id(0) * t + lax.broadcasted_iota(jnp.int32, (t, 1), 0)
    o_ref[...] = jnp.where(n == L, 0.0, filt) * jnp.exp(-z[:, 0:1] * delta_ref[...])


def _hy_filter(L, lp):
    n = jnp.arange(2 * L)
    lag = jnp.minimum(jnp.where(n < L, n, 2 * L - n), L - 1).astype(_f32)[:, None]
    t = lag / (L - 1)
    w = 2.0 * math.pi * lag / L
    bands = jnp.linspace(1e-4, HY_BANDS - 1, HY_BANDS, dtype=_f32)[None, :]
    z = jnp.concatenate([t, jnp.cos(bands * w), -jnp.sin(bands * w)], axis=-1)
    emb = z.shape[1]
    z = jnp.pad(z, ((0, 0), (0, LANES - emb)))
    w1 = jnp.pad(lp['hy_w1'].astype(_f32), ((0, LANES - emb), (0, 0)))
    deltas = jnp.abs(jnp.linspace(HY_MIN_DECAY, HY_MAX_DECAY, HY_W, dtype=_f32))[None]
    tr = min(512, L)
    w3hi, w3lo = _split(lp['hy_w3'].astype(_f32))
    full = lambda a: pl.BlockSpec(a.shape, lambda i: (0,) * a.ndim)
    half = pl.BlockSpec((w3hi.shape[0], HY_W), lambda i: (0, i // (L // tr)))
    pre = [w1, lp['hy_b1'].astype(_f32)[None], lp['hy_w2'].astype(_f32), lp['hy_b2'].astype(_f32)[None]]
    post = [lp['hy_freq'].astype(_f32)[None], deltas]
    return pl.pallas_call(
        functools.partial(_filter_kernel, L=L),
        grid=(2 * L // tr,),
        in_specs=[pl.BlockSpec((tr, LANES), lambda i: (i, 0))] + [full(a) for a in pre] + [half, half]
        + [full(a) for a in post],
        out_specs=pl.BlockSpec((tr, HY_W), lambda i: (i, 0)),
        out_shape=jax.ShapeDtypeStruct((2 * L, HY_W), _f32),
        compiler_params=_cparams("parallel"),
        name="hy_filter",
    )(z, *pre, w3hi, w3lo, *post)


def _cs(num, den):
    ang = 2.0 * np.pi * (np.asarray(num, np.int64) % den) / den
    return np.cos(ang), np.sin(ang)


FFT_N2 = LANES


def _fft_rows(n1):
    return -(-(n1 // 2 + 1) // SUBLANES) * SUBLANES


def _fft_first_kernel(x_ref, mhi_ref, mlo_ref, o_ref):
    o_ref[...] = _dot3(mhi_ref[...], mlo_ref[...], x_ref[...])


def _fft_first(x2, n1):
    rows, cols = x2.shape
    kp = _fft_rows(n1)
    c, s = _cs(np.outer(np.arange(kp), np.arange(rows)), n1)
    mhi, mlo = _const_split(np.concatenate([c, -s], 0))
    tcol = 4096
    return pl.pallas_call(
        _fft_first_kernel,
        grid=(cols // tcol,),
        in_specs=[pl.BlockSpec((rows, tcol), lambda j: (0, j)),
                  pl.BlockSpec((2 * kp, rows), lambda j: (0, 0)),
                  pl.BlockSpec((2 * kp, rows), lambda j: (0, 0))],
        out_specs=pl.BlockSpec((2 * kp, tcol), lambda j: (0, j)),
        out_shape=jax.ShapeDtypeStruct((2 * kp, cols), _f32),
        compiler_params=_cparams("parallel"),
        name="fft_first",
    )(x2, mhi, mlo)


def _fft_mid_kernel(*refs, conv):
    if conv:
        a_ref, twr_ref, twi_ref, fhi_ref, flo_ref, h_ref, ghi_ref, glo_ref, o_ref = refs
    else:
        a_ref, twr_ref, twi_ref, fhi_ref, flo_ref, o_ref = refs
    n2 = FFT_N2
    reps = a_ref.shape[-1] // LANES
    twr = jnp.concatenate([twr_ref[0]] * reps, axis=1)
    twi = jnp.concatenate([twi_ref[0]] * reps, axis=1)
    ar, ai = a_ref[0, 0], a_ref[1, 0]
    x = _dot3(fhi_ref[...], flo_ref[...],
              jnp.concatenate([ar * twr - ai * twi, ar * twi + ai * twr], axis=0))
    if not conv:
        o_ref[0, 0] = x[:n2]
        o_ref[1, 0] = x[n2:]
        return
    xr, xi = x[:n2], x[n2:]
    hr, hi = h_ref[0, 0], h_ref[1, 0]
    b = _dot3(ghi_ref[...], glo_ref[...],
              jnp.concatenate([xr * hr - xi * hi, xr * hi + xi * hr], axis=0))
    br, bi = b[:n2], b[n2:]
    o_ref[0, 0] = br * twr + bi * twi
    o_ref[1, 0] = bi * twr - br * twi


def _fft_mid(a, n1, hf=None):
    n2 = FFT_N2
    kp, ch = a.shape[1], a.shape[-1]
    n = n1 * n2
    idx = jnp.arange(kp)[:, None] * jnp.arange(n2)[None, :]
    ang = (2.0 * math.pi / n) * (idx % n).astype(_f32)
    twr = jnp.broadcast_to(jnp.cos(ang)[:, :, None], (kp, n2, LANES))
    twi = jnp.broadcast_to(-jnp.sin(ang)[:, :, None], (kp, n2, LANES))
    c, s = _cs(np.outer(np.arange(n2), np.arange(n2)), n2)
    fhi, flo = _const_split(np.block([[c, s], [-s, c]]))
    blk = pl.BlockSpec((2, 1, n2, ch), lambda k: (0, k, 0, 0))
    tw = pl.BlockSpec((1, n2, LANES), lambda k: (k, 0, 0))
    mat = pl.BlockSpec((2 * n2, 2 * n2), lambda k: (0, 0))
    args, specs = [a, twr, twi, fhi, flo], [blk, tw, tw, mat, mat]
    if hf is not None:
        ghi, glo = _const_split(np.block([[c, -s], [s, c]]))
        args += [hf, ghi, glo]
        specs += [blk, mat, mat]
    return pl.pallas_call(
        functools.partial(_fft_mid_kernel, conv=hf is not None),
        grid=(kp,),
        in_specs=specs,
        out_specs=blk,
        out_shape=jax.ShapeDtypeStruct(a.shape, _f32),
        compiler_params=_cparams("parallel"),
        name="fft_mid",
    )(*args)


def _fft_last_kernel(c_ref, mhi_ref, mlo_ref, w_ref, x0g_ref, bias_ref, o_ref):
    y = _dot3(mhi_ref[...], mlo_ref[...], c_ref[...])
    o_ref[...] = (x0g_ref[...] * (y + w_ref[...] * bias_ref[...])).astype(o_ref.dtype)


def _fft_last(c2, n1, w2, x0g2, bias):
    rows, cols = w2.shape
    kp = _fft_rows(n1)
    n = n1 * FFT_N2
    c, s = _cs(np.outer(np.arange(rows), np.arange(kp)), n1)
    k1 = np.arange(kp)
    mult = np.where((k1 == 0) | (k1 == n1 // 2), 1.0, np.where(k1 < n1 // 2, 2.0, 0.0))
    mhi, mlo = _const_split(np.concatenate([c * mult, -s * mult], 1) / n)
    tcol = 4096
    bias_t = jnp.tile(bias.astype(_f32), tcol // bias.shape[0])[None]
    blk = pl.BlockSpec((rows, tcol), lambda j: (0, j))
    return pl.pallas_call(
        _fft_last_kernel,
        grid=(cols // tcol,),
        in_specs=[pl.BlockSpec((2 * kp, tcol), lambda j: (0, j)),
                  pl.BlockSpec((rows, 2 * kp), lambda j: (0, 0)),
                  pl.BlockSpec((rows, 2 * kp), lambda j: (0, 0)),
                  blk, blk, pl.BlockSpec((1, tcol), lambda j: (0, 0))],
        out_specs=blk,
        out_shape=jax.ShapeDtypeStruct((rows, cols), MXU_DTYPE),
        compiler_params=_cparams("parallel"),
        name="fft_last",
    )(c2, mhi, mlo, w2, x0g2, bias_t)


def _hy_small_kernel(w_ref, buf_ref, x0g_ref, bias_ref, fwhi_ref, fwlo_ref, fbhi_ref, fblo_ref,
                     ihi_ref, ilo_ref, o_ref):
    n = buf_ref.shape[0]
    w = w_ref[...]
    wf = _dot3(fwhi_ref[...], fwlo_ref[...], w)
    hf = _dot3(fbhi_ref[...], fblo_ref[...], buf_ref[...])
    wr, wi, hr, hi = wf[:n], wf[n:], hf[:n], hf[n:]
    y = _dot3(ihi_ref[...], ilo_ref[...], jnp.concatenate([wr * hr - wi * hi, wr * hi + wi * hr], axis=0))
    o_ref[...] = (x0g_ref[...] * (y + w * bias_ref[...])).astype(o_ref.dtype)


def _hy_conv_small(w, buf, x0g, bias):
    L, ch = w.shape
    n = 2 * L
    tc = 256
    c, s = _cs(np.outer(np.arange(n), np.arange(n)), n)
    fb = np.concatenate([c, -s], 0)
    mats = [*_const_split(fb[:, :L]), *_const_split(fb),
            *_const_split(np.concatenate([c[:L], -s[:L]], 1) / n)]
    col = lambda r: pl.BlockSpec((r, tc), lambda j: (0, j))
    return pl.pallas_call(
        _hy_small_kernel,
        grid=(ch // tc,),
        in_specs=[col(L), col(n), col(L), col(1)] + [pl.BlockSpec(m.shape, lambda j: (0, 0)) for m in mats],
        out_specs=col(L),
        out_shape=jax.ShapeDtypeStruct((L, ch), MXU_DTYPE),
        compiler_params=_cparams("parallel"),
        name="hy_conv_small",
    )(w, buf, x0g, bias.astype(_f32)[None], *mats)


def _hy_conv(w, buf, x0g, bias):
    L, ch = w.shape
    if L < 512:
        return _hy_conv_small(w, buf, x0g, bias)
    n1 = 2 * L // FFT_N2
    kp = _fft_rows(n1)
    cols = FFT_N2 * ch
    hf = _fft_mid(_fft_first(buf.reshape(n1, cols), n1).reshape(2, kp, FFT_N2, ch), n1)
    a = _fft_first(w.reshape(n1 // 2, cols), n1).reshape(2, kp, FFT_N2, ch)
    cc = _fft_mid(a, n1, hf)
    y = _fft_last(cc.reshape(2 * kp, cols), n1, w.reshape(n1 // 2, cols), x0g.reshape(n1 // 2, cols), bias)
    return y.reshape(L, ch)


def _zero_states():
    return (jnp.zeros((RET_HEADS, RET_DH, RET_DH), _f32), jnp.zeros((RET_HEADS, RET_DH, RET_DH), _f32),
            jnp.zeros((SSM_HEADS, SSM_STATE, SSM_HEADDIM), _f32),
            jnp.zeros((SSM_HEADS, SSM_STATE, SSM_HEADDIM), _f32))


def _recurrent(proj, L, lp, states, latent):
    pa, pb = proj
    q, k, v = _prep_ret(pa, pb, L, latent)
    ra, rb, ret_f, ret_b = _scan_ret(q, k, v, lp['ret_decay_logit'], states[0], states[1])
    cs, bs, xs, pack = _prep_ssd(pa, pb, L, lp['conv_ssm_w'], lp['conv_ssm_b'], lp['ssm_dt_bias'],
                                 lp['ssm_A_log'])
    sa, sb_, ssm_f, ssm_b = _scan_ssd(cs, bs, xs, pack, lp['ssm_D'], states[2], states[3])
    return (ra, rb, sa, sb_), (ret_f, ret_b, ssm_f, ssm_b)


def _mix(h, mod, lp, states, latent):
    L = h.shape[0]
    proj = _in_proj(h, mod[0], mod[1], lp)
    pb = proj[1]
    (ra, rb, sa, sb_), fin = _recurrent(proj, L, lp, states, latent)
    y_ret, y_ssm = _merge(ra, rb, sa, sb_, pb, lp['ssm_norm_w'])
    w, x0g = _prep_hy(pb, L, lp['conv_hy_w'], lp['conv_hy_b'])
    y_hy = _hy_conv(w, _hy_filter(L, lp), x0g, lp['hy_bias'])
    y_pool = _pool(pb, L, lp['pool_w'], lp['pool_scale'])
    out = _out_proj([y_hy, y_ret, y_pool, y_ssm], lp['w_out'], h, mod[2], lp['ln_g'], lp['ln_b'])
    return out, fin


def _context_states(hc, mod, lp):
    proj = _in_proj(hc, mod[0], mod[1], lp)
    _, fin = _recurrent(proj, hc.shape[0], lp, _zero_states(), False)
    return fin


def kernel(x, c, ctx, c_ctx, w_mod, b_mod, w_in, conv_ssm_w, conv_ssm_b, conv_hy_w, conv_hy_b,
           ret_decay_logit, ssm_A_log, ssm_dt_bias, ssm_D, ssm_norm_w, hy_w1, hy_b1, hy_w2, hy_b2,
           hy_w3, hy_freq, hy_bias, pool_w, pool_scale, w_out, ln_g, ln_b):
    assert x.shape[0] == 1
    h, hc = x[0], ctx[0]
    w_in_t = jnp.swapaxes(w_in, 1, 2)
    mods = _adaln(jnp.concatenate([c, c_ctx[None]], axis=0), w_mod, b_mod)
    for l in range(DEPTH):
        lp = {
            'w_in_t': _cast_layer(w_in_t, l, N_IN // 2, 512), 'w_out': _cast_layer(w_out, l, 1024, 2048),
            'conv_ssm_w': conv_ssm_w[l], 'conv_ssm_b': conv_ssm_b[l],
            'conv_hy_w': conv_hy_w[l], 'conv_hy_b': conv_hy_b[l], 'ret_decay_logit': ret_decay_logit[l],
            'ssm_A_log': ssm_A_log[l], 'ssm_dt_bias': ssm_dt_bias[l], 'ssm_D': ssm_D[l],
            'ssm_norm_w': ssm_norm_w[l], 'hy_w1': hy_w1[l], 'hy_b1': hy_b1[l], 'hy_w2': hy_w2[l],
            'hy_b2': hy_b2[l], 'hy_w3': hy_w3[l], 'hy_freq': hy_freq[l], 'hy_bias': hy_bias[l],
            'pool_w': pool_w[l], 'pool_scale': pool_scale[l], 'ln_g': ln_g[l], 'ln_b': ln_b[l],
        }
        mod = lambda r: tuple(mods[l, r:r + 1, n * D_MODEL:(n + 1) * D_MODEL] for n in range(3))
        if l < DEPTH - 1:
            hc_next, states = _mix(hc, mod(1), lp, _zero_states(), False)
        else:
            states = _context_states(hc, mod(1), lp)
            hc_next = hc
        h, _ = _mix(h, mod(0), lp, states, True)
        hc = hc_next
    return h[None]
```

```python
import functools
import math

import jax
import jax.numpy as jnp
import numpy as np
from jax import lax
from jax.experimental import pallas as pl
from jax.experimental.pallas import tpu as pltpu

D_MODEL = 4096
DEPTH = 2
GRID_W = 64
MIX_W = D_MODEL
BR_W = MIX_W // 4
HY_W = RET_W = POOL_W = SSM_W = BR_W
RET_HEADS = 8
RET_DH = RET_W // RET_HEADS
ROPE_BASE = 10000.0
SSM_HEADDIM = 64
SSM_HEADS = SSM_W // SSM_HEADDIM
SSM_GROUPS = 4
SSM_HPG = SSM_HEADS // SSM_GROUPS
SSM_STATE = 128
SSM_GN = SSM_GROUPS * SSM_STATE
CHUNK = 128
POOL_WINDOWS = (2, 4, 8, 16)
POOL_GROUPS = len(POOL_WINDOWS)
POOL_GROUP = POOL_W // POOL_GROUPS
HY_BANDS = 16
HY_TARGET = 1e-2
HY_FAST = 0.3
HY_SLOW = 1.5
HY_MIN_DECAY = math.log(HY_TARGET) / HY_SLOW
HY_MAX_DECAY = math.log(HY_TARGET) / HY_FAST
ALPHA = (2.0 * DEPTH) ** 0.25
LN_EPS = 1e-5

O_RET_K = 0
O_RET_V = O_RET_K + RET_W
O_SSM_DT = O_RET_V + RET_W
O_SSM_X = O_SSM_DT + 2 * SSM_HEADS
O_SSM_B = O_SSM_X + SSM_W
O_RET_Q = O_SSM_B + SSM_GN
O_SSM_C = O_RET_Q + RET_W
O_HY = O_SSM_C + SSM_GN
O_POOL = O_HY + 3 * HY_W
O_GATE = O_POOL + POOL_W
N_IN = O_GATE + MIX_W

LANES = 128
SUBLANES = 8
N_A = O_SSM_DT + LANES
B_SX = 0
B_SB = O_SSM_B - O_SSM_X
B_RQ = O_RET_Q - O_SSM_X
B_SC = O_SSM_C - O_SSM_X
B_HY = O_HY - O_SSM_X
B_POOL = O_POOL - O_SSM_X
B_GATE = O_GATE - O_SSM_X
N_B = N_IN - O_SSM_X

VMEM_LIMIT_BYTES = 56 * 1024 * 1024
MXU_DTYPE = jnp.bfloat16
HIGHEST = lax.Precision.HIGHEST

_f32 = jnp.float32


def _cparams(*sem, vmem=VMEM_LIMIT_BYTES):
    return pltpu.CompilerParams(dimension_semantics=sem, vmem_limit_bytes=vmem)


def _silu(x):
    return x * jax.nn.sigmoid(x)


def _ln_rows(z):
    mu = jnp.mean(z, -1, keepdims=True)
    zc = z - mu
    var = jnp.mean(zc * zc, -1, keepdims=True)
    return zc * lax.rsqrt(var + LN_EPS)


def _cast_kernel(w_ref, o_ref):
    o_ref[...] = w_ref[...].astype(o_ref.dtype)


def _cast_layer(w, l, tr, tc):
    _, r, c = w.shape
    assert r % tr == 0 and c % tc == 0
    return pl.pallas_call(
        _cast_kernel,
        grid=(r // tr, c // tc),
        in_specs=[pl.BlockSpec((1, tr, tc), lambda i, j: (l, i, j))],
        out_specs=pl.BlockSpec((1, tr, tc), lambda i, j: (0, i, j)),
        out_shape=jax.ShapeDtypeStruct((1, r, c), MXU_DTYPE),
        compiler_params=_cparams("parallel", "parallel"),
        name="cast_layer",
    )(w)[0]


def _adaln_kernel(c_ref, w_ref, b_ref, o_ref):
    tn = w_ref.shape[-1]
    outs = []
    for m in range(c_ref.shape[0]):
        x = _silu(c_ref[m])
        cols = [jnp.sum(x * w_ref[0, :, j * LANES:(j + 1) * LANES], axis=0, keepdims=True)
                for j in range(tn // LANES)]
        outs.append(jnp.concatenate(cols, axis=1))
    outs.append(jnp.zeros((SUBLANES - len(outs), tn), _f32))
    o_ref[0] = jnp.concatenate(outs, axis=0) + b_ref[0]


def _adaln(c_rows, w_mod, b_mod):
    dep, k, n = w_mod.shape
    r = c_rows.shape[0]
    tn = 512
    cb = jnp.broadcast_to(c_rows.astype(_f32)[:, :, None], (r, k, LANES))
    return pl.pallas_call(
        _adaln_kernel,
        grid=(dep, n // tn),
        in_specs=[pl.BlockSpec((r, k, LANES), lambda l, j: (0, 0, 0)),
                  pl.BlockSpec((1, k, tn), lambda l, j: (l, 0, j)),
                  pl.BlockSpec((1, 1, tn), lambda l, j: (l, 0, j))],
        out_specs=pl.BlockSpec((1, SUBLANES, tn), lambda l, j: (l, 0, j)),
        out_shape=jax.ShapeDtypeStruct((dep, SUBLANES, n), _f32),
        compiler_params=_cparams("parallel", "parallel"),
        name="adaln",
    )(cb, w_mod, b_mod[:, None, :])


def _matmul_nt_kernel(a_ref, b_ref, o_ref):
    o_ref[...] = lax.dot_general(a_ref[...], b_ref[...], (((1,), (1,)), ((), ())),
                                 preferred_element_type=_f32).astype(o_ref.dtype)


def _matmul_nt(a, wt, row0, n, tm, tn, out_dtype):
    m, k = a.shape
    assert m % tm == 0 and n % tn == 0 and row0 % 32 == 0 and tn % 32 == 0
    return pl.pallas_call(
        _matmul_nt_kernel,
        grid=(m // tm, n // tn),
        in_specs=[pl.BlockSpec((tm, k), lambda i, j: (i, 0)),
                  pl.BlockSpec((pl.Element(tn), pl.Element(k)),
                               lambda i, j: (pl.multiple_of(row0 + j * tn, 32), 0))],
        out_specs=pl.BlockSpec((tm, tn), lambda i, j: (i, j)),
        out_shape=jax.ShapeDtypeStruct((m, n), out_dtype),
        compiler_params=_cparams("parallel", "parallel"),
        name="matmul_nt",
    )(a, wt)


def _modulate_kernel(h_ref, shift_ref, scale_ref, o_ref):
    o_ref[...] = (_ln_rows(h_ref[...]) * (1.0 + scale_ref[...]) + shift_ref[...]).astype(o_ref.dtype)


def _modulate(h, shift, scale):
    L, d = h.shape
    tr = 256
    vec = pl.BlockSpec((1, d), lambda i: (0, 0))
    return pl.pallas_call(
        _modulate_kernel,
        grid=(L // tr,),
        in_specs=[pl.BlockSpec((tr, d), lambda i: (i, 0)), vec, vec],
        out_specs=pl.BlockSpec((tr, d), lambda i: (i, 0)),
        out_shape=jax.ShapeDtypeStruct((L, d), MXU_DTYPE),
        compiler_params=_cparams("parallel"),
        name="modulate",
    )(h, shift, scale)


def _in_proj(h, shift, scale, lp):
    L = h.shape[0]
    tm = 512 if L % 512 == 0 else 256
    u = _modulate(h, shift, scale)
    wt = lp['w_in_t']
    return _matmul_nt(u, wt, 0, N_A, tm, N_A, _f32), _matmul_nt(u, wt, O_SSM_X, N_B, tm, 1024, MXU_DTYPE)


def _out_proj_kernel(y0_ref, y1_ref, y2_ref, y3_ref, w_ref, h_ref, gate_ref, g_ref, b_ref, o_ref):
    out = None
    for n, y_ref in enumerate((y0_ref, y1_ref, y2_ref, y3_ref)):
        d = jnp.dot(y_ref[...], w_ref[n * BR_W:(n + 1) * BR_W, :], preferred_element_type=_f32)
        out = d if out is None else out + d
    z = ALPHA * h_ref[...] + gate_ref[...] * out
    o_ref[...] = _ln_rows(z) * g_ref[...] + b_ref[...]


OUT_PROJ_VMEM_BYTES = 60 * 1024 * 1024


def _out_proj(ys, w, h, gate, g, b):
    L, d = h.shape
    tm = 256
    lhs = pl.BlockSpec((tm, BR_W), lambda i: (i, 0))
    vec = pl.BlockSpec((1, d), lambda i: (0, 0))
    row = pl.BlockSpec((tm, d), lambda i: (i, 0))
    return pl.pallas_call(
        _out_proj_kernel,
        grid=(L // tm,),
        in_specs=[lhs] * len(ys) + [
            pl.BlockSpec(w.shape, lambda i: (0, 0), pipeline_mode=pl.Buffered(1)), row, vec, vec, vec],
        out_specs=row,
        out_shape=jax.ShapeDtypeStruct((L, d), _f32),
        compiler_params=_cparams("parallel", vmem=OUT_PROJ_VMEM_BYTES),
        name="out_proj",
    )(*ys, w, h, gate, g[None], b[None])


def _rope_tables(L):
    rows = L // GRID_W
    row = jnp.repeat(jnp.arange(rows), GRID_W).astype(_f32)
    col = jnp.tile(jnp.arange(GRID_W), rows).astype(_f32)
    nq = RET_DH // 4
    inv = ROPE_BASE ** (-jnp.arange(nq, dtype=_f32) / nq)
    ang = jnp.concatenate([row[:, None] * inv, col[:, None] * inv], -1)
    cos, sin = jnp.cos(ang), jnp.sin(ang)
    return jnp.concatenate([cos, cos], -1), jnp.concatenate([-sin, sin], -1)


def _prep_ret_kernel(qlo_ref, qhi_ref, k_ref, v_ref, cos_ref, sin_ref, qo_ref, ko_ref, vo_ref, *, rope):
    def rot(t):
        if not rope:
            return t
        return t * cos_ref[...] + pltpu.roll(t, RET_DH // 2, axis=1) * sin_ref[...]

    half = RET_HEADS // 2
    for h in range(RET_HEADS):
        sl = slice(h * RET_DH, (h + 1) * RET_DH)
        q_ref, qs = (qlo_ref, sl) if h < half else (qhi_ref, slice((h - half) * RET_DH, (h - half + 1) * RET_DH))
        qo_ref[:, sl] = rot(q_ref[:, qs].astype(_f32)).astype(qo_ref.dtype)
        ko_ref[:, sl] = rot(k_ref[:, sl] * (RET_DH ** -0.5)).astype(ko_ref.dtype)
    vo_ref[...] = v_ref[...].astype(vo_ref.dtype)


def _prep_ret(pa, pb, L, rope):
    tr = 256
    cos, sin = _rope_tables(L) if rope else (jnp.ones((L, LANES), _f32), jnp.zeros((L, LANES), _f32))
    hw = RET_W // 2
    qsp = lambda n: pl.BlockSpec((tr, hw), lambda i: (i, B_RQ // hw + n))
    sec = lambda c: pl.BlockSpec((tr, RET_W), lambda i: (i, c // RET_W))
    tab = pl.BlockSpec((tr, LANES), lambda i: (i, 0))
    out = pl.BlockSpec((tr, RET_W), lambda i: (i, 0))
    shp = jax.ShapeDtypeStruct((L, RET_W), MXU_DTYPE)
    return pl.pallas_call(
        functools.partial(_prep_ret_kernel, rope=rope),
        grid=(L // tr,),
        in_specs=[qsp(0), qsp(1), sec(O_RET_K), sec(O_RET_V), tab, tab],
        out_specs=[out, out, out],
        out_shape=[shp, shp, shp],
        compiler_params=_cparams("parallel"),
        name="prep_ret",
    )(pb, pb, pa, pa, cos, sin)


def _scan_ret_kernel(logit_ref, qi_ref, ki_ref, vi_ref, qj_ref, kj_ref, vj_ref, s0f_ref, s0b_ref,
                     ya_ref, yb_ref, finf_ref, finb_ref,
                     sf, sb, dmask, f_out, f_upd, f_all, b_out, b_upd, b_all):
    i = pl.program_id(0)
    c = CHUNK

    @pl.when(i == 0)
    def _():
        sf[...] = s0f_ref[...]
        sb[...] = s0b_ref[...]
        ii = lax.broadcasted_iota(jnp.int32, (c, c), 0).astype(_f32)
        jj = lax.broadcasted_iota(jnp.int32, (c, c), 1).astype(_f32)
        for h in range(RET_HEADS):
            def lg(d):
                x = logit_ref[d, h]
                v = -jnp.log1p(jnp.exp(-x))
                return jnp.broadcast_to(v[0:1, :], (c, c))
            lf, lb = lg(0), lg(1)
            dmask[h] = jnp.where(ii > jj, jnp.exp(lf * (ii - jj)),
                                 jnp.where(jj > ii, jnp.exp(lb * (jj - ii)), 2.0))
            f_out[h] = jnp.exp(lf * (ii + 1.0))
            f_upd[h] = jnp.exp(lf * (c - 1.0 - ii))
            f_all[h] = jnp.exp(lf * float(c))
            b_out[h] = jnp.exp(lb * (c - ii))
            b_upd[h] = jnp.exp(lb * ii)
            b_all[h] = jnp.exp(lb * float(c))

    tn = (((0,), (0,)), ((), ()))
    nt = (((1,), (1,)), ((), ()))
    for h in range(RET_HEADS):
        sl = slice(h * RET_DH, (h + 1) * RET_DH)
        q, k, v = qi_ref[:, sl], ki_ref[:, sl], vi_ref[:, sl]
        s = lax.dot_general(q, k, nt, preferred_element_type=_f32) * dmask[h]
        lhs = jnp.concatenate([s.astype(MXU_DTYPE), (q.astype(_f32) * f_out[h]).astype(MXU_DTYPE)], axis=1)
        rhs = jnp.concatenate([v, sf[h].astype(MXU_DTYPE)], axis=0)
        ya_ref[:, sl] = jnp.dot(lhs, rhs, preferred_element_type=_f32)
        sf[h] = f_all[h] * sf[h] + lax.dot_general(
            (k.astype(_f32) * f_upd[h]).astype(MXU_DTYPE), v, tn, preferred_element_type=_f32)
        q, k, v = qj_ref[:, sl], kj_ref[:, sl], vj_ref[:, sl]
        yb_ref[:, sl] = jnp.dot((q.astype(_f32) * b_out[h]).astype(MXU_DTYPE), sb[h].astype(MXU_DTYPE),
                                preferred_element_type=_f32)
        sb[h] = b_all[h] * sb[h] + lax.dot_general(
            (k.astype(_f32) * b_upd[h]).astype(MXU_DTYPE), v, tn, preferred_element_type=_f32)

    @pl.when(i == pl.num_programs(0) - 1)
    def _():
        finf_ref[...] = sf[...]
        finb_ref[...] = sb[...]


def _scan_ret(q, k, v, logit, s0f, s0b):
    L = q.shape[0]
    nc = L // CHUNK
    logit_b = jnp.broadcast_to(logit.astype(_f32)[:, :, None, None], (2, RET_HEADS, SUBLANES, LANES))
    fw = pl.BlockSpec((CHUNK, RET_W), lambda i: (i, 0))
    bw = pl.BlockSpec((CHUNK, RET_W), lambda i: (nc - 1 - i, 0))
    st = pl.BlockSpec((RET_HEADS, RET_DH, RET_DH), lambda i: (0, 0, 0))
    yshape = jax.ShapeDtypeStruct((L, RET_W), _f32)
    sshape = jax.ShapeDtypeStruct((RET_HEADS, RET_DH, RET_DH), _f32)
    tile = pltpu.VMEM((RET_HEADS, CHUNK, CHUNK), _f32)
    return pl.pallas_call(
        _scan_ret_kernel,
        grid=(nc,),
        in_specs=[pl.BlockSpec((2, RET_HEADS, SUBLANES, LANES), lambda i: (0, 0, 0, 0)),
                  fw, fw, fw, bw, bw, bw, st, st],
        out_specs=[fw, bw, st, st],
        out_shape=[yshape, yshape, sshape, sshape],
        scratch_shapes=[pltpu.VMEM((RET_HEADS, RET_DH, RET_DH), _f32)] * 2 + [tile] * 7,
        compiler_params=_cparams("arbitrary"),
        name="scan_ret",
    )(logit_b, q, k, v, q, k, v, s0f, s0b)


def _shift_rows(x, prev_row, next_row):
    r = x.shape[0]
    rid = lax.broadcasted_iota(jnp.int32, x.shape, 0)
    up = jnp.where(rid == 0, prev_row, pltpu.roll(x, 1, axis=0))
    dn = jnp.where(rid == r - 1, next_row, pltpu.roll(x, r - 1, axis=0))
    return up, dn


HALO = 16


def _conv3(x_ref, prev_ref, next_ref, w_ref, b_ref, has_prev, has_next):
    x = x_ref[...].astype(_f32)
    prev_row = prev_ref[...].astype(_f32)[HALO - 1:HALO, :] * has_prev
    next_row = next_ref[...].astype(_f32)[0:1, :] * has_next
    up, dn = _shift_rows(x, prev_row, next_row)
    return up * w_ref[0:1, :] + x * w_ref[1:2, :] + dn * w_ref[2:3, :] + b_ref[...]


def _halo_specs(tr, L, width, col):
    nb = tr // HALO
    last = L // HALO - 1
    cb = col // width
    return [pl.BlockSpec((tr, width), lambda i: (i, cb)),
            pl.BlockSpec((HALO, width), lambda i: (jnp.maximum(i * nb - 1, 0), cb)),
            pl.BlockSpec((HALO, width), lambda i: (jnp.minimum((i + 1) * nb, last), cb))]


def _prep_ssd_kernel(x_ref, xp_ref, xn_ref, b_ref, bp_ref, bn_ref, c_ref, cp_ref, cn_ref, dt_ref,
                     w_ref, cb_ref, dtb_ref, alog_ref, co_ref, bo_ref, xo_ref, pack_ref):
    i = pl.program_id(0)
    has_prev = (i > 0).astype(_f32)
    has_next = (i < pl.num_programs(0) - 1).astype(_f32)

    def conv(lo, hi, t_ref, p_ref, n_ref, o_ref):
        y = _conv3(t_ref, p_ref, n_ref, w_ref.at[:, lo:hi], cb_ref.at[:, lo:hi], has_prev, has_next)
        o_ref[...] = _silu(y).astype(o_ref.dtype)

    conv(0, SSM_W, x_ref, xp_ref, xn_ref, xo_ref)
    conv(SSM_W, SSM_W + SSM_GN, b_ref, bp_ref, bn_ref, bo_ref)
    conv(SSM_W + SSM_GN, SSM_W + 2 * SSM_GN, c_ref, cp_ref, cn_ref, co_ref)
    z = dt_ref[...] + dtb_ref[...]
    dt = jnp.maximum(z, 0.0) + jnp.log1p(jnp.exp(-jnp.abs(z)))
    a = dt * (-jnp.exp(alog_ref[...]))
    c = CHUNK
    ii = lax.broadcasted_iota(jnp.int32, (c, c), 0)
    jj = lax.broadcasted_iota(jnp.int32, (c, c), 1)
    lower = (jj <= ii).astype(_f32)
    upper = (jj >= ii).astype(_f32)
    lane = lax.broadcasted_iota(jnp.int32, (c, LANES), 1)
    dt_sh = pltpu.roll(dt, 2 * SSM_HEADS, axis=1)
    for n in range(x_ref.shape[0] // c):
        rs = slice(n * c, (n + 1) * c)
        pre = jnp.dot(lower, a[rs], precision=HIGHEST, preferred_element_type=_f32)
        suf = jnp.dot(upper, a[rs], precision=HIGHEST, preferred_element_type=_f32)
        pack_ref[rs, :] = jnp.where(lane < SSM_HEADS, pre,
                                    jnp.where(lane < 2 * SSM_HEADS, suf, dt_sh[rs]))


def _prep_ssd(pa, pb, L, conv_w, conv_b, dt_bias, a_log):
    tr = 256
    w = jnp.pad(conv_w.astype(_f32), ((0, SUBLANES - 3), (0, 0)))
    lanes = lambda t: jnp.pad(t.astype(_f32).reshape(1, 2 * SSM_HEADS), ((0, 0), (0, LANES - 2 * SSM_HEADS)))
    wd = SSM_W + 2 * SSM_GN
    row = lambda c: pl.BlockSpec((tr, c), lambda i: (i, 0))
    return pl.pallas_call(
        _prep_ssd_kernel,
        grid=(L // tr,),
        in_specs=_halo_specs(tr, L, SSM_W, B_SX) + _halo_specs(tr, L, SSM_GN, B_SB)
        + _halo_specs(tr, L, SSM_GN, B_SC) + [
            pl.BlockSpec((tr, LANES), lambda i: (i, O_SSM_DT // LANES)),
            pl.BlockSpec((SUBLANES, wd), lambda i: (0, 0)),
            pl.BlockSpec((1, wd), lambda i: (0, 0)),
            pl.BlockSpec((1, LANES), lambda i: (0, 0)),
            pl.BlockSpec((1, LANES), lambda i: (0, 0))],
        out_specs=[row(SSM_GN), row(SSM_GN), row(SSM_W), row(LANES)],
        out_shape=[jax.ShapeDtypeStruct((L, SSM_GN), MXU_DTYPE),
                   jax.ShapeDtypeStruct((L, SSM_GN), MXU_DTYPE),
                   jax.ShapeDtypeStruct((L, SSM_W), MXU_DTYPE),
                   jax.ShapeDtypeStruct((L, LANES), _f32)],
        compiler_params=_cparams("parallel"),
        name="prep_ssd",
    )(*([pb] * 9), pa, w, conv_b.astype(_f32)[None], lanes(dt_bias), lanes(a_log))


def _scan_ssd_kernel(ci_ref, bi_ref, xi_ref, pi_ref, cj_ref, bj_ref, xj_ref, pj_ref, dskip_ref,
                     s0f_ref, s0b_ref, ya_ref, yb_ref, finf_ref, finb_ref, sf, sb):
    i = pl.program_id(0)
    c = CHUNK
    H = SSM_HEADS

    @pl.when(i == 0)
    def _():
        sf[...] = s0f_ref[...]
        sb[...] = s0b_ref[...]

    tn = (((0,), (0,)), ((), ()))
    nt = (((1,), (1,)), ((), ()))
    ii = lax.broadcasted_iota(jnp.int32, (c, c), 0)
    jj = lax.broadcasted_iota(jnp.int32, (c, c), 1)
    low = lax.broadcasted_iota(jnp.int32, (c, LANES), 1) < SSM_HEADDIM
    low2 = lax.broadcasted_iota(jnp.int32, (2 * SSM_STATE, LANES), 1) < SSM_HEADDIM
    diag = (lax.broadcasted_iota(jnp.int32, (2 * SSM_STATE, LANES), 0) < SSM_STATE) == low2
    pi = pi_ref[...]
    pit = pi.T
    pj = pj_ref[...]
    ei = jnp.exp(jnp.minimum(pi, 0.0))
    ej = jnp.exp(jnp.minimum(pj, 0.0))
    tot_i, tot_j = pi[c - 1:c, :], pj[0:1, :]
    wi = jnp.exp(jnp.minimum(tot_i - pi, 0.0)) * pltpu.roll(pi, LANES - 2 * H, axis=1)
    wj = jnp.exp(jnp.minimum(tot_j - pj, 0.0)) * pltpu.roll(pj, LANES - 2 * H, axis=1)
    eti, etj = jnp.exp(jnp.minimum(tot_i, 0.0)), jnp.exp(jnp.minimum(tot_j, 0.0))
    colb = lambda t, k: jnp.broadcast_to(t[:, k:k + 1], (c, LANES))
    for g in range(SSM_GROUPS):
        gs = slice(g * SSM_STATE, (g + 1) * SSM_STATE)
        ci, bi = ci_ref[:, gs], bi_ref[:, gs]
        cj, bj = cj_ref[:, gs], bj_ref[:, gs]
        cb = lax.dot_general(ci, bi, nt, preferred_element_type=_f32)
        ci32, bi32, cj32, bj32 = (t.astype(_f32) for t in (ci, bi, cj, bj))
        for pp in range(SSM_HPG // 2):
            q = g * (SSM_HPG // 2) + pp
            heads = (2 * q, 2 * q + 1)
            xs = slice(q * LANES, (q + 1) * LANES)
            x = xi_ref[:, xs]
            x32 = x.astype(_f32)
            scores, cw, bw = [], [], []
            for h in heads:
                row = lambda o: pit[o + h:o + h + 1, :]
                mf = jnp.where(ii >= jj, jnp.exp(jnp.minimum(colb(pi, h) - row(0), 0.0)), 0.0) * row(2 * H)
                mb = jnp.where(jj >= ii, jnp.exp(jnp.minimum(colb(pi, H + h) - row(H), 0.0)), 0.0) * row(3 * H)
                scores.append((cb * (mf + mb)).astype(MXU_DTYPE))
                cw.append((ci32 * colb(ei, h)).astype(MXU_DTYPE))
                bw.append((bi32 * colb(wi, h)).astype(MXU_DTYPE))
            xa = jnp.where(low, x32, 0.0).astype(MXU_DTYPE)
            xb = jnp.where(low, 0.0, x32).astype(MXU_DTYPE)
            lhs = jnp.concatenate(scores + cw, axis=1)
            rhs = jnp.concatenate([xa, xb, sf[q].astype(MXU_DTYPE)], axis=0)
            y = jnp.dot(lhs, rhs, preferred_element_type=_f32)
            ya_ref[:, xs] = y + dskip_ref[:, xs] * x32
            upd = lax.dot_general(jnp.concatenate(bw, axis=1), x, tn, preferred_element_type=_f32)
            dec = jnp.where(low2, eti[0:1, heads[0]:heads[0] + 1], eti[0:1, heads[1]:heads[1] + 1])
            sf[q] = dec * sf[q] + jnp.where(diag, upd, 0.0)
            x = xj_ref[:, xs]
            cw = [(cj32 * colb(ej, H + h)).astype(MXU_DTYPE) for h in heads]
            bw = [(bj32 * colb(wj, H + h)).astype(MXU_DTYPE) for h in heads]
            yb_ref[:, xs] = jnp.dot(jnp.concatenate(cw, axis=1), sb[q].astype(MXU_DTYPE),
                                    preferred_element_type=_f32)
            upd = lax.dot_general(jnp.concatenate(bw, axis=1), x, tn, preferred_element_type=_f32)
            dec = jnp.where(low2, etj[0:1, H + heads[0]:H + heads[0] + 1], etj[0:1, H + heads[1]:H + heads[1] + 1])
            sb[q] = dec * sb[q] + jnp.where(diag, upd, 0.0)

    @pl.when(i == pl.num_programs(0) - 1)
    def _():
        finf_ref[...] = sf[...]
        finb_ref[...] = sb[...]


def _pair_states(s):
    s = s.reshape(SSM_HEADS // 2, 2, SSM_STATE, SSM_HEADDIM)
    z = jnp.zeros_like(s[:, 0])
    return jnp.concatenate([jnp.concatenate([s[:, 0], z], -1), jnp.concatenate([z, s[:, 1]], -1)], 1)


def _unpair_states(s):
    top, bot = s[:, :SSM_STATE, :SSM_HEADDIM], s[:, SSM_STATE:, SSM_HEADDIM:]
    return jnp.stack([top, bot], 1).reshape(SSM_HEADS, SSM_STATE, SSM_HEADDIM)


def _scan_ssd(cs, bs, xs, pack, d_skip, s0f, s0b):
    L = xs.shape[0]
    nc = L // CHUNK
    dvec = jnp.repeat(d_skip.astype(_f32), SSM_HEADDIM)[None]
    fw = lambda w: pl.BlockSpec((CHUNK, w), lambda i: (i, 0))
    bw = lambda w: pl.BlockSpec((CHUNK, w), lambda i: (nc - 1 - i, 0))
    pshape = (SSM_HEADS // 2, 2 * SSM_STATE, 2 * SSM_HEADDIM)
    st = pl.BlockSpec(pshape, lambda i: (0, 0, 0))
    yshape = jax.ShapeDtypeStruct((L, SSM_W), _f32)
    sshape = jax.ShapeDtypeStruct(pshape, _f32)
    ya, yb, fin_f, fin_b = pl.pallas_call(
        _scan_ssd_kernel,
        grid=(nc,),
        in_specs=[fw(SSM_GN), fw(SSM_GN), fw(SSM_W), fw(LANES), bw(SSM_GN), bw(SSM_GN), bw(SSM_W), bw(LANES),
                  pl.BlockSpec((1, SSM_W), lambda i: (0, 0)), st, st],
        out_specs=[fw(SSM_W), bw(SSM_W), st, st],
        out_shape=[yshape, yshape, sshape, sshape],
        scratch_shapes=[pltpu.VMEM(pshape, _f32)] * 2,
        compiler_params=_cparams("arbitrary"),
        name="scan_ssd",
    )(cs, bs, xs, pack, cs, bs, xs, pack, dvec, _pair_states(s0f), _pair_states(s0b))
    return ya, yb, _unpair_states(fin_f), _unpair_states(fin_b)


def _merge_kernel(ra_ref, rb_ref, sa_ref, sb_ref, gr_ref, gs_ref, nw_ref, yr_ref, ys_ref):
    for h in range(RET_HEADS):
        sl = slice(h * RET_DH, (h + 1) * RET_DH)
        y = _ln_rows(ra_ref[:, sl] + rb_ref[:, sl])
        yr_ref[:, sl] = (y * _silu(gr_ref[:, sl].astype(_f32))).astype(yr_ref.dtype)
    gw = SSM_W // SSM_GROUPS
    for g in range(SSM_GROUPS):
        sl = slice(g * gw, (g + 1) * gw)
        y = (sa_ref[:, sl] + sb_ref[:, sl]) * _silu(gs_ref[:, sl].astype(_f32))
        y = y * lax.rsqrt(jnp.mean(y * y, -1, keepdims=True) + LN_EPS)
        ys_ref[:, sl] = (y * nw_ref[:, sl]).astype(ys_ref.dtype)


def _gate_spec(tr, n):
    return pl.BlockSpec((tr, BR_W), lambda i: (i, B_GATE // BR_W + n))


def _merge(ra, rb, sa, sb_, pb, norm_w):
    L = ra.shape[0]
    tr = 256
    row = pl.BlockSpec((tr, BR_W), lambda i: (i, 0))
    shp = jax.ShapeDtypeStruct((L, BR_W), MXU_DTYPE)
    return pl.pallas_call(
        _merge_kernel,
        grid=(L // tr,),
        in_specs=[row, row, row, row, _gate_spec(tr, 1), _gate_spec(tr, 3),
                  pl.BlockSpec((1, BR_W), lambda i: (0, 0))],
        out_specs=[row, row],
        out_shape=[shp, shp],
        compiler_params=_cparams("parallel"),
        name="merge",
    )(ra, rb, sa, sb_, pb, pb, norm_w.astype(_f32)[None])


def _pool_kernel(x_ref, prev_ref, next_ref, g_ref, pw_ref, ps_ref, o_ref, *, L):
    i = pl.program_id(0)
    t = x_ref.shape[0]
    halo = HALO
    has_prev = (i > 0).astype(_f32)
    has_next = (i < pl.num_programs(0) - 1).astype(_f32)
    pos = i * t + lax.broadcasted_iota(jnp.int32, (t, 1), 0)
    for g, win in enumerate(POOL_WINDOWS):
        sl = slice(g * POOL_GROUP, (g + 1) * POOL_GROUP)
        x = x_ref[:, sl].astype(_f32)
        s = jnp.concatenate([prev_ref[:, sl].astype(_f32) * has_prev, x,
                             next_ref[:, sl].astype(_f32) * has_next], axis=0)
        rows = t + 2 * halo
        width = 1
        while width < win:
            s = s + pltpu.roll(s, rows - width, axis=0)
            width *= 2
        off = halo - win // 2
        if off:
            s = pltpu.roll(s, rows - off, axis=0)
        cnt = jnp.minimum(pos + win // 2, L) - jnp.maximum(pos - win // 2, 0)
        d = s[:t] / cnt.astype(_f32) - x
        y = jnp.dot(d.astype(MXU_DTYPE), pw_ref[g], preferred_element_type=_f32)
        o_ref[:, sl] = (y * ps_ref[:, sl] * _silu(g_ref[:, sl].astype(_f32))).astype(o_ref.dtype)


def _pool(pb, L, pool_w, pool_scale):
    tr = 256
    return pl.pallas_call(
        functools.partial(_pool_kernel, L=L),
        grid=(L // tr,),
        in_specs=_halo_specs(tr, L, POOL_W, B_POOL) + [
            _gate_spec(tr, 2),
            pl.BlockSpec((POOL_GROUPS, POOL_GROUP, POOL_GROUP), lambda i: (0, 0, 0)),
            pl.BlockSpec((1, POOL_W), lambda i: (0, 0))],
        out_specs=pl.BlockSpec((tr, POOL_W), lambda i: (i, 0)),
        out_shape=jax.ShapeDtypeStruct((L, POOL_W), MXU_DTYPE),
        compiler_params=_cparams("parallel"),
        name="pool",
    )(pb, pb, pb, pb, pool_w.astype(MXU_DTYPE), pool_scale.astype(_f32)[None])


def _prep_hy_kernel(v_ref, vp_ref, vn_ref, x0_ref, x0p_ref, x0n_ref, x1_ref, x1p_ref, x1n_ref,
                    g_ref, w_ref, b_ref, wo_ref, x0g_ref):
    i = pl.program_id(0)
    has_prev = (i > 0).astype(_f32)
    has_next = (i < pl.num_programs(0) - 1).astype(_f32)

    def conv(n, x_ref, p_ref, n_ref):
        sl = slice(n * HY_W, (n + 1) * HY_W)
        return _conv3(x_ref, p_ref, n_ref, w_ref.at[:, sl], b_ref.at[:, sl], has_prev, has_next)

    hv = conv(0, v_ref, vp_ref, vn_ref)
    hx0 = conv(1, x0_ref, x0p_ref, x0n_ref)
    hx1 = conv(2, x1_ref, x1p_ref, x1n_ref)
    wo_ref[...] = hx1 * hv
    x0g_ref[...] = hx0 * _silu(g_ref[...].astype(_f32))


def _prep_hy(pb, L, conv_w, conv_b):
    tr = 256
    w = jnp.pad(conv_w.astype(_f32), ((0, SUBLANES - 3), (0, 0)))
    row = pl.BlockSpec((tr, HY_W), lambda i: (i, 0))
    shp = jax.ShapeDtypeStruct((L, HY_W), _f32)
    secs = sum((_halo_specs(tr, L, HY_W, B_HY + n * HY_W) for n in range(3)), [])
    return pl.pallas_call(
        _prep_hy_kernel,
        grid=(L // tr,),
        in_specs=secs + [_gate_spec(tr, 0),
                         pl.BlockSpec((SUBLANES, 3 * HY_W), lambda i: (0, 0)),
                         pl.BlockSpec((1, 3 * HY_W), lambda i: (0, 0))],
        out_specs=[row, row],
        out_shape=[shp, shp],
        compiler_params=_cparams("parallel"),
        name="prep_hy",
    )(*([pb] * 10), w, conv_b.astype(_f32)[None])


def _split(x):
    hi = x.astype(MXU_DTYPE)
    return hi, (x - hi.astype(_f32)).astype(MXU_DTYPE)


def _dot3(a_hi, a_lo, b):
    b_hi, b_lo = _split(b)
    d = lambda p, q: jnp.dot(p, q, preferred_element_type=_f32)
    return d(a_hi, b_hi) + (d(a_hi, b_lo) + d(a_lo, b_hi))


def _dot2(a_hi, a_lo, b):
    b = b.astype(MXU_DTYPE)
    d = lambda p, q: jnp.dot(p, q, preferred_element_type=_f32)
    return d(a_hi, b) + d(a_lo, b)


def _const_split(m):
    return _split(jnp.asarray(m, _f32))


def _filter_kernel(z_ref, w1_ref, b1_ref, w2_ref, b2_ref, w3hi_ref, w3lo_ref, freq_ref, delta_ref, o_ref, *, L):
    t = z_ref.shape[1]
    dot = functools.partial(jnp.dot, precision=HIGHEST, preferred_element_type=_f32)
    freq = freq_ref[...]
    hdn = jnp.sin(freq * (dot(w1_ref[...], z_ref[...]) + b1_ref[...]))
    hdn = jnp.sin(freq * (dot(w2_ref[...], hdn) + b2_ref[...]))
    h_hi, h_lo = _split(hdn)
    d = lambda p, q: lax.dot_general(p, q, (((0,), (0,)), ((), ())), preferred_element_type=_f32)
    filt = d(h_hi, w3hi_ref[...]) + (d(h_hi, w3lo_ref[...]) + d(h_lo, w3hi_ref[...]))
    n = pl.program_id(0) * t + lax.broadcasted_iota(jnp.int32, (t, 1), 0)
    lag = jnp.minimum(jnp.where(n < L, n, 2 * L - n), L - 1).astype(_f32)
    o_ref[...] = jnp.where(n == L, 0.0, filt) * jnp.exp(-(lag / (L - 1)) * delta_ref[...])


def _hy_filter(L, lp):
    n = jnp.arange(2 * L)
    lag = jnp.minimum(jnp.where(n < L, n, 2 * L - n), L - 1).astype(_f32)[:, None]
    t = lag / (L - 1)
    w = 2.0 * math.pi * lag / L
    bands = jnp.linspace(1e-4, HY_BANDS - 1, HY_BANDS, dtype=_f32)[None, :]
    z = jnp.concatenate([t, jnp.cos(bands * w), -jnp.sin(bands * w)], axis=-1)
    emb = z.shape[1]
    zt = jnp.pad(z, ((0, 0), (0, LANES - emb))).T
    w1t = jnp.pad(lp['hy_w1'].astype(_f32), ((0, LANES - emb), (0, 0))).T
    deltas = jnp.abs(jnp.linspace(HY_MIN_DECAY, HY_MAX_DECAY, HY_W, dtype=_f32))[None]
    tr = min(512, L)
    w3hi, w3lo = _split(lp['hy_w3'].astype(_f32))
    full = lambda a: pl.BlockSpec(a.shape, lambda i: (0,) * a.ndim)
    half = pl.BlockSpec((w3hi.shape[0], HY_W), lambda i: (0, i // (L // tr)))
    colv = lambda v: v.astype(_f32)[:, None]
    pre = [w1t, colv(lp['hy_b1']), lp['hy_w2'].astype(_f32).T, colv(lp['hy_b2'])]
    post = [colv(lp['hy_freq']), deltas]
    return pl.pallas_call(
        functools.partial(_filter_kernel, L=L),
        grid=(2 * L // tr,),
        in_specs=[pl.BlockSpec((LANES, tr), lambda i: (0, i))] + [full(a) for a in pre] + [half, half]
        + [full(a) for a in post],
        out_specs=pl.BlockSpec((tr, HY_W), lambda i: (i, 0)),
        out_shape=jax.ShapeDtypeStruct((2 * L, HY_W), _f32),
        compiler_params=_cparams("parallel"),
        name="hy_filter",
    )(zt, *pre, w3hi, w3lo, *post)


def _cs(num, den):
    ang = 2.0 * np.pi * (np.asarray(num, np.int64) % den) / den
    return np.cos(ang), np.sin(ang)


FFT_N2 = LANES


def _fft_rows(n1):
    return -(-(n1 // 2 + 1) // SUBLANES) * SUBLANES


def _fft_first_kernel(x_ref, mhi_ref, mlo_ref, o_ref):
    o_ref[...] = _dot2(mhi_ref[...], mlo_ref[...], x_ref[...])


def _fft_first(x2, n1):
    rows, cols = x2.shape
    kp = _fft_rows(n1)
    c, s = _cs(np.outer(np.arange(kp), np.arange(rows)), n1)
    mhi, mlo = _const_split(np.concatenate([c, -s], 0))
    tcol = 4096
    return pl.pallas_call(
        _fft_first_kernel,
        grid=(cols // tcol,),
        in_specs=[pl.BlockSpec((rows, tcol), lambda j: (0, j)),
                  pl.BlockSpec((2 * kp, rows), lambda j: (0, 0)),
                  pl.BlockSpec((2 * kp, rows), lambda j: (0, 0))],
        out_specs=pl.BlockSpec((2 * kp, tcol), lambda j: (0, j)),
        out_shape=jax.ShapeDtypeStruct((2 * kp, cols), _f32),
        compiler_params=_cparams("parallel"),
        name="fft_first",
    )(x2, mhi, mlo)


def _fft_mid_kernel(*refs, conv):
    if conv:
        a_ref, twr_ref, twi_ref, fhi_ref, flo_ref, h_ref, ghi_ref, glo_ref, o_ref = refs
    else:
        a_ref, twr_ref, twi_ref, fhi_ref, flo_ref, o_ref = refs
    n2 = FFT_N2
    reps = a_ref.shape[-1] // LANES
    twr = jnp.concatenate([twr_ref[0]] * reps, axis=1)
    twi = jnp.concatenate([twi_ref[0]] * reps, axis=1)
    ar, ai = a_ref[0, 0], a_ref[1, 0]
    x = _dot2(fhi_ref[...], flo_ref[...],
              jnp.concatenate([ar * twr - ai * twi, ar * twi + ai * twr], axis=0))
    if not conv:
        o_ref[0, 0] = x[:n2]
        o_ref[1, 0] = x[n2:]
        return
    xr, xi = x[:n2], x[n2:]
    hr, hi = h_ref[0, 0], h_ref[1, 0]
    b = _dot2(ghi_ref[...], glo_ref[...],
              jnp.concatenate([xr * hr - xi * hi, xr * hi + xi * hr], axis=0))
    br, bi = b[:n2], b[n2:]
    o_ref[0, 0] = br * twr + bi * twi
    o_ref[1, 0] = bi * twr - br * twi


def _fft_mid(a, n1, hf=None):
    n2 = FFT_N2
    kp, ch = a.shape[1], a.shape[-1]
    n = n1 * n2
    idx = jnp.arange(kp)[:, None] * jnp.arange(n2)[None, :]
    ang = (2.0 * math.pi / n) * (idx % n).astype(_f32)
    twr = jnp.broadcast_to(jnp.cos(ang)[:, :, None], (kp, n2, LANES))
    twi = jnp.broadcast_to(-jnp.sin(ang)[:, :, None], (kp, n2, LANES))
    c, s = _cs(np.outer(np.arange(n2), np.arange(n2)), n2)
    fhi, flo = _const_split(np.block([[c, s], [-s, c]]))
    blk = pl.BlockSpec((2, 1, n2, ch), lambda k: (0, k, 0, 0))
    tw = pl.BlockSpec((1, n2, LANES), lambda k: (k, 0, 0))
    mat = pl.BlockSpec((2 * n2, 2 * n2), lambda k: (0, 0))
    args, specs = [a, twr, twi, fhi, flo], [blk, tw, tw, mat, mat]
    if hf is not None:
        ghi, glo = _const_split(np.block([[c, -s], [s, c]]))
        args += [hf, ghi, glo]
        specs += [blk, mat, mat]
    return pl.pallas_call(
        functools.partial(_fft_mid_kernel, conv=hf is not None),
        grid=(kp,),
        in_specs=specs,
        out_specs=blk,
        out_shape=jax.ShapeDtypeStruct(a.shape, _f32),
        compiler_params=_cparams("parallel"),
        name="fft_mid",
    )(*args)


def _fft_last_kernel(c_ref, mhi_ref, mlo_ref, w_ref, x0g_ref, bias_ref, o_ref):
    y = _dot2(mhi_ref[...], mlo_ref[...], c_ref[...])
    o_ref[...] = (x0g_ref[...] * (y + w_ref[...] * bias_ref[...])).astype(o_ref.dtype)


def _fft_last(c2, n1, w2, x0g2, bias):
    rows, cols = w2.shape
    kp = _fft_rows(n1)
    n = n1 * FFT_N2
    c, s = _cs(np.outer(np.arange(rows), np.arange(kp)), n1)
    k1 = np.arange(kp)
    mult = np.where((k1 == 0) | (k1 == n1 // 2), 1.0, np.where(k1 < n1 // 2, 2.0, 0.0))
    mhi, mlo = _const_split(np.concatenate([c * mult, -s * mult], 1) / n)
    tcol = 4096
    bias_t = jnp.tile(bias.astype(_f32), tcol // bias.shape[0])[None]
    blk = pl.BlockSpec((rows, tcol), lambda j: (0, j))
    return pl.pallas_call(
        _fft_last_kernel,
        grid=(cols // tcol,),
        in_specs=[pl.BlockSpec((2 * kp, tcol), lambda j: (0, j)),
                  pl.BlockSpec((rows, 2 * kp), lambda j: (0, 0)),
                  pl.BlockSpec((rows, 2 * kp), lambda j: (0, 0)),
                  blk, blk, pl.BlockSpec((1, tcol), lambda j: (0, 0))],
        out_specs=blk,
        out_shape=jax.ShapeDtypeStruct((rows, cols), MXU_DTYPE),
        compiler_params=_cparams("parallel"),
        name="fft_last",
    )(c2, mhi, mlo, w2, x0g2, bias_t)


def _hy_small_kernel(w_ref, buf_ref, x0g_ref, bias_ref, fwhi_ref, fwlo_ref, fbhi_ref, fblo_ref,
                     ihi_ref, ilo_ref, o_ref):
    n = buf_ref.shape[0]
    w = w_ref[...]
    wf = _dot3(fwhi_ref[...], fwlo_ref[...], w)
    hf = _dot3(fbhi_ref[...], fblo_ref[...], buf_ref[...])
    wr, wi, hr, hi = wf[:n], wf[n:], hf[:n], hf[n:]
    y = _dot3(ihi_ref[...], ilo_ref[...], jnp.concatenate([wr * hr - wi * hi, wr * hi + wi * hr], axis=0))
    o_ref[...] = (x0g_ref[...] * (y + w * bias_ref[...])).astype(o_ref.dtype)


def _hy_conv_small(w, buf, x0g, bias):
    L, ch = w.shape
    n = 2 * L
    tc = 256
    c, s = _cs(np.outer(np.arange(n), np.arange(n)), n)
    fb = np.concatenate([c, -s], 0)
    mats = [*_const_split(fb[:, :L]), *_const_split(fb),
            *_const_split(np.concatenate([c[:L], -s[:L]], 1) / n)]
    col = lambda r: pl.BlockSpec((r, tc), lambda j: (0, j))
    return pl.pallas_call(
        _hy_small_kernel,
        grid=(ch // tc,),
        in_specs=[col(L), col(n), col(L), col(1)] + [pl.BlockSpec(m.shape, lambda j: (0, 0)) for m in mats],
        out_specs=col(L),
        out_shape=jax.ShapeDtypeStruct((L, ch), MXU_DTYPE),
        compiler_params=_cparams("parallel"),
        name="hy_conv_small",
    )(w, buf, x0g, bias.astype(_f32)[None], *mats)


def _hy_conv(w, buf, x0g, bias):
    L, ch = w.shape
    if L < 512:
        return _hy_conv_small(w, buf, x0g, bias)
    n1 = 2 * L // FFT_N2
    kp = _fft_rows(n1)
    cols = FFT_N2 * ch
    hf = _fft_mid(_fft_first(buf.reshape(n1, cols), n1).reshape(2, kp, FFT_N2, ch), n1)
    a = _fft_first(w.reshape(n1 // 2, cols), n1).reshape(2, kp, FFT_N2, ch)
    cc = _fft_mid(a, n1, hf)
    y = _fft_last(cc.reshape(2 * kp, cols), n1, w.reshape(n1 // 2, cols), x0g.reshape(n1 // 2, cols), bias)
    return y.reshape(L, ch)


def _zero_states():
    return (jnp.zeros((RET_HEADS, RET_DH, RET_DH), _f32), jnp.zeros((RET_HEADS, RET_DH, RET_DH), _f32),
            jnp.zeros((SSM_HEADS, SSM_STATE, SSM_HEADDIM), _f32),
            jnp.zeros((SSM_HEADS, SSM_STATE, SSM_HEADDIM), _f32))


def _recurrent(proj, L, lp, states, latent):
    pa, pb = proj
    q, k, v = _prep_ret(pa, pb, L, latent)
    ra, rb, ret_f, ret_b = _scan_ret(q, k, v, lp['ret_decay_logit'], states[0], states[1])
    cs, bs, xs, pack = _prep_ssd(pa, pb, L, lp['conv_ssm_w'], lp['conv_ssm_b'], lp['ssm_dt_bias'],
                                 lp['ssm_A_log'])
    sa, sb_, ssm_f, ssm_b = _scan_ssd(cs, bs, xs, pack, lp['ssm_D'], states[2], states[3])
    return (ra, rb, sa, sb_), (ret_f, ret_b, ssm_f, ssm_b)


def _mix(h, mod, lp, states, latent):
    L = h.shape[0]
    proj = _in_proj(h, mod[0], mod[1], lp)
    pb = proj[1]
    (ra, rb, sa, sb_), fin = _recurrent(proj, L, lp, states, latent)
    y_ret, y_ssm = _merge(ra, rb, sa, sb_, pb, lp['ssm_norm_w'])
    w, x0g = _prep_hy(pb, L, lp['conv_hy_w'], lp['conv_hy_b'])
    y_hy = _hy_conv(w, _hy_filter(L, lp), x0g, lp['hy_bias'])
    y_pool = _pool(pb, L, lp['pool_w'], lp['pool_scale'])
    out = _out_proj([y_hy, y_ret, y_pool, y_ssm], lp['w_out'], h, mod[2], lp['ln_g'], lp['ln_b'])
    return out, fin


def _context_states(hc, mod, lp):
    proj = _in_proj(hc, mod[0], mod[1], lp)
    _, fin = _recurrent(proj, hc.shape[0], lp, _zero_states(), False)
    return fin


def kernel(x, c, ctx, c_ctx, w_mod, b_mod, w_in, conv_ssm_w, conv_ssm_b, conv_hy_w, conv_hy_b,
           ret_decay_logit, ssm_A_log, ssm_dt_bias, ssm_D, ssm_norm_w, hy_w1, hy_b1, hy_w2, hy_b2,
           hy_w3, hy_freq, hy_bias, pool_w, pool_scale, w_out, ln_g, ln_b):
    assert x.shape[0] == 1
    h, hc = x[0], ctx[0]
    w_in_t = jnp.swapaxes(w_in, 1, 2)
    mods = _adaln(jnp.concatenate([c, c_ctx[None]], axis=0), w_mod, b_mod)
    for l in range(DEPTH):
        lp = {
            'w_in_t': _cast_layer(w_in_t, l, N_IN // 2, 512), 'w_out': _cast_layer(w_out, l, 1024, 2048),
            'conv_ssm_w': conv_ssm_w[l], 'conv_ssm_b': conv_ssm_b[l],
            'conv_hy_w': conv_hy_w[l], 'conv_hy_b': conv_hy_b[l], 'ret_decay_logit': ret_decay_logit[l],
            'ssm_A_log': ssm_A_log[l], 'ssm_dt_bias': ssm_dt_bias[l], 'ssm_D': ssm_D[l],
            'ssm_norm_w': ssm_norm_w[l], 'hy_w1': hy_w1[l], 'hy_b1': hy_b1[l], 'hy_w2': hy_w2[l],
            'hy_b2': hy_b2[l], 'hy_w3': hy_w3[l], 'hy_freq': hy_freq[l], 'hy_bias': hy_bias[l],
            'pool_w': pool_w[l], 'pool_scale': pool_scale[l], 'ln_g': ln_g[l], 'ln_b': ln_b[l],
        }
        mod = lambda r: tuple(mods[l, r:r + 1, n * D_MODEL:(n + 1) * D_MODEL] for n in range(3))
        if l < DEPTH - 1:
            hc_next, states = _mix(hc, mod(1), lp, _zero_states(), False)
        else:
            states = _context_states(hc, mod(1), lp)
            hc_next = hc
        h, _ = _mix(h, mod(0), lp, states, True)
        hc = hc_next
    return h[None]
```

```python
import functools
import math

import jax
import jax.numpy as jnp
import numpy as np
from jax import lax
from jax.experimental import pallas as pl
from jax.experimental.pallas import tpu as pltpu

D_MODEL = 4096
DEPTH = 2
GRID_W = 64
MIX_W = D_MODEL
BR_W = MIX_W // 4
HY_W = RET_W = POOL_W = SSM_W = BR_W
RET_HEADS = 8
RET_DH = RET_W // RET_HEADS
ROPE_BASE = 10000.0
SSM_HEADDIM = 64
SSM_HEADS = SSM_W // SSM_HEADDIM
SSM_GROUPS = 4
SSM_HPG = SSM_HEADS // SSM_GROUPS
SSM_STATE = 128
SSM_GN = SSM_GROUPS * SSM_STATE
CHUNK = 128
POOL_WINDOWS = (2, 4, 8, 16)
POOL_GROUPS = len(POOL_WINDOWS)
POOL_GROUP = POOL_W // POOL_GROUPS
HY_BANDS = 16
HY_TARGET = 1e-2
HY_FAST = 0.3
HY_SLOW = 1.5
HY_MIN_DECAY = math.log(HY_TARGET) / HY_SLOW
HY_MAX_DECAY = math.log(HY_TARGET) / HY_FAST
ALPHA = (2.0 * DEPTH) ** 0.25
LN_EPS = 1e-5

O_RET_K = 0
O_RET_V = O_RET_K + RET_W
O_SSM_DT = O_RET_V + RET_W
O_SSM_X = O_SSM_DT + 2 * SSM_HEADS
O_SSM_B = O_SSM_X + SSM_W
O_RET_Q = O_SSM_B + SSM_GN
O_SSM_C = O_RET_Q + RET_W
O_HY = O_SSM_C + SSM_GN
O_POOL = O_HY + 3 * HY_W
O_GATE = O_POOL + POOL_W
N_IN = O_GATE + MIX_W

LANES = 128
SUBLANES = 8
N_A = O_SSM_DT + LANES
B_SX = 0
B_SB = O_SSM_B - O_SSM_X
B_RQ = O_RET_Q - O_SSM_X
B_SC = O_SSM_C - O_SSM_X
B_HY = O_HY - O_SSM_X
B_POOL = O_POOL - O_SSM_X
B_GATE = O_GATE - O_SSM_X
N_B = N_IN - O_SSM_X

VMEM_LIMIT_BYTES = 56 * 1024 * 1024
MXU_DTYPE = jnp.bfloat16
HIGHEST = lax.Precision.HIGHEST

_f32 = jnp.float32


def _cparams(*sem, vmem=VMEM_LIMIT_BYTES):
    return pltpu.CompilerParams(dimension_semantics=sem, vmem_limit_bytes=vmem)


def _silu(x):
    return x * jax.nn.sigmoid(x)


def _ln_rows(z):
    mu = jnp.mean(z, -1, keepdims=True)
    zc = z - mu
    var = jnp.mean(zc * zc, -1, keepdims=True)
    return zc * lax.rsqrt(var + LN_EPS)


def _cast_kernel(w_ref, o_ref):
    o_ref[...] = w_ref[...].astype(o_ref.dtype)


def _cast_layer(w, l, r, tr, tc):
    c = w.shape[2]
    assert r % tr == 0 and c % tc == 0
    return pl.pallas_call(
        _cast_kernel,
        grid=(r // tr, c // tc),
        in_specs=[pl.BlockSpec((1, tr, tc), lambda i, j: (l, i, j))],
        out_specs=pl.BlockSpec((1, tr, tc), lambda i, j: (0, i, j)),
        out_shape=jax.ShapeDtypeStruct((1, r, c), MXU_DTYPE),
        compiler_params=_cparams("parallel", "parallel"),
        name="cast_layer",
    )(w)


def _adaln_kernel(c_ref, w_ref, b_ref, o_ref):
    tn = w_ref.shape[-1]
    outs = []
    for m in range(c_ref.shape[0]):
        x = _silu(c_ref[m])
        cols = [jnp.sum(x * w_ref[0, :, j * LANES:(j + 1) * LANES], axis=0, keepdims=True)
                for j in range(tn // LANES)]
        outs.append(jnp.concatenate(cols, axis=1))
    outs.append(jnp.zeros((SUBLANES - len(outs), tn), _f32))
    o_ref[0] = jnp.concatenate(outs, axis=0) + b_ref[0]


def _adaln(c_rows, w_mod, b_mod):
    dep, k, n = w_mod.shape
    r = c_rows.shape[0]
    tn = 512
    cb = jnp.broadcast_to(c_rows.astype(_f32)[:, :, None], (r, k, LANES))
    return pl.pallas_call(
        _adaln_kernel,
        grid=(dep, n // tn),
        in_specs=[pl.BlockSpec((r, k, LANES), lambda l, j: (0, 0, 0)),
                  pl.BlockSpec((1, k, tn), lambda l, j: (l, 0, j)),
                  pl.BlockSpec((1, 1, tn), lambda l, j: (l, 0, j))],
        out_specs=pl.BlockSpec((1, SUBLANES, tn), lambda l, j: (l, 0, j)),
        out_shape=jax.ShapeDtypeStruct((dep, SUBLANES, n), _f32),
        compiler_params=_cparams("parallel", "parallel"),
        name="adaln",
    )(cb, w_mod, b_mod[:, None, :])


def _matmul_nt_kernel(a_ref, b_ref, o_ref):
    o_ref[...] = lax.dot_general(a_ref[...], b_ref[0].astype(MXU_DTYPE), (((1,), (1,)), ((), ())),
                                 preferred_element_type=_f32).astype(o_ref.dtype)


def _matmul_nt(a, wt, l, row0, n, tm, tn, out_dtype):
    m, k = a.shape
    assert m % tm == 0 and n % tn == 0 and row0 % 32 == 0 and tn % 32 == 0
    return pl.pallas_call(
        _matmul_nt_kernel,
        grid=(m // tm, n // tn),
        in_specs=[pl.BlockSpec((tm, k), lambda i, j: (i, 0)),
                  pl.BlockSpec((pl.Element(1), pl.Element(tn), pl.Element(k)),
                               lambda i, j: (l, pl.multiple_of(row0 + j * tn, 32), 0))],
        out_specs=pl.BlockSpec((tm, tn), lambda i, j: (i, j)),
        out_shape=jax.ShapeDtypeStruct((m, n), out_dtype),
        compiler_params=_cparams("parallel", "parallel"),
        name="matmul_nt",
    )(a, wt)


def _modulate_kernel(h_ref, shift_ref, scale_ref, o_ref):
    o_ref[...] = (_ln_rows(h_ref[...]) * (1.0 + scale_ref[...]) + shift_ref[...]).astype(o_ref.dtype)


def _modulate(h, shift, scale):
    L, d = h.shape
    tr = 256
    vec = pl.BlockSpec((1, d), lambda i: (0, 0))
    return pl.pallas_call(
        _modulate_kernel,
        grid=(L // tr,),
        in_specs=[pl.BlockSpec((tr, d), lambda i: (i, 0)), vec, vec],
        out_specs=pl.BlockSpec((tr, d), lambda i: (i, 0)),
        out_shape=jax.ShapeDtypeStruct((L, d), MXU_DTYPE),
        compiler_params=_cparams("parallel"),
        name="modulate",
    )(h, shift, scale)


def _in_proj(h, shift, scale, lp):
    L = h.shape[0]
    u = _modulate(h, shift, scale)
    pa = _matmul_nt(u, lp['w_a_t'], 0, 0, N_A, 512 if L % 512 == 0 else 256, N_A, _f32)
    pb = _matmul_nt(u, lp['w_in_t'], lp['layer'], O_SSM_X, N_B, 1024 if L % 1024 == 0 else 256, 512, MXU_DTYPE)
    return pa, pb


def _out_proj_kernel(y0_ref, y1_ref, y2_ref, y3_ref, w_ref, h_ref, gate_ref, g_ref, b_ref, o_ref):
    out = None
    for n, y_ref in enumerate((y0_ref, y1_ref, y2_ref, y3_ref)):
        d = jnp.dot(y_ref[...], w_ref[n * BR_W:(n + 1) * BR_W, :], preferred_element_type=_f32)
        out = d if out is None else out + d
    z = ALPHA * h_ref[...] + gate_ref[...] * out
    o_ref[...] = _ln_rows(z) * g_ref[...] + b_ref[...]


OUT_PROJ_VMEM_BYTES = 60 * 1024 * 1024


def _out_proj(ys, w, h, gate, g, b):
    L, d = h.shape
    tm = 256
    lhs = pl.BlockSpec((tm, BR_W), lambda i: (i, 0))
    vec = pl.BlockSpec((1, d), lambda i: (0, 0))
    row = pl.BlockSpec((tm, d), lambda i: (i, 0))
    return pl.pallas_call(
        _out_proj_kernel,
        grid=(L // tm,),
        in_specs=[lhs] * len(ys) + [
            pl.BlockSpec(w.shape, lambda i: (0, 0), pipeline_mode=pl.Buffered(1)), row, vec, vec, vec],
        out_specs=row,
        out_shape=jax.ShapeDtypeStruct((L, d), _f32),
        compiler_params=_cparams("parallel", vmem=OUT_PROJ_VMEM_BYTES),
        name="out_proj",
    )(*ys, w, h, gate, g[None], b[None])


def _rope_tables(L):
    rows = L // GRID_W
    row = jnp.repeat(jnp.arange(rows), GRID_W).astype(_f32)
    col = jnp.tile(jnp.arange(GRID_W), rows).astype(_f32)
    nq = RET_DH // 4
    inv = ROPE_BASE ** (-jnp.arange(nq, dtype=_f32) / nq)
    ang = jnp.concatenate([row[:, None] * inv, col[:, None] * inv], -1)
    cos, sin = jnp.cos(ang), jnp.sin(ang)
    return jnp.concatenate([cos, cos], -1), jnp.concatenate([-sin, sin], -1)


def _prep_ret_kernel(qlo_ref, qhi_ref, k_ref, v_ref, cos_ref, sin_ref, qo_ref, ko_ref, vo_ref, *, rope):
    def rot(t):
        if not rope:
            return t
        return t * cos_ref[...] + pltpu.roll(t, RET_DH // 2, axis=1) * sin_ref[...]

    half = RET_HEADS // 2
    for h in range(RET_HEADS):
        sl = slice(h * RET_DH, (h + 1) * RET_DH)
        q_ref, qs = (qlo_ref, sl) if h < half else (qhi_ref, slice((h - half) * RET_DH, (h - half + 1) * RET_DH))
        qo_ref[:, sl] = rot(q_ref[:, qs].astype(_f32)).astype(qo_ref.dtype)
        ko_ref[:, sl] = rot(k_ref[:, sl] * (RET_DH ** -0.5)).astype(ko_ref.dtype)
    vo_ref[...] = v_ref[...].astype(vo_ref.dtype)


def _prep_ret(pa, pb, L, rope):
    tr = 256
    cos, sin = _rope_tables(L) if rope else (jnp.ones((L, LANES), _f32), jnp.zeros((L, LANES), _f32))
    hw = RET_W // 2
    qsp = lambda n: pl.BlockSpec((tr, hw), lambda i: (i, B_RQ // hw + n))
    sec = lambda c: pl.BlockSpec((tr, RET_W), lambda i: (i, c // RET_W))
    tab = pl.BlockSpec((tr, LANES), lambda i: (i, 0))
    out = pl.BlockSpec((tr, RET_W), lambda i: (i, 0))
    shp = jax.ShapeDtypeStruct((L, RET_W), MXU_DTYPE)
    return pl.pallas_call(
        functools.partial(_prep_ret_kernel, rope=rope),
        grid=(L // tr,),
        in_specs=[qsp(0), qsp(1), sec(O_RET_K), sec(O_RET_V), tab, tab],
        out_specs=[out, out, out],
        out_shape=[shp, shp, shp],
        compiler_params=_cparams("parallel"),
        name="prep_ret",
    )(pb, pb, pa, pa, cos, sin)


def _scan_ret_kernel(logit_ref, qi_ref, ki_ref, vi_ref, qj_ref, kj_ref, vj_ref, s0f_ref, s0b_ref,
                     ya_ref, yb_ref, finf_ref, finb_ref,
                     sf, sb, dmask, f_out, f_upd, f_all, b_out, b_upd, b_all):
    i = pl.program_id(0)
    c = CHUNK

    @pl.when(i == 0)
    def _():
        sf[...] = s0f_ref[...]
        sb[...] = s0b_ref[...]
        ii = lax.broadcasted_iota(jnp.int32, (c, c), 0).astype(_f32)
        jj = lax.broadcasted_iota(jnp.int32, (c, c), 1).astype(_f32)
        for h in range(RET_HEADS):
            def lg(d):
                x = logit_ref[d, h]
                v = -jnp.log1p(jnp.exp(-x))
                return jnp.broadcast_to(v[0:1, :], (c, c))
            lf, lb = lg(0), lg(1)
            dmask[h] = jnp.where(ii > jj, jnp.exp(lf * (ii - jj)),
                                 jnp.where(jj > ii, jnp.exp(lb * (jj - ii)), 2.0))
            f_out[h] = jnp.exp(lf * (ii + 1.0))
            f_upd[h] = jnp.exp(lf * (c - 1.0 - ii))
            f_all[h] = jnp.exp(lf * float(c))
            b_out[h] = jnp.exp(lb * (c - ii))
            b_upd[h] = jnp.exp(lb * ii)
            b_all[h] = jnp.exp(lb * float(c))

    tn = (((0,), (0,)), ((), ()))
    nt = (((1,), (1,)), ((), ()))
    heads = range(RET_HEADS)
    sls = [slice(h * RET_DH, (h + 1) * RET_DH) for h in heads]
    scores = [lax.dot_general(qi_ref[:, sl], ki_ref[:, sl], nt, preferred_element_type=_f32) for sl in sls]
    upd_f = [lax.dot_general((ki_ref[:, sl].astype(_f32) * f_upd[h]).astype(MXU_DTYPE), vi_ref[:, sl], tn,
                             preferred_element_type=_f32) for h, sl in zip(heads, sls)]
    upd_b = [lax.dot_general((kj_ref[:, sl].astype(_f32) * b_upd[h]).astype(MXU_DTYPE), vj_ref[:, sl], tn,
                             preferred_element_type=_f32) for h, sl in zip(heads, sls)]
    for h, sl in zip(heads, sls):
        lhs = jnp.concatenate([(scores[h] * dmask[h]).astype(MXU_DTYPE),
                               (qi_ref[:, sl].astype(_f32) * f_out[h]).astype(MXU_DTYPE)], axis=1)
        rhs = jnp.concatenate([vi_ref[:, sl], sf[h].astype(MXU_DTYPE)], axis=0)
        ya_ref[:, sl] = jnp.dot(lhs, rhs, preferred_element_type=_f32)
        yb_ref[:, sl] = jnp.dot((qj_ref[:, sl].astype(_f32) * b_out[h]).astype(MXU_DTYPE),
                                sb[h].astype(MXU_DTYPE), preferred_element_type=_f32)
    for h in heads:
        sf[h] = f_all[h] * sf[h] + upd_f[h]
        sb[h] = b_all[h] * sb[h] + upd_b[h]

    @pl.when(i == pl.num_programs(0) - 1)
    def _():
        finf_ref[...] = sf[...]
        finb_ref[...] = sb[...]


def _scan_ret(q, k, v, logit, s0f, s0b):
    L = q.shape[0]
    nc = L // CHUNK
    logit_b = jnp.broadcast_to(logit.astype(_f32)[:, :, None, None], (2, RET_HEADS, SUBLANES, LANES))
    fw = pl.BlockSpec((CHUNK, RET_W), lambda i: (i, 0))
    bw = pl.BlockSpec((CHUNK, RET_W), lambda i: (nc - 1 - i, 0))
    st = pl.BlockSpec((RET_HEADS, RET_DH, RET_DH), lambda i: (0, 0, 0))
    yshape = jax.ShapeDtypeStruct((L, RET_W), _f32)
    sshape = jax.ShapeDtypeStruct((RET_HEADS, RET_DH, RET_DH), _f32)
    tile = pltpu.VMEM((RET_HEADS, CHUNK, CHUNK), _f32)
    return pl.pallas_call(
        _scan_ret_kernel,
        grid=(nc,),
        in_specs=[pl.BlockSpec((2, RET_HEADS, SUBLANES, LANES), lambda i: (0, 0, 0, 0)),
                  fw, fw, fw, bw, bw, bw, st, st],
        out_specs=[fw, bw, st, st],
        out_shape=[yshape, yshape, sshape, sshape],
        scratch_shapes=[pltpu.VMEM((RET_HEADS, RET_DH, RET_DH), _f32)] * 2 + [tile] * 7,
        compiler_params=_cparams("arbitrary"),
        name="scan_ret",
    )(logit_b, q, k, v, q, k, v, s0f, s0b)


def _shift_rows(x, prev_row, next_row):
    r = x.shape[0]
    rid = lax.broadcasted_iota(jnp.int32, x.shape, 0)
    up = jnp.where(rid == 0, prev_row, pltpu.roll(x, 1, axis=0))
    dn = jnp.where(rid == r - 1, next_row, pltpu.roll(x, r - 1, axis=0))
    return up, dn


HALO = 16


def _conv3(x_ref, prev_ref, next_ref, w_ref, b_ref, has_prev, has_next):
    x = x_ref[...].astype(_f32)
    prev_row = prev_ref[...].astype(_f32)[HALO - 1:HALO, :] * has_prev
    next_row = next_ref[...].astype(_f32)[0:1, :] * has_next
    up, dn = _shift_rows(x, prev_row, next_row)
    return up * w_ref[0:1, :] + x * w_ref[1:2, :] + dn * w_ref[2:3, :] + b_ref[...]


def _halo_specs(tr, L, width, col):
    nb = tr // HALO
    last = L // HALO - 1
    cb = col // width
    return [pl.BlockSpec((tr, width), lambda i: (i, cb)),
            pl.BlockSpec((HALO, width), lambda i: (jnp.maximum(i * nb - 1, 0), cb)),
            pl.BlockSpec((HALO, width), lambda i: (jnp.minimum((i + 1) * nb, last), cb))]


def _prep_ssd_kernel(x_ref, xp_ref, xn_ref, b_ref, bp_ref, bn_ref, c_ref, cp_ref, cn_ref, dt_ref,
                     w_ref, cb_ref, dtb_ref, alog_ref, co_ref, bo_ref, xo_ref, pack_ref):
    i = pl.program_id(0)
    has_prev = (i > 0).astype(_f32)
    has_next = (i < pl.num_programs(0) - 1).astype(_f32)

    def conv(lo, hi, t_ref, p_ref, n_ref, o_ref):
        y = _conv3(t_ref, p_ref, n_ref, w_ref.at[:, lo:hi], cb_ref.at[:, lo:hi], has_prev, has_next)
        o_ref[...] = _silu(y).astype(o_ref.dtype)

    conv(0, SSM_W, x_ref, xp_ref, xn_ref, xo_ref)
    conv(SSM_W, SSM_W + SSM_GN, b_ref, bp_ref, bn_ref, bo_ref)
    conv(SSM_W + SSM_GN, SSM_W + 2 * SSM_GN, c_ref, cp_ref, cn_ref, co_ref)
    z = dt_ref[...] + dtb_ref[...]
    dt = jnp.maximum(z, 0.0) + jnp.log1p(jnp.exp(-jnp.abs(z)))
    a = dt * (-jnp.exp(alog_ref[...]))
    c = CHUNK
    ii = lax.broadcasted_iota(jnp.int32, (c, c), 0)
    jj = lax.broadcasted_iota(jnp.int32, (c, c), 1)
    lower = (jj <= ii).astype(_f32)
    upper = (jj >= ii).astype(_f32)
    lane = lax.broadcasted_iota(jnp.int32, (c, LANES), 1)
    dt_sh = pltpu.roll(dt, 2 * SSM_HEADS, axis=1)
    for n in range(x_ref.shape[0] // c):
        rs = slice(n * c, (n + 1) * c)
        pre = jnp.dot(lower, a[rs], precision=HIGHEST, preferred_element_type=_f32)
        suf = jnp.dot(upper, a[rs], precision=HIGHEST, preferred_element_type=_f32)
        pack_ref[rs, :] = jnp.where(lane < SSM_HEADS, pre,
                                    jnp.where(lane < 2 * SSM_HEADS, suf, dt_sh[rs]))


def _prep_ssd(pa, pb, L, conv_w, conv_b, dt_bias, a_log):
    tr = 256
    w = jnp.pad(conv_w.astype(_f32), ((0, SUBLANES - 3), (0, 0)))
    lanes = lambda t: jnp.pad(t.astype(_f32).reshape(1, 2 * SSM_HEADS), ((0, 0), (0, LANES - 2 * SSM_HEADS)))
    wd = SSM_W + 2 * SSM_GN
    row = lambda c: pl.BlockSpec((tr, c), lambda i: (i, 0))
    return pl.pallas_call(
        _prep_ssd_kernel,
        grid=(L // tr,),
        in_specs=_halo_specs(tr, L, SSM_W, B_SX) + _halo_specs(tr, L, SSM_GN, B_SB)
        + _halo_specs(tr, L, SSM_GN, B_SC) + [
            pl.BlockSpec((tr, LANES), lambda i: (i, O_SSM_DT // LANES)),
            pl.BlockSpec((SUBLANES, wd), lambda i: (0, 0)),
            pl.BlockSpec((1, wd), lambda i: (0, 0)),
            pl.BlockSpec((1, LANES), lambda i: (0, 0)),
            pl.BlockSpec((1, LANES), lambda i: (0, 0))],
        out_specs=[row(SSM_GN), row(SSM_GN), row(SSM_W), row(LANES)],
        out_shape=[jax.ShapeDtypeStruct((L, SSM_GN), MXU_DTYPE),
                   jax.ShapeDtypeStruct((L, SSM_GN), MXU_DTYPE),
                   jax.ShapeDtypeStruct((L, SSM_W), MXU_DTYPE),
                   jax.ShapeDtypeStruct((L, LANES), _f32)],
        compiler_params=_cparams("parallel"),
        name="prep_ssd",
    )(*([pb] * 9), pa, w, conv_b.astype(_f32)[None], lanes(dt_bias), lanes(a_log))


def _scan_ssd_kernel(ci_ref, bi_ref, xi_ref, pi_ref, cj_ref, bj_ref, xj_ref, pj_ref, dskip_ref,
                     s0f_ref, s0b_ref, ya_ref, yb_ref, finf_ref, finb_ref, sf, sb):
    i = pl.program_id(0)
    c = CHUNK
    H = SSM_HEADS

    @pl.when(i == 0)
    def _():
        sf[...] = s0f_ref[...]
        sb[...] = s0b_ref[...]

    tn = (((0,), (0,)), ((), ()))
    nt = (((1,), (1,)), ((), ()))
    ii = lax.broadcasted_iota(jnp.int32, (c, c), 0)
    jj = lax.broadcasted_iota(jnp.int32, (c, c), 1)
    low = lax.broadcasted_iota(jnp.int32, (c, LANES), 1) < SSM_HEADDIM
    low2 = lax.broadcasted_iota(jnp.int32, (2 * SSM_STATE, LANES), 1) < SSM_HEADDIM
    diag = (lax.broadcasted_iota(jnp.int32, (2 * SSM_STATE, LANES), 0) < SSM_STATE) == low2
    pi = pi_ref[...]
    pit = pi.T
    pj = pj_ref[...]
    ei = jnp.exp(jnp.minimum(pi, 0.0))
    ej = jnp.exp(jnp.minimum(pj, 0.0))
    tot_i, tot_j = pi[c - 1:c, :], pj[0:1, :]
    wi = jnp.exp(jnp.minimum(tot_i - pi, 0.0)) * pltpu.roll(pi, LANES - 2 * H, axis=1)
    wj = jnp.exp(jnp.minimum(tot_j - pj, 0.0)) * pltpu.roll(pj, LANES - 2 * H, axis=1)
    eti, etj = jnp.exp(jnp.minimum(tot_i, 0.0)), jnp.exp(jnp.minimum(tot_j, 0.0))
    colb = lambda t, k: jnp.broadcast_to(t[:, k:k + 1], (c, LANES))
    for g in range(SSM_GROUPS):
        gs = slice(g * SSM_STATE, (g + 1) * SSM_STATE)
        ci, bi = ci_ref[:, gs], bi_ref[:, gs]
        cj, bj = cj_ref[:, gs], bj_ref[:, gs]
        cb = lax.dot_general(ci, bi, nt, preferred_element_type=_f32)
        ci32, bi32, cj32, bj32 = (t.astype(_f32) for t in (ci, bi, cj, bj))
        for pp in range(SSM_HPG // 2):
            q = g * (SSM_HPG // 2) + pp
            heads = (2 * q, 2 * q + 1)
            xs = slice(q * LANES, (q + 1) * LANES)
            x = xi_ref[:, xs]
            x32 = x.astype(_f32)
            scores, cw, bw = [], [], []
            for h in heads:
                row = lambda o: pit[o + h:o + h + 1, :]
                mf = jnp.where(ii >= jj, jnp.exp(jnp.minimum(colb(pi, h) - row(0), 0.0)), 0.0) * row(2 * H)
                mb = jnp.where(jj >= ii, jnp.exp(jnp.minimum(colb(pi, H + h) - row(H), 0.0)), 0.0) * row(3 * H)
                scores.append((cb * (mf + mb)).astype(MXU_DTYPE))
                cw.append((ci32 * colb(ei, h)).astype(MXU_DTYPE))
                bw.append((bi32 * colb(wi, h)).astype(MXU_DTYPE))
            xa = jnp.where(low, x32, 0.0).astype(MXU_DTYPE)
            xb = jnp.where(low, 0.0, x32).astype(MXU_DTYPE)
            lhs = jnp.concatenate(scores + cw, axis=1)
            rhs = jnp.concatenate([xa, xb, sf[q].astype(MXU_DTYPE)], axis=0)
            y = jnp.dot(lhs, rhs, preferred_element_type=_f32)
            ya_ref[:, xs] = y + dskip_ref[:, xs] * x32
            upd = lax.dot_general(jnp.concatenate(bw, axis=1), x, tn, preferred_element_type=_f32)
            dec = jnp.where(low2, eti[0:1, heads[0]:heads[0] + 1], eti[0:1, heads[1]:heads[1] + 1])
            sf[q] = dec * sf[q] + jnp.where(diag, upd, 0.0)
            x = xj_ref[:, xs]
            cw = [(cj32 * colb(ej, H + h)).astype(MXU_DTYPE) for h in heads]
            bw = [(bj32 * colb(wj, H + h)).astype(MXU_DTYPE) for h in heads]
            yb_ref[:, xs] = jnp.dot(jnp.concatenate(cw, axis=1), sb[q].astype(MXU_DTYPE),
                                    preferred_element_type=_f32)
            upd = lax.dot_general(jnp.concatenate(bw, axis=1), x, tn, preferred_element_type=_f32)
            dec = jnp.where(low2, etj[0:1, H + heads[0]:H + heads[0] + 1], etj[0:1, H + heads[1]:H + heads[1] + 1])
            sb[q] = dec * sb[q] + jnp.where(diag, upd, 0.0)

    @pl.when(i == pl.num_programs(0) - 1)
    def _():
        finf_ref[...] = sf[...]
        finb_ref[...] = sb[...]


def _pair_states(s):
    s = s.reshape(SSM_HEADS // 2, 2, SSM_STATE, SSM_HEADDIM)
    z = jnp.zeros_like(s[:, 0])
    return jnp.concatenate([jnp.concatenate([s[:, 0], z], -1), jnp.concatenate([z, s[:, 1]], -1)], 1)


def _unpair_states(s):
    top, bot = s[:, :SSM_STATE, :SSM_HEADDIM], s[:, SSM_STATE:, SSM_HEADDIM:]
    return jnp.stack([top, bot], 1).reshape(SSM_HEADS, SSM_STATE, SSM_HEADDIM)


def _scan_ssd(cs, bs, xs, pack, d_skip, s0f, s0b):
    L = xs.shape[0]
    nc = L // CHUNK
    dvec = jnp.repeat(d_skip.astype(_f32), SSM_HEADDIM)[None]
    fw = lambda w: pl.BlockSpec((CHUNK, w), lambda i: (i, 0))
    bw = lambda w: pl.BlockSpec((CHUNK, w), lambda i: (nc - 1 - i, 0))
    pshape = (SSM_HEADS // 2, 2 * SSM_STATE, 2 * SSM_HEADDIM)
    st = pl.BlockSpec(pshape, lambda i: (0, 0, 0))
    yshape = jax.ShapeDtypeStruct((L, SSM_W), _f32)
    sshape = jax.ShapeDtypeStruct(pshape, _f32)
    ya, yb, fin_f, fin_b = pl.pallas_call(
        _scan_ssd_kernel,
        grid=(nc,),
        in_specs=[fw(SSM_GN), fw(SSM_GN), fw(SSM_W), fw(LANES), bw(SSM_GN), bw(SSM_GN), bw(SSM_W), bw(LANES),
                  pl.BlockSpec((1, SSM_W), lambda i: (0, 0)), st, st],
        out_specs=[fw(SSM_W), bw(SSM_W), st, st],
        out_shape=[yshape, yshape, sshape, sshape],
        scratch_shapes=[pltpu.VMEM(pshape, _f32)] * 2,
        compiler_params=_cparams("arbitrary"),
        name="scan_ssd",
    )(cs, bs, xs, pack, cs, bs, xs, pack, dvec, _pair_states(s0f), _pair_states(s0b))
    return ya, yb, _unpair_states(fin_f), _unpair_states(fin_b)


def _merge_kernel(ra_ref, rb_ref, sa_ref, sb_ref, gr_ref, gs_ref, nw_ref, yr_ref, ys_ref):
    for h in range(RET_HEADS):
        sl = slice(h * RET_DH, (h + 1) * RET_DH)
        y = _ln_rows(ra_ref[:, sl] + rb_ref[:, sl])
        yr_ref[:, sl] = (y * _silu(gr_ref[:, sl].astype(_f32))).astype(yr_ref.dtype)
    gw = SSM_W // SSM_GROUPS
    for g in range(SSM_GROUPS):
        sl = slice(g * gw, (g + 1) * gw)
        y = (sa_ref[:, sl] + sb_ref[:, sl]) * _silu(gs_ref[:, sl].astype(_f32))
        y = y * lax.rsqrt(jnp.mean(y * y, -1, keepdims=True) + LN_EPS)
        ys_ref[:, sl] = (y * nw_ref[:, sl]).astype(ys_ref.dtype)


def _gate_spec(tr, n):
    return pl.BlockSpec((tr, BR_W), lambda i: (i, B_GATE // BR_W + n))


def _merge(ra, rb, sa, sb_, pb, norm_w):
    L = ra.shape[0]
    tr = 256
    row = pl.BlockSpec((tr, BR_W), lambda i: (i, 0))
    shp = jax.ShapeDtypeStruct((L, BR_W), MXU_DTYPE)
    return pl.pallas_call(
        _merge_kernel,
        grid=(L // tr,),
        in_specs=[row, row, row, row, _gate_spec(tr, 1), _gate_spec(tr, 3),
                  pl.BlockSpec((1, BR_W), lambda i: (0, 0))],
        out_specs=[row, row],
        out_shape=[shp, shp],
        compiler_params=_cparams("parallel"),
        name="merge",
    )(ra, rb, sa, sb_, pb, pb, norm_w.astype(_f32)[None])


def _pool_kernel(x_ref, prev_ref, next_ref, g_ref, pw_ref, ps_ref, o_ref, *, L):
    i = pl.program_id(0)
    t = x_ref.shape[0]
    halo = HALO
    has_prev = (i > 0).astype(_f32)
    has_next = (i < pl.num_programs(0) - 1).astype(_f32)
    pos = i * t + lax.broadcasted_iota(jnp.int32, (t, 1), 0)
    for g, win in enumerate(POOL_WINDOWS):
        sl = slice(g * POOL_GROUP, (g + 1) * POOL_GROUP)
        x = x_ref[:, sl].astype(_f32)
        s = jnp.concatenate([prev_ref[:, sl].astype(_f32) * has_prev, x,
                             next_ref[:, sl].astype(_f32) * has_next], axis=0)
        rows = t + 2 * halo
        width = 1
        while width < win:
            s = s + pltpu.roll(s, rows - width, axis=0)
            width *= 2
        off = halo - win // 2
        if off:
            s = pltpu.roll(s, rows - off, axis=0)
        cnt = jnp.minimum(pos + win // 2, L) - jnp.maximum(pos - win // 2, 0)
        d = s[:t] / cnt.astype(_f32) - x
        y = jnp.dot(d.astype(MXU_DTYPE), pw_ref[g], preferred_element_type=_f32)
        o_ref[:, sl] = (y * ps_ref[:, sl] * _silu(g_ref[:, sl].astype(_f32))).astype(o_ref.dtype)


def _pool(pb, L, pool_w, pool_scale):
    tr = 256
    return pl.pallas_call(
        functools.partial(_pool_kernel, L=L),
        grid=(L // tr,),
        in_specs=_halo_specs(tr, L, POOL_W, B_POOL) + [
            _gate_spec(tr, 2),
            pl.BlockSpec((POOL_GROUPS, POOL_GROUP, POOL_GROUP), lambda i: (0, 0, 0)),
            pl.BlockSpec((1, POOL_W), lambda i: (0, 0))],
        out_specs=pl.BlockSpec((tr, POOL_W), lambda i: (i, 0)),
        out_shape=jax.ShapeDtypeStruct((L, POOL_W), MXU_DTYPE),
        compiler_params=_cparams("parallel"),
        name="pool",
    )(pb, pb, pb, pb, pool_w.astype(MXU_DTYPE), pool_scale.astype(_f32)[None])


def _prep_hy_kernel(v_ref, vp_ref, vn_ref, x0_ref, x0p_ref, x0n_ref, x1_ref, x1p_ref, x1n_ref,
                    g_ref, w_ref, b_ref, wo_ref, x0g_ref):
    i = pl.program_id(0)
    has_prev = (i > 0).astype(_f32)
    has_next = (i < pl.num_programs(0) - 1).astype(_f32)

    def conv(n, x_ref, p_ref, n_ref):
        sl = slice(n * HY_W, (n + 1) * HY_W)
        return _conv3(x_ref, p_ref, n_ref, w_ref.at[:, sl], b_ref.at[:, sl], has_prev, has_next)

    hv = conv(0, v_ref, vp_ref, vn_ref)
    hx0 = conv(1, x0_ref, x0p_ref, x0n_ref)
    hx1 = conv(2, x1_ref, x1p_ref, x1n_ref)
    wo_ref[...] = hx1 * hv
    x0g_ref[...] = hx0 * _silu(g_ref[...].astype(_f32))


def _prep_hy(pb, L, conv_w, conv_b):
    tr = 256
    w = jnp.pad(conv_w.astype(_f32), ((0, SUBLANES - 3), (0, 0)))
    row = pl.BlockSpec((tr, HY_W), lambda i: (i, 0))
    shp = jax.ShapeDtypeStruct((L, HY_W), _f32)
    secs = sum((_halo_specs(tr, L, HY_W, B_HY + n * HY_W) for n in range(3)), [])
    return pl.pallas_call(
        _prep_hy_kernel,
        grid=(L // tr,),
        in_specs=secs + [_gate_spec(tr, 0),
                         pl.BlockSpec((SUBLANES, 3 * HY_W), lambda i: (0, 0)),
                         pl.BlockSpec((1, 3 * HY_W), lambda i: (0, 0))],
        out_specs=[row, row],
        out_shape=[shp, shp],
        compiler_params=_cparams("parallel"),
        name="prep_hy",
    )(*([pb] * 10), w, conv_b.astype(_f32)[None])


def _split(x):
    hi = x.astype(MXU_DTYPE)
    return hi, (x - hi.astype(_f32)).astype(MXU_DTYPE)


def _dot3(a_hi, a_lo, b):
    b_hi, b_lo = _split(b)
    d = lambda p, q: jnp.dot(p, q, preferred_element_type=_f32)
    return d(a_hi, b_hi) + (d(a_hi, b_lo) + d(a_lo, b_hi))


def _dot2(a_hi, a_lo, b):
    b = b.astype(MXU_DTYPE)
    d = lambda p, q: jnp.dot(p, q, preferred_element_type=_f32)
    return d(a_hi, b) + d(a_lo, b)


def _const_split(m):
    return _split(jnp.asarray(m, _f32))


def _filter_kernel(z_ref, w1_ref, b1_ref, w2_ref, b2_ref, w3hi_ref, w3lo_ref, freq_ref, delta_ref, o_ref, *, L):
    t = z_ref.shape[1]
    dot = functools.partial(jnp.dot, precision=HIGHEST, preferred_element_type=_f32)
    freq = freq_ref[...]
    hdn = jnp.sin(freq * (dot(w1_ref[...], z_ref[...]) + b1_ref[...]))
    hdn = jnp.sin(freq * (dot(w2_ref[...], hdn) + b2_ref[...]))
    h_hi, h_lo = _split(hdn)
    d = lambda p, q: lax.dot_general(p, q, (((0,), (0,)), ((), ())), preferred_element_type=_f32)
    filt = d(h_hi, w3hi_ref[...]) + (d(h_hi, w3lo_ref[...]) + d(h_lo, w3hi_ref[...]))
    n = pl.program_id(0) * t + lax.broadcasted_iota(jnp.int32, (t, 1), 0)
    lag = jnp.minimum(jnp.where(n < L, n, 2 * L - n), L - 1).astype(_f32)
    o_ref[...] = jnp.where(n == L, 0.0, filt) * jnp.exp(-(lag / (L - 1)) * delta_ref[...])


def _hy_filter(L, lp):
    n = jnp.arange(2 * L)
    lag = jnp.minimum(jnp.where(n < L, n, 2 * L - n), L - 1).astype(_f32)[:, None]
    t = lag / (L - 1)
    w = 2.0 * math.pi * lag / L
    bands = jnp.linspace(1e-4, HY_BANDS - 1, HY_BANDS, dtype=_f32)[None, :]
    z = jnp.concatenate([t, jnp.cos(bands * w), -jnp.sin(bands * w)], axis=-1)
    emb = z.shape[1]
    zt = jnp.pad(z, ((0, 0), (0, LANES - emb))).T
    w1t = jnp.pad(lp['hy_w1'].astype(_f32), ((0, LANES - emb), (0, 0))).T
    deltas = jnp.abs(jnp.linspace(HY_MIN_DECAY, HY_MAX_DECAY, HY_W, dtype=_f32))[None]
    tr = min(512, L)
    w3hi, w3lo = _split(lp['hy_w3'].astype(_f32))
    full = lambda a: pl.BlockSpec(a.shape, lambda i: (0,) * a.ndim)
    half = pl.BlockSpec((w3hi.shape[0], HY_W), lambda i: (0, i // (L // tr)))
    colv = lambda v: v.astype(_f32)[:, None]
    pre = [w1t, colv(lp['hy_b1']), lp['hy_w2'].astype(_f32).T, colv(lp['hy_b2'])]
    post = [colv(lp['hy_freq']), deltas]
    return pl.pallas_call(
        functools.partial(_filter_kernel, L=L),
        grid=(2 * L // tr,),
        in_specs=[pl.BlockSpec((LANES, tr), lambda i: (0, i))] + [full(a) for a in pre] + [half, half]
        + [full(a) for a in post],
        out_specs=pl.BlockSpec((tr, HY_W), lambda i: (i, 0)),
        out_shape=jax.ShapeDtypeStruct((2 * L, HY_W), _f32),
        compiler_params=_cparams("parallel"),
        name="hy_filter",
    )(zt, *pre, w3hi, w3lo, *post)


def _cs(num, den):
    ang = 2.0 * np.pi * (np.asarray(num, np.int64) % den) / den
    return np.cos(ang), np.sin(ang)


FFT_N2 = LANES


def _fft_rows(n1):
    return -(-(n1 // 2 + 1) // SUBLANES) * SUBLANES


def _fft_first_kernel(x_ref, mhi_ref, mlo_ref, o_ref):
    o_ref[...] = _dot2(mhi_ref[...], mlo_ref[...], x_ref[...])


def _fft_first(x2, n1):
    rows, cols = x2.shape
    kp = _fft_rows(n1)
    c, s = _cs(np.outer(np.arange(kp), np.arange(rows)), n1)
    mhi, mlo = _const_split(np.concatenate([c, -s], 0))
    tcol = 4096
    return pl.pallas_call(
        _fft_first_kernel,
        grid=(cols // tcol,),
        in_specs=[pl.BlockSpec((rows, tcol), lambda j: (0, j)),
                  pl.BlockSpec((2 * kp, rows), lambda j: (0, 0)),
                  pl.BlockSpec((2 * kp, rows), lambda j: (0, 0))],
        out_specs=pl.BlockSpec((2 * kp, tcol), lambda j: (0, j)),
        out_shape=jax.ShapeDtypeStruct((2 * kp, cols), _f32),
        compiler_params=_cparams("parallel"),
        name="fft_first",
    )(x2, mhi, mlo)


def _fft_mid_kernel(*refs, conv):
    if conv:
        a_ref, twr_ref, twi_ref, fhi_ref, flo_ref, h_ref, ghi_ref, glo_ref, o_ref = refs
    else:
        a_ref, twr_ref, twi_ref, fhi_ref, flo_ref, o_ref = refs
    n2 = FFT_N2
    reps = a_ref.shape[-1] // LANES
    twr = jnp.concatenate([twr_ref[0]] * reps, axis=1)
    twi = jnp.concatenate([twi_ref[0]] * reps, axis=1)
    ar, ai = a_ref[0, 0], a_ref[1, 0]
    x = _dot2(fhi_ref[...], flo_ref[...],
              jnp.concatenate([ar * twr - ai * twi, ar * twi + ai * twr], axis=0))
    if not conv:
        o_ref[0, 0] = x[:n2]
        o_ref[1, 0] = x[n2:]
        return
    xr, xi = x[:n2], x[n2:]
    hr, hi = h_ref[0, 0], h_ref[1, 0]
    b = _dot2(ghi_ref[...], glo_ref[...],
              jnp.concatenate([xr * hr - xi * hi, xr * hi + xi * hr], axis=0))
    br, bi = b[:n2], b[n2:]
    o_ref[0, 0] = br * twr + bi * twi
    o_ref[1, 0] = bi * twr - br * twi


def _fft_mid(a, n1, hf=None):
    n2 = FFT_N2
    kp, ch = a.shape[1], a.shape[-1]
    n = n1 * n2
    idx = jnp.arange(kp)[:, None] * jnp.arange(n2)[None, :]
    ang = (2.0 * math.pi / n) * (idx % n).astype(_f32)
    twr = jnp.broadcast_to(jnp.cos(ang)[:, :, None], (kp, n2, LANES))
    twi = jnp.broadcast_to(-jnp.sin(ang)[:, :, None], (kp, n2, LANES))
    c, s = _cs(np.outer(np.arange(n2), np.arange(n2)), n2)
    fhi, flo = _const_split(np.block([[c, s], [-s, c]]))
    blk = pl.BlockSpec((2, 1, n2, ch), lambda k: (0, k, 0, 0))
    tw = pl.BlockSpec((1, n2, LANES), lambda k: (k, 0, 0))
    mat = pl.BlockSpec((2 * n2, 2 * n2), lambda k: (0, 0))
    args, specs = [a, twr, twi, fhi, flo], [blk, tw, tw, mat, mat]
    if hf is not None:
        ghi, glo = _const_split(np.block([[c, -s], [s, c]]))
        args += [hf, ghi, glo]
        specs += [blk, mat, mat]
    return pl.pallas_call(
        functools.partial(_fft_mid_kernel, conv=hf is not None),
        grid=(kp,),
        in_specs=specs,
        out_specs=blk,
        out_shape=jax.ShapeDtypeStruct(a.shape, _f32),
        compiler_params=_cparams("parallel"),
        name="fft_mid",
    )(*args)


def _fft_last_kernel(c_ref, mhi_ref, mlo_ref, w_ref, x0g_ref, bias_ref, o_ref):
    y = _dot2(mhi_ref[...], mlo_ref[...], c_ref[...])
    o_ref[...] = (x0g_ref[...] * (y + w_ref[...] * bias_ref[...])).astype(o_ref.dtype)


def _fft_last(c2, n1, w2, x0g2, bias):
    rows, cols = w2.shape
    kp = _fft_rows(n1)
    n = n1 * FFT_N2
    c, s = _cs(np.outer(np.arange(rows), np.arange(kp)), n1)
    k1 = np.arange(kp)
    mult = np.where((k1 == 0) | (k1 == n1 // 2), 1.0, np.where(k1 < n1 // 2, 2.0, 0.0))
    mhi, mlo = _const_split(np.concatenate([c * mult, -s * mult], 1) / n)
    tcol = 4096
    bias_t = jnp.tile(bias.astype(_f32), tcol // bias.shape[0])[None]
    blk = pl.BlockSpec((rows, tcol), lambda j: (0, j))
    return pl.pallas_call(
        _fft_last_kernel,
        grid=(cols // tcol,),
        in_specs=[pl.BlockSpec((2 * kp, tcol), lambda j: (0, j)),
                  pl.BlockSpec((rows, 2 * kp), lambda j: (0, 0)),
                  pl.BlockSpec((rows, 2 * kp), lambda j: (0, 0)),
                  blk, blk, pl.BlockSpec((1, tcol), lambda j: (0, 0))],
        out_specs=blk,
        out_shape=jax.ShapeDtypeStruct((rows, cols), MXU_DTYPE),
        compiler_params=_cparams("parallel"),
        name="fft_last",
    )(c2, mhi, mlo, w2, x0g2, bias_t)


def _hy_small_kernel(w_ref, buf_ref, x0g_ref, bias_ref, fwhi_ref, fwlo_ref, fbhi_ref, fblo_ref,
                     ihi_ref, ilo_ref, o_ref):
    n = buf_ref.shape[0]
    w = w_ref[...]
    wf = _dot3(fwhi_ref[...], fwlo_ref[...], w)
    hf = _dot3(fbhi_ref[...], fblo_ref[...], buf_ref[...])
    wr, wi, hr, hi = wf[:n], wf[n:], hf[:n], hf[n:]
    y = _dot3(ihi_ref[...], ilo_ref[...], jnp.concatenate([wr * hr - wi * hi, wr * hi + wi * hr], axis=0))
    o_ref[...] = (x0g_ref[...] * (y + w * bias_ref[...])).astype(o_ref.dtype)


def _hy_conv_small(w, buf, x0g, bias):
    L, ch = w.shape
    n = 2 * L
    tc = 256
    c, s = _cs(np.outer(np.arange(n), np.arange(n)), n)
    fb = np.concatenate([c, -s], 0)
    mats = [*_const_split(fb[:, :L]), *_const_split(fb),
            *_const_split(np.concatenate([c[:L], -s[:L]], 1) / n)]
    col = lambda r: pl.BlockSpec((r, tc), lambda j: (0, j))
    return pl.pallas_call(
        _hy_small_kernel,
        grid=(ch // tc,),
        in_specs=[col(L), col(n), col(L), col(1)] + [pl.BlockSpec(m.shape, lambda j: (0, 0)) for m in mats],
        out_specs=col(L),
        out_shape=jax.ShapeDtypeStruct((L, ch), MXU_DTYPE),
        compiler_params=_cparams("parallel"),
        name="hy_conv_small",
    )(w, buf, x0g, bias.astype(_f32)[None], *mats)


def _hy_conv(w, buf, x0g, bias):
    L, ch = w.shape
    if L < 512:
        return _hy_conv_small(w, buf, x0g, bias)
    n1 = 2 * L // FFT_N2
    kp = _fft_rows(n1)
    cols = FFT_N2 * ch
    hf = _fft_mid(_fft_first(buf.reshape(n1, cols), n1).reshape(2, kp, FFT_N2, ch), n1)
    a = _fft_first(w.reshape(n1 // 2, cols), n1).reshape(2, kp, FFT_N2, ch)
    cc = _fft_mid(a, n1, hf)
    y = _fft_last(cc.reshape(2 * kp, cols), n1, w.reshape(n1 // 2, cols), x0g.reshape(n1 // 2, cols), bias)
    return y.reshape(L, ch)


def _zero_states():
    return (jnp.zeros((RET_HEADS, RET_DH, RET_DH), _f32), jnp.zeros((RET_HEADS, RET_DH, RET_DH), _f32),
            jnp.zeros((SSM_HEADS, SSM_STATE, SSM_HEADDIM), _f32),
            jnp.zeros((SSM_HEADS, SSM_STATE, SSM_HEADDIM), _f32))


def _recurrent(proj, L, lp, states, latent):
    pa, pb = proj
    q, k, v = _prep_ret(pa, pb, L, latent)
    ra, rb, ret_f, ret_b = _scan_ret(q, k, v, lp['ret_decay_logit'], states[0], states[1])
    cs, bs, xs, pack = _prep_ssd(pa, pb, L, lp['conv_ssm_w'], lp['conv_ssm_b'], lp['ssm_dt_bias'],
                                 lp['ssm_A_log'])
    sa, sb_, ssm_f, ssm_b = _scan_ssd(cs, bs, xs, pack, lp['ssm_D'], states[2], states[3])
    return (ra, rb, sa, sb_), (ret_f, ret_b, ssm_f, ssm_b)


def _mix(h, mod, lp, states, latent):
    L = h.shape[0]
    proj = _in_proj(h, mod[0], mod[1], lp)
    pb = proj[1]
    (ra, rb, sa, sb_), fin = _recurrent(proj, L, lp, states, latent)
    y_ret, y_ssm = _merge(ra, rb, sa, sb_, pb, lp['ssm_norm_w'])
    w, x0g = _prep_hy(pb, L, lp['conv_hy_w'], lp['conv_hy_b'])
    y_hy = _hy_conv(w, _hy_filter(L, lp), x0g, lp['hy_bias'])
    y_pool = _pool(pb, L, lp['pool_w'], lp['pool_scale'])
    out = _out_proj([y_hy, y_ret, y_pool, y_ssm], lp['w_out'], h, mod[2], lp['ln_g'], lp['ln_b'])
    return out, fin


def _context_states(hc, mod, lp):
    proj = _in_proj(hc, mod[0], mod[1], lp)
    _, fin = _recurrent(proj, hc.shape[0], lp, _zero_states(), False)
    return fin


def kernel(x, c, ctx, c_ctx, w_mod, b_mod, w_in, conv_ssm_w, conv_ssm_b, conv_hy_w, conv_hy_b,
           ret_decay_logit, ssm_A_log, ssm_dt_bias, ssm_D, ssm_norm_w, hy_w1, hy_b1, hy_w2, hy_b2,
           hy_w3, hy_freq, hy_bias, pool_w, pool_scale, w_out, ln_g, ln_b):
    assert x.shape[0] == 1
    h, hc = x[0], ctx[0]
    w_in_t = jnp.swapaxes(w_in, 1, 2)
    mods = _adaln(jnp.concatenate([c, c_ctx[None]], axis=0), w_mod, b_mod)
    for l in range(DEPTH):
        lp = {
            'layer': l, 'w_in_t': w_in_t, 'w_a_t': _cast_layer(w_in_t, l, N_A, N_A // 2, 512),
            'w_out': _cast_layer(w_out, l, w_out.shape[1], 1024, 2048)[0],
            'conv_ssm_w': conv_ssm_w[l], 'conv_ssm_b': conv_ssm_b[l],
            'conv_hy_w': conv_hy_w[l], 'conv_hy_b': conv_hy_b[l], 'ret_decay_logit': ret_decay_logit[l],
            'ssm_A_log': ssm_A_log[l], 'ssm_dt_bias': ssm_dt_bias[l], 'ssm_D': ssm_D[l],
            'ssm_norm_w': ssm_norm_w[l], 'hy_w1': hy_w1[l], 'hy_b1': hy_b1[l], 'hy_w2': hy_w2[l],
            'hy_b2': hy_b2[l], 'hy_w3': hy_w3[l], 'hy_freq': hy_freq[l], 'hy_bias': hy_bias[l],
            'pool_w': pool_w[l], 'pool_scale': pool_scale[l], 'ln_g': ln_g[l], 'ln_b': ln_b[l],
        }
        mod = lambda r: tuple(mods[l, r:r + 1, n * D_MODEL:(n + 1) * D_MODEL] for n in range(3))
        if l < DEPTH - 1:
            hc_next, states = _mix(hc, mod(1), lp, _zero_states(), False)
        else:
            states = _context_states(hc, mod(1), lp)
            hc_next = hc
        h, _ = _mix(h, mod(0), lp, states, True)
        hc = hc_next
    return h[None]
```

```python
import functools
import math

import jax
import jax.numpy as jnp
import numpy as np
from jax import lax
from jax.experimental import pallas as pl
from jax.experimental.pallas import tpu as pltpu

D_MODEL = 4096
DEPTH = 2
GRID_W = 64
MIX_W = D_MODEL
BR_W = MIX_W // 4
HY_W = RET_W = POOL_W = SSM_W = BR_W
RET_HEADS = 8
RET_DH = RET_W // RET_HEADS
ROPE_BASE = 10000.0
SSM_HEADDIM = 64
SSM_HEADS = SSM_W // SSM_HEADDIM
SSM_GROUPS = 4
SSM_HPG = SSM_HEADS // SSM_GROUPS
SSM_STATE = 128
SSM_GN = SSM_GROUPS * SSM_STATE
CHUNK = 128
POOL_WINDOWS = (2, 4, 8, 16)
POOL_GROUPS = len(POOL_WINDOWS)
POOL_GROUP = POOL_W // POOL_GROUPS
HY_BANDS = 16
HY_TARGET = 1e-2
HY_FAST = 0.3
HY_SLOW = 1.5
HY_MIN_DECAY = math.log(HY_TARGET) / HY_SLOW
HY_MAX_DECAY = math.log(HY_TARGET) / HY_FAST
ALPHA = (2.0 * DEPTH) ** 0.25
LN_EPS = 1e-5

O_RET_K = 0
O_RET_V = O_RET_K + RET_W
O_SSM_DT = O_RET_V + RET_W
O_SSM_X = O_SSM_DT + 2 * SSM_HEADS
O_SSM_B = O_SSM_X + SSM_W
O_RET_Q = O_SSM_B + SSM_GN
O_SSM_C = O_RET_Q + RET_W
O_HY = O_SSM_C + SSM_GN
O_POOL = O_HY + 3 * HY_W
O_GATE = O_POOL + POOL_W
N_IN = O_GATE + MIX_W

LANES = 128
SUBLANES = 8
N_A = O_SSM_DT + LANES
B_SX = 0
B_SB = O_SSM_B - O_SSM_X
B_RQ = O_RET_Q - O_SSM_X
B_SC = O_SSM_C - O_SSM_X
B_HY = O_HY - O_SSM_X
B_POOL = O_POOL - O_SSM_X
B_GATE = O_GATE - O_SSM_X
N_B = N_IN - O_SSM_X

VMEM_LIMIT_BYTES = 56 * 1024 * 1024
MXU_DTYPE = jnp.bfloat16
HIGHEST = lax.Precision.HIGHEST

_f32 = jnp.float32


def _cparams(*sem, vmem=VMEM_LIMIT_BYTES):
    return pltpu.CompilerParams(dimension_semantics=sem, vmem_limit_bytes=vmem)


def _silu(x):
    return x * jax.nn.sigmoid(x)


def _ln_rows(z):
    mu = jnp.mean(z, -1, keepdims=True)
    zc = z - mu
    var = jnp.mean(zc * zc, -1, keepdims=True)
    return zc * lax.rsqrt(var + LN_EPS)


def _cast_kernel(w_ref, o_ref):
    o_ref[...] = w_ref[...].astype(o_ref.dtype)


def _cast_layer(w, l, r, tr, tc):
    c = w.shape[2]
    assert r % tr == 0 and c % tc == 0
    return pl.pallas_call(
        _cast_kernel,
        grid=(r // tr, c // tc),
        in_specs=[pl.BlockSpec((1, tr, tc), lambda i, j: (l, i, j))],
        out_specs=pl.BlockSpec((1, tr, tc), lambda i, j: (0, i, j)),
        out_shape=jax.ShapeDtypeStruct((1, r, c), MXU_DTYPE),
        compiler_params=_cparams("parallel", "parallel"),
        name="cast_layer",
    )(w)


ADALN_ROWS = 32


def _adaln_kernel(c_ref, w_ref, b_ref, o_ref, xs_ref):
    @pl.when((pl.program_id(0) == 0) & (pl.program_id(1) == 0))
    def _():
        xs_ref[...] = _silu(c_ref[...])

    r, k = c_ref.shape[0], c_ref.shape[1]
    nj = w_ref.shape[-1] // LANES

    def body(t, accs):
        rows = pl.ds(pl.multiple_of(t * ADALN_ROWS, ADALN_ROWS), ADALN_ROWS)
        xs = [xs_ref[m, rows, :] for m in range(r)]
        ws = [w_ref[0, rows, j * LANES:(j + 1) * LANES] for j in range(nj)]
        return tuple(accs[m * nj + j] + xs[m] * ws[j] for m in range(r) for j in range(nj))

    accs = lax.fori_loop(0, k // ADALN_ROWS, body,
                         tuple(jnp.zeros((ADALN_ROWS, LANES), _f32) for _ in range(r * nj)), unroll=4)
    outs = [jnp.concatenate([jnp.sum(accs[m * nj + j], axis=0, keepdims=True) for j in range(nj)], axis=1)
            for m in range(r)]
    outs.append(jnp.zeros((SUBLANES - r, w_ref.shape[-1]), _f32))
    o_ref[0] = jnp.concatenate(outs, axis=0) + b_ref[0]


def _adaln(c_rows, w_mod, b_mod):
    dep, k, n = w_mod.shape
    r = c_rows.shape[0]
    tn = 512
    cb = jnp.broadcast_to(c_rows.astype(_f32)[:, :, None], (r, k, LANES))
    return pl.pallas_call(
        _adaln_kernel,
        grid=(dep, n // tn),
        in_specs=[pl.BlockSpec((r, k, LANES), lambda l, j: (0, 0, 0)),
                  pl.BlockSpec((1, k, tn), lambda l, j: (l, 0, j)),
                  pl.BlockSpec((1, 1, tn), lambda l, j: (l, 0, j))],
        out_specs=pl.BlockSpec((1, SUBLANES, tn), lambda l, j: (l, 0, j)),
        out_shape=jax.ShapeDtypeStruct((dep, SUBLANES, n), _f32),
        scratch_shapes=[pltpu.VMEM((r, k, LANES), _f32)],
        compiler_params=_cparams("arbitrary", "arbitrary"),
        name="adaln",
    )(cb, w_mod, b_mod[:, None, :])


def _matmul_nt_kernel(a_ref, b_ref, o_ref):
    o_ref[...] = lax.dot_general(a_ref[...], b_ref[0].astype(MXU_DTYPE), (((1,), (1,)), ((), ())),
                                 preferred_element_type=_f32).astype(o_ref.dtype)


def _matmul_nt(a, wt, l, row0, n, tm, tn, out_dtype):
    m, k = a.shape
    assert m % tm == 0 and n % tn == 0 and row0 % 32 == 0 and tn % 32 == 0
    return pl.pallas_call(
        _matmul_nt_kernel,
        grid=(m // tm, n // tn),
        in_specs=[pl.BlockSpec((tm, k), lambda i, j: (i, 0)),
                  pl.BlockSpec((pl.Element(1), pl.Element(tn), pl.Element(k)),
                               lambda i, j: (l, pl.multiple_of(row0 + j * tn, 32), 0))],
        out_specs=pl.BlockSpec((tm, tn), lambda i, j: (i, j)),
        out_shape=jax.ShapeDtypeStruct((m, n), out_dtype),
        compiler_params=_cparams("parallel", "parallel"),
        name="matmul_nt",
    )(a, wt)


def _modulate_kernel(h_ref, shift_ref, scale_ref, o_ref):
    o_ref[...] = (_ln_rows(h_ref[...]) * (1.0 + scale_ref[...]) + shift_ref[...]).astype(o_ref.dtype)


def _modulate(h, shift, scale):
    L, d = h.shape
    tr = min(512, L)
    vec = pl.BlockSpec((1, d), lambda i: (0, 0))
    return pl.pallas_call(
        _modulate_kernel,
        grid=(L // tr,),
        in_specs=[pl.BlockSpec((tr, d), lambda i: (i, 0)), vec, vec],
        out_specs=pl.BlockSpec((tr, d), lambda i: (i, 0)),
        out_shape=jax.ShapeDtypeStruct((L, d), MXU_DTYPE),
        compiler_params=_cparams("parallel"),
        name="modulate",
    )(h, shift, scale)


def _in_proj(h, shift, scale, lp):
    L = h.shape[0]
    u = _modulate(h, shift, scale)
    pa = _matmul_nt(u, lp['w_a_t'], 0, 0, N_A, 512 if L % 512 == 0 else 256, N_A, _f32)
    pb = _matmul_nt(u, lp['w_in_t'], lp['layer'], O_SSM_X, N_B, 1024 if L % 1024 == 0 else 256, 512, MXU_DTYPE)
    return pa, pb


def _out_proj_kernel(y0_ref, y1_ref, y2_ref, y3_ref, w_ref, h_ref, gate_ref, g_ref, b_ref, o_ref):
    out = None
    for n, y_ref in enumerate((y0_ref, y1_ref, y2_ref, y3_ref)):
        d = jnp.dot(y_ref[...], w_ref[n * BR_W:(n + 1) * BR_W, :], preferred_element_type=_f32)
        out = d if out is None else out + d
    z = ALPHA * h_ref[...] + gate_ref[...] * out
    o_ref[...] = _ln_rows(z) * g_ref[...] + b_ref[...]


OUT_PROJ_VMEM_BYTES = 60 * 1024 * 1024


def _out_proj(ys, w, h, gate, g, b):
    L, d = h.shape
    tm = 256
    lhs = pl.BlockSpec((tm, BR_W), lambda i: (i, 0))
    vec = pl.BlockSpec((1, d), lambda i: (0, 0))
    row = pl.BlockSpec((tm, d), lambda i: (i, 0))
    return pl.pallas_call(
        _out_proj_kernel,
        grid=(L // tm,),
        in_specs=[lhs] * len(ys) + [
            pl.BlockSpec(w.shape, lambda i: (0, 0), pipeline_mode=pl.Buffered(1)), row, vec, vec, vec],
        out_specs=row,
        out_shape=jax.ShapeDtypeStruct((L, d), _f32),
        compiler_params=_cparams("parallel", vmem=OUT_PROJ_VMEM_BYTES),
        name="out_proj",
    )(*ys, w, h, gate, g[None], b[None])


def _rope_tables(L):
    rows = L // GRID_W
    row = jnp.repeat(jnp.arange(rows), GRID_W).astype(_f32)
    col = jnp.tile(jnp.arange(GRID_W), rows).astype(_f32)
    nq = RET_DH // 4
    inv = ROPE_BASE ** (-jnp.arange(nq, dtype=_f32) / nq)
    ang = jnp.concatenate([row[:, None] * inv, col[:, None] * inv], -1)
    cos, sin = jnp.cos(ang), jnp.sin(ang)
    return jnp.concatenate([cos, cos], -1), jnp.concatenate([-sin, sin], -1)


def _prep_ret_kernel(qlo_ref, qhi_ref, k_ref, v_ref, cos_ref, sin_ref, qo_ref, ko_ref, vo_ref, *, rope):
    def rot(t):
        if not rope:
            return t
        return t * cos_ref[...] + pltpu.roll(t, RET_DH // 2, axis=1) * sin_ref[...]

    half = RET_HEADS // 2
    for h in range(RET_HEADS):
        sl = slice(h * RET_DH, (h + 1) * RET_DH)
        q_ref, qs = (qlo_ref, sl) if h < half else (qhi_ref, slice((h - half) * RET_DH, (h - half + 1) * RET_DH))
        qo_ref[:, sl] = rot(q_ref[:, qs].astype(_f32)).astype(qo_ref.dtype)
        ko_ref[:, sl] = rot(k_ref[:, sl] * (RET_DH ** -0.5)).astype(ko_ref.dtype)
    vo_ref[...] = v_ref[...].astype(vo_ref.dtype)


def _prep_ret(pa, pb, L, rope):
    tr = 256
    cos, sin = _rope_tables(L) if rope else (jnp.ones((L, LANES), _f32), jnp.zeros((L, LANES), _f32))
    hw = RET_W // 2
    qsp = lambda n: pl.BlockSpec((tr, hw), lambda i: (i, B_RQ // hw + n))
    sec = lambda c: pl.BlockSpec((tr, RET_W), lambda i: (i, c // RET_W))
    tab = pl.BlockSpec((tr, LANES), lambda i: (i, 0))
    out = pl.BlockSpec((tr, RET_W), lambda i: (i, 0))
    shp = jax.ShapeDtypeStruct((L, RET_W), MXU_DTYPE)
    return pl.pallas_call(
        functools.partial(_prep_ret_kernel, rope=rope),
        grid=(L // tr,),
        in_specs=[qsp(0), qsp(1), sec(O_RET_K), sec(O_RET_V), tab, tab],
        out_specs=[out, out, out],
        out_shape=[shp, shp, shp],
        compiler_params=_cparams("parallel"),
        name="prep_ret",
    )(pb, pb, pa, pa, cos, sin)


def _scan_ret_kernel(logit_ref, qi_ref, ki_ref, vi_ref, qj_ref, kj_ref, vj_ref, s0f_ref, s0b_ref,
                     ya_ref, yb_ref, finf_ref, finb_ref,
                     sf, sb, dmask, f_out, f_upd, f_all, b_out, b_upd, b_all):
    i = pl.program_id(0)
    c = CHUNK

    @pl.when(i == 0)
    def _():
        sf[...] = s0f_ref[...]
        sb[...] = s0b_ref[...]
        ii = lax.broadcasted_iota(jnp.int32, (c, c), 0).astype(_f32)
        jj = lax.broadcasted_iota(jnp.int32, (c, c), 1).astype(_f32)
        for h in range(RET_HEADS):
            def lg(d):
                x = logit_ref[d, h]
                v = -jnp.log1p(jnp.exp(-x))
                return jnp.broadcast_to(v[0:1, :], (c, c))
            lf, lb = lg(0), lg(1)
            dmask[h] = jnp.where(ii > jj, jnp.exp(lf * (ii - jj)),
                                 jnp.where(jj > ii, jnp.exp(lb * (jj - ii)), 2.0))
            f_out[h] = jnp.exp(lf * (ii + 1.0))
            f_upd[h] = jnp.exp(lf * (c - 1.0 - ii))
            f_all[h] = jnp.exp(lf * float(c))
            b_out[h] = jnp.exp(lb * (c - ii))
            b_upd[h] = jnp.exp(lb * ii)
            b_all[h] = jnp.exp(lb * float(c))

    tn = (((0,), (0,)), ((), ()))
    nt = (((1,), (1,)), ((), ()))
    heads = range(RET_HEADS)
    sls = [slice(h * RET_DH, (h + 1) * RET_DH) for h in heads]
    scores = [lax.dot_general(qi_ref[:, sl], ki_ref[:, sl], nt, preferred_element_type=_f32) for sl in sls]
    upd_f = [lax.dot_general((ki_ref[:, sl].astype(_f32) * f_upd[h]).astype(MXU_DTYPE), vi_ref[:, sl], tn,
                             preferred_element_type=_f32) for h, sl in zip(heads, sls)]
    upd_b = [lax.dot_general((kj_ref[:, sl].astype(_f32) * b_upd[h]).astype(MXU_DTYPE), vj_ref[:, sl], tn,
                             preferred_element_type=_f32) for h, sl in zip(heads, sls)]
    for h, sl in zip(heads, sls):
        lhs = jnp.concatenate([(scores[h] * dmask[h]).astype(MXU_DTYPE),
                               (qi_ref[:, sl].astype(_f32) * f_out[h]).astype(MXU_DTYPE)], axis=1)
        rhs = jnp.concatenate([vi_ref[:, sl], sf[h].astype(MXU_DTYPE)], axis=0)
        ya_ref[:, sl] = jnp.dot(lhs, rhs, preferred_element_type=_f32)
        yb_ref[:, sl] = jnp.dot((qj_ref[:, sl].astype(_f32) * b_out[h]).astype(MXU_DTYPE),
                                sb[h].astype(MXU_DTYPE), preferred_element_type=_f32)
    for h in heads:
        sf[h] = f_all[h] * sf[h] + upd_f[h]
        sb[h] = b_all[h] * sb[h] + upd_b[h]

    @pl.when(i == pl.num_programs(0) - 1)
    def _():
        finf_ref[...] = sf[...]
        finb_ref[...] = sb[...]


def _scan_ret(q, k, v, logit, s0f, s0b):
    L = q.shape[0]
    nc = L // CHUNK
    logit_b = jnp.broadcast_to(logit.astype(_f32)[:, :, None, None], (2, RET_HEADS, SUBLANES, LANES))
    fw = pl.BlockSpec((CHUNK, RET_W), lambda i: (i, 0))
    bw = pl.BlockSpec((CHUNK, RET_W), lambda i: (nc - 1 - i, 0))
    st = pl.BlockSpec((RET_HEADS, RET_DH, RET_DH), lambda i: (0, 0, 0))
    yshape = jax.ShapeDtypeStruct((L, RET_W), _f32)
    sshape = jax.ShapeDtypeStruct((RET_HEADS, RET_DH, RET_DH), _f32)
    tile = pltpu.VMEM((RET_HEADS, CHUNK, CHUNK), _f32)
    return pl.pallas_call(
        _scan_ret_kernel,
        grid=(nc,),
        in_specs=[pl.BlockSpec((2, RET_HEADS, SUBLANES, LANES), lambda i: (0, 0, 0, 0)),
                  fw, fw, fw, bw, bw, bw, st, st],
        out_specs=[fw, bw, st, st],
        out_shape=[yshape, yshape, sshape, sshape],
        scratch_shapes=[pltpu.VMEM((RET_HEADS, RET_DH, RET_DH), _f32)] * 2 + [tile] * 7,
        compiler_params=_cparams("arbitrary"),
        name="scan_ret",
    )(logit_b, q, k, v, q, k, v, s0f, s0b)


def _shift_rows(x, prev_row, next_row):
    r = x.shape[0]
    rid = lax.broadcasted_iota(jnp.int32, x.shape, 0)
    up = jnp.where(rid == 0, prev_row, pltpu.roll(x, 1, axis=0))
    dn = jnp.where(rid == r - 1, next_row, pltpu.roll(x, r - 1, axis=0))
    return up, dn


HALO = 16


def _conv3(x_ref, prev_ref, next_ref, w_ref, b_ref, has_prev, has_next):
    x = x_ref[...].astype(_f32)
    prev_row = prev_ref[...].astype(_f32)[HALO - 1:HALO, :] * has_prev
    next_row = next_ref[...].astype(_f32)[0:1, :] * has_next
    up, dn = _shift_rows(x, prev_row, next_row)
    return up * w_ref[0:1, :] + x * w_ref[1:2, :] + dn * w_ref[2:3, :] + b_ref[...]


def _halo_specs(tr, L, width, col):
    nb = tr // HALO
    last = L // HALO - 1
    cb = col // width
    return [pl.BlockSpec((tr, width), lambda i: (i, cb)),
            pl.BlockSpec((HALO, width), lambda i: (jnp.maximum(i * nb - 1, 0), cb)),
            pl.BlockSpec((HALO, width), lambda i: (jnp.minimum((i + 1) * nb, last), cb))]


def _prep_ssd_kernel(x_ref, xp_ref, xn_ref, b_ref, bp_ref, bn_ref, c_ref, cp_ref, cn_ref, dt_ref,
                     w_ref, cb_ref, dtb_ref, alog_ref, co_ref, bo_ref, xo_ref, pack_ref):
    i = pl.program_id(0)
    has_prev = (i > 0).astype(_f32)
    has_next = (i < pl.num_programs(0) - 1).astype(_f32)

    def conv(lo, hi, t_ref, p_ref, n_ref, o_ref):
        y = _conv3(t_ref, p_ref, n_ref, w_ref.at[:, lo:hi], cb_ref.at[:, lo:hi], has_prev, has_next)
        o_ref[...] = _silu(y).astype(o_ref.dtype)

    conv(0, SSM_W, x_ref, xp_ref, xn_ref, xo_ref)
    conv(SSM_W, SSM_W + SSM_GN, b_ref, bp_ref, bn_ref, bo_ref)
    conv(SSM_W + SSM_GN, SSM_W + 2 * SSM_GN, c_ref, cp_ref, cn_ref, co_ref)
    z = dt_ref[...] + dtb_ref[...]
    dt = jnp.maximum(z, 0.0) + jnp.log1p(jnp.exp(-jnp.abs(z)))
    a = dt * (-jnp.exp(alog_ref[...]))
    c = CHUNK
    ii = lax.broadcasted_iota(jnp.int32, (c, c), 0)
    jj = lax.broadcasted_iota(jnp.int32, (c, c), 1)
    lower = (jj <= ii).astype(_f32)
    upper = (jj >= ii).astype(_f32)
    lane = lax.broadcasted_iota(jnp.int32, (c, LANES), 1)
    dt_sh = pltpu.roll(dt, 2 * SSM_HEADS, axis=1)
    for n in range(x_ref.shape[0] // c):
        rs = slice(n * c, (n + 1) * c)
        pre = jnp.dot(lower, a[rs], precision=HIGHEST, preferred_element_type=_f32)
        suf = jnp.dot(upper, a[rs], precision=HIGHEST, preferred_element_type=_f32)
        pack_ref[rs, :] = jnp.where(lane < SSM_HEADS, pre,
                                    jnp.where(lane < 2 * SSM_HEADS, suf, dt_sh[rs]))


def _prep_ssd(pa, pb, L, conv_w, conv_b, dt_bias, a_log):
    tr = 256
    w = jnp.pad(conv_w.astype(_f32), ((0, SUBLANES - 3), (0, 0)))
    lanes = lambda t: jnp.pad(t.astype(_f32).reshape(1, 2 * SSM_HEADS), ((0, 0), (0, LANES - 2 * SSM_HEADS)))
    wd = SSM_W + 2 * SSM_GN
    row = lambda c: pl.BlockSpec((tr, c), lambda i: (i, 0))
    return pl.pallas_call(
        _prep_ssd_kernel,
        grid=(L // tr,),
        in_specs=_halo_specs(tr, L, SSM_W, B_SX) + _halo_specs(tr, L, SSM_GN, B_SB)
        + _halo_specs(tr, L, SSM_GN, B_SC) + [
            pl.BlockSpec((tr, LANES), lambda i: (i, O_SSM_DT // LANES)),
            pl.BlockSpec((SUBLANES, wd), lambda i: (0, 0)),
            pl.BlockSpec((1, wd), lambda i: (0, 0)),
            pl.BlockSpec((1, LANES), lambda i: (0, 0)),
            pl.BlockSpec((1, LANES), lambda i: (0, 0))],
        out_specs=[row(SSM_GN), row(SSM_GN), row(SSM_W), row(LANES)],
        out_shape=[jax.ShapeDtypeStruct((L, SSM_GN), MXU_DTYPE),
                   jax.ShapeDtypeStruct((L, SSM_GN), MXU_DTYPE),
                   jax.ShapeDtypeStruct((L, SSM_W), MXU_DTYPE),
                   jax.ShapeDtypeStruct((L, LANES), _f32)],
        compiler_params=_cparams("parallel"),
        name="prep_ssd",
    )(*([pb] * 9), pa, w, conv_b.astype(_f32)[None], lanes(dt_bias), lanes(a_log))


def _scan_ssd_kernel(ci_ref, bi_ref, xi_ref, pi_ref, cj_ref, bj_ref, xj_ref, pj_ref, dskip_ref,
                     s0f_ref, s0b_ref, ya_ref, yb_ref, finf_ref, finb_ref, sf, sb):
    i = pl.program_id(0)
    c = CHUNK
    H = SSM_HEADS

    @pl.when(i == 0)
    def _():
        sf[...] = s0f_ref[...]
        sb[...] = s0b_ref[...]

    tn = (((0,), (0,)), ((), ()))
    nt = (((1,), (1,)), ((), ()))
    ii = lax.broadcasted_iota(jnp.int32, (c, c), 0)
    jj = lax.broadcasted_iota(jnp.int32, (c, c), 1)
    low = lax.broadcasted_iota(jnp.int32, (c, LANES), 1) < SSM_HEADDIM
    low2 = lax.broadcasted_iota(jnp.int32, (2 * SSM_STATE, LANES), 1) < SSM_HEADDIM
    diag = (lax.broadcasted_iota(jnp.int32, (2 * SSM_STATE, LANES), 0) < SSM_STATE) == low2
    pi = pi_ref[...]
    pit = pi.T
    pj = pj_ref[...]
    ei = jnp.exp(jnp.minimum(pi, 0.0))
    ej = jnp.exp(jnp.minimum(pj, 0.0))
    tot_i, tot_j = pi[c - 1:c, :], pj[0:1, :]
    wi = jnp.exp(jnp.minimum(tot_i - pi, 0.0)) * pltpu.roll(pi, LANES - 2 * H, axis=1)
    wj = jnp.exp(jnp.minimum(tot_j - pj, 0.0)) * pltpu.roll(pj, LANES - 2 * H, axis=1)
    eti, etj = jnp.exp(jnp.minimum(tot_i, 0.0)), jnp.exp(jnp.minimum(tot_j, 0.0))
    colb = lambda t, k: jnp.broadcast_to(t[:, k:k + 1], (c, LANES))
    for g in range(SSM_GROUPS):
        gs = slice(g * SSM_STATE, (g + 1) * SSM_STATE)
        ci, bi = ci_ref[:, gs], bi_ref[:, gs]
        cj, bj = cj_ref[:, gs], bj_ref[:, gs]
        cb = lax.dot_general(ci, bi, nt, preferred_element_type=_f32)
        ci32, bi32, cj32, bj32 = (t.astype(_f32) for t in (ci, bi, cj, bj))
        for pp in range(SSM_HPG // 2):
            q = g * (SSM_HPG // 2) + pp
            heads = (2 * q, 2 * q + 1)
            xs = slice(q * LANES, (q + 1) * LANES)
            x = xi_ref[:, xs]
            x32 = x.astype(_f32)
            scores, cw, bw = [], [], []
            for h in heads:
                row = lambda o: pit[o + h:o + h + 1, :]
                mf = jnp.where(ii >= jj, jnp.exp(jnp.minimum(colb(pi, h) - row(0), 0.0)), 0.0) * row(2 * H)
                mb = jnp.where(jj >= ii, jnp.exp(jnp.minimum(colb(pi, H + h) - row(H), 0.0)), 0.0) * row(3 * H)
                scores.append((cb * (mf + mb)).astype(MXU_DTYPE))
                cw.append((ci32 * colb(ei, h)).astype(MXU_DTYPE))
                bw.append((bi32 * colb(wi, h)).astype(MXU_DTYPE))
            xa = jnp.where(low, x32, 0.0).astype(MXU_DTYPE)
            xb = jnp.where(low, 0.0, x32).astype(MXU_DTYPE)
            lhs = jnp.concatenate(scores + cw, axis=1)
            rhs = jnp.concatenate([xa, xb, sf[q].astype(MXU_DTYPE)], axis=0)
            y = jnp.dot(lhs, rhs, preferred_element_type=_f32)
            ya_ref[:, xs] = y + dskip_ref[:, xs] * x32
            upd = lax.dot_general(jnp.concatenate(bw, axis=1), x, tn, preferred_element_type=_f32)
            dec = jnp.where(low2, eti[0:1, heads[0]:heads[0] + 1], eti[0:1, heads[1]:heads[1] + 1])
            sf[q] = dec * sf[q] + jnp.where(diag, upd, 0.0)
            x = xj_ref[:, xs]
            cw = [(cj32 * colb(ej, H + h)).astype(MXU_DTYPE) for h in heads]
            bw = [(bj32 * colb(wj, H + h)).astype(MXU_DTYPE) for h in heads]
            yb_ref[:, xs] = jnp.dot(jnp.concatenate(cw, axis=1), sb[q].astype(MXU_DTYPE),
                                    preferred_element_type=_f32)
            upd = lax.dot_general(jnp.concatenate(bw, axis=1), x, tn, preferred_element_type=_f32)
            dec = jnp.where(low2, etj[0:1, H + heads[0]:H + heads[0] + 1], etj[0:1, H + heads[1]:H + heads[1] + 1])
            sb[q] = dec * sb[q] + jnp.where(diag, upd, 0.0)

    @pl.when(i == pl.num_programs(0) - 1)
    def _():
        finf_ref[...] = sf[...]
        finb_ref[...] = sb[...]


def _pair_states(s):
    s = s.reshape(SSM_HEADS // 2, 2, SSM_STATE, SSM_HEADDIM)
    z = jnp.zeros_like(s[:, 0])
    return jnp.concatenate([jnp.concatenate([s[:, 0], z], -1), jnp.concatenate([z, s[:, 1]], -1)], 1)


def _unpair_states(s):
    top, bot = s[:, :SSM_STATE, :SSM_HEADDIM], s[:, SSM_STATE:, SSM_HEADDIM:]
    return jnp.stack([top, bot], 1).reshape(SSM_HEADS, SSM_STATE, SSM_HEADDIM)


def _scan_ssd(cs, bs, xs, pack, d_skip, s0f, s0b):
    L = xs.shape[0]
    nc = L // CHUNK
    dvec = jnp.repeat(d_skip.astype(_f32), SSM_HEADDIM)[None]
    fw = lambda w: pl.BlockSpec((CHUNK, w), lambda i: (i, 0))
    bw = lambda w: pl.BlockSpec((CHUNK, w), lambda i: (nc - 1 - i, 0))
    pshape = (SSM_HEADS // 2, 2 * SSM_STATE, 2 * SSM_HEADDIM)
    st = pl.BlockSpec(pshape, lambda i: (0, 0, 0))
    yshape = jax.ShapeDtypeStruct((L, SSM_W), _f32)
    sshape = jax.ShapeDtypeStruct(pshape, _f32)
    ya, yb, fin_f, fin_b = pl.pallas_call(
        _scan_ssd_kernel,
        grid=(nc,),
        in_specs=[fw(SSM_GN), fw(SSM_GN), fw(SSM_W), fw(LANES), bw(SSM_GN), bw(SSM_GN), bw(SSM_W), bw(LANES),
                  pl.BlockSpec((1, SSM_W), lambda i: (0, 0)), st, st],
        out_specs=[fw(SSM_W), bw(SSM_W), st, st],
        out_shape=[yshape, yshape, sshape, sshape],
        scratch_shapes=[pltpu.VMEM(pshape, _f32)] * 2,
        compiler_params=_cparams("arbitrary"),
        name="scan_ssd",
    )(cs, bs, xs, pack, cs, bs, xs, pack, dvec, _pair_states(s0f), _pair_states(s0b))
    return ya, yb, _unpair_states(fin_f), _unpair_states(fin_b)


def _merge_kernel(ra_ref, rb_ref, sa_ref, sb_ref, gr_ref, gs_ref, nw_ref, yr_ref, ys_ref):
    for h in range(RET_HEADS):
        sl = slice(h * RET_DH, (h + 1) * RET_DH)
        y = _ln_rows(ra_ref[:, sl] + rb_ref[:, sl])
        yr_ref[:, sl] = (y * _silu(gr_ref[:, sl].astype(_f32))).astype(yr_ref.dtype)
    gw = SSM_W // SSM_GROUPS
    for g in range(SSM_GROUPS):
        sl = slice(g * gw, (g + 1) * gw)
        y = (sa_ref[:, sl] + sb_ref[:, sl]) * _silu(gs_ref[:, sl].astype(_f32))
        y = y * lax.rsqrt(jnp.mean(y * y, -1, keepdims=True) + LN_EPS)
        ys_ref[:, sl] = (y * nw_ref[:, sl]).astype(ys_ref.dtype)


def _gate_spec(tr, n):
    return pl.BlockSpec((tr, BR_W), lambda i: (i, B_GATE // BR_W + n))


def _merge(ra, rb, sa, sb_, pb, norm_w):
    L = ra.shape[0]
    tr = min(512, L)
    row = pl.BlockSpec((tr, BR_W), lambda i: (i, 0))
    shp = jax.ShapeDtypeStruct((L, BR_W), MXU_DTYPE)
    return pl.pallas_call(
        _merge_kernel,
        grid=(L // tr,),
        in_specs=[row, row, row, row, _gate_spec(tr, 1), _gate_spec(tr, 3),
                  pl.BlockSpec((1, BR_W), lambda i: (0, 0))],
        out_specs=[row, row],
        out_shape=[shp, shp],
        compiler_params=_cparams("parallel"),
        name="merge",
    )(ra, rb, sa, sb_, pb, pb, norm_w.astype(_f32)[None])


def _pool_kernel(x_ref, prev_ref, next_ref, g_ref, pw_ref, ps_ref, o_ref, *, L):
    i = pl.program_id(0)
    t = x_ref.shape[0]
    halo = HALO
    has_prev = (i > 0).astype(_f32)
    has_next = (i < pl.num_programs(0) - 1).astype(_f32)
    pos = i * t + lax.broadcasted_iota(jnp.int32, (t, 1), 0)
    for g, win in enumerate(POOL_WINDOWS):
        sl = slice(g * POOL_GROUP, (g + 1) * POOL_GROUP)
        x = x_ref[:, sl].astype(_f32)
        s = jnp.concatenate([prev_ref[:, sl].astype(_f32) * has_prev, x,
                             next_ref[:, sl].astype(_f32) * has_next], axis=0)
        rows = t + 2 * halo
        width = 1
        while width < win:
            s = s + pltpu.roll(s, rows - width, axis=0)
            width *= 2
        off = halo - win // 2
        if off:
            s = pltpu.roll(s, rows - off, axis=0)
        cnt = jnp.minimum(pos + win // 2, L) - jnp.maximum(pos - win // 2, 0)
        d = s[:t] / cnt.astype(_f32) - x
        y = jnp.dot(d.astype(MXU_DTYPE), pw_ref[g], preferred_element_type=_f32)
        o_ref[:, sl] = (y * ps_ref[:, sl] * _silu(g_ref[:, sl].astype(_f32))).astype(o_ref.dtype)


def _pool(pb, L, pool_w, pool_scale):
    tr = 256
    return pl.pallas_call(
        functools.partial(_pool_kernel, L=L),
        grid=(L // tr,),
        in_specs=_halo_specs(tr, L, POOL_W, B_POOL) + [
            _gate_spec(tr, 2),
            pl.BlockSpec((POOL_GROUPS, POOL_GROUP, POOL_GROUP), lambda i: (0, 0, 0)),
            pl.BlockSpec((1, POOL_W), lambda i: (0, 0))],
        out_specs=pl.BlockSpec((tr, POOL_W), lambda i: (i, 0)),
        out_shape=jax.ShapeDtypeStruct((L, POOL_W), MXU_DTYPE),
        compiler_params=_cparams("parallel"),
        name="pool",
    )(pb, pb, pb, pb, pool_w.astype(MXU_DTYPE), pool_scale.astype(_f32)[None])


def _prep_hy_kernel(v_ref, vp_ref, vn_ref, x0_ref, x0p_ref, x0n_ref, x1_ref, x1p_ref, x1n_ref,
                    g_ref, w_ref, b_ref, wo_ref, x0g_ref):
    i = pl.program_id(0)
    has_prev = (i > 0).astype(_f32)
    has_next = (i < pl.num_programs(0) - 1).astype(_f32)

    def conv(n, x_ref, p_ref, n_ref):
        sl = slice(n * HY_W, (n + 1) * HY_W)
        return _conv3(x_ref, p_ref, n_ref, w_ref.at[:, sl], b_ref.at[:, sl], has_prev, has_next)

    hv = conv(0, v_ref, vp_ref, vn_ref)
    hx0 = conv(1, x0_ref, x0p_ref, x0n_ref)
    hx1 = conv(2, x1_ref, x1p_ref, x1n_ref)
    wo_ref[...] = hx1 * hv
    x0g_ref[...] = hx0 * _silu(g_ref[...].astype(_f32))


def _prep_hy(pb, L, conv_w, conv_b):
    tr = 256
    w = jnp.pad(conv_w.astype(_f32), ((0, SUBLANES - 3), (0, 0)))
    row = pl.BlockSpec((tr, HY_W), lambda i: (i, 0))
    shp = jax.ShapeDtypeStruct((L, HY_W), _f32)
    secs = sum((_halo_specs(tr, L, HY_W, B_HY + n * HY_W) for n in range(3)), [])
    return pl.pallas_call(
        _prep_hy_kernel,
        grid=(L // tr,),
        in_specs=secs + [_gate_spec(tr, 0),
                         pl.BlockSpec((SUBLANES, 3 * HY_W), lambda i: (0, 0)),
                         pl.BlockSpec((1, 3 * HY_W), lambda i: (0, 0))],
        out_specs=[row, row],
        out_shape=[shp, shp],
        compiler_params=_cparams("parallel"),
        name="prep_hy",
    )(*([pb] * 10), w, conv_b.astype(_f32)[None])


def _split(x):
    hi = x.astype(MXU_DTYPE)
    return hi, (x - hi.astype(_f32)).astype(MXU_DTYPE)


def _dot3(a_hi, a_lo, b):
    b_hi, b_lo = _split(b)
    d = lambda p, q: jnp.dot(p, q, preferred_element_type=_f32)
    return d(a_hi, b_hi) + (d(a_hi, b_lo) + d(a_lo, b_hi))


def _dot2(a_hi, a_lo, b):
    b = b.astype(MXU_DTYPE)
    d = lambda p, q: jnp.dot(p, q, preferred_element_type=_f32)
    return d(a_hi, b) + d(a_lo, b)


def _const_split(m):
    return _split(jnp.asarray(m, _f32))


def _filter_kernel(z_ref, w1_ref, b1_ref, w2_ref, b2_ref, w3hi_ref, w3lo_ref, freq_ref, delta_ref, o_ref, *, L):
    t = z_ref.shape[1]
    dot = functools.partial(jnp.dot, precision=HIGHEST, preferred_element_type=_f32)
    freq = freq_ref[...]
    hdn = jnp.sin(freq * (dot(w1_ref[...], z_ref[...]) + b1_ref[...]))
    hdn = jnp.sin(freq * (dot(w2_ref[...], hdn) + b2_ref[...]))
    h_hi, h_lo = _split(hdn)
    d = lambda p, q: lax.dot_general(p, q, (((0,), (0,)), ((), ())), preferred_element_type=_f32)
    filt = d(h_hi, w3hi_ref[...]) + (d(h_hi, w3lo_ref[...]) + d(h_lo, w3hi_ref[...]))
    n = pl.program_id(0) * t + lax.broadcasted_iota(jnp.int32, (t, 1), 0)
    lag = jnp.minimum(jnp.where(n < L, n, 2 * L - n), L - 1).astype(_f32)
    o_ref[...] = jnp.where(n == L, 0.0, filt) * jnp.exp(-(lag / (L - 1)) * delta_ref[...])


def _hy_filter(L, lp):
    n = jnp.arange(2 * L)
    lag = jnp.minimum(jnp.where(n < L, n, 2 * L - n), L - 1).astype(_f32)[:, None]
    t = lag / (L - 1)
    w = 2.0 * math.pi * lag / L
    bands = jnp.linspace(1e-4, HY_BANDS - 1, HY_BANDS, dtype=_f32)[None, :]
    z = jnp.concatenate([t, jnp.cos(bands * w), -jnp.sin(bands * w)], axis=-1)
    emb = z.shape[1]
    zt = jnp.pad(z, ((0, 0), (0, LANES - emb))).T
    w1t = jnp.pad(lp['hy_w1'].astype(_f32), ((0, LANES - emb), (0, 0))).T
    deltas = jnp.abs(jnp.linspace(HY_MIN_DECAY, HY_MAX_DECAY, HY_W, dtype=_f32))[None]
    tr = min(512, L)
    w3hi, w3lo = _split(lp['hy_w3'].astype(_f32))
    full = lambda a: pl.BlockSpec(a.shape, lambda i: (0,) * a.ndim)
    half = pl.BlockSpec((w3hi.shape[0], HY_W), lambda i: (0, i // (L // tr)))
    colv = lambda v: v.astype(_f32)[:, None]
    pre = [w1t, colv(lp['hy_b1']), lp['hy_w2'].astype(_f32).T, colv(lp['hy_b2'])]
    post = [colv(lp['hy_freq']), deltas]
    return pl.pallas_call(
        functools.partial(_filter_kernel, L=L),
        grid=(2 * L // tr,),
        in_specs=[pl.BlockSpec((LANES, tr), lambda i: (0, i))] + [full(a) for a in pre] + [half, half]
        + [full(a) for a in post],
        out_specs=pl.BlockSpec((tr, HY_W), lambda i: (i, 0)),
        out_shape=jax.ShapeDtypeStruct((2 * L, HY_W), _f32),
        compiler_params=_cparams("parallel"),
        name="hy_filter",
    )(zt, *pre, w3hi, w3lo, *post)


def _cs(num, den):
    ang = 2.0 * np.pi * (np.asarray(num, np.int64) % den) / den
    return np.cos(ang), np.sin(ang)


FFT_N2 = LANES


def _fft_rows(n1):
    return -(-(n1 // 2 + 1) // SUBLANES) * SUBLANES


def _fft_first_kernel(x_ref, mhi_ref, mlo_ref, o_ref):
    o_ref[...] = _dot2(mhi_ref[...], mlo_ref[...], x_ref[...])


def _fft_first(x2, n1):
    rows, cols = x2.shape
    kp = _fft_rows(n1)
    c, s = _cs(np.outer(np.arange(kp), np.arange(rows)), n1)
    mhi, mlo = _const_split(np.concatenate([c, -s], 0))
    tcol = 4096
    return pl.pallas_call(
        _fft_first_kernel,
        grid=(cols // tcol,),
        in_specs=[pl.BlockSpec((rows, tcol), lambda j: (0, j)),
                  pl.BlockSpec((2 * kp, rows), lambda j: (0, 0)),
                  pl.BlockSpec((2 * kp, rows), lambda j: (0, 0))],
        out_specs=pl.BlockSpec((2 * kp, tcol), lambda j: (0, j)),
        out_shape=jax.ShapeDtypeStruct((2 * kp, cols), _f32),
        compiler_params=_cparams("parallel"),
        name="fft_first",
    )(x2, mhi, mlo)


def _fft_mid_kernel(a_ref, f_ref, twr_ref, twi_ref, fhi_ref, flo_ref, ghi_ref, glo_ref, o_ref):
    n2 = FFT_N2
    reps = a_ref.shape[-1] // LANES
    twr = jnp.concatenate([twr_ref[0]] * reps, axis=1)
    twi = jnp.concatenate([twi_ref[0]] * reps, axis=1)

    def second_stage(t_ref):
        tr, ti = t_ref[0, 0], t_ref[1, 0]
        y = _dot2(fhi_ref[...], flo_ref[...], jnp.concatenate([tr * twr - ti * twi, tr * twi + ti * twr], axis=0))
        return y[:n2], y[n2:]

    xr, xi = second_stage(a_ref)
    hr, hi = second_stage(f_ref)
    b = _dot2(ghi_ref[...], glo_ref[...],
              jnp.concatenate([xr * hr - xi * hi, xr * hi + xi * hr], axis=0))
    br, bi = b[:n2], b[n2:]
    o_ref[0, 0] = br * twr + bi * twi
    o_ref[1, 0] = bi * twr - br * twi


def _fft_mid(a, f, n1):
    n2 = FFT_N2
    kp, ch = a.shape[1], a.shape[-1]
    n = n1 * n2
    idx = jnp.arange(kp)[:, None] * jnp.arange(n2)[None, :]
    ang = (2.0 * math.pi / n) * (idx % n).astype(_f32)
    twr = jnp.broadcast_to(jnp.cos(ang)[:, :, None], (kp, n2, LANES))
    twi = jnp.broadcast_to(-jnp.sin(ang)[:, :, None], (kp, n2, LANES))
    c, s = _cs(np.outer(np.arange(n2), np.arange(n2)), n2)
    fhi, flo = _const_split(np.block([[c, s], [-s, c]]))
    blk = pl.BlockSpec((2, 1, n2, ch), lambda k: (0, k, 0, 0))
    tw = pl.BlockSpec((1, n2, LANES), lambda k: (k, 0, 0))
    mat = pl.BlockSpec((2 * n2, 2 * n2), lambda k: (0, 0))
    ghi, glo = _const_split(np.block([[c, -s], [s, c]]))
    return pl.pallas_call(
        _fft_mid_kernel,
        grid=(kp,),
        in_specs=[blk, blk, tw, tw, mat, mat, mat, mat],
        out_specs=blk,
        out_shape=jax.ShapeDtypeStruct(a.shape, _f32),
        compiler_params=_cparams("parallel"),
        name="fft_mid",
    )(a, f, twr, twi, fhi, flo, ghi, glo)


def _fft_last_kernel(c_ref, mhi_ref, mlo_ref, w_ref, x0g_ref, bias_ref, o_ref):
    y = _dot2(mhi_ref[...], mlo_ref[...], c_ref[...])
    o_ref[...] = (x0g_ref[...] * (y + w_ref[...] * bias_ref[...])).astype(o_ref.dtype)


def _fft_last(c2, n1, w2, x0g2, bias):
    rows, cols = w2.shape
    kp = _fft_rows(n1)
    n = n1 * FFT_N2
    c, s = _cs(np.outer(np.arange(rows), np.arange(kp)), n1)
    k1 = np.arange(kp)
    mult = np.where((k1 == 0) | (k1 == n1 // 2), 1.0, np.where(k1 < n1 // 2, 2.0, 0.0))
    mhi, mlo = _const_split(np.concatenate([c * mult, -s * mult], 1) / n)
    tcol = 4096
    bias_t = jnp.tile(bias.astype(_f32), tcol // bias.shape[0])[None]
    blk = pl.BlockSpec((rows, tcol), lambda j: (0, j))
    return pl.pallas_call(
        _fft_last_kernel,
        grid=(cols // tcol,),
        in_specs=[pl.BlockSpec((2 * kp, tcol), lambda j: (0, j)),
                  pl.BlockSpec((rows, 2 * kp), lambda j: (0, 0)),
                  pl.BlockSpec((rows, 2 * kp), lambda j: (0, 0)),
                  blk, blk, pl.BlockSpec((1, tcol), lambda j: (0, 0))],
        out_specs=blk,
        out_shape=jax.ShapeDtypeStruct((rows, cols), MXU_DTYPE),
        compiler_params=_cparams("parallel"),
        name="fft_last",
    )(c2, mhi, mlo, w2, x0g2, bias_t)


def _hy_small_kernel(w_ref, buf_ref, x0g_ref, bias_ref, fwhi_ref, fwlo_ref, fbhi_ref, fblo_ref,
                     ihi_ref, ilo_ref, o_ref):
    n = buf_ref.shape[0]
    w = w_ref[...]
    wf = _dot3(fwhi_ref[...], fwlo_ref[...], w)
    hf = _dot3(fbhi_ref[...], fblo_ref[...], buf_ref[...])
    wr, wi, hr, hi = wf[:n], wf[n:], hf[:n], hf[n:]
    y = _dot3(ihi_ref[...], ilo_ref[...], jnp.concatenate([wr * hr - wi * hi, wr * hi + wi * hr], axis=0))
    o_ref[...] = (x0g_ref[...] * (y + w * bias_ref[...])).astype(o_ref.dtype)


def _hy_conv_small(w, buf, x0g, bias):
    L, ch = w.shape
    n = 2 * L
    tc = 256
    c, s = _cs(np.outer(np.arange(n), np.arange(n)), n)
    fb = np.concatenate([c, -s], 0)
    mats = [*_const_split(fb[:, :L]), *_const_split(fb),
            *_const_split(np.concatenate([c[:L], -s[:L]], 1) / n)]
    col = lambda r: pl.BlockSpec((r, tc), lambda j: (0, j))
    return pl.pallas_call(
        _hy_small_kernel,
        grid=(ch // tc,),
        in_specs=[col(L), col(n), col(L), col(1)] + [pl.BlockSpec(m.shape, lambda j: (0, 0)) for m in mats],
        out_specs=col(L),
        out_shape=jax.ShapeDtypeStruct((L, ch), MXU_DTYPE),
        compiler_params=_cparams("parallel"),
        name="hy_conv_small",
    )(w, buf, x0g, bias.astype(_f32)[None], *mats)


def _hy_conv(w, buf, x0g, bias):
    L, ch = w.shape
    if L < 512:
        return _hy_conv_small(w, buf, x0g, bias)
    n1 = 2 * L // FFT_N2
    kp = _fft_rows(n1)
    cols = FFT_N2 * ch
    f = _fft_first(buf.reshape(n1, cols), n1).reshape(2, kp, FFT_N2, ch)
    a = _fft_first(w.reshape(n1 // 2, cols), n1).reshape(2, kp, FFT_N2, ch)
    cc = _fft_mid(a, f, n1)
    y = _fft_last(cc.reshape(2 * kp, cols), n1, w.reshape(n1 // 2, cols), x0g.reshape(n1 // 2, cols), bias)
    return y.reshape(L, ch)


def _zero_states():
    return (jnp.zeros((RET_HEADS, RET_DH, RET_DH), _f32), jnp.zeros((RET_HEADS, RET_DH, RET_DH), _f32),
            jnp.zeros((SSM_HEADS, SSM_STATE, SSM_HEADDIM), _f32),
            jnp.zeros((SSM_HEADS, SSM_STATE, SSM_HEADDIM), _f32))


def _recurrent(proj, L, lp, states, latent):
    pa, pb = proj
    q, k, v = _prep_ret(pa, pb, L, latent)
    ra, rb, ret_f, ret_b = _scan_ret(q, k, v, lp['ret_decay_logit'], states[0], states[1])
    cs, bs, xs, pack = _prep_ssd(pa, pb, L, lp['conv_ssm_w'], lp['conv_ssm_b'], lp['ssm_dt_bias'],
                                 lp['ssm_A_log'])
    sa, sb_, ssm_f, ssm_b = _scan_ssd(cs, bs, xs, pack, lp['ssm_D'], states[2], states[3])
    return (ra, rb, sa, sb_), (ret_f, ret_b, ssm_f, ssm_b)


def _mix(h, mod, lp, states, latent):
    L = h.shape[0]
    proj = _in_proj(h, mod[0], mod[1], lp)
    pb = proj[1]
    (ra, rb, sa, sb_), fin = _recurrent(proj, L, lp, states, latent)
    y_ret, y_ssm = _merge(ra, rb, sa, sb_, pb, lp['ssm_norm_w'])
    w, x0g = _prep_hy(pb, L, lp['conv_hy_w'], lp['conv_hy_b'])
    y_hy = _hy_conv(w, _hy_filter(L, lp), x0g, lp['hy_bias'])
    y_pool = _pool(pb, L, lp['pool_w'], lp['pool_scale'])
    out = _out_proj([y_hy, y_ret, y_pool, y_ssm], lp['w_out'], h, mod[2], lp['ln_g'], lp['ln_b'])
    return out, fin


def _context_states(hc, mod, lp):
    proj = _in_proj(hc, mod[0], mod[1], lp)
    _, fin = _recurrent(proj, hc.shape[0], lp, _zero_states(), False)
    return fin


def kernel(x, c, ctx, c_ctx, w_mod, b_mod, w_in, conv_ssm_w, conv_ssm_b, conv_hy_w, conv_hy_b,
           ret_decay_logit, ssm_A_log, ssm_dt_bias, ssm_D, ssm_norm_w, hy_w1, hy_b1, hy_w2, hy_b2,
           hy_w3, hy_freq, hy_bias, pool_w, pool_scale, w_out, ln_g, ln_b):
    assert x.shape[0] == 1
    h, hc = x[0], ctx[0]
    w_in_t = jnp.swapaxes(w_in, 1, 2)
    mods = _adaln(jnp.concatenate([c, c_ctx[None]], axis=0), w_mod, b_mod)
    for l in range(DEPTH):
        lp = {
            'layer': l, 'w_in_t': w_in_t, 'w_a_t': _cast_layer(w_in_t, l, N_A, N_A // 2, 512),
            'w_out': _cast_layer(w_out, l, w_out.shape[1], 1024, 2048)[0],
            'conv_ssm_w': conv_ssm_w[l], 'conv_ssm_b': conv_ssm_b[l],
            'conv_hy_w': conv_hy_w[l], 'conv_hy_b': conv_hy_b[l], 'ret_decay_logit': ret_decay_logit[l],
            'ssm_A_log': ssm_A_log[l], 'ssm_dt_bias': ssm_dt_bias[l], 'ssm_D': ssm_D[l],
            'ssm_norm_w': ssm_norm_w[l], 'hy_w1': hy_w1[l], 'hy_b1': hy_b1[l], 'hy_w2': hy_w2[l],
            'hy_b2': hy_b2[l], 'hy_w3': hy_w3[l], 'hy_freq': hy_freq[l], 'hy_bias': hy_bias[l],
            'pool_w': pool_w[l], 'pool_scale': pool_scale[l], 'ln_g': ln_g[l], 'ln_b': ln_b[l],
        }
        mod = lambda r: tuple(mods[l, r:r + 1, n * D_MODEL:(n + 1) * D_MODEL] for n in range(3))
        if l < DEPTH - 1:
            hc_next, states = _mix(hc, mod(1), lp, _zero_states(), False)
        else:
            states = _context_states(hc, mod(1), lp)
            hc_next = hc
        h, _ = _mix(h, mod(0), lp, states, True)
        hc = hc_next
    return h[None]
```

```python
import functools
import math

import jax
import jax.numpy as jnp
import numpy as np
from jax import lax
from jax.experimental import pallas as pl
from jax.experimental.pallas import tpu as pltpu

D_MODEL = 4096
DEPTH = 2
GRID_W = 64
MIX_W = D_MODEL
BR_W = MIX_W // 4
HY_W = RET_W = POOL_W = SSM_W = BR_W
RET_HEADS = 8
RET_DH = RET_W // RET_HEADS
ROPE_BASE = 10000.0
SSM_HEADDIM = 64
SSM_HEADS = SSM_W // SSM_HEADDIM
SSM_GROUPS = 4
SSM_HPG = SSM_HEADS // SSM_GROUPS
SSM_STATE = 128
SSM_GN = SSM_GROUPS * SSM_STATE
CHUNK = 128
POOL_WINDOWS = (2, 4, 8, 16)
POOL_GROUPS = len(POOL_WINDOWS)
POOL_GROUP = POOL_W // POOL_GROUPS
HY_BANDS = 16
HY_TARGET = 1e-2
HY_FAST = 0.3
HY_SLOW = 1.5
HY_MIN_DECAY = math.log(HY_TARGET) / HY_SLOW
HY_MAX_DECAY = math.log(HY_TARGET) / HY_FAST
ALPHA = (2.0 * DEPTH) ** 0.25
LN_EPS = 1e-5

O_RET_K = 0
O_RET_V = O_RET_K + RET_W
O_SSM_DT = O_RET_V + RET_W
O_SSM_X = O_SSM_DT + 2 * SSM_HEADS
O_SSM_B = O_SSM_X + SSM_W
O_RET_Q = O_SSM_B + SSM_GN
O_SSM_C = O_RET_Q + RET_W
O_HY = O_SSM_C + SSM_GN
O_POOL = O_HY + 3 * HY_W
O_GATE = O_POOL + POOL_W
N_IN = O_GATE + MIX_W

LANES = 128
SUBLANES = 8
N_A = O_SSM_DT + LANES
B_SX = 0
B_SB = O_SSM_B - O_SSM_X
B_RQ = O_RET_Q - O_SSM_X
B_SC = O_SSM_C - O_SSM_X
B_HY = O_HY - O_SSM_X
B_POOL = O_POOL - O_SSM_X
B_GATE = O_GATE - O_SSM_X
N_B = N_IN - O_SSM_X

VMEM_LIMIT_BYTES = 56 * 1024 * 1024
MXU_DTYPE = jnp.bfloat16
HIGHEST = lax.Precision.HIGHEST

_f32 = jnp.float32


def _cparams(*sem, vmem=VMEM_LIMIT_BYTES):
    return pltpu.CompilerParams(dimension_semantics=sem, vmem_limit_bytes=vmem)


def _silu(x):
    return x * jax.nn.sigmoid(x)


def _ln_rows(z):
    mu = jnp.mean(z, -1, keepdims=True)
    zc = z - mu
    var = jnp.mean(zc * zc, -1, keepdims=True)
    return zc * lax.rsqrt(var + LN_EPS)


def _cast_kernel(w_ref, o_ref):
    o_ref[...] = w_ref[...].astype(o_ref.dtype)


def _cast_layer(w, l, r, tr, tc):
    c = w.shape[2]
    assert r % tr == 0 and c % tc == 0
    return pl.pallas_call(
        _cast_kernel,
        grid=(r // tr, c // tc),
        in_specs=[pl.BlockSpec((1, tr, tc), lambda i, j: (l, i, j))],
        out_specs=pl.BlockSpec((1, tr, tc), lambda i, j: (0, i, j)),
        out_shape=jax.ShapeDtypeStruct((1, r, c), MXU_DTYPE),
        compiler_params=_cparams("parallel", "parallel"),
        name="cast_layer",
    )(w)


ADALN_ROWS = 32


def _adaln_kernel(c_ref, w_ref, b_ref, o_ref, xs_ref):
    @pl.when((pl.program_id(0) == 0) & (pl.program_id(1) == 0))
    def _():
        xs_ref[...] = _silu(c_ref[...])

    r, k = c_ref.shape[0], c_ref.shape[1]
    nj = w_ref.shape[-1] // LANES

    def body(t, accs):
        rows = pl.ds(pl.multiple_of(t * ADALN_ROWS, ADALN_ROWS), ADALN_ROWS)
        xs = [xs_ref[m, rows, :] for m in range(r)]
        ws = [w_ref[0, rows, j * LANES:(j + 1) * LANES] for j in range(nj)]
        return tuple(accs[m * nj + j] + xs[m] * ws[j] for m in range(r) for j in range(nj))

    accs = lax.fori_loop(0, k // ADALN_ROWS, body,
                         tuple(jnp.zeros((ADALN_ROWS, LANES), _f32) for _ in range(r * nj)), unroll=4)
    outs = [jnp.concatenate([jnp.sum(accs[m * nj + j], axis=0, keepdims=True) for j in range(nj)], axis=1)
            for m in range(r)]
    outs.append(jnp.zeros((SUBLANES - r, w_ref.shape[-1]), _f32))
    o_ref[0] = jnp.concatenate(outs, axis=0) + b_ref[0]


def _adaln(c_rows, w_mod, b_mod):
    dep, k, n = w_mod.shape
    r = c_rows.shape[0]
    tn = 512
    cb = jnp.broadcast_to(c_rows.astype(_f32)[:, :, None], (r, k, LANES))
    return pl.pallas_call(
        _adaln_kernel,
        grid=(dep, n // tn),
        in_specs=[pl.BlockSpec((r, k, LANES), lambda l, j: (0, 0, 0)),
                  pl.BlockSpec((1, k, tn), lambda l, j: (l, 0, j)),
                  pl.BlockSpec((1, 1, tn), lambda l, j: (l, 0, j))],
        out_specs=pl.BlockSpec((1, SUBLANES, tn), lambda l, j: (l, 0, j)),
        out_shape=jax.ShapeDtypeStruct((dep, SUBLANES, n), _f32),
        scratch_shapes=[pltpu.VMEM((r, k, LANES), _f32)],
        compiler_params=_cparams("arbitrary", "arbitrary"),
        name="adaln",
    )(cb, w_mod, b_mod[:, None, :])


def _matmul_nt_kernel(a_ref, b_ref, o_ref):
    o_ref[...] = lax.dot_general(a_ref[...], b_ref[0].astype(MXU_DTYPE), (((1,), (1,)), ((), ())),
                                 preferred_element_type=_f32).astype(o_ref.dtype)


def _matmul_nt(a, wt, l, row0, n, tm, tn, out_dtype):
    m, k = a.shape
    assert m % tm == 0 and n % tn == 0 and row0 % 32 == 0 and tn % 32 == 0
    return pl.pallas_call(
        _matmul_nt_kernel,
        grid=(m // tm, n // tn),
        in_specs=[pl.BlockSpec((tm, k), lambda i, j: (i, 0)),
                  pl.BlockSpec((pl.Element(1), pl.Element(tn), pl.Element(k)),
                               lambda i, j: (l, pl.multiple_of(row0 + j * tn, 32), 0))],
        out_specs=pl.BlockSpec((tm, tn), lambda i, j: (i, j)),
        out_shape=jax.ShapeDtypeStruct((m, n), out_dtype),
        compiler_params=_cparams("parallel", "parallel"),
        name="matmul_nt",
    )(a, wt)


def _modulate_kernel(h_ref, shift_ref, scale_ref, o_ref):
    o_ref[...] = (_ln_rows(h_ref[...]) * (1.0 + scale_ref[...]) + shift_ref[...]).astype(o_ref.dtype)


def _modulate(h, shift, scale):
    L, d = h.shape
    tr = min(512, L)
    vec = pl.BlockSpec((1, d), lambda i: (0, 0))
    return pl.pallas_call(
        _modulate_kernel,
        grid=(L // tr,),
        in_specs=[pl.BlockSpec((tr, d), lambda i: (i, 0)), vec, vec],
        out_specs=pl.BlockSpec((tr, d), lambda i: (i, 0)),
        out_shape=jax.ShapeDtypeStruct((L, d), MXU_DTYPE),
        compiler_params=_cparams("parallel"),
        name="modulate",
    )(h, shift, scale)


def _in_proj(h, shift, scale, lp):
    L = h.shape[0]
    u = _modulate(h, shift, scale)
    pa = _matmul_nt(u, lp['w_a_t'], 0, 0, N_A, 512 if L % 512 == 0 else 256, N_A, _f32)
    pb = _matmul_nt(u, lp['w_in_t'], lp['layer'], O_SSM_X, N_B, 1024 if L % 1024 == 0 else 256, 512, MXU_DTYPE)
    return pa, pb


def _out_proj_kernel(y0_ref, y1_ref, y2_ref, y3_ref, w_ref, h_ref, gate_ref, g_ref, b_ref, o_ref):
    out = None
    for n, y_ref in enumerate((y0_ref, y1_ref, y2_ref, y3_ref)):
        d = jnp.dot(y_ref[...], w_ref[n * BR_W:(n + 1) * BR_W, :], preferred_element_type=_f32)
        out = d if out is None else out + d
    z = ALPHA * h_ref[...] + gate_ref[...] * out
    o_ref[...] = _ln_rows(z) * g_ref[...] + b_ref[...]


OUT_PROJ_VMEM_BYTES = 60 * 1024 * 1024


def _out_proj(ys, w, h, gate, g, b):
    L, d = h.shape
    tm = 256
    lhs = pl.BlockSpec((tm, BR_W), lambda i: (i, 0))
    vec = pl.BlockSpec((1, d), lambda i: (0, 0))
    row = pl.BlockSpec((tm, d), lambda i: (i, 0))
    return pl.pallas_call(
        _out_proj_kernel,
        grid=(L // tm,),
        in_specs=[lhs] * len(ys) + [
            pl.BlockSpec(w.shape, lambda i: (0, 0), pipeline_mode=pl.Buffered(1)), row, vec, vec, vec],
        out_specs=row,
        out_shape=jax.ShapeDtypeStruct((L, d), _f32),
        compiler_params=_cparams("parallel", vmem=OUT_PROJ_VMEM_BYTES),
        name="out_proj",
    )(*ys, w, h, gate, g[None], b[None])


def _rope_tables(L):
    rows = L // GRID_W
    row = jnp.repeat(jnp.arange(rows), GRID_W).astype(_f32)
    col = jnp.tile(jnp.arange(GRID_W), rows).astype(_f32)
    nq = RET_DH // 4
    inv = ROPE_BASE ** (-jnp.arange(nq, dtype=_f32) / nq)
    ang = jnp.concatenate([row[:, None] * inv, col[:, None] * inv], -1)
    cos, sin = jnp.cos(ang), jnp.sin(ang)
    return jnp.concatenate([cos, cos], -1), jnp.concatenate([-sin, sin], -1)


def _prep_ret_kernel(qlo_ref, qhi_ref, k_ref, v_ref, cos_ref, sin_ref, qo_ref, ko_ref, vo_ref, *, rope):
    def rot(t):
        if not rope:
            return t
        return t * cos_ref[...] + pltpu.roll(t, RET_DH // 2, axis=1) * sin_ref[...]

    half = RET_HEADS // 2
    for h in range(RET_HEADS):
        sl = slice(h * RET_DH, (h + 1) * RET_DH)
        q_ref, qs = (qlo_ref, sl) if h < half else (qhi_ref, slice((h - half) * RET_DH, (h - half + 1) * RET_DH))
        qo_ref[:, sl] = rot(q_ref[:, qs].astype(_f32)).astype(qo_ref.dtype)
        ko_ref[:, sl] = rot(k_ref[:, sl] * (RET_DH ** -0.5)).astype(ko_ref.dtype)
    vo_ref[...] = v_ref[...].astype(vo_ref.dtype)


def _prep_ret(pa, pb, L, rope):
    tr = 256
    cos, sin = _rope_tables(L) if rope else (jnp.ones((L, LANES), _f32), jnp.zeros((L, LANES), _f32))
    hw = RET_W // 2
    qsp = lambda n: pl.BlockSpec((tr, hw), lambda i: (i, B_RQ // hw + n))
    sec = lambda c: pl.BlockSpec((tr, RET_W), lambda i: (i, c // RET_W))
    tab = pl.BlockSpec((tr, LANES), lambda i: (i, 0))
    out = pl.BlockSpec((tr, RET_W), lambda i: (i, 0))
    shp = jax.ShapeDtypeStruct((L, RET_W), MXU_DTYPE)
    return pl.pallas_call(
        functools.partial(_prep_ret_kernel, rope=rope),
        grid=(L // tr,),
        in_specs=[qsp(0), qsp(1), sec(O_RET_K), sec(O_RET_V), tab, tab],
        out_specs=[out, out, out],
        out_shape=[shp, shp, shp],
        compiler_params=_cparams("parallel"),
        name="prep_ret",
    )(pb, pb, pa, pa, cos, sin)


def _scan_ret_kernel(logit_ref, qi_ref, ki_ref, vi_ref, qj_ref, kj_ref, vj_ref, s0f_ref, s0b_ref,
                     ya_ref, yb_ref, finf_ref, finb_ref,
                     sf, sb, dmask, f_out, f_upd, f_all, b_out, b_upd, b_all):
    i = pl.program_id(0)
    c = CHUNK

    @pl.when(i == 0)
    def _():
        sf[...] = s0f_ref[...]
        sb[...] = s0b_ref[...]
        ii = lax.broadcasted_iota(jnp.int32, (c, c), 0).astype(_f32)
        jj = lax.broadcasted_iota(jnp.int32, (c, c), 1).astype(_f32)
        for h in range(RET_HEADS):
            def lg(d):
                x = logit_ref[d, h]
                v = -jnp.log1p(jnp.exp(-x))
                return jnp.broadcast_to(v[0:1, :], (c, c))
            lf, lb = lg(0), lg(1)
            dmask[h] = jnp.where(ii > jj, jnp.exp(lf * (ii - jj)),
                                 jnp.where(jj > ii, jnp.exp(lb * (jj - ii)), 2.0))
            f_out[h] = jnp.exp(lf * (ii + 1.0))
            f_upd[h] = jnp.exp(lf * (c - 1.0 - ii))
            f_all[h] = jnp.exp(lf * float(c))
            b_out[h] = jnp.exp(lb * (c - ii))
            b_upd[h] = jnp.exp(lb * ii)
            b_all[h] = jnp.exp(lb * float(c))

    tn = (((0,), (0,)), ((), ()))
    nt = (((1,), (1,)), ((), ()))
    heads = range(RET_HEADS)
    sls = [slice(h * RET_DH, (h + 1) * RET_DH) for h in heads]
    scores = [lax.dot_general(qi_ref[:, sl], ki_ref[:, sl], nt, preferred_element_type=_f32) for sl in sls]
    upd_f = [lax.dot_general((ki_ref[:, sl].astype(_f32) * f_upd[h]).astype(MXU_DTYPE), vi_ref[:, sl], tn,
                             preferred_element_type=_f32) for h, sl in zip(heads, sls)]
    upd_b = [lax.dot_general((kj_ref[:, sl].astype(_f32) * b_upd[h]).astype(MXU_DTYPE), vj_ref[:, sl], tn,
                             preferred_element_type=_f32) for h, sl in zip(heads, sls)]
    for h, sl in zip(heads, sls):
        lhs = jnp.concatenate([(scores[h] * dmask[h]).astype(MXU_DTYPE),
                               (qi_ref[:, sl].astype(_f32) * f_out[h]).astype(MXU_DTYPE)], axis=1)
        rhs = jnp.concatenate([vi_ref[:, sl], sf[h].astype(MXU_DTYPE)], axis=0)
        ya_ref[:, sl] = jnp.dot(lhs, rhs, preferred_element_type=_f32)
        yb_ref[:, sl] = jnp.dot((qj_ref[:, sl].astype(_f32) * b_out[h]).astype(MXU_DTYPE),
                                sb[h].astype(MXU_DTYPE), preferred_element_type=_f32)
    for h in heads:
        sf[h] = f_all[h] * sf[h] + upd_f[h]
        sb[h] = b_all[h] * sb[h] + upd_b[h]

    @pl.when(i == pl.num_programs(0) - 1)
    def _():
        finf_ref[...] = sf[...]
        finb_ref[...] = sb[...]


def _scan_ret(q, k, v, logit, s0f, s0b):
    L = q.shape[0]
    nc = L // CHUNK
    logit_b = jnp.broadcast_to(logit.astype(_f32)[:, :, None, None], (2, RET_HEADS, SUBLANES, LANES))
    fw = pl.BlockSpec((CHUNK, RET_W), lambda i: (i, 0))
    bw = pl.BlockSpec((CHUNK, RET_W), lambda i: (nc - 1 - i, 0))
    st = pl.BlockSpec((RET_HEADS, RET_DH, RET_DH), lambda i: (0, 0, 0))
    yshape = jax.ShapeDtypeStruct((L, RET_W), _f32)
    sshape = jax.ShapeDtypeStruct((RET_HEADS, RET_DH, RET_DH), _f32)
    tile = pltpu.VMEM((RET_HEADS, CHUNK, CHUNK), _f32)
    return pl.pallas_call(
        _scan_ret_kernel,
        grid=(nc,),
        in_specs=[pl.BlockSpec((2, RET_HEADS, SUBLANES, LANES), lambda i: (0, 0, 0, 0)),
                  fw, fw, fw, bw, bw, bw, st, st],
        out_specs=[fw, bw, st, st],
        out_shape=[yshape, yshape, sshape, sshape],
        scratch_shapes=[pltpu.VMEM((RET_HEADS, RET_DH, RET_DH), _f32)] * 2 + [tile] * 7,
        compiler_params=_cparams("arbitrary"),
        name="scan_ret",
    )(logit_b, q, k, v, q, k, v, s0f, s0b)


def _shift_rows(x, prev_row, next_row):
    r = x.shape[0]
    rid = lax.broadcasted_iota(jnp.int32, x.shape, 0)
    up = jnp.where(rid == 0, prev_row, pltpu.roll(x, 1, axis=0))
    dn = jnp.where(rid == r - 1, next_row, pltpu.roll(x, r - 1, axis=0))
    return up, dn


HALO = 16


def _conv3(x_ref, prev_ref, next_ref, w_ref, b_ref, has_prev, has_next):
    x = x_ref[...].astype(_f32)
    prev_row = prev_ref[...].astype(_f32)[HALO - 1:HALO, :] * has_prev
    next_row = next_ref[...].astype(_f32)[0:1, :] * has_next
    up, dn = _shift_rows(x, prev_row, next_row)
    return up * w_ref[0:1, :] + x * w_ref[1:2, :] + dn * w_ref[2:3, :] + b_ref[...]


def _halo_specs(tr, L, width, col):
    nb = tr // HALO
    last = L // HALO - 1
    cb = col // width
    return [pl.BlockSpec((tr, width), lambda i: (i, cb)),
            pl.BlockSpec((HALO, width), lambda i: (jnp.maximum(i * nb - 1, 0), cb)),
            pl.BlockSpec((HALO, width), lambda i: (jnp.minimum((i + 1) * nb, last), cb))]


def _prep_ssd_kernel(x_ref, xp_ref, xn_ref, b_ref, bp_ref, bn_ref, c_ref, cp_ref, cn_ref, dt_ref,
                     w_ref, cb_ref, dtb_ref, alog_ref, co_ref, bo_ref, xo_ref, pack_ref):
    i = pl.program_id(0)
    has_prev = (i > 0).astype(_f32)
    has_next = (i < pl.num_programs(0) - 1).astype(_f32)

    def conv(lo, hi, t_ref, p_ref, n_ref, o_ref):
        y = _conv3(t_ref, p_ref, n_ref, w_ref.at[:, lo:hi], cb_ref.at[:, lo:hi], has_prev, has_next)
        o_ref[...] = _silu(y).astype(o_ref.dtype)

    conv(0, SSM_W, x_ref, xp_ref, xn_ref, xo_ref)
    conv(SSM_W, SSM_W + SSM_GN, b_ref, bp_ref, bn_ref, bo_ref)
    conv(SSM_W + SSM_GN, SSM_W + 2 * SSM_GN, c_ref, cp_ref, cn_ref, co_ref)
    z = dt_ref[...] + dtb_ref[...]
    dt = jnp.maximum(z, 0.0) + jnp.log1p(jnp.exp(-jnp.abs(z)))
    a = dt * (-jnp.exp(alog_ref[...]))
    c = CHUNK
    ii = lax.broadcasted_iota(jnp.int32, (c, c), 0)
    jj = lax.broadcasted_iota(jnp.int32, (c, c), 1)
    lower = (jj <= ii).astype(_f32)
    upper = (jj >= ii).astype(_f32)
    lane = lax.broadcasted_iota(jnp.int32, (c, LANES), 1)
    dt_sh = pltpu.roll(dt, 2 * SSM_HEADS, axis=1)
    for n in range(x_ref.shape[0] // c):
        rs = slice(n * c, (n + 1) * c)
        pre = jnp.dot(lower, a[rs], precision=HIGHEST, preferred_element_type=_f32)
        suf = jnp.dot(upper, a[rs], precision=HIGHEST, preferred_element_type=_f32)
        pack_ref[rs, :] = jnp.where(lane < SSM_HEADS, pre,
                                    jnp.where(lane < 2 * SSM_HEADS, suf, dt_sh[rs]))


def _prep_ssd(pa, pb, L, conv_w, conv_b, dt_bias, a_log):
    tr = 256
    w = jnp.pad(conv_w.astype(_f32), ((0, SUBLANES - 3), (0, 0)))
    lanes = lambda t: jnp.pad(t.astype(_f32).reshape(1, 2 * SSM_HEADS), ((0, 0), (0, LANES - 2 * SSM_HEADS)))
    wd = SSM_W + 2 * SSM_GN
    row = lambda c: pl.BlockSpec((tr, c), lambda i: (i, 0))
    return pl.pallas_call(
        _prep_ssd_kernel,
        grid=(L // tr,),
        in_specs=_halo_specs(tr, L, SSM_W, B_SX) + _halo_specs(tr, L, SSM_GN, B_SB)
        + _halo_specs(tr, L, SSM_GN, B_SC) + [
            pl.BlockSpec((tr, LANES), lambda i: (i, O_SSM_DT // LANES)),
            pl.BlockSpec((SUBLANES, wd), lambda i: (0, 0)),
            pl.BlockSpec((1, wd), lambda i: (0, 0)),
            pl.BlockSpec((1, LANES), lambda i: (0, 0)),
            pl.BlockSpec((1, LANES), lambda i: (0, 0))],
        out_specs=[row(SSM_GN), row(SSM_GN), row(SSM_W), row(LANES)],
        out_shape=[jax.ShapeDtypeStruct((L, SSM_GN), MXU_DTYPE),
                   jax.ShapeDtypeStruct((L, SSM_GN), MXU_DTYPE),
                   jax.ShapeDtypeStruct((L, SSM_W), MXU_DTYPE),
                   jax.ShapeDtypeStruct((L, LANES), _f32)],
        compiler_params=_cparams("parallel"),
        name="prep_ssd",
    )(*([pb] * 9), pa, w, conv_b.astype(_f32)[None], lanes(dt_bias), lanes(a_log))


def _scan_ssd_kernel(ci_ref, bi_ref, xi_ref, pi_ref, cj_ref, bj_ref, xj_ref, pj_ref, dskip_ref,
                     s0f_ref, s0b_ref, ya_ref, yb_ref, finf_ref, finb_ref, sf, sb):
    i = pl.program_id(0)
    c = CHUNK
    H = SSM_HEADS

    @pl.when(i == 0)
    def _():
        sf[...] = s0f_ref[...]
        sb[...] = s0b_ref[...]

    tn = (((0,), (0,)), ((), ()))
    nt = (((1,), (1,)), ((), ()))
    ii = lax.broadcasted_iota(jnp.int32, (c, c), 0)
    jj = lax.broadcasted_iota(jnp.int32, (c, c), 1)
    low = lax.broadcasted_iota(jnp.int32, (c, LANES), 1) < SSM_HEADDIM
    low2 = lax.broadcasted_iota(jnp.int32, (2 * SSM_STATE, LANES), 1) < SSM_HEADDIM
    diag = (lax.broadcasted_iota(jnp.int32, (2 * SSM_STATE, LANES), 0) < SSM_STATE) == low2
    pi = pi_ref[...]
    pit = pi.T
    pj = pj_ref[...]
    ei = jnp.exp(jnp.minimum(pi, 0.0))
    ej = jnp.exp(jnp.minimum(pj, 0.0))
    tot_i, tot_j = pi[c - 1:c, :], pj[0:1, :]
    wi = jnp.exp(jnp.minimum(tot_i - pi, 0.0)) * pltpu.roll(pi, LANES - 2 * H, axis=1)
    wj = jnp.exp(jnp.minimum(tot_j - pj, 0.0)) * pltpu.roll(pj, LANES - 2 * H, axis=1)
    eti, etj = jnp.exp(jnp.minimum(tot_i, 0.0)), jnp.exp(jnp.minimum(tot_j, 0.0))
    colb = lambda t, k: jnp.broadcast_to(t[:, k:k + 1], (c, LANES))
    for g in range(SSM_GROUPS):
        gs = slice(g * SSM_STATE, (g + 1) * SSM_STATE)
        ci, bi = ci_ref[:, gs], bi_ref[:, gs]
        cj, bj = cj_ref[:, gs], bj_ref[:, gs]
        cb = lax.dot_general(ci, bi, nt, preferred_element_type=_f32)
        ci32, bi32, cj32, bj32 = (t.astype(_f32) for t in (ci, bi, cj, bj))
        for pp in range(SSM_HPG // 2):
            q = g * (SSM_HPG // 2) + pp
            heads = (2 * q, 2 * q + 1)
            xs = slice(q * LANES, (q + 1) * LANES)
            x = xi_ref[:, xs]
            x32 = x.astype(_f32)
            scores, cw, bw = [], [], []
            for h in heads:
                row = lambda o: pit[o + h:o + h + 1, :]
                mf = jnp.where(ii >= jj, jnp.exp(jnp.minimum(colb(pi, h) - row(0), 0.0)), 0.0) * row(2 * H)
                mb = jnp.where(jj >= ii, jnp.exp(jnp.minimum(colb(pi, H + h) - row(H), 0.0)), 0.0) * row(3 * H)
                scores.append((cb * (mf + mb)).astype(MXU_DTYPE))
                cw.append((ci32 * colb(ei, h)).astype(MXU_DTYPE))
                bw.append((bi32 * colb(wi, h)).astype(MXU_DTYPE))
            xa = jnp.where(low, x32, 0.0).astype(MXU_DTYPE)
            xb = jnp.where(low, 0.0, x32).astype(MXU_DTYPE)
            lhs = jnp.concatenate(scores + cw, axis=1)
            rhs = jnp.concatenate([xa, xb, sf[q].astype(MXU_DTYPE)], axis=0)
            y = jnp.dot(lhs, rhs, preferred_element_type=_f32)
            ya_ref[:, xs] = y + dskip_ref[:, xs] * x32
            upd = lax.dot_general(jnp.concatenate(bw, axis=1), x, tn, preferred_element_type=_f32)
            dec = jnp.where(low2, eti[0:1, heads[0]:heads[0] + 1], eti[0:1, heads[1]:heads[1] + 1])
            sf[q] = dec * sf[q] + jnp.where(diag, upd, 0.0)
            x = xj_ref[:, xs]
            cw = [(cj32 * colb(ej, H + h)).astype(MXU_DTYPE) for h in heads]
            bw = [(bj32 * colb(wj, H + h)).astype(MXU_DTYPE) for h in heads]
            yb_ref[:, xs] = jnp.dot(jnp.concatenate(cw, axis=1), sb[q].astype(MXU_DTYPE),
                                    preferred_element_type=_f32)
            upd = lax.dot_general(jnp.concatenate(bw, axis=1), x, tn, preferred_element_type=_f32)
            dec = jnp.where(low2, etj[0:1, H + heads[0]:H + heads[0] + 1], etj[0:1, H + heads[1]:H + heads[1] + 1])
            sb[q] = dec * sb[q] + jnp.where(diag, upd, 0.0)

    @pl.when(i == pl.num_programs(0) - 1)
    def _():
        finf_ref[...] = sf[...]
        finb_ref[...] = sb[...]


def _pair_states(s):
    s = s.reshape(SSM_HEADS // 2, 2, SSM_STATE, SSM_HEADDIM)
    z = jnp.zeros_like(s[:, 0])
    return jnp.concatenate([jnp.concatenate([s[:, 0], z], -1), jnp.concatenate([z, s[:, 1]], -1)], 1)


def _unpair_states(s):
    top, bot = s[:, :SSM_STATE, :SSM_HEADDIM], s[:, SSM_STATE:, SSM_HEADDIM:]
    return jnp.stack([top, bot], 1).reshape(SSM_HEADS, SSM_STATE, SSM_HEADDIM)


def _scan_ssd(cs, bs, xs, pack, d_skip, s0f, s0b):
    L = xs.shape[0]
    nc = L // CHUNK
    dvec = jnp.repeat(d_skip.astype(_f32), SSM_HEADDIM)[None]
    fw = lambda w: pl.BlockSpec((CHUNK, w), lambda i: (i, 0))
    bw = lambda w: pl.BlockSpec((CHUNK, w), lambda i: (nc - 1 - i, 0))
    pshape = (SSM_HEADS // 2, 2 * SSM_STATE, 2 * SSM_HEADDIM)
    st = pl.BlockSpec(pshape, lambda i: (0, 0, 0))
    yshape = jax.ShapeDtypeStruct((L, SSM_W), _f32)
    sshape = jax.ShapeDtypeStruct(pshape, _f32)
    ya, yb, fin_f, fin_b = pl.pallas_call(
        _scan_ssd_kernel,
        grid=(nc,),
        in_specs=[fw(SSM_GN), fw(SSM_GN), fw(SSM_W), fw(LANES), bw(SSM_GN), bw(SSM_GN), bw(SSM_W), bw(LANES),
                  pl.BlockSpec((1, SSM_W), lambda i: (0, 0)), st, st],
        out_specs=[fw(SSM_W), bw(SSM_W), st, st],
        out_shape=[yshape, yshape, sshape, sshape],
        scratch_shapes=[pltpu.VMEM(pshape, _f32)] * 2,
        compiler_params=_cparams("arbitrary"),
        name="scan_ssd",
    )(cs, bs, xs, pack, cs, bs, xs, pack, dvec, _pair_states(s0f), _pair_states(s0b))
    return ya, yb, _unpair_states(fin_f), _unpair_states(fin_b)


def _merge_kernel(ra_ref, rb_ref, sa_ref, sb_ref, gr_ref, gs_ref, nw_ref, yr_ref, ys_ref):
    for h in range(RET_HEADS):
        sl = slice(h * RET_DH, (h + 1) * RET_DH)
        y = _ln_rows(ra_ref[:, sl] + rb_ref[:, sl])
        yr_ref[:, sl] = (y * _silu(gr_ref[:, sl].astype(_f32))).astype(yr_ref.dtype)
    gw = SSM_W // SSM_GROUPS
    for g in range(SSM_GROUPS):
        sl = slice(g * gw, (g + 1) * gw)
        y = (sa_ref[:, sl] + sb_ref[:, sl]) * _silu(gs_ref[:, sl].astype(_f32))
        y = y * lax.rsqrt(jnp.mean(y * y, -1, keepdims=True) + LN_EPS)
        ys_ref[:, sl] = (y * nw_ref[:, sl]).astype(ys_ref.dtype)


def _gate_spec(tr, n):
    return pl.BlockSpec((tr, BR_W), lambda i: (i, B_GATE // BR_W + n))


def _merge(ra, rb, sa, sb_, pb, norm_w):
    L = ra.shape[0]
    tr = min(512, L)
    row = pl.BlockSpec((tr, BR_W), lambda i: (i, 0))
    shp = jax.ShapeDtypeStruct((L, BR_W), MXU_DTYPE)
    return pl.pallas_call(
        _merge_kernel,
        grid=(L // tr,),
        in_specs=[row, row, row, row, _gate_spec(tr, 1), _gate_spec(tr, 3),
                  pl.BlockSpec((1, BR_W), lambda i: (0, 0))],
        out_specs=[row, row],
        out_shape=[shp, shp],
        compiler_params=_cparams("parallel"),
        name="merge",
    )(ra, rb, sa, sb_, pb, pb, norm_w.astype(_f32)[None])


def _pool_kernel(x_ref, prev_ref, next_ref, g_ref, pw_ref, ps_ref, o_ref, *, L):
    i = pl.program_id(0)
    t = x_ref.shape[0]
    halo = HALO
    has_prev = (i > 0).astype(_f32)
    has_next = (i < pl.num_programs(0) - 1).astype(_f32)
    pos = i * t + lax.broadcasted_iota(jnp.int32, (t, 1), 0)
    for g, win in enumerate(POOL_WINDOWS):
        sl = slice(g * POOL_GROUP, (g + 1) * POOL_GROUP)
        x = x_ref[:, sl].astype(_f32)
        s = jnp.concatenate([prev_ref[:, sl].astype(_f32) * has_prev, x,
                             next_ref[:, sl].astype(_f32) * has_next], axis=0)
        rows = t + 2 * halo
        width = 1
        while width < win:
            s = s + pltpu.roll(s, rows - width, axis=0)
            width *= 2
        off = halo - win // 2
        if off:
            s = pltpu.roll(s, rows - off, axis=0)
        cnt = jnp.minimum(pos + win // 2, L) - jnp.maximum(pos - win // 2, 0)
        d = s[:t] / cnt.astype(_f32) - x
        y = jnp.dot(d.astype(MXU_DTYPE), pw_ref[g], preferred_element_type=_f32)
        o_ref[:, sl] = (y * ps_ref[:, sl] * _silu(g_ref[:, sl].astype(_f32))).astype(o_ref.dtype)


def _pool(pb, L, pool_w, pool_scale):
    tr = 256
    return pl.pallas_call(
        functools.partial(_pool_kernel, L=L),
        grid=(L // tr,),
        in_specs=_halo_specs(tr, L, POOL_W, B_POOL) + [
            _gate_spec(tr, 2),
            pl.BlockSpec((POOL_GROUPS, POOL_GROUP, POOL_GROUP), lambda i: (0, 0, 0)),
            pl.BlockSpec((1, POOL_W), lambda i: (0, 0))],
        out_specs=pl.BlockSpec((tr, POOL_W), lambda i: (i, 0)),
        out_shape=jax.ShapeDtypeStruct((L, POOL_W), MXU_DTYPE),
        compiler_params=_cparams("parallel"),
        name="pool",
    )(pb, pb, pb, pb, pool_w.astype(MXU_DTYPE), pool_scale.astype(_f32)[None])


def _prep_hy_kernel(v_ref, vp_ref, vn_ref, x0_ref, x0p_ref, x0n_ref, x1_ref, x1p_ref, x1n_ref,
                    g_ref, w_ref, b_ref, wo_ref, x0g_ref):
    i = pl.program_id(0)
    has_prev = (i > 0).astype(_f32)
    has_next = (i < pl.num_programs(0) - 1).astype(_f32)

    def conv(n, x_ref, p_ref, n_ref):
        sl = slice(n * HY_W, (n + 1) * HY_W)
        return _conv3(x_ref, p_ref, n_ref, w_ref.at[:, sl], b_ref.at[:, sl], has_prev, has_next)

    hv = conv(0, v_ref, vp_ref, vn_ref)
    hx0 = conv(1, x0_ref, x0p_ref, x0n_ref)
    hx1 = conv(2, x1_ref, x1p_ref, x1n_ref)
    wo_ref[...] = hx1 * hv
    x0g_ref[...] = hx0 * _silu(g_ref[...].astype(_f32))


def _prep_hy(pb, L, conv_w, conv_b):
    tr = 256
    w = jnp.pad(conv_w.astype(_f32), ((0, SUBLANES - 3), (0, 0)))
    row = pl.BlockSpec((tr, HY_W), lambda i: (i, 0))
    shp = jax.ShapeDtypeStruct((L, HY_W), _f32)
    secs = sum((_halo_specs(tr, L, HY_W, B_HY + n * HY_W) for n in range(3)), [])
    return pl.pallas_call(
        _prep_hy_kernel,
        grid=(L // tr,),
        in_specs=secs + [_gate_spec(tr, 0),
                         pl.BlockSpec((SUBLANES, 3 * HY_W), lambda i: (0, 0)),
                         pl.BlockSpec((1, 3 * HY_W), lambda i: (0, 0))],
        out_specs=[row, row],
        out_shape=[shp, shp],
        compiler_params=_cparams("parallel"),
        name="prep_hy",
    )(*([pb] * 10), w, conv_b.astype(_f32)[None])


def _split(x):
    hi = x.astype(MXU_DTYPE)
    return hi, (x - hi.astype(_f32)).astype(MXU_DTYPE)


def _dot3(a_hi, a_lo, b):
    b_hi, b_lo = _split(b)
    d = lambda p, q: jnp.dot(p, q, preferred_element_type=_f32)
    return d(a_hi, b_hi) + (d(a_hi, b_lo) + d(a_lo, b_hi))


def _dot2(a_hi, a_lo, b):
    b = b.astype(MXU_DTYPE)
    d = lambda p, q: jnp.dot(p, q, preferred_element_type=_f32)
    return d(a_hi, b) + d(a_lo, b)


def _const_split(m):
    return _split(jnp.asarray(m, _f32))


def _filter_kernel(z_ref, w1_ref, b1_ref, w2_ref, b2_ref, w3hi_ref, w3lo_ref, freq_ref, delta_ref, o_ref, *, L):
    t = z_ref.shape[1]
    dot = functools.partial(jnp.dot, precision=HIGHEST, preferred_element_type=_f32)
    freq = freq_ref[...]
    hdn = jnp.sin(freq * (dot(w1_ref[...], z_ref[...]) + b1_ref[...]))
    hdn = jnp.sin(freq * (dot(w2_ref[...], hdn) + b2_ref[...]))
    h_hi, h_lo = _split(hdn)
    d = lambda p, q: lax.dot_general(p, q, (((0,), (0,)), ((), ())), preferred_element_type=_f32)
    filt = d(h_hi, w3hi_ref[...]) + (d(h_hi, w3lo_ref[...]) + d(h_lo, w3hi_ref[...]))
    n = pl.program_id(0) * t + lax.broadcasted_iota(jnp.int32, (t, 1), 0)
    lag = jnp.minimum(jnp.where(n < L, n, 2 * L - n), L - 1).astype(_f32)
    o_ref[...] = jnp.where(n == L, 0.0, filt) * jnp.exp(-(lag / (L - 1)) * delta_ref[...])


def _hy_filter(L, lp):
    n = jnp.arange(2 * L)
    lag = jnp.minimum(jnp.where(n < L, n, 2 * L - n), L - 1).astype(_f32)[:, None]
    t = lag / (L - 1)
    w = 2.0 * math.pi * lag / L
    bands = jnp.linspace(1e-4, HY_BANDS - 1, HY_BANDS, dtype=_f32)[None, :]
    z = jnp.concatenate([t, jnp.cos(bands * w), -jnp.sin(bands * w)], axis=-1)
    emb = z.shape[1]
    zt = jnp.pad(z, ((0, 0), (0, LANES - emb))).T
    w1t = jnp.pad(lp['hy_w1'].astype(_f32), ((0, LANES - emb), (0, 0))).T
    deltas = jnp.abs(jnp.linspace(HY_MIN_DECAY, HY_MAX_DECAY, HY_W, dtype=_f32))[None]
    tr = min(512, L)
    w3hi, w3lo = _split(lp['hy_w3'].astype(_f32))
    full = lambda a: pl.BlockSpec(a.shape, lambda i: (0,) * a.ndim)
    half = pl.BlockSpec((w3hi.shape[0], HY_W), lambda i: (0, i // (L // tr)))
    colv = lambda v: v.astype(_f32)[:, None]
    pre = [w1t, colv(lp['hy_b1']), lp['hy_w2'].astype(_f32).T, colv(lp['hy_b2'])]
    post = [colv(lp['hy_freq']), deltas]
    return pl.pallas_call(
        functools.partial(_filter_kernel, L=L),
        grid=(2 * L // tr,),
        in_specs=[pl.BlockSpec((LANES, tr), lambda i: (0, i))] + [full(a) for a in pre] + [half, half]
        + [full(a) for a in post],
        out_specs=pl.BlockSpec((tr, HY_W), lambda i: (i, 0)),
        out_shape=jax.ShapeDtypeStruct((2 * L, HY_W), _f32),
        compiler_params=_cparams("parallel"),
        name="hy_filter",
    )(zt, *pre, w3hi, w3lo, *post)


def _cs(num, den):
    ang = 2.0 * np.pi * (np.asarray(num, np.int64) % den) / den
    return np.cos(ang), np.sin(ang)


FFT_N2 = LANES


def _fft_rows(n1):
    return -(-(n1 // 2 + 1) // SUBLANES) * SUBLANES


FFT_TCOL = 4096


def _fft_first_kernel(x_ref, mhi_ref, mlo_ref, o_ref):
    kb, ct = o_ref.shape[1], o_ref.shape[2]
    y = _dot2(mhi_ref[...], mlo_ref[...], x_ref[...])
    for t in range(y.shape[1] // LANES):
        n2l, cj = divmod(t, ct)
        tile = y[:, t * LANES:(t + 1) * LANES].reshape(2, kb, SUBLANES, LANES)
        o_ref[:, :, cj, n2l * SUBLANES:(n2l + 1) * SUBLANES, :] = tile


def _fft_first(x2, n1, ch):
    rows, cols = x2.shape
    kp = _fft_rows(n1)
    c, s = _cs(np.outer(np.arange(kp), np.arange(rows)), n1)
    mhi, mlo = _const_split(np.concatenate([c, -s], 0))
    kb, ct, rpt = kp // SUBLANES, ch // LANES, FFT_TCOL // ch * SUBLANES
    return pl.pallas_call(
        _fft_first_kernel,
        grid=(cols // FFT_TCOL,),
        in_specs=[pl.BlockSpec((rows, FFT_TCOL), lambda j: (0, j)),
                  pl.BlockSpec((2 * kp, rows), lambda j: (0, 0)),
                  pl.BlockSpec((2 * kp, rows), lambda j: (0, 0))],
        out_specs=pl.BlockSpec((2, kb, ct, rpt, LANES), lambda j: (0, 0, 0, j, 0)),
        out_shape=jax.ShapeDtypeStruct((2, kb, ct, FFT_N2 * SUBLANES, LANES), _f32),
        compiler_params=_cparams("parallel"),
        name="fft_first",
    )(x2, mhi, mlo)


def _fft_mid_kernel(a_ref, f_ref, twr_ref, twi_ref, fhi_ref, flo_ref, ghi_ref, glo_ref, o_ref):
    n2 = FFT_N2
    for s0 in range(0, SUBLANES, 2):
        ts, tws = [], []
        for s in (s0, s0 + 1):
            rows = pl.ds(s, n2, stride=SUBLANES)
            twr, twi = twr_ref[s], twi_ref[s]
            tws.append((twr, twi))
            for t_ref in (a_ref, f_ref):
                tr, ti = t_ref[0, 0, 0, rows, :], t_ref[1, 0, 0, rows, :]
                ts.append(jnp.concatenate([tr * twr - ti * twi, tr * twi + ti * twr], axis=0))
        y = _dot2(fhi_ref[...], flo_ref[...], jnp.concatenate(ts, axis=1))
        ps = []
        for p in range(2):
            x, h = y[:, 2 * p * LANES:(2 * p + 1) * LANES], y[:, (2 * p + 1) * LANES:(2 * p + 2) * LANES]
            xr, xi, hr, hi = x[:n2], x[n2:], h[:n2], h[n2:]
            ps.append(jnp.concatenate([xr * hr - xi * hi, xr * hi + xi * hr], axis=0))
        b = _dot2(ghi_ref[...], glo_ref[...], jnp.concatenate(ps, axis=1))
        for p, s in enumerate((s0, s0 + 1)):
            rows = pl.ds(s, n2, stride=SUBLANES)
            twr, twi = tws[p]
            br, bi = b[:n2, p * LANES:(p + 1) * LANES], b[n2:, p * LANES:(p + 1) * LANES]
            o_ref[0, 0, 0, rows, :] = br * twr + bi * twi
            o_ref[1, 0, 0, rows, :] = bi * twr - br * twi


def _fft_mid(a, f, n1):
    n2 = FFT_N2
    _, kb, ct, rows, _ = a.shape
    kp = kb * SUBLANES
    n = n1 * n2
    idx = jnp.arange(kp)[:, None] * jnp.arange(n2)[None, :]
    ang = (2.0 * math.pi / n) * (idx % n).astype(_f32)
    twr = jnp.broadcast_to(jnp.cos(ang)[:, :, None], (kp, n2, LANES))
    twi = jnp.broadcast_to(-jnp.sin(ang)[:, :, None], (kp, n2, LANES))
    c, s = _cs(np.outer(np.arange(n2), np.arange(n2)), n2)
    fhi, flo = _const_split(np.block([[c, s], [-s, c]]))
    ghi, glo = _const_split(np.block([[c, -s], [s, c]]))
    blk = pl.BlockSpec((2, 1, 1, rows, LANES), lambda k, j: (0, k, j, 0, 0))
    tw = pl.BlockSpec((SUBLANES, n2, LANES), lambda k, j: (k, 0, 0))
    mat = pl.BlockSpec((2 * n2, 2 * n2), lambda k, j: (0, 0))
    return pl.pallas_call(
        _fft_mid_kernel,
        grid=(kb, ct),
        in_specs=[blk, blk, tw, tw, mat, mat, mat, mat],
        out_specs=blk,
        out_shape=jax.ShapeDtypeStruct(a.shape, _f32),
        compiler_params=_cparams("parallel", "parallel"),
        name="fft_mid",
    )(a, f, twr, twi, fhi, flo, ghi, glo)


def _fft_last_kernel(c_ref, mhi_ref, mlo_ref, w_ref, x0g_ref, bias_ref, o_ref):
    kb, ct = c_ref.shape[1], c_ref.shape[2]
    tiles = []
    for t in range(w_ref.shape[1] // LANES):
        n2l, cj = divmod(t, ct)
        tiles.append(c_ref[:, :, cj, n2l * SUBLANES:(n2l + 1) * SUBLANES, :].reshape(2 * kb * SUBLANES, LANES))
    y = _dot2(mhi_ref[...], mlo_ref[...], jnp.concatenate(tiles, axis=1))
    o_ref[...] = (x0g_ref[...] * (y + w_ref[...] * bias_ref[...])).astype(o_ref.dtype)


def _fft_last(cc, n1, w2, x0g2, bias):
    rows, cols = w2.shape
    _, kb, ct, _, _ = cc.shape
    kp = kb * SUBLANES
    n = n1 * FFT_N2
    c, s = _cs(np.outer(np.arange(rows), np.arange(kp)), n1)
    k1 = np.arange(kp)
    mult = np.where((k1 == 0) | (k1 == n1 // 2), 1.0, np.where(k1 < n1 // 2, 2.0, 0.0))
    mhi, mlo = _const_split(np.concatenate([c * mult, -s * mult], 1) / n)
    tcol = FFT_TCOL
    bias_t = jnp.tile(bias.astype(_f32), tcol // bias.shape[0])[None]
    blk = pl.BlockSpec((rows, tcol), lambda j: (0, j))
    return pl.pallas_call(
        _fft_last_kernel,
        grid=(cols // tcol,),
        in_specs=[pl.BlockSpec((2, kb, ct, tcol // (ct * LANES) * SUBLANES, LANES), lambda j: (0, 0, 0, j, 0)),
                  pl.BlockSpec((rows, 2 * kp), lambda j: (0, 0)),
                  pl.BlockSpec((rows, 2 * kp), lambda j: (0, 0)),
                  blk, blk, pl.BlockSpec((1, tcol), lambda j: (0, 0))],
        out_specs=blk,
        out_shape=jax.ShapeDtypeStruct((rows, cols), MXU_DTYPE),
        compiler_params=_cparams("parallel"),
        name="fft_last",
    )(cc, mhi, mlo, w2, x0g2, bias_t)


def _hy_small_kernel(w_ref, buf_ref, x0g_ref, bias_ref, fwhi_ref, fwlo_ref, fbhi_ref, fblo_ref,
                     ihi_ref, ilo_ref, o_ref):
    n = buf_ref.shape[0]
    w = w_ref[...]
    wf = _dot3(fwhi_ref[...], fwlo_ref[...], w)
    hf = _dot3(fbhi_ref[...], fblo_ref[...], buf_ref[...])
    wr, wi, hr, hi = wf[:n], wf[n:], hf[:n], hf[n:]
    y = _dot3(ihi_ref[...], ilo_ref[...], jnp.concatenate([wr * hr - wi * hi, wr * hi + wi * hr], axis=0))
    o_ref[...] = (x0g_ref[...] * (y + w * bias_ref[...])).astype(o_ref.dtype)


def _hy_conv_small(w, buf, x0g, bias):
    L, ch = w.shape
    n = 2 * L
    tc = 256
    c, s = _cs(np.outer(np.arange(n), np.arange(n)), n)
    fb = np.concatenate([c, -s], 0)
    mats = [*_const_split(fb[:, :L]), *_const_split(fb),
            *_const_split(np.concatenate([c[:L], -s[:L]], 1) / n)]
    col = lambda r: pl.BlockSpec((r, tc), lambda j: (0, j))
    return pl.pallas_call(
        _hy_small_kernel,
        grid=(ch // tc,),
        in_specs=[col(L), col(n), col(L), col(1)] + [pl.BlockSpec(m.shape, lambda j: (0, 0)) for m in mats],
        out_specs=col(L),
        out_shape=jax.ShapeDtypeStruct((L, ch), MXU_DTYPE),
        compiler_params=_cparams("parallel"),
        name="hy_conv_small",
    )(w, buf, x0g, bias.astype(_f32)[None], *mats)


def _hy_conv(w, buf, x0g, bias):
    L, ch = w.shape
    if L < 512:
        return _hy_conv_small(w, buf, x0g, bias)
    n1 = 2 * L // FFT_N2
    cols = FFT_N2 * ch
    f = _fft_first(buf.reshape(n1, cols), n1, ch)
    a = _fft_first(w.reshape(n1 // 2, cols), n1, ch)
    cc = _fft_mid(a, f, n1)
    y = _fft_last(cc, n1, w.reshape(n1 // 2, cols), x0g.reshape(n1 // 2, cols), bias)
    return y.reshape(L, ch)


def _zero_states():
    return (jnp.zeros((RET_HEADS, RET_DH, RET_DH), _f32), jnp.zeros((RET_HEADS, RET_DH, RET_DH), _f32),
            jnp.zeros((SSM_HEADS, SSM_STATE, SSM_HEADDIM), _f32),
            jnp.zeros((SSM_HEADS, SSM_STATE, SSM_HEADDIM), _f32))


def _recurrent(proj, L, lp, states, latent):
    pa, pb = proj
    q, k, v = _prep_ret(pa, pb, L, latent)
    ra, rb, ret_f, ret_b = _scan_ret(q, k, v, lp['ret_decay_logit'], states[0], states[1])
    cs, bs, xs, pack = _prep_ssd(pa, pb, L, lp['conv_ssm_w'], lp['conv_ssm_b'], lp['ssm_dt_bias'],
                                 lp['ssm_A_log'])
    sa, sb_, ssm_f, ssm_b = _scan_ssd(cs, bs, xs, pack, lp['ssm_D'], states[2], states[3])
    return (ra, rb, sa, sb_), (ret_f, ret_b, ssm_f, ssm_b)


def _mix(h, mod, lp, states, latent):
    L = h.shape[0]
    proj = _in_proj(h, mod[0], mod[1], lp)
    pb = proj[1]
    (ra, rb, sa, sb_), fin = _recurrent(proj, L, lp, states, latent)
    y_ret, y_ssm = _merge(ra, rb, sa, sb_, pb, lp['ssm_norm_w'])
    w, x0g = _prep_hy(pb, L, lp['conv_hy_w'], lp['conv_hy_b'])
    y_hy = _hy_conv(w, _hy_filter(L, lp), x0g, lp['hy_bias'])
    y_pool = _pool(pb, L, lp['pool_w'], lp['pool_scale'])
    out = _out_proj([y_hy, y_ret, y_pool, y_ssm], lp['w_out'], h, mod[2], lp['ln_g'], lp['ln_b'])
    return out, fin


def _context_states(hc, mod, lp):
    proj = _in_proj(hc, mod[0], mod[1], lp)
    _, fin = _recurrent(proj, hc.shape[0], lp, _zero_states(), False)
    return fin


def kernel(x, c, ctx, c_ctx, w_mod, b_mod, w_in, conv_ssm_w, conv_ssm_b, conv_hy_w, conv_hy_b,
           ret_decay_logit, ssm_A_log, ssm_dt_bias, ssm_D, ssm_norm_w, hy_w1, hy_b1, hy_w2, hy_b2,
           hy_w3, hy_freq, hy_bias, pool_w, pool_scale, w_out, ln_g, ln_b):
    assert x.shape[0] == 1
    h, hc = x[0], ctx[0]
    w_in_t = jnp.swapaxes(w_in, 1, 2)
    mods = _adaln(jnp.concatenate([c, c_ctx[None]], axis=0), w_mod, b_mod)
    for l in range(DEPTH):
        lp = {
            'layer': l, 'w_in_t': w_in_t, 'w_a_t': _cast_layer(w_in_t, l, N_A, N_A // 2, 512),
            'w_out': _cast_layer(w_out, l, w_out.shape[1], 1024, 2048)[0],
            'conv_ssm_w': conv_ssm_w[l], 'conv_ssm_b': conv_ssm_b[l],
            'conv_hy_w': conv_hy_w[l], 'conv_hy_b': conv_hy_b[l], 'ret_decay_logit': ret_decay_logit[l],
            'ssm_A_log': ssm_A_log[l], 'ssm_dt_bias': ssm_dt_bias[l], 'ssm_D': ssm_D[l],
            'ssm_norm_w': ssm_norm_w[l], 'hy_w1': hy_w1[l], 'hy_b1': hy_b1[l], 'hy_w2': hy_w2[l],
            'hy_b2': hy_b2[l], 'hy_w3': hy_w3[l], 'hy_freq': hy_freq[l], 'hy_bias': hy_bias[l],
            'pool_w': pool_w[l], 'pool_scale': pool_scale[l], 'ln_g': ln_g[l], 'ln_b': ln_b[l],
        }
        mod = lambda r: tuple(mods[l, r:r + 1, n * D_MODEL:(n + 1) * D_MODEL] for n in range(3))
        if l < DEPTH - 1:
            hc_next, states = _mix(hc, mod(1), lp, _zero_states(), False)
        else:
            states = _context_states(hc, mod(1), lp)
            hc_next = hc
        h, _ = _mix(h, mod(0), lp, states, True)
        hc = hc_next
    return h[None]
```

```python
import functools
import math

import jax
import jax.numpy as jnp
import numpy as np
from jax import lax
from jax.experimental import pallas as pl
from jax.experimental.pallas import tpu as pltpu

D_MODEL = 4096
DEPTH = 2
GRID_W = 64
MIX_W = D_MODEL
BR_W = MIX_W // 4
HY_W = RET_W = POOL_W = SSM_W = BR_W
RET_HEADS = 8
RET_DH = RET_W // RET_HEADS
ROPE_BASE = 10000.0
SSM_HEADDIM = 64
SSM_HEADS = SSM_W // SSM_HEADDIM
SSM_GROUPS = 4
SSM_HPG = SSM_HEADS // SSM_GROUPS
SSM_STATE = 128
SSM_GN = SSM_GROUPS * SSM_STATE
CHUNK = 128
POOL_WINDOWS = (2, 4, 8, 16)
POOL_GROUPS = len(POOL_WINDOWS)
POOL_GROUP = POOL_W // POOL_GROUPS
HY_BANDS = 16
HY_TARGET = 1e-2
HY_FAST = 0.3
HY_SLOW = 1.5
HY_MIN_DECAY = math.log(HY_TARGET) / HY_SLOW
HY_MAX_DECAY = math.log(HY_TARGET) / HY_FAST
ALPHA = (2.0 * DEPTH) ** 0.25
LN_EPS = 1e-5

O_RET_K = 0
O_RET_V = O_RET_K + RET_W
O_SSM_DT = O_RET_V + RET_W
O_SSM_X = O_SSM_DT + 2 * SSM_HEADS
O_SSM_B = O_SSM_X + SSM_W
O_RET_Q = O_SSM_B + SSM_GN
O_SSM_C = O_RET_Q + RET_W
O_HY = O_SSM_C + SSM_GN
O_POOL = O_HY + 3 * HY_W
O_GATE = O_POOL + POOL_W
N_IN = O_GATE + MIX_W

LANES = 128
SUBLANES = 8
N_A = O_SSM_DT + LANES
B_SX = 0
B_SB = O_SSM_B - O_SSM_X
B_RQ = O_RET_Q - O_SSM_X
B_SC = O_SSM_C - O_SSM_X
B_HY = O_HY - O_SSM_X
B_POOL = O_POOL - O_SSM_X
B_GATE = O_GATE - O_SSM_X
N_B = N_IN - O_SSM_X

VMEM_LIMIT_BYTES = 56 * 1024 * 1024
MXU_DTYPE = jnp.bfloat16
HIGHEST = lax.Precision.HIGHEST

_f32 = jnp.float32


def _cparams(*sem, vmem=VMEM_LIMIT_BYTES):
    return pltpu.CompilerParams(dimension_semantics=sem, vmem_limit_bytes=vmem)


def _silu(x):
    return x * jax.nn.sigmoid(x)


def _ln_rows(z):
    mu = jnp.mean(z, -1, keepdims=True)
    zc = z - mu
    var = jnp.mean(zc * zc, -1, keepdims=True)
    return zc * lax.rsqrt(var + LN_EPS)


def _cast_kernel(w_ref, o_ref):
    o_ref[...] = w_ref[...].astype(o_ref.dtype)


def _cast_layer(w, l, r, tr, tc):
    c = w.shape[2]
    assert r % tr == 0 and c % tc == 0
    return pl.pallas_call(
        _cast_kernel,
        grid=(r // tr, c // tc),
        in_specs=[pl.BlockSpec((1, tr, tc), lambda i, j: (l, i, j))],
        out_specs=pl.BlockSpec((1, tr, tc), lambda i, j: (0, i, j)),
        out_shape=jax.ShapeDtypeStruct((1, r, c), MXU_DTYPE),
        compiler_params=_cparams("parallel", "parallel"),
        name="cast_layer",
    )(w)


ADALN_ROWS = 32


def _adaln_kernel(c_ref, w_ref, b_ref, o_ref, xs_ref):
    @pl.when((pl.program_id(0) == 0) & (pl.program_id(1) == 0))
    def _():
        xs_ref[...] = _silu(c_ref[...])

    r, k = c_ref.shape[0], c_ref.shape[1]
    nj = w_ref.shape[-1] // LANES

    def body(t, accs):
        rows = pl.ds(pl.multiple_of(t * ADALN_ROWS, ADALN_ROWS), ADALN_ROWS)
        xs = [xs_ref[m, rows, :] for m in range(r)]
        ws = [w_ref[0, rows, j * LANES:(j + 1) * LANES] for j in range(nj)]
        return tuple(accs[m * nj + j] + xs[m] * ws[j] for m in range(r) for j in range(nj))

    accs = lax.fori_loop(0, k // ADALN_ROWS, body,
                         tuple(jnp.zeros((ADALN_ROWS, LANES), _f32) for _ in range(r * nj)), unroll=4)
    outs = [jnp.concatenate([jnp.sum(accs[m * nj + j], axis=0, keepdims=True) for j in range(nj)], axis=1)
            for m in range(r)]
    outs.append(jnp.zeros((SUBLANES - r, w_ref.shape[-1]), _f32))
    o_ref[0] = jnp.concatenate(outs, axis=0) + b_ref[0]


def _adaln(c_rows, w_mod, b_mod):
    dep, k, n = w_mod.shape
    r = c_rows.shape[0]
    tn = 512
    cb = jnp.broadcast_to(c_rows.astype(_f32)[:, :, None], (r, k, LANES))
    return pl.pallas_call(
        _adaln_kernel,
        grid=(dep, n // tn),
        in_specs=[pl.BlockSpec((r, k, LANES), lambda l, j: (0, 0, 0)),
                  pl.BlockSpec((1, k, tn), lambda l, j: (l, 0, j)),
                  pl.BlockSpec((1, 1, tn), lambda l, j: (l, 0, j))],
        out_specs=pl.BlockSpec((1, SUBLANES, tn), lambda l, j: (l, 0, j)),
        out_shape=jax.ShapeDtypeStruct((dep, SUBLANES, n), _f32),
        scratch_shapes=[pltpu.VMEM((r, k, LANES), _f32)],
        compiler_params=_cparams("arbitrary", "arbitrary"),
        name="adaln",
    )(cb, w_mod, b_mod[:, None, :])


def _matmul_nt_kernel(a_ref, b_ref, o_ref):
    o_ref[...] = lax.dot_general(a_ref[...], b_ref[0].astype(MXU_DTYPE), (((1,), (1,)), ((), ())),
                                 preferred_element_type=_f32).astype(o_ref.dtype)


def _matmul_nt(a, wt, l, row0, n, tm, tn, out_dtype):
    m, k = a.shape
    assert m % tm == 0 and n % tn == 0 and row0 % 32 == 0 and tn % 32 == 0
    return pl.pallas_call(
        _matmul_nt_kernel,
        grid=(m // tm, n // tn),
        in_specs=[pl.BlockSpec((tm, k), lambda i, j: (i, 0)),
                  pl.BlockSpec((pl.Element(1), pl.Element(tn), pl.Element(k)),
                               lambda i, j: (l, pl.multiple_of(row0 + j * tn, 32), 0))],
        out_specs=pl.BlockSpec((tm, tn), lambda i, j: (i, j)),
        out_shape=jax.ShapeDtypeStruct((m, n), out_dtype),
        compiler_params=_cparams("parallel", "parallel"),
        name="matmul_nt",
    )(a, wt)


def _modulate_kernel(h_ref, shift_ref, scale_ref, o_ref):
    o_ref[...] = (_ln_rows(h_ref[...]) * (1.0 + scale_ref[...]) + shift_ref[...]).astype(o_ref.dtype)


def _modulate(h, shift, scale):
    L, d = h.shape
    tr = min(512, L)
    vec = pl.BlockSpec((1, d), lambda i: (0, 0))
    return pl.pallas_call(
        _modulate_kernel,
        grid=(L // tr,),
        in_specs=[pl.BlockSpec((tr, d), lambda i: (i, 0)), vec, vec],
        out_specs=pl.BlockSpec((tr, d), lambda i: (i, 0)),
        out_shape=jax.ShapeDtypeStruct((L, d), MXU_DTYPE),
        compiler_params=_cparams("parallel"),
        name="modulate",
    )(h, shift, scale)


def _in_proj(h, shift, scale, lp):
    L = h.shape[0]
    u = _modulate(h, shift, scale)
    pa = _matmul_nt(u, lp['w_a_t'], 0, 0, N_A, 512 if L % 512 == 0 else 256, N_A, _f32)
    pb = _matmul_nt(u, lp['w_in_t'], lp['layer'], O_SSM_X, N_B, 1024 if L % 1024 == 0 else 256, 512, MXU_DTYPE)
    return pa, pb


def _out_proj_kernel(y0_ref, y1_ref, y2_ref, y3_ref, w_ref, h_ref, gate_ref, g_ref, b_ref, o_ref):
    out = None
    for n, y_ref in enumerate((y0_ref, y1_ref, y2_ref, y3_ref)):
        d = jnp.dot(y_ref[...].astype(MXU_DTYPE), w_ref[n * BR_W:(n + 1) * BR_W, :], preferred_element_type=_f32)
        out = d if out is None else out + d
    z = ALPHA * h_ref[...] + gate_ref[...] * out
    o_ref[...] = _ln_rows(z) * g_ref[...] + b_ref[...]


OUT_PROJ_VMEM_BYTES = 60 * 1024 * 1024


def _out_proj(ys, w, h, gate, g, b):
    L, d = h.shape
    tm = 256
    lhs = pl.BlockSpec((tm, BR_W), lambda i: (i, 0))
    vec = pl.BlockSpec((1, d), lambda i: (0, 0))
    row = pl.BlockSpec((tm, d), lambda i: (i, 0))
    return pl.pallas_call(
        _out_proj_kernel,
        grid=(L // tm,),
        in_specs=[lhs] * len(ys) + [
            pl.BlockSpec(w.shape, lambda i: (0, 0), pipeline_mode=pl.Buffered(1)), row, vec, vec, vec],
        out_specs=row,
        out_shape=jax.ShapeDtypeStruct((L, d), _f32),
        compiler_params=_cparams("parallel", vmem=OUT_PROJ_VMEM_BYTES),
        name="out_proj",
    )(*ys, w, h, gate, g[None], b[None])


def _rope_tables(L):
    rows = L // GRID_W
    row = jnp.repeat(jnp.arange(rows), GRID_W).astype(_f32)
    col = jnp.tile(jnp.arange(GRID_W), rows).astype(_f32)
    nq = RET_DH // 4
    inv = ROPE_BASE ** (-jnp.arange(nq, dtype=_f32) / nq)
    ang = jnp.concatenate([row[:, None] * inv, col[:, None] * inv], -1)
    cos, sin = jnp.cos(ang), jnp.sin(ang)
    return jnp.concatenate([cos, cos], -1), jnp.concatenate([-sin, sin], -1)


def _prep_ret_kernel(qlo_ref, qhi_ref, k_ref, v_ref, cos_ref, sin_ref, qo_ref, ko_ref, vo_ref, *, rope):
    def rot(t):
        if not rope:
            return t
        return t * cos_ref[...] + pltpu.roll(t, RET_DH // 2, axis=1) * sin_ref[...]

    half = RET_HEADS // 2
    for h in range(RET_HEADS):
        sl = slice(h * RET_DH, (h + 1) * RET_DH)
        q_ref, qs = (qlo_ref, sl) if h < half else (qhi_ref, slice((h - half) * RET_DH, (h - half + 1) * RET_DH))
        qo_ref[:, sl] = rot(q_ref[:, qs].astype(_f32)).astype(qo_ref.dtype)
        ko_ref[:, sl] = rot(k_ref[:, sl] * (RET_DH ** -0.5)).astype(ko_ref.dtype)
    vo_ref[...] = v_ref[...].astype(vo_ref.dtype)


def _prep_ret(pa, pb, L, rope):
    tr = 256
    cos, sin = _rope_tables(L) if rope else (jnp.ones((L, LANES), _f32), jnp.zeros((L, LANES), _f32))
    hw = RET_W // 2
    qsp = lambda n: pl.BlockSpec((tr, hw), lambda i: (i, B_RQ // hw + n))
    sec = lambda c: pl.BlockSpec((tr, RET_W), lambda i: (i, c // RET_W))
    tab = pl.BlockSpec((tr, LANES), lambda i: (i, 0))
    out = pl.BlockSpec((tr, RET_W), lambda i: (i, 0))
    shp = jax.ShapeDtypeStruct((L, RET_W), MXU_DTYPE)
    return pl.pallas_call(
        functools.partial(_prep_ret_kernel, rope=rope),
        grid=(L // tr,),
        in_specs=[qsp(0), qsp(1), sec(O_RET_K), sec(O_RET_V), tab, tab],
        out_specs=[out, out, out],
        out_shape=[shp, shp, shp],
        compiler_params=_cparams("parallel"),
        name="prep_ret",
    )(pb, pb, pa, pa, cos, sin)


def _scan_ret_kernel(logit_ref, qi_ref, ki_ref, vi_ref, qj_ref, kj_ref, vj_ref, s0f_ref, s0b_ref,
                     ya_ref, yb_ref, finf_ref, finb_ref,
                     sf, sb, dmask, f_out, f_upd, f_all, b_out, b_upd, b_all):
    i = pl.program_id(0)
    c = CHUNK

    @pl.when(i == 0)
    def _():
        sf[...] = s0f_ref[...]
        sb[...] = s0b_ref[...]
        ii = lax.broadcasted_iota(jnp.int32, (c, c), 0).astype(_f32)
        jj = lax.broadcasted_iota(jnp.int32, (c, c), 1).astype(_f32)
        for h in range(RET_HEADS):
            def lg(d):
                x = logit_ref[d, h]
                v = -jnp.log1p(jnp.exp(-x))
                return jnp.broadcast_to(v[0:1, :], (c, c))
            lf, lb = lg(0), lg(1)
            dmask[h] = jnp.where(ii > jj, jnp.exp(lf * (ii - jj)),
                                 jnp.where(jj > ii, jnp.exp(lb * (jj - ii)), 2.0))
            f_out[h] = jnp.exp(lf * (ii + 1.0))
            f_upd[h] = jnp.exp(lf * (c - 1.0 - ii))
            f_all[h] = jnp.exp(lf * float(c))
            b_out[h] = jnp.exp(lb * (c - ii))
            b_upd[h] = jnp.exp(lb * ii)
            b_all[h] = jnp.exp(lb * float(c))

    tn = (((0,), (0,)), ((), ()))
    nt = (((1,), (1,)), ((), ()))
    heads = range(RET_HEADS)
    sls = [slice(h * RET_DH, (h + 1) * RET_DH) for h in heads]
    scores = [lax.dot_general(qi_ref[:, sl], ki_ref[:, sl], nt, preferred_element_type=_f32) for sl in sls]
    upd_f = [lax.dot_general((ki_ref[:, sl].astype(_f32) * f_upd[h]).astype(MXU_DTYPE), vi_ref[:, sl], tn,
                             preferred_element_type=_f32) for h, sl in zip(heads, sls)]
    upd_b = [lax.dot_general((kj_ref[:, sl].astype(_f32) * b_upd[h]).astype(MXU_DTYPE), vj_ref[:, sl], tn,
                             preferred_element_type=_f32) for h, sl in zip(heads, sls)]
    for h, sl in zip(heads, sls):
        lhs = jnp.concatenate([(scores[h] * dmask[h]).astype(MXU_DTYPE),
                               (qi_ref[:, sl].astype(_f32) * f_out[h]).astype(MXU_DTYPE)], axis=1)
        rhs = jnp.concatenate([vi_ref[:, sl], sf[h].astype(MXU_DTYPE)], axis=0)
        ya_ref[:, sl] = jnp.dot(lhs, rhs, preferred_element_type=_f32)
        yb_ref[:, sl] = jnp.dot((qj_ref[:, sl].astype(_f32) * b_out[h]).astype(MXU_DTYPE),
                                sb[h].astype(MXU_DTYPE), preferred_element_type=_f32)
    for h in heads:
        sf[h] = f_all[h] * sf[h] + upd_f[h]
        sb[h] = b_all[h] * sb[h] + upd_b[h]

    @pl.when(i == pl.num_programs(0) - 1)
    def _():
        finf_ref[...] = sf[...]
        finb_ref[...] = sb[...]


def _scan_ret(q, k, v, logit, s0f, s0b):
    L = q.shape[0]
    nc = L // CHUNK
    logit_b = jnp.broadcast_to(logit.astype(_f32)[:, :, None, None], (2, RET_HEADS, SUBLANES, LANES))
    fw = pl.BlockSpec((CHUNK, RET_W), lambda i: (i, 0))
    bw = pl.BlockSpec((CHUNK, RET_W), lambda i: (nc - 1 - i, 0))
    st = pl.BlockSpec((RET_HEADS, RET_DH, RET_DH), lambda i: (0, 0, 0))
    yshape = jax.ShapeDtypeStruct((L, RET_W), _f32)
    sshape = jax.ShapeDtypeStruct((RET_HEADS, RET_DH, RET_DH), _f32)
    tile = pltpu.VMEM((RET_HEADS, CHUNK, CHUNK), _f32)
    return pl.pallas_call(
        _scan_ret_kernel,
        grid=(nc,),
        in_specs=[pl.BlockSpec((2, RET_HEADS, SUBLANES, LANES), lambda i: (0, 0, 0, 0)),
                  fw, fw, fw, bw, bw, bw, st, st],
        out_specs=[fw, bw, st, st],
        out_shape=[yshape, yshape, sshape, sshape],
        scratch_shapes=[pltpu.VMEM((RET_HEADS, RET_DH, RET_DH), _f32)] * 2 + [tile] * 7,
        compiler_params=_cparams("arbitrary"),
        name="scan_ret",
    )(logit_b, q, k, v, q, k, v, s0f, s0b)


def _shift_rows(x, prev_row, next_row):
    r = x.shape[0]
    rid = lax.broadcasted_iota(jnp.int32, x.shape, 0)
    up = jnp.where(rid == 0, prev_row, pltpu.roll(x, 1, axis=0))
    dn = jnp.where(rid == r - 1, next_row, pltpu.roll(x, r - 1, axis=0))
    return up, dn


HALO = 16


def _conv3(x_ref, prev_ref, next_ref, w_ref, b_ref, has_prev, has_next):
    x = x_ref[...].astype(_f32)
    prev_row = prev_ref[...].astype(_f32)[HALO - 1:HALO, :] * has_prev
    next_row = next_ref[...].astype(_f32)[0:1, :] * has_next
    up, dn = _shift_rows(x, prev_row, next_row)
    return up * w_ref[0:1, :] + x * w_ref[1:2, :] + dn * w_ref[2:3, :] + b_ref[...]


def _halo_specs(tr, L, width, col):
    nb = tr // HALO
    last = L // HALO - 1
    cb = col // width
    return [pl.BlockSpec((tr, width), lambda i: (i, cb)),
            pl.BlockSpec((HALO, width), lambda i: (jnp.maximum(i * nb - 1, 0), cb)),
            pl.BlockSpec((HALO, width), lambda i: (jnp.minimum((i + 1) * nb, last), cb))]


def _prep_ssd_kernel(x_ref, xp_ref, xn_ref, b_ref, bp_ref, bn_ref, c_ref, cp_ref, cn_ref, dt_ref,
                     w_ref, cb_ref, dtb_ref, alog_ref, co_ref, bo_ref, xo_ref, pack_ref):
    i = pl.program_id(0)
    has_prev = (i > 0).astype(_f32)
    has_next = (i < pl.num_programs(0) - 1).astype(_f32)

    def conv(lo, hi, t_ref, p_ref, n_ref, o_ref):
        y = _conv3(t_ref, p_ref, n_ref, w_ref.at[:, lo:hi], cb_ref.at[:, lo:hi], has_prev, has_next)
        o_ref[...] = _silu(y).astype(o_ref.dtype)

    conv(0, SSM_W, x_ref, xp_ref, xn_ref, xo_ref)
    conv(SSM_W, SSM_W + SSM_GN, b_ref, bp_ref, bn_ref, bo_ref)
    conv(SSM_W + SSM_GN, SSM_W + 2 * SSM_GN, c_ref, cp_ref, cn_ref, co_ref)
    z = dt_ref[...] + dtb_ref[...]
    dt = jnp.maximum(z, 0.0) + jnp.log1p(jnp.exp(-jnp.abs(z)))
    a = dt * (-jnp.exp(alog_ref[...]))
    c = CHUNK
    ii = lax.broadcasted_iota(jnp.int32, (c, c), 0)
    jj = lax.broadcasted_iota(jnp.int32, (c, c), 1)
    lower = (jj <= ii).astype(_f32)
    upper = (jj >= ii).astype(_f32)
    lane = lax.broadcasted_iota(jnp.int32, (c, LANES), 1)
    dt_sh = pltpu.roll(dt, 2 * SSM_HEADS, axis=1)
    for n in range(x_ref.shape[0] // c):
        rs = slice(n * c, (n + 1) * c)
        pre = jnp.dot(lower, a[rs], precision=HIGHEST, preferred_element_type=_f32)
        suf = jnp.dot(upper, a[rs], precision=HIGHEST, preferred_element_type=_f32)
        pack_ref[rs, :] = jnp.where(lane < SSM_HEADS, pre,
                                    jnp.where(lane < 2 * SSM_HEADS, suf, dt_sh[rs]))


def _prep_ssd(pa, pb, L, conv_w, conv_b, dt_bias, a_log):
    tr = 256
    w = jnp.pad(conv_w.astype(_f32), ((0, SUBLANES - 3), (0, 0)))
    lanes = lambda t: jnp.pad(t.astype(_f32).reshape(1, 2 * SSM_HEADS), ((0, 0), (0, LANES - 2 * SSM_HEADS)))
    wd = SSM_W + 2 * SSM_GN
    row = lambda c: pl.BlockSpec((tr, c), lambda i: (i, 0))
    return pl.pallas_call(
        _prep_ssd_kernel,
        grid=(L // tr,),
        in_specs=_halo_specs(tr, L, SSM_W, B_SX) + _halo_specs(tr, L, SSM_GN, B_SB)
        + _halo_specs(tr, L, SSM_GN, B_SC) + [
            pl.BlockSpec((tr, LANES), lambda i: (i, O_SSM_DT // LANES)),
            pl.BlockSpec((SUBLANES, wd), lambda i: (0, 0)),
            pl.BlockSpec((1, wd), lambda i: (0, 0)),
            pl.BlockSpec((1, LANES), lambda i: (0, 0)),
            pl.BlockSpec((1, LANES), lambda i: (0, 0))],
        out_specs=[row(SSM_GN), row(SSM_GN), row(SSM_W), row(LANES)],
        out_shape=[jax.ShapeDtypeStruct((L, SSM_GN), MXU_DTYPE),
                   jax.ShapeDtypeStruct((L, SSM_GN), MXU_DTYPE),
                   jax.ShapeDtypeStruct((L, SSM_W), MXU_DTYPE),
                   jax.ShapeDtypeStruct((L, LANES), _f32)],
        compiler_params=_cparams("parallel"),
        name="prep_ssd",
    )(*([pb] * 9), pa, w, conv_b.astype(_f32)[None], lanes(dt_bias), lanes(a_log))


def _scan_ssd_kernel(ci_ref, bi_ref, xi_ref, pi_ref, cj_ref, bj_ref, xj_ref, pj_ref, dskip_ref,
                     s0f_ref, s0b_ref, ya_ref, yb_ref, finf_ref, finb_ref, sf, sb):
    i = pl.program_id(0)
    c = CHUNK
    H = SSM_HEADS

    @pl.when(i == 0)
    def _():
        sf[...] = s0f_ref[...]
        sb[...] = s0b_ref[...]

    tn = (((0,), (0,)), ((), ()))
    nt = (((1,), (1,)), ((), ()))
    ii = lax.broadcasted_iota(jnp.int32, (c, c), 0)
    jj = lax.broadcasted_iota(jnp.int32, (c, c), 1)
    low = lax.broadcasted_iota(jnp.int32, (c, LANES), 1) < SSM_HEADDIM
    low2 = lax.broadcasted_iota(jnp.int32, (2 * SSM_STATE, LANES), 1) < SSM_HEADDIM
    diag = (lax.broadcasted_iota(jnp.int32, (2 * SSM_STATE, LANES), 0) < SSM_STATE) == low2
    pi = pi_ref[...]
    pit = pi.T
    pj = pj_ref[...]
    ei = jnp.exp(jnp.minimum(pi, 0.0))
    ej = jnp.exp(jnp.minimum(pj, 0.0))
    tot_i, tot_j = pi[c - 1:c, :], pj[0:1, :]
    wi = jnp.exp(jnp.minimum(tot_i - pi, 0.0)) * pltpu.roll(pi, LANES - 2 * H, axis=1)
    wj = jnp.exp(jnp.minimum(tot_j - pj, 0.0)) * pltpu.roll(pj, LANES - 2 * H, axis=1)
    eti, etj = jnp.exp(jnp.minimum(tot_i, 0.0)), jnp.exp(jnp.minimum(tot_j, 0.0))
    colb = lambda t, k: jnp.broadcast_to(t[:, k:k + 1], (c, LANES))
    for g in range(SSM_GROUPS):
        gs = slice(g * SSM_STATE, (g + 1) * SSM_STATE)
        ci, bi = ci_ref[:, gs], bi_ref[:, gs]
        cj, bj = cj_ref[:, gs], bj_ref[:, gs]
        cb = lax.dot_general(ci, bi, nt, preferred_element_type=_f32)
        ci32, bi32, cj32, bj32 = (t.astype(_f32) for t in (ci, bi, cj, bj))
        for pp in range(SSM_HPG // 2):
            q = g * (SSM_HPG // 2) + pp
            heads = (2 * q, 2 * q + 1)
            xs = slice(q * LANES, (q + 1) * LANES)
            x = xi_ref[:, xs]
            x32 = x.astype(_f32)
            scores, cw, bw = [], [], []
            for h in heads:
                row = lambda o: pit[o + h:o + h + 1, :]
                mf = jnp.where(ii >= jj, jnp.exp(jnp.minimum(colb(pi, h) - row(0), 0.0)), 0.0) * row(2 * H)
                mb = jnp.where(jj >= ii, jnp.exp(jnp.minimum(colb(pi, H + h) - row(H), 0.0)), 0.0) * row(3 * H)
                scores.append((cb * (mf + mb)).astype(MXU_DTYPE))
                cw.append((ci32 * colb(ei, h)).astype(MXU_DTYPE))
                bw.append((bi32 * colb(wi, h)).astype(MXU_DTYPE))
            xa = jnp.where(low, x32, 0.0).astype(MXU_DTYPE)
            xb = jnp.where(low, 0.0, x32).astype(MXU_DTYPE)
            lhs = jnp.concatenate(scores + cw, axis=1)
            rhs = jnp.concatenate([xa, xb, sf[q].astype(MXU_DTYPE)], axis=0)
            y = jnp.dot(lhs, rhs, preferred_element_type=_f32)
            ya_ref[:, xs] = y + dskip_ref[:, xs] * x32
            upd = lax.dot_general(jnp.concatenate(bw, axis=1), x, tn, preferred_element_type=_f32)
            dec = jnp.where(low2, eti[0:1, heads[0]:heads[0] + 1], eti[0:1, heads[1]:heads[1] + 1])
            sf[q] = dec * sf[q] + jnp.where(diag, upd, 0.0)
            x = xj_ref[:, xs]
            cw = [(cj32 * colb(ej, H + h)).astype(MXU_DTYPE) for h in heads]
            bw = [(bj32 * colb(wj, H + h)).astype(MXU_DTYPE) for h in heads]
            yb_ref[:, xs] = jnp.dot(jnp.concatenate(cw, axis=1), sb[q].astype(MXU_DTYPE),
                                    preferred_element_type=_f32)
            upd = lax.dot_general(jnp.concatenate(bw, axis=1), x, tn, preferred_element_type=_f32)
            dec = jnp.where(low2, etj[0:1, H + heads[0]:H + heads[0] + 1], etj[0:1, H + heads[1]:H + heads[1] + 1])
            sb[q] = dec * sb[q] + jnp.where(diag, upd, 0.0)

    @pl.when(i == pl.num_programs(0) - 1)
    def _():
        finf_ref[...] = sf[...]
        finb_ref[...] = sb[...]


def _pair_states(s):
    s = s.reshape(SSM_HEADS // 2, 2, SSM_STATE, SSM_HEADDIM)
    z = jnp.zeros_like(s[:, 0])
    return jnp.concatenate([jnp.concatenate([s[:, 0], z], -1), jnp.concatenate([z, s[:, 1]], -1)], 1)


def _unpair_states(s):
    top, bot = s[:, :SSM_STATE, :SSM_HEADDIM], s[:, SSM_STATE:, SSM_HEADDIM:]
    return jnp.stack([top, bot], 1).reshape(SSM_HEADS, SSM_STATE, SSM_HEADDIM)


def _scan_ssd(cs, bs, xs, pack, d_skip, s0f, s0b):
    L = xs.shape[0]
    nc = L // CHUNK
    dvec = jnp.repeat(d_skip.astype(_f32), SSM_HEADDIM)[None]
    fw = lambda w: pl.BlockSpec((CHUNK, w), lambda i: (i, 0))
    bw = lambda w: pl.BlockSpec((CHUNK, w), lambda i: (nc - 1 - i, 0))
    pshape = (SSM_HEADS // 2, 2 * SSM_STATE, 2 * SSM_HEADDIM)
    st = pl.BlockSpec(pshape, lambda i: (0, 0, 0))
    yshape = jax.ShapeDtypeStruct((L, SSM_W), _f32)
    sshape = jax.ShapeDtypeStruct(pshape, _f32)
    ya, yb, fin_f, fin_b = pl.pallas_call(
        _scan_ssd_kernel,
        grid=(nc,),
        in_specs=[fw(SSM_GN), fw(SSM_GN), fw(SSM_W), fw(LANES), bw(SSM_GN), bw(SSM_GN), bw(SSM_W), bw(LANES),
                  pl.BlockSpec((1, SSM_W), lambda i: (0, 0)), st, st],
        out_specs=[fw(SSM_W), bw(SSM_W), st, st],
        out_shape=[yshape, yshape, sshape, sshape],
        scratch_shapes=[pltpu.VMEM(pshape, _f32)] * 2,
        compiler_params=_cparams("arbitrary"),
        name="scan_ssd",
    )(cs, bs, xs, pack, cs, bs, xs, pack, dvec, _pair_states(s0f), _pair_states(s0b))
    return ya, yb, _unpair_states(fin_f), _unpair_states(fin_b)


def _merge_kernel(ra_ref, rb_ref, sa_ref, sb_ref, gr_ref, gs_ref, nw_ref, yr_ref, ys_ref):
    for h in range(RET_HEADS):
        sl = slice(h * RET_DH, (h + 1) * RET_DH)
        y = _ln_rows(ra_ref[:, sl] + rb_ref[:, sl])
        yr_ref[:, sl] = (y * _silu(gr_ref[:, sl].astype(_f32))).astype(yr_ref.dtype)
    gw = SSM_W // SSM_GROUPS
    for g in range(SSM_GROUPS):
        sl = slice(g * gw, (g + 1) * gw)
        y = (sa_ref[:, sl] + sb_ref[:, sl]) * _silu(gs_ref[:, sl].astype(_f32))
        y = y * lax.rsqrt(jnp.mean(y * y, -1, keepdims=True) + LN_EPS)
        ys_ref[:, sl] = (y * nw_ref[:, sl]).astype(ys_ref.dtype)


def _gate_spec(tr, n):
    return pl.BlockSpec((tr, BR_W), lambda i: (i, B_GATE // BR_W + n))


def _merge(ra, rb, sa, sb_, pb, norm_w):
    L = ra.shape[0]
    tr = min(512, L)
    row = pl.BlockSpec((tr, BR_W), lambda i: (i, 0))
    shp = jax.ShapeDtypeStruct((L, BR_W), MXU_DTYPE)
    return pl.pallas_call(
        _merge_kernel,
        grid=(L // tr,),
        in_specs=[row, row, row, row, _gate_spec(tr, 1), _gate_spec(tr, 3),
                  pl.BlockSpec((1, BR_W), lambda i: (0, 0))],
        out_specs=[row, row],
        out_shape=[shp, shp],
        compiler_params=_cparams("parallel"),
        name="merge",
    )(ra, rb, sa, sb_, pb, pb, norm_w.astype(_f32)[None])


def _pool_kernel(x_ref, prev_ref, next_ref, g_ref, pw_ref, ps_ref, o_ref, *, L):
    i = pl.program_id(0)
    t = x_ref.shape[0]
    halo = HALO
    has_prev = (i > 0).astype(_f32)
    has_next = (i < pl.num_programs(0) - 1).astype(_f32)
    pos = i * t + lax.broadcasted_iota(jnp.int32, (t, 1), 0)
    for g, win in enumerate(POOL_WINDOWS):
        sl = slice(g * POOL_GROUP, (g + 1) * POOL_GROUP)
        x = x_ref[:, sl].astype(_f32)
        s = jnp.concatenate([prev_ref[:, sl].astype(_f32) * has_prev, x,
                             next_ref[:, sl].astype(_f32) * has_next], axis=0)
        rows = t + 2 * halo
        width = 1
        while width < win:
            s = s + pltpu.roll(s, rows - width, axis=0)
            width *= 2
        off = halo - win // 2
        if off:
            s = pltpu.roll(s, rows - off, axis=0)
        cnt = jnp.minimum(pos + win // 2, L) - jnp.maximum(pos - win // 2, 0)
        d = s[:t] / cnt.astype(_f32) - x
        y = jnp.dot(d.astype(MXU_DTYPE), pw_ref[g], preferred_element_type=_f32)
        o_ref[:, sl] = (y * ps_ref[:, sl] * _silu(g_ref[:, sl].astype(_f32))).astype(o_ref.dtype)


def _pool(pb, L, pool_w, pool_scale):
    tr = 256
    return pl.pallas_call(
        functools.partial(_pool_kernel, L=L),
        grid=(L // tr,),
        in_specs=_halo_specs(tr, L, POOL_W, B_POOL) + [
            _gate_spec(tr, 2),
            pl.BlockSpec((POOL_GROUPS, POOL_GROUP, POOL_GROUP), lambda i: (0, 0, 0)),
            pl.BlockSpec((1, POOL_W), lambda i: (0, 0))],
        out_specs=pl.BlockSpec((tr, POOL_W), lambda i: (i, 0)),
        out_shape=jax.ShapeDtypeStruct((L, POOL_W), MXU_DTYPE),
        compiler_params=_cparams("parallel"),
        name="pool",
    )(pb, pb, pb, pb, pool_w.astype(MXU_DTYPE), pool_scale.astype(_f32)[None])


def _prep_hy_kernel(v_ref, vp_ref, vn_ref, x0_ref, x0p_ref, x0n_ref, x1_ref, x1p_ref, x1n_ref,
                    g_ref, w_ref, b_ref, wo_ref, x0g_ref):
    i = pl.program_id(0)
    has_prev = (i > 0).astype(_f32)
    has_next = (i < pl.num_programs(0) - 1).astype(_f32)

    def conv(n, x_ref, p_ref, n_ref):
        sl = slice(n * HY_W, (n + 1) * HY_W)
        return _conv3(x_ref, p_ref, n_ref, w_ref.at[:, sl], b_ref.at[:, sl], has_prev, has_next)

    hv = conv(0, v_ref, vp_ref, vn_ref)
    hx0 = conv(1, x0_ref, x0p_ref, x0n_ref)
    hx1 = conv(2, x1_ref, x1p_ref, x1n_ref)
    wo_ref[...] = hx1 * hv
    x0g_ref[...] = hx0 * _silu(g_ref[...].astype(_f32))


def _prep_hy(pb, L, conv_w, conv_b):
    tr = 256
    w = jnp.pad(conv_w.astype(_f32), ((0, SUBLANES - 3), (0, 0)))
    row = pl.BlockSpec((tr, HY_W), lambda i: (i, 0))
    shp = jax.ShapeDtypeStruct((L, HY_W), _f32)
    secs = sum((_halo_specs(tr, L, HY_W, B_HY + n * HY_W) for n in range(3)), [])
    return pl.pallas_call(
        _prep_hy_kernel,
        grid=(L // tr,),
        in_specs=secs + [_gate_spec(tr, 0),
                         pl.BlockSpec((SUBLANES, 3 * HY_W), lambda i: (0, 0)),
                         pl.BlockSpec((1, 3 * HY_W), lambda i: (0, 0))],
        out_specs=[row, row],
        out_shape=[shp, shp],
        compiler_params=_cparams("parallel"),
        name="prep_hy",
    )(*([pb] * 10), w, conv_b.astype(_f32)[None])


def _split(x):
    hi = x.astype(MXU_DTYPE)
    return hi, (x - hi.astype(_f32)).astype(MXU_DTYPE)


def _dot3(a_hi, a_lo, b):
    b_hi, b_lo = _split(b)
    d = lambda p, q: jnp.dot(p, q, preferred_element_type=_f32)
    return d(a_hi, b_hi) + (d(a_hi, b_lo) + d(a_lo, b_hi))


def _dot2(a_hi, a_lo, b):
    b = b.astype(MXU_DTYPE)
    d = lambda p, q: jnp.dot(p, q, preferred_element_type=_f32)
    return d(a_hi, b) + d(a_lo, b)


def _const_split(m):
    return _split(jnp.asarray(m, _f32))


def _filter_kernel(z_ref, w1_ref, b1_ref, w2_ref, b2_ref, w3hi_ref, w3lo_ref, freq_ref, delta_ref, o_ref, *, L):
    t = z_ref.shape[1]
    dot = functools.partial(jnp.dot, precision=HIGHEST, preferred_element_type=_f32)
    freq = freq_ref[...]
    hdn = jnp.sin(freq * (dot(w1_ref[...], z_ref[...]) + b1_ref[...]))
    hdn = jnp.sin(freq * (dot(w2_ref[...], hdn) + b2_ref[...]))
    h_hi, h_lo = _split(hdn)
    d = lambda p, q: lax.dot_general(p, q, (((0,), (0,)), ((), ())), preferred_element_type=_f32)
    filt = d(h_hi, w3hi_ref[...]) + (d(h_hi, w3lo_ref[...]) + d(h_lo, w3hi_ref[...]))
    n = pl.program_id(0) * t + lax.broadcasted_iota(jnp.int32, (t, 1), 0)
    lag = jnp.minimum(jnp.where(n < L, n, 2 * L - n), L - 1).astype(_f32)
    o_ref[...] = jnp.where(n == L, 0.0, filt) * jnp.exp(-(lag / (L - 1)) * delta_ref[...])


def _hy_filter(L, lp):
    n = jnp.arange(2 * L)
    lag = jnp.minimum(jnp.where(n < L, n, 2 * L - n), L - 1).astype(_f32)[:, None]
    t = lag / (L - 1)
    w = 2.0 * math.pi * lag / L
    bands = jnp.linspace(1e-4, HY_BANDS - 1, HY_BANDS, dtype=_f32)[None, :]
    z = jnp.concatenate([t, jnp.cos(bands * w), -jnp.sin(bands * w)], axis=-1)
    emb = z.shape[1]
    zt = jnp.pad(z, ((0, 0), (0, LANES - emb))).T
    w1t = jnp.pad(lp['hy_w1'].astype(_f32), ((0, LANES - emb), (0, 0))).T
    deltas = jnp.abs(jnp.linspace(HY_MIN_DECAY, HY_MAX_DECAY, HY_W, dtype=_f32))[None]
    tr = min(512, L)
    w3hi, w3lo = _split(lp['hy_w3'].astype(_f32))
    full = lambda a: pl.BlockSpec(a.shape, lambda i: (0,) * a.ndim)
    half = pl.BlockSpec((w3hi.shape[0], HY_W), lambda i: (0, i // (L // tr)))
    colv = lambda v: v.astype(_f32)[:, None]
    pre = [w1t, colv(lp['hy_b1']), lp['hy_w2'].astype(_f32).T, colv(lp['hy_b2'])]
    post = [colv(lp['hy_freq']), deltas]
    return pl.pallas_call(
        functools.partial(_filter_kernel, L=L),
        grid=(2 * L // tr,),
        in_specs=[pl.BlockSpec((LANES, tr), lambda i: (0, i))] + [full(a) for a in pre] + [half, half]
        + [full(a) for a in post],
        out_specs=pl.BlockSpec((tr, HY_W), lambda i: (i, 0)),
        out_shape=jax.ShapeDtypeStruct((2 * L, HY_W), _f32),
        compiler_params=_cparams("parallel"),
        name="hy_filter",
    )(zt, *pre, w3hi, w3lo, *post)


def _cs(num, den):
    ang = 2.0 * np.pi * (np.asarray(num, np.int64) % den) / den
    return np.cos(ang), np.sin(ang)


FFT_N2 = LANES


def _fft_rows(n1):
    return -(-(n1 // 2 + 1) // SUBLANES) * SUBLANES


FFT_GROUP = 16


def _fft_first_kernel(x_ref, mhi_ref, mlo_ref, o_ref):
    n2 = FFT_N2
    rows, kb = x_ref.shape[0] // n2, o_ref.shape[1]
    for g0 in range(0, n2, FFT_GROUP):
        x = jnp.concatenate([x_ref[pl.ds(g0 + g, rows, stride=n2), :] for g in range(FFT_GROUP)], axis=1)
        y = _dot2(mhi_ref[...], mlo_ref[...], x)
        for g in range(FFT_GROUP):
            tile = y[:, g * LANES:(g + 1) * LANES].reshape(2, kb, SUBLANES, LANES)
            o_ref[:, :, 0, (g0 + g) * SUBLANES:(g0 + g + 1) * SUBLANES, :] = tile


def _fft_first(x, n1):
    n2 = FFT_N2
    rows, ch = x.shape[0] // n2, x.shape[1]
    kp = _fft_rows(n1)
    c, s = _cs(np.outer(np.arange(kp), np.arange(rows)), n1)
    mhi, mlo = _const_split(np.concatenate([c, -s], 0))
    kb, ct = kp // SUBLANES, ch // LANES
    return pl.pallas_call(
        _fft_first_kernel,
        grid=(ct,),
        in_specs=[pl.BlockSpec((rows * n2, LANES), lambda j: (0, j)),
                  pl.BlockSpec((2 * kp, rows), lambda j: (0, 0)),
                  pl.BlockSpec((2 * kp, rows), lambda j: (0, 0))],
        out_specs=pl.BlockSpec((2, kb, 1, n2 * SUBLANES, LANES), lambda j: (0, 0, j, 0, 0)),
        out_shape=jax.ShapeDtypeStruct((2, kb, ct, n2 * SUBLANES, LANES), _f32),
        compiler_params=_cparams("parallel"),
        name="fft_first",
    )(x, mhi, mlo)


def _fft_mid_kernel(a_ref, f_ref, twr_ref, twi_ref, fhi_ref, flo_ref, ghi_ref, glo_ref, o_ref):
    n2 = FFT_N2
    for s0 in range(0, SUBLANES, 2):
        ts, tws = [], []
        for s in (s0, s0 + 1):
            rows = pl.ds(s, n2, stride=SUBLANES)
            twr, twi = twr_ref[s], twi_ref[s]
            tws.append((twr, twi))
            for t_ref in (a_ref, f_ref):
                tr, ti = t_ref[0, 0, 0, rows, :], t_ref[1, 0, 0, rows, :]
                ts.append(jnp.concatenate([tr * twr - ti * twi, tr * twi + ti * twr], axis=0))
        y = _dot2(fhi_ref[...], flo_ref[...], jnp.concatenate(ts, axis=1))
        ps = []
        for p in range(2):
            x, h = y[:, 2 * p * LANES:(2 * p + 1) * LANES], y[:, (2 * p + 1) * LANES:(2 * p + 2) * LANES]
            xr, xi, hr, hi = x[:n2], x[n2:], h[:n2], h[n2:]
            ps.append(jnp.concatenate([xr * hr - xi * hi, xr * hi + xi * hr], axis=0))
        b = _dot2(ghi_ref[...], glo_ref[...], jnp.concatenate(ps, axis=1))
        for p, s in enumerate((s0, s0 + 1)):
            rows = pl.ds(s, n2, stride=SUBLANES)
            twr, twi = tws[p]
            br, bi = b[:n2, p * LANES:(p + 1) * LANES], b[n2:, p * LANES:(p + 1) * LANES]
            o_ref[0, 0, 0, rows, :] = br * twr + bi * twi
            o_ref[1, 0, 0, rows, :] = bi * twr - br * twi


def _fft_mid(a, f, n1):
    n2 = FFT_N2
    _, kb, ct, rows, _ = a.shape
    kp = kb * SUBLANES
    n = n1 * n2
    idx = jnp.arange(kp)[:, None] * jnp.arange(n2)[None, :]
    ang = (2.0 * math.pi / n) * (idx % n).astype(_f32)
    twr = jnp.broadcast_to(jnp.cos(ang)[:, :, None], (kp, n2, LANES))
    twi = jnp.broadcast_to(-jnp.sin(ang)[:, :, None], (kp, n2, LANES))
    c, s = _cs(np.outer(np.arange(n2), np.arange(n2)), n2)
    fhi, flo = _const_split(np.block([[c, s], [-s, c]]))
    ghi, glo = _const_split(np.block([[c, -s], [s, c]]))
    blk = pl.BlockSpec((2, 1, 1, rows, LANES), lambda k, j: (0, k, j, 0, 0))
    tw = pl.BlockSpec((SUBLANES, n2, LANES), lambda k, j: (k, 0, 0))
    mat = pl.BlockSpec((2 * n2, 2 * n2), lambda k, j: (0, 0))
    return pl.pallas_call(
        _fft_mid_kernel,
        grid=(kb, ct),
        in_specs=[blk, blk, tw, tw, mat, mat, mat, mat],
        out_specs=blk,
        out_shape=jax.ShapeDtypeStruct(a.shape, _f32),
        compiler_params=_cparams("parallel", "parallel"),
        name="fft_mid",
    )(a, f, twr, twi, fhi, flo, ghi, glo)


def _fft_last_kernel(c_ref, mhi_ref, mlo_ref, w_ref, x0g_ref, bias_ref, o_ref):
    n2 = FFT_N2
    rows, kb = w_ref.shape[0] // n2, c_ref.shape[1]
    for g0 in range(0, n2, FFT_GROUP):
        tiles = [c_ref[:, :, 0, (g0 + g) * SUBLANES:(g0 + g + 1) * SUBLANES, :].reshape(2 * kb * SUBLANES, LANES)
                 for g in range(FFT_GROUP)]
        y = _dot2(mhi_ref[...], mlo_ref[...], jnp.concatenate(tiles, axis=1))
        for g in range(FFT_GROUP):
            at = pl.ds(g0 + g, rows, stride=n2)
            o_ref[at, :] = x0g_ref[at, :] * (y[:, g * LANES:(g + 1) * LANES] + w_ref[at, :] * bias_ref[...])


def _fft_last(cc, n1, w, x0g, bias):
    n2 = FFT_N2
    rows, ch = w.shape[0] // n2, w.shape[1]
    _, kb, ct, _, _ = cc.shape
    kp = kb * SUBLANES
    n = n1 * n2
    c, s = _cs(np.outer(np.arange(rows), np.arange(kp)), n1)
    k1 = np.arange(kp)
    mult = np.where((k1 == 0) | (k1 == n1 // 2), 1.0, np.where(k1 < n1 // 2, 2.0, 0.0))
    mhi, mlo = _const_split(np.concatenate([c * mult, -s * mult], 1) / n)
    blk = pl.BlockSpec((rows * n2, LANES), lambda j: (0, j))
    return pl.pallas_call(
        _fft_last_kernel,
        grid=(ct,),
        in_specs=[pl.BlockSpec((2, kb, 1, n2 * SUBLANES, LANES), lambda j: (0, 0, j, 0, 0)),
                  pl.BlockSpec((rows, 2 * kp), lambda j: (0, 0)),
                  pl.BlockSpec((rows, 2 * kp), lambda j: (0, 0)),
                  blk, blk, pl.BlockSpec((1, LANES), lambda j: (0, j))],
        out_specs=blk,
        out_shape=jax.ShapeDtypeStruct((rows * n2, ch), _f32),
        compiler_params=_cparams("parallel"),
        name="fft_last",
    )(cc, mhi, mlo, w, x0g, bias.astype(_f32)[None])


def _hy_small_kernel(w_ref, buf_ref, x0g_ref, bias_ref, fwhi_ref, fwlo_ref, fbhi_ref, fblo_ref,
                     ihi_ref, ilo_ref, o_ref):
    n = buf_ref.shape[0]
    w = w_ref[...]
    wf = _dot3(fwhi_ref[...], fwlo_ref[...], w)
    hf = _dot3(fbhi_ref[...], fblo_ref[...], buf_ref[...])
    wr, wi, hr, hi = wf[:n], wf[n:], hf[:n], hf[n:]
    y = _dot3(ihi_ref[...], ilo_ref[...], jnp.concatenate([wr * hr - wi * hi, wr * hi + wi * hr], axis=0))
    o_ref[...] = (x0g_ref[...] * (y + w * bias_ref[...])).astype(o_ref.dtype)


def _hy_conv_small(w, buf, x0g, bias):
    L, ch = w.shape
    n = 2 * L
    tc = 256
    c, s = _cs(np.outer(np.arange(n), np.arange(n)), n)
    fb = np.concatenate([c, -s], 0)
    mats = [*_const_split(fb[:, :L]), *_const_split(fb),
            *_const_split(np.concatenate([c[:L], -s[:L]], 1) / n)]
    col = lambda r: pl.BlockSpec((r, tc), lambda j: (0, j))
    return pl.pallas_call(
        _hy_small_kernel,
        grid=(ch // tc,),
        in_specs=[col(L), col(n), col(L), col(1)] + [pl.BlockSpec(m.shape, lambda j: (0, 0)) for m in mats],
        out_specs=col(L),
        out_shape=jax.ShapeDtypeStruct((L, ch), MXU_DTYPE),
        compiler_params=_cparams("parallel"),
        name="hy_conv_small",
    )(w, buf, x0g, bias.astype(_f32)[None], *mats)


def _hy_conv(w, buf, x0g, bias):
    L, ch = w.shape
    if L < 512:
        return _hy_conv_small(w, buf, x0g, bias)
    n1 = 2 * L // FFT_N2
    return _fft_last(_fft_mid(_fft_first(w, n1), _fft_first(buf, n1), n1), n1, w, x0g, bias)


def _zero_states():
    return (jnp.zeros((RET_HEADS, RET_DH, RET_DH), _f32), jnp.zeros((RET_HEADS, RET_DH, RET_DH), _f32),
            jnp.zeros((SSM_HEADS, SSM_STATE, SSM_HEADDIM), _f32),
            jnp.zeros((SSM_HEADS, SSM_STATE, SSM_HEADDIM), _f32))


def _recurrent(proj, L, lp, states, latent):
    pa, pb = proj
    q, k, v = _prep_ret(pa, pb, L, latent)
    ra, rb, ret_f, ret_b = _scan_ret(q, k, v, lp['ret_decay_logit'], states[0], states[1])
    cs, bs, xs, pack = _prep_ssd(pa, pb, L, lp['conv_ssm_w'], lp['conv_ssm_b'], lp['ssm_dt_bias'],
                                 lp['ssm_A_log'])
    sa, sb_, ssm_f, ssm_b = _scan_ssd(cs, bs, xs, pack, lp['ssm_D'], states[2], states[3])
    return (ra, rb, sa, sb_), (ret_f, ret_b, ssm_f, ssm_b)


def _mix(h, mod, lp, states, latent):
    L = h.shape[0]
    proj = _in_proj(h, mod[0], mod[1], lp)
    pb = proj[1]
    (ra, rb, sa, sb_), fin = _recurrent(proj, L, lp, states, latent)
    y_ret, y_ssm = _merge(ra, rb, sa, sb_, pb, lp['ssm_norm_w'])
    w, x0g = _prep_hy(pb, L, lp['conv_hy_w'], lp['conv_hy_b'])
    y_hy = _hy_conv(w, _hy_filter(L, lp), x0g, lp['hy_bias'])
    y_pool = _pool(pb, L, lp['pool_w'], lp['pool_scale'])
    out = _out_proj([y_hy, y_ret, y_pool, y_ssm], lp['w_out'], h, mod[2], lp['ln_g'], lp['ln_b'])
    return out, fin


def _context_states(hc, mod, lp):
    proj = _in_proj(hc, mod[0], mod[1], lp)
    _, fin = _recurrent(proj, hc.shape[0], lp, _zero_states(), False)
    return fin


def kernel(x, c, ctx, c_ctx, w_mod, b_mod, w_in, conv_ssm_w, conv_ssm_b, conv_hy_w, conv_hy_b,
           ret_decay_logit, ssm_A_log, ssm_dt_bias, ssm_D, ssm_norm_w, hy_w1, hy_b1, hy_w2, hy_b2,
           hy_w3, hy_freq, hy_bias, pool_w, pool_scale, w_out, ln_g, ln_b):
    assert x.shape[0] == 1
    h, hc = x[0], ctx[0]
    w_in_t = jnp.swapaxes(w_in, 1, 2)
    mods = _adaln(jnp.concatenate([c, c_ctx[None]], axis=0), w_mod, b_mod)
    for l in range(DEPTH):
        lp = {
            'layer': l, 'w_in_t': w_in_t, 'w_a_t': _cast_layer(w_in_t, l, N_A, N_A // 2, 512),
            'w_out': _cast_layer(w_out, l, w_out.shape[1], 1024, 2048)[0],
            'conv_ssm_w': conv_ssm_w[l], 'conv_ssm_b': conv_ssm_b[l],
            'conv_hy_w': conv_hy_w[l], 'conv_hy_b': conv_hy_b[l], 'ret_decay_logit': ret_decay_logit[l],
            'ssm_A_log': ssm_A_log[l], 'ssm_dt_bias': ssm_dt_bias[l], 'ssm_D': ssm_D[l],
            'ssm_norm_w': ssm_norm_w[l], 'hy_w1': hy_w1[l], 'hy_b1': hy_b1[l], 'hy_w2': hy_w2[l],
            'hy_b2': hy_b2[l], 'hy_w3': hy_w3[l], 'hy_freq': hy_freq[l], 'hy_bias': hy_bias[l],
            'pool_w': pool_w[l], 'pool_scale': pool_scale[l], 'ln_g': ln_g[l], 'ln_b': ln_b[l],
        }
        mod = lambda r: tuple(mods[l, r:r + 1, n * D_MODEL:(n + 1) * D_MODEL] for n in range(3))
        if l < DEPTH - 1:
            hc_next, states = _mix(hc, mod(1), lp, _zero_states(), False)
        else:
            states = _context_states(hc, mod(1), lp)
            hc_next = hc
        h, _ = _mix(h, mod(0), lp, states, True)
        hc = hc_next
    return h[None]
```

```python
import functools
import math

import jax
import jax.numpy as jnp
import numpy as np
from jax import lax
from jax.experimental import pallas as pl
from jax.experimental.pallas import tpu as pltpu

D_MODEL = 4096
DEPTH = 2
GRID_W = 64
MIX_W = D_MODEL
BR_W = MIX_W // 4
HY_W = RET_W = POOL_W = SSM_W = BR_W
RET_HEADS = 8
RET_DH = RET_W // RET_HEADS
ROPE_BASE = 10000.0
SSM_HEADDIM = 64
SSM_HEADS = SSM_W // SSM_HEADDIM
SSM_GROUPS = 4
SSM_HPG = SSM_HEADS // SSM_GROUPS
SSM_STATE = 128
SSM_GN = SSM_GROUPS * SSM_STATE
CHUNK = 128
POOL_WINDOWS = (2, 4, 8, 16)
POOL_GROUPS = len(POOL_WINDOWS)
POOL_GROUP = POOL_W // POOL_GROUPS
HY_BANDS = 16
HY_TARGET = 1e-2
HY_FAST = 0.3
HY_SLOW = 1.5
HY_MIN_DECAY = math.log(HY_TARGET) / HY_SLOW
HY_MAX_DECAY = math.log(HY_TARGET) / HY_FAST
ALPHA = (2.0 * DEPTH) ** 0.25
LN_EPS = 1e-5

O_RET_K = 0
O_RET_V = O_RET_K + RET_W
O_SSM_DT = O_RET_V + RET_W
O_SSM_X = O_SSM_DT + 2 * SSM_HEADS
O_SSM_B = O_SSM_X + SSM_W
O_RET_Q = O_SSM_B + SSM_GN
O_SSM_C = O_RET_Q + RET_W
O_HY = O_SSM_C + SSM_GN
O_POOL = O_HY + 3 * HY_W
O_GATE = O_POOL + POOL_W
N_IN = O_GATE + MIX_W

LANES = 128
SUBLANES = 8
N_A = O_SSM_DT + LANES
B_SX = 0
B_SB = O_SSM_B - O_SSM_X
B_RQ = O_RET_Q - O_SSM_X
B_SC = O_SSM_C - O_SSM_X
B_HY = O_HY - O_SSM_X
B_POOL = O_POOL - O_SSM_X
B_GATE = O_GATE - O_SSM_X
N_B = N_IN - O_SSM_X

VMEM_LIMIT_BYTES = 56 * 1024 * 1024
MXU_DTYPE = jnp.bfloat16
HIGHEST = lax.Precision.HIGHEST

_f32 = jnp.float32


def _cparams(*sem, vmem=VMEM_LIMIT_BYTES):
    return pltpu.CompilerParams(dimension_semantics=sem, vmem_limit_bytes=vmem)


def _silu(x):
    return x * jax.nn.sigmoid(x)


def _ln_rows(z):
    mu = jnp.mean(z, -1, keepdims=True)
    zc = z - mu
    var = jnp.mean(zc * zc, -1, keepdims=True)
    return zc * lax.rsqrt(var + LN_EPS)


def _cast_kernel(w_ref, o_ref):
    o_ref[...] = w_ref[...].astype(o_ref.dtype)


def _cast_layer(w, l, r, tr, tc):
    c = w.shape[2]
    assert r % tr == 0 and c % tc == 0
    return pl.pallas_call(
        _cast_kernel,
        grid=(r // tr, c // tc),
        in_specs=[pl.BlockSpec((1, tr, tc), lambda i, j: (l, i, j))],
        out_specs=pl.BlockSpec((1, tr, tc), lambda i, j: (0, i, j)),
        out_shape=jax.ShapeDtypeStruct((1, r, c), MXU_DTYPE),
        compiler_params=_cparams("parallel", "parallel"),
        name="cast_layer",
    )(w)


ADALN_ROWS = 32


def _adaln_kernel(c_ref, w_ref, b_ref, o_ref, xs_ref):
    @pl.when((pl.program_id(0) == 0) & (pl.program_id(1) == 0))
    def _():
        xs_ref[...] = _silu(c_ref[...])

    r, k = c_ref.shape[0], c_ref.shape[1]
    nj = w_ref.shape[-1] // LANES

    def body(t, accs):
        rows = pl.ds(pl.multiple_of(t * ADALN_ROWS, ADALN_ROWS), ADALN_ROWS)
        xs = [xs_ref[m, rows, :] for m in range(r)]
        ws = [w_ref[0, rows, j * LANES:(j + 1) * LANES] for j in range(nj)]
        return tuple(accs[m * nj + j] + xs[m] * ws[j] for m in range(r) for j in range(nj))

    accs = lax.fori_loop(0, k // ADALN_ROWS, body,
                         tuple(jnp.zeros((ADALN_ROWS, LANES), _f32) for _ in range(r * nj)), unroll=4)
    outs = [jnp.concatenate([jnp.sum(accs[m * nj + j], axis=0, keepdims=True) for j in range(nj)], axis=1)
            for m in range(r)]
    outs.append(jnp.zeros((SUBLANES - r, w_ref.shape[-1]), _f32))
    o_ref[0] = jnp.concatenate(outs, axis=0) + b_ref[0]


def _adaln(c_rows, w_mod, b_mod):
    dep, k, n = w_mod.shape
    r = c_rows.shape[0]
    tn = 512
    cb = jnp.broadcast_to(c_rows.astype(_f32)[:, :, None], (r, k, LANES))
    return pl.pallas_call(
        _adaln_kernel,
        grid=(dep, n // tn),
        in_specs=[pl.BlockSpec((r, k, LANES), lambda l, j: (0, 0, 0)),
                  pl.BlockSpec((1, k, tn), lambda l, j: (l, 0, j)),
                  pl.BlockSpec((1, 1, tn), lambda l, j: (l, 0, j))],
        out_specs=pl.BlockSpec((1, SUBLANES, tn), lambda l, j: (l, 0, j)),
        out_shape=jax.ShapeDtypeStruct((dep, SUBLANES, n), _f32),
        scratch_shapes=[pltpu.VMEM((r, k, LANES), _f32)],
        compiler_params=_cparams("arbitrary", "arbitrary"),
        name="adaln",
    )(cb, w_mod, b_mod[:, None, :])


def _matmul_nt_kernel(a_ref, b_ref, o_ref):
    o_ref[...] = lax.dot_general(a_ref[...], b_ref[0].astype(MXU_DTYPE), (((1,), (1,)), ((), ())),
                                 preferred_element_type=_f32).astype(o_ref.dtype)


def _matmul_nt(a, wt, l, row0, n, tm, tn, out_dtype):
    m, k = a.shape
    assert m % tm == 0 and n % tn == 0 and row0 % 32 == 0 and tn % 32 == 0
    return pl.pallas_call(
        _matmul_nt_kernel,
        grid=(m // tm, n // tn),
        in_specs=[pl.BlockSpec((tm, k), lambda i, j: (i, 0)),
                  pl.BlockSpec((pl.Element(1), pl.Element(tn), pl.Element(k)),
                               lambda i, j: (l, pl.multiple_of(row0 + j * tn, 32), 0))],
        out_specs=pl.BlockSpec((tm, tn), lambda i, j: (i, j)),
        out_shape=jax.ShapeDtypeStruct((m, n), out_dtype),
        compiler_params=_cparams("parallel", "parallel"),
        name="matmul_nt",
    )(a, wt)


def _modulate_kernel(h_ref, shift_ref, scale_ref, o_ref):
    o_ref[...] = (_ln_rows(h_ref[...]) * (1.0 + scale_ref[...]) + shift_ref[...]).astype(o_ref.dtype)


def _modulate(h, shift, scale):
    L, d = h.shape
    tr = min(512, L)
    vec = pl.BlockSpec((1, d), lambda i: (0, 0))
    return pl.pallas_call(
        _modulate_kernel,
        grid=(L // tr,),
        in_specs=[pl.BlockSpec((tr, d), lambda i: (i, 0)), vec, vec],
        out_specs=pl.BlockSpec((tr, d), lambda i: (i, 0)),
        out_shape=jax.ShapeDtypeStruct((L, d), MXU_DTYPE),
        compiler_params=_cparams("parallel"),
        name="modulate",
    )(h, shift, scale)


def _in_proj(h, shift, scale, lp, n_b=N_B):
    L = h.shape[0]
    u = _modulate(h, shift, scale)
    pa = _matmul_nt(u, lp['w_a_t'], 0, 0, N_A, 512 if L % 512 == 0 else 256, N_A, _f32)
    pb = _matmul_nt(u, lp['w_in_t'], lp['layer'], O_SSM_X, n_b, 1024 if L % 1024 == 0 else 256, 512, MXU_DTYPE)
    return pa, pb


def _out_proj_kernel(y0_ref, y1_ref, y2_ref, y3_ref, w_ref, h_ref, gate_ref, g_ref, b_ref, o_ref):
    out = None
    for n, y_ref in enumerate((y0_ref, y1_ref, y2_ref, y3_ref)):
        d = jnp.dot(y_ref[...].astype(MXU_DTYPE), w_ref[n * BR_W:(n + 1) * BR_W, :], preferred_element_type=_f32)
        out = d if out is None else out + d
    z = ALPHA * h_ref[...] + gate_ref[...] * out
    o_ref[...] = _ln_rows(z) * g_ref[...] + b_ref[...]


OUT_PROJ_VMEM_BYTES = 60 * 1024 * 1024


def _out_proj(ys, w, h, gate, g, b):
    L, d = h.shape
    tm = 256
    lhs = pl.BlockSpec((tm, BR_W), lambda i: (i, 0))
    vec = pl.BlockSpec((1, d), lambda i: (0, 0))
    row = pl.BlockSpec((tm, d), lambda i: (i, 0))
    return pl.pallas_call(
        _out_proj_kernel,
        grid=(L // tm,),
        in_specs=[lhs] * len(ys) + [
            pl.BlockSpec(w.shape, lambda i: (0, 0), pipeline_mode=pl.Buffered(1)), row, vec, vec, vec],
        out_specs=row,
        out_shape=jax.ShapeDtypeStruct((L, d), _f32),
        compiler_params=_cparams("parallel", vmem=OUT_PROJ_VMEM_BYTES),
        name="out_proj",
    )(*ys, w, h, gate, g[None], b[None])


def _rope_tables(L):
    rows = L // GRID_W
    row = jnp.repeat(jnp.arange(rows), GRID_W).astype(_f32)
    col = jnp.tile(jnp.arange(GRID_W), rows).astype(_f32)
    nq = RET_DH // 4
    inv = ROPE_BASE ** (-jnp.arange(nq, dtype=_f32) / nq)
    ang = jnp.concatenate([row[:, None] * inv, col[:, None] * inv], -1)
    cos, sin = jnp.cos(ang), jnp.sin(ang)
    return jnp.concatenate([cos, cos], -1), jnp.concatenate([-sin, sin], -1)


def _prep_ret_kernel(qlo_ref, qhi_ref, k_ref, v_ref, cos_ref, sin_ref, qo_ref, ko_ref, vo_ref, *, rope):
    def rot(t):
        if not rope:
            return t
        return t * cos_ref[...] + pltpu.roll(t, RET_DH // 2, axis=1) * sin_ref[...]

    half = RET_HEADS // 2
    for h in range(RET_HEADS):
        sl = slice(h * RET_DH, (h + 1) * RET_DH)
        q_ref, qs = (qlo_ref, sl) if h < half else (qhi_ref, slice((h - half) * RET_DH, (h - half + 1) * RET_DH))
        qo_ref[:, sl] = rot(q_ref[:, qs].astype(_f32)).astype(qo_ref.dtype)
        ko_ref[:, sl] = rot(k_ref[:, sl] * (RET_DH ** -0.5)).astype(ko_ref.dtype)
    vo_ref[...] = v_ref[...].astype(vo_ref.dtype)


def _prep_ret(pa, pb, L, rope, states_only=False):
    tr = 256
    cos, sin = _rope_tables(L) if rope else (jnp.ones((L, LANES), _f32), jnp.zeros((L, LANES), _f32))
    hw = RET_W // 2
    q_src, q_col = (pa, O_RET_K) if states_only else (pb, B_RQ)
    qsp = lambda n: pl.BlockSpec((tr, hw), lambda i: (i, q_col // hw + n))
    sec = lambda c: pl.BlockSpec((tr, RET_W), lambda i: (i, c // RET_W))
    tab = pl.BlockSpec((tr, LANES), lambda i: (i, 0))
    out = pl.BlockSpec((tr, RET_W), lambda i: (i, 0))
    shp = jax.ShapeDtypeStruct((L, RET_W), MXU_DTYPE)
    return pl.pallas_call(
        functools.partial(_prep_ret_kernel, rope=rope),
        grid=(L // tr,),
        in_specs=[qsp(0), qsp(1), sec(O_RET_K), sec(O_RET_V), tab, tab],
        out_specs=[out, out, out],
        out_shape=[shp, shp, shp],
        compiler_params=_cparams("parallel"),
        name="prep_ret",
    )(q_src, q_src, pa, pa, cos, sin)


def _scan_ret_kernel(logit_ref, qi_ref, ki_ref, vi_ref, qj_ref, kj_ref, vj_ref, s0f_ref, s0b_ref,
                     ya_ref, yb_ref, finf_ref, finb_ref,
                     sf, sb, dmask, f_out, f_upd, f_all, b_out, b_upd, b_all):
    i = pl.program_id(0)
    c = CHUNK

    @pl.when(i == 0)
    def _():
        sf[...] = s0f_ref[...]
        sb[...] = s0b_ref[...]
        ii = lax.broadcasted_iota(jnp.int32, (c, c), 0).astype(_f32)
        jj = lax.broadcasted_iota(jnp.int32, (c, c), 1).astype(_f32)
        for h in range(RET_HEADS):
            def lg(d):
                x = logit_ref[d, h]
                v = -jnp.log1p(jnp.exp(-x))
                return jnp.broadcast_to(v[0:1, :], (c, c))
            lf, lb = lg(0), lg(1)
            dmask[h] = jnp.where(ii > jj, jnp.exp(lf * (ii - jj)),
                                 jnp.where(jj > ii, jnp.exp(lb * (jj - ii)), 2.0))
            f_out[h] = jnp.exp(lf * (ii + 1.0))
            f_upd[h] = jnp.exp(lf * (c - 1.0 - ii))
            f_all[h] = jnp.exp(lf * float(c))
            b_out[h] = jnp.exp(lb * (c - ii))
            b_upd[h] = jnp.exp(lb * ii)
            b_all[h] = jnp.exp(lb * float(c))

    tn = (((0,), (0,)), ((), ()))
    nt = (((1,), (1,)), ((), ()))
    heads = range(RET_HEADS)
    sls = [slice(h * RET_DH, (h + 1) * RET_DH) for h in heads]
    scores = [lax.dot_general(qi_ref[:, sl], ki_ref[:, sl], nt, preferred_element_type=_f32) for sl in sls]
    upd_f = [lax.dot_general((ki_ref[:, sl].astype(_f32) * f_upd[h]).astype(MXU_DTYPE), vi_ref[:, sl], tn,
                             preferred_element_type=_f32) for h, sl in zip(heads, sls)]
    upd_b = [lax.dot_general((kj_ref[:, sl].astype(_f32) * b_upd[h]).astype(MXU_DTYPE), vj_ref[:, sl], tn,
                             preferred_element_type=_f32) for h, sl in zip(heads, sls)]
    for h, sl in zip(heads, sls):
        lhs = jnp.concatenate([(scores[h] * dmask[h]).astype(MXU_DTYPE),
                               (qi_ref[:, sl].astype(_f32) * f_out[h]).astype(MXU_DTYPE)], axis=1)
        rhs = jnp.concatenate([vi_ref[:, sl], sf[h].astype(MXU_DTYPE)], axis=0)
        ya_ref[:, sl] = jnp.dot(lhs, rhs, preferred_element_type=_f32)
        yb_ref[:, sl] = jnp.dot((qj_ref[:, sl].astype(_f32) * b_out[h]).astype(MXU_DTYPE),
                                sb[h].astype(MXU_DTYPE), preferred_element_type=_f32)
    for h in heads:
        sf[h] = f_all[h] * sf[h] + upd_f[h]
        sb[h] = b_all[h] * sb[h] + upd_b[h]

    @pl.when(i == pl.num_programs(0) - 1)
    def _():
        finf_ref[...] = sf[...]
        finb_ref[...] = sb[...]


def _scan_ret(q, k, v, logit, s0f, s0b):
    L = q.shape[0]
    nc = L // CHUNK
    logit_b = jnp.broadcast_to(logit.astype(_f32)[:, :, None, None], (2, RET_HEADS, SUBLANES, LANES))
    fw = pl.BlockSpec((CHUNK, RET_W), lambda i: (i, 0))
    bw = pl.BlockSpec((CHUNK, RET_W), lambda i: (nc - 1 - i, 0))
    st = pl.BlockSpec((RET_HEADS, RET_DH, RET_DH), lambda i: (0, 0, 0))
    yshape = jax.ShapeDtypeStruct((L, RET_W), _f32)
    sshape = jax.ShapeDtypeStruct((RET_HEADS, RET_DH, RET_DH), _f32)
    tile = pltpu.VMEM((RET_HEADS, CHUNK, CHUNK), _f32)
    return pl.pallas_call(
        _scan_ret_kernel,
        grid=(nc,),
        in_specs=[pl.BlockSpec((2, RET_HEADS, SUBLANES, LANES), lambda i: (0, 0, 0, 0)),
                  fw, fw, fw, bw, bw, bw, st, st],
        out_specs=[fw, bw, st, st],
        out_shape=[yshape, yshape, sshape, sshape],
        scratch_shapes=[pltpu.VMEM((RET_HEADS, RET_DH, RET_DH), _f32)] * 2 + [tile] * 7,
        compiler_params=_cparams("arbitrary"),
        name="scan_ret",
    )(logit_b, q, k, v, q, k, v, s0f, s0b)


def _shift_rows(x, prev_row, next_row):
    r = x.shape[0]
    rid = lax.broadcasted_iota(jnp.int32, x.shape, 0)
    up = jnp.where(rid == 0, prev_row, pltpu.roll(x, 1, axis=0))
    dn = jnp.where(rid == r - 1, next_row, pltpu.roll(x, r - 1, axis=0))
    return up, dn


HALO = 16


def _conv3(x_ref, prev_ref, next_ref, w_ref, b_ref, has_prev, has_next):
    x = x_ref[...].astype(_f32)
    prev_row = prev_ref[...].astype(_f32)[HALO - 1:HALO, :] * has_prev
    next_row = next_ref[...].astype(_f32)[0:1, :] * has_next
    up, dn = _shift_rows(x, prev_row, next_row)
    return up * w_ref[0:1, :] + x * w_ref[1:2, :] + dn * w_ref[2:3, :] + b_ref[...]


def _halo_specs(tr, L, width, col):
    nb = tr // HALO
    last = L // HALO - 1
    cb = col // width
    return [pl.BlockSpec((tr, width), lambda i: (i, cb)),
            pl.BlockSpec((HALO, width), lambda i: (jnp.maximum(i * nb - 1, 0), cb)),
            pl.BlockSpec((HALO, width), lambda i: (jnp.minimum((i + 1) * nb, last), cb))]


def _prep_ssd_kernel(x_ref, xp_ref, xn_ref, b_ref, bp_ref, bn_ref, c_ref, cp_ref, cn_ref, dt_ref,
                     w_ref, cb_ref, dtb_ref, alog_ref, co_ref, bo_ref, xo_ref, pack_ref):
    i = pl.program_id(0)
    has_prev = (i > 0).astype(_f32)
    has_next = (i < pl.num_programs(0) - 1).astype(_f32)

    def conv(lo, hi, t_ref, p_ref, n_ref, o_ref):
        y = _conv3(t_ref, p_ref, n_ref, w_ref.at[:, lo:hi], cb_ref.at[:, lo:hi], has_prev, has_next)
        o_ref[...] = _silu(y).astype(o_ref.dtype)

    conv(0, SSM_W, x_ref, xp_ref, xn_ref, xo_ref)
    conv(SSM_W, SSM_W + SSM_GN, b_ref, bp_ref, bn_ref, bo_ref)
    conv(SSM_W + SSM_GN, SSM_W + 2 * SSM_GN, c_ref, cp_ref, cn_ref, co_ref)
    z = dt_ref[...] + dtb_ref[...]
    dt = jnp.maximum(z, 0.0) + jnp.log1p(jnp.exp(-jnp.abs(z)))
    a = dt * (-jnp.exp(alog_ref[...]))
    c = CHUNK
    ii = lax.broadcasted_iota(jnp.int32, (c, c), 0)
    jj = lax.broadcasted_iota(jnp.int32, (c, c), 1)
    lower = (jj <= ii).astype(_f32)
    upper = (jj >= ii).astype(_f32)
    lane = lax.broadcasted_iota(jnp.int32, (c, LANES), 1)
    dt_sh = pltpu.roll(dt, 2 * SSM_HEADS, axis=1)
    for n in range(x_ref.shape[0] // c):
        rs = slice(n * c, (n + 1) * c)
        pre = jnp.dot(lower, a[rs], precision=HIGHEST, preferred_element_type=_f32)
        suf = jnp.dot(upper, a[rs], precision=HIGHEST, preferred_element_type=_f32)
        pack_ref[rs, :] = jnp.where(lane < SSM_HEADS, pre,
                                    jnp.where(lane < 2 * SSM_HEADS, suf, dt_sh[rs]))


def _prep_ssd(pa, pb, L, conv_w, conv_b, dt_bias, a_log, states_only=False):
    tr = 256
    c_col = B_SB if states_only else B_SC
    w = jnp.pad(conv_w.astype(_f32), ((0, SUBLANES - 3), (0, 0)))
    lanes = lambda t: jnp.pad(t.astype(_f32).reshape(1, 2 * SSM_HEADS), ((0, 0), (0, LANES - 2 * SSM_HEADS)))
    wd = SSM_W + 2 * SSM_GN
    row = lambda c: pl.BlockSpec((tr, c), lambda i: (i, 0))
    return pl.pallas_call(
        _prep_ssd_kernel,
        grid=(L // tr,),
        in_specs=_halo_specs(tr, L, SSM_W, B_SX) + _halo_specs(tr, L, SSM_GN, B_SB)
        + _halo_specs(tr, L, SSM_GN, c_col) + [
            pl.BlockSpec((tr, LANES), lambda i: (i, O_SSM_DT // LANES)),
            pl.BlockSpec((SUBLANES, wd), lambda i: (0, 0)),
            pl.BlockSpec((1, wd), lambda i: (0, 0)),
            pl.BlockSpec((1, LANES), lambda i: (0, 0)),
            pl.BlockSpec((1, LANES), lambda i: (0, 0))],
        out_specs=[row(SSM_GN), row(SSM_GN), row(SSM_W), row(LANES)],
        out_shape=[jax.ShapeDtypeStruct((L, SSM_GN), MXU_DTYPE),
                   jax.ShapeDtypeStruct((L, SSM_GN), MXU_DTYPE),
                   jax.ShapeDtypeStruct((L, SSM_W), MXU_DTYPE),
                   jax.ShapeDtypeStruct((L, LANES), _f32)],
        compiler_params=_cparams("parallel"),
        name="prep_ssd",
    )(*([pb] * 9), pa, w, conv_b.astype(_f32)[None], lanes(dt_bias), lanes(a_log))


def _scan_ssd_kernel(ci_ref, bi_ref, xi_ref, pi_ref, cj_ref, bj_ref, xj_ref, pj_ref, dskip_ref,
                     s0f_ref, s0b_ref, ya_ref, yb_ref, finf_ref, finb_ref, sf, sb):
    i = pl.program_id(0)
    c = CHUNK
    H = SSM_HEADS

    @pl.when(i == 0)
    def _():
        sf[...] = s0f_ref[...]
        sb[...] = s0b_ref[...]

    tn = (((0,), (0,)), ((), ()))
    nt = (((1,), (1,)), ((), ()))
    ii = lax.broadcasted_iota(jnp.int32, (c, c), 0)
    jj = lax.broadcasted_iota(jnp.int32, (c, c), 1)
    low = lax.broadcasted_iota(jnp.int32, (c, LANES), 1) < SSM_HEADDIM
    low2 = lax.broadcasted_iota(jnp.int32, (2 * SSM_STATE, LANES), 1) < SSM_HEADDIM
    diag = (lax.broadcasted_iota(jnp.int32, (2 * SSM_STATE, LANES), 0) < SSM_STATE) == low2
    pi = pi_ref[...]
    pit = pi.T
    pj = pj_ref[...]
    ei = jnp.exp(jnp.minimum(pi, 0.0))
    ej = jnp.exp(jnp.minimum(pj, 0.0))
    tot_i, tot_j = pi[c - 1:c, :], pj[0:1, :]
    wi = jnp.exp(jnp.minimum(tot_i - pi, 0.0)) * pltpu.roll(pi, LANES - 2 * H, axis=1)
    wj = jnp.exp(jnp.minimum(tot_j - pj, 0.0)) * pltpu.roll(pj, LANES - 2 * H, axis=1)
    eti, etj = jnp.exp(jnp.minimum(tot_i, 0.0)), jnp.exp(jnp.minimum(tot_j, 0.0))
    colb = lambda t, k: jnp.broadcast_to(t[:, k:k + 1], (c, LANES))
    for g in range(SSM_GROUPS):
        gs = slice(g * SSM_STATE, (g + 1) * SSM_STATE)
        ci, bi = ci_ref[:, gs], bi_ref[:, gs]
        cj, bj = cj_ref[:, gs], bj_ref[:, gs]
        cb = lax.dot_general(ci, bi, nt, preferred_element_type=_f32)
        ci32, bi32, cj32, bj32 = (t.astype(_f32) for t in (ci, bi, cj, bj))
        for pp in range(SSM_HPG // 2):
            q = g * (SSM_HPG // 2) + pp
            heads = (2 * q, 2 * q + 1)
            xs = slice(q * LANES, (q + 1) * LANES)
            x = xi_ref[:, xs]
            x32 = x.astype(_f32)
            scores, cw, bw = [], [], []
            for h in heads:
                row = lambda o: pit[o + h:o + h + 1, :]
                mf = jnp.where(ii >= jj, jnp.exp(jnp.minimum(colb(pi, h) - row(0), 0.0)), 0.0) * row(2 * H)
                mb = jnp.where(jj >= ii, jnp.exp(jnp.minimum(colb(pi, H + h) - row(H), 0.0)), 0.0) * row(3 * H)
                scores.append((cb * (mf + mb)).astype(MXU_DTYPE))
                cw.append((ci32 * colb(ei, h)).astype(MXU_DTYPE))
                bw.append((bi32 * colb(wi, h)).astype(MXU_DTYPE))
            xa = jnp.where(low, x32, 0.0).astype(MXU_DTYPE)
            xb = jnp.where(low, 0.0, x32).astype(MXU_DTYPE)
            lhs = jnp.concatenate(scores + cw, axis=1)
            rhs = jnp.concatenate([xa, xb, sf[q].astype(MXU_DTYPE)], axis=0)
            y = jnp.dot(lhs, rhs, preferred_element_type=_f32)
            ya_ref[:, xs] = y + dskip_ref[:, xs] * x32
            upd = lax.dot_general(jnp.concatenate(bw, axis=1), x, tn, preferred_element_type=_f32)
            dec = jnp.where(low2, eti[0:1, heads[0]:heads[0] + 1], eti[0:1, heads[1]:heads[1] + 1])
            sf[q] = dec * sf[q] + jnp.where(diag, upd, 0.0)
            x = xj_ref[:, xs]
            cw = [(cj32 * colb(ej, H + h)).astype(MXU_DTYPE) for h in heads]
            bw = [(bj32 * colb(wj, H + h)).astype(MXU_DTYPE) for h in heads]
            yb_ref[:, xs] = jnp.dot(jnp.concatenate(cw, axis=1), sb[q].astype(MXU_DTYPE),
                                    preferred_element_type=_f32)
            upd = lax.dot_general(jnp.concatenate(bw, axis=1), x, tn, preferred_element_type=_f32)
            dec = jnp.where(low2, etj[0:1, H + heads[0]:H + heads[0] + 1], etj[0:1, H + heads[1]:H + heads[1] + 1])
            sb[q] = dec * sb[q] + jnp.where(diag, upd, 0.0)

    @pl.when(i == pl.num_programs(0) - 1)
    def _():
        finf_ref[...] = sf[...]
        finb_ref[...] = sb[...]


def _pair_states(s):
    s = s.reshape(SSM_HEADS // 2, 2, SSM_STATE, SSM_HEADDIM)
    z = jnp.zeros_like(s[:, 0])
    return jnp.concatenate([jnp.concatenate([s[:, 0], z], -1), jnp.concatenate([z, s[:, 1]], -1)], 1)


def _unpair_states(s):
    top, bot = s[:, :SSM_STATE, :SSM_HEADDIM], s[:, SSM_STATE:, SSM_HEADDIM:]
    return jnp.stack([top, bot], 1).reshape(SSM_HEADS, SSM_STATE, SSM_HEADDIM)


def _scan_ssd(cs, bs, xs, pack, d_skip, s0f, s0b):
    L = xs.shape[0]
    nc = L // CHUNK
    dvec = jnp.repeat(d_skip.astype(_f32), SSM_HEADDIM)[None]
    fw = lambda w: pl.BlockSpec((CHUNK, w), lambda i: (i, 0))
    bw = lambda w: pl.BlockSpec((CHUNK, w), lambda i: (nc - 1 - i, 0))
    pshape = (SSM_HEADS // 2, 2 * SSM_STATE, 2 * SSM_HEADDIM)
    st = pl.BlockSpec(pshape, lambda i: (0, 0, 0))
    yshape = jax.ShapeDtypeStruct((L, SSM_W), _f32)
    sshape = jax.ShapeDtypeStruct(pshape, _f32)
    ya, yb, fin_f, fin_b = pl.pallas_call(
        _scan_ssd_kernel,
        grid=(nc,),
        in_specs=[fw(SSM_GN), fw(SSM_GN), fw(SSM_W), fw(LANES), bw(SSM_GN), bw(SSM_GN), bw(SSM_W), bw(LANES),
                  pl.BlockSpec((1, SSM_W), lambda i: (0, 0)), st, st],
        out_specs=[fw(SSM_W), bw(SSM_W), st, st],
        out_shape=[yshape, yshape, sshape, sshape],
        scratch_shapes=[pltpu.VMEM(pshape, _f32)] * 2,
        compiler_params=_cparams("arbitrary"),
        name="scan_ssd",
    )(cs, bs, xs, pack, cs, bs, xs, pack, dvec, _pair_states(s0f), _pair_states(s0b))
    return ya, yb, _unpair_states(fin_f), _unpair_states(fin_b)


def _merge_kernel(ra_ref, rb_ref, sa_ref, sb_ref, gr_ref, gs_ref, nw_ref, yr_ref, ys_ref):
    for h in range(RET_HEADS):
        sl = slice(h * RET_DH, (h + 1) * RET_DH)
        y = _ln_rows(ra_ref[:, sl] + rb_ref[:, sl])
        yr_ref[:, sl] = (y * _silu(gr_ref[:, sl].astype(_f32))).astype(yr_ref.dtype)
    gw = SSM_W // SSM_GROUPS
    for g in range(SSM_GROUPS):
        sl = slice(g * gw, (g + 1) * gw)
        y = (sa_ref[:, sl] + sb_ref[:, sl]) * _silu(gs_ref[:, sl].astype(_f32))
        y = y * lax.rsqrt(jnp.mean(y * y, -1, keepdims=True) + LN_EPS)
        ys_ref[:, sl] = (y * nw_ref[:, sl]).astype(ys_ref.dtype)


def _gate_spec(tr, n):
    return pl.BlockSpec((tr, BR_W), lambda i: (i, B_GATE // BR_W + n))


def _merge(ra, rb, sa, sb_, pb, norm_w):
    L = ra.shape[0]
    tr = min(512, L)
    row = pl.BlockSpec((tr, BR_W), lambda i: (i, 0))
    shp = jax.ShapeDtypeStruct((L, BR_W), MXU_DTYPE)
    return pl.pallas_call(
        _merge_kernel,
        grid=(L // tr,),
        in_specs=[row, row, row, row, _gate_spec(tr, 1), _gate_spec(tr, 3),
                  pl.BlockSpec((1, BR_W), lambda i: (0, 0))],
        out_specs=[row, row],
        out_shape=[shp, shp],
        compiler_params=_cparams("parallel"),
        name="merge",
    )(ra, rb, sa, sb_, pb, pb, norm_w.astype(_f32)[None])


def _pool_kernel(x_ref, prev_ref, next_ref, g_ref, pw_ref, ps_ref, o_ref, *, L):
    i = pl.program_id(0)
    t = x_ref.shape[0]
    halo = HALO
    has_prev = (i > 0).astype(_f32)
    has_next = (i < pl.num_programs(0) - 1).astype(_f32)
    pos = i * t + lax.broadcasted_iota(jnp.int32, (t, 1), 0)
    for g, win in enumerate(POOL_WINDOWS):
        sl = slice(g * POOL_GROUP, (g + 1) * POOL_GROUP)
        x = x_ref[:, sl].astype(_f32)
        s = jnp.concatenate([prev_ref[:, sl].astype(_f32) * has_prev, x,
                             next_ref[:, sl].astype(_f32) * has_next], axis=0)
        rows = t + 2 * halo
        width = 1
        while width < win:
            s = s + pltpu.roll(s, rows - width, axis=0)
            width *= 2
        off = halo - win // 2
        if off:
            s = pltpu.roll(s, rows - off, axis=0)
        cnt = jnp.minimum(pos + win // 2, L) - jnp.maximum(pos - win // 2, 0)
        d = s[:t] / cnt.astype(_f32) - x
        y = jnp.dot(d.astype(MXU_DTYPE), pw_ref[g], preferred_element_type=_f32)
        o_ref[:, sl] = (y * ps_ref[:, sl] * _silu(g_ref[:, sl].astype(_f32))).astype(o_ref.dtype)


def _pool(pb, L, pool_w, pool_scale):
    tr = 256
    return pl.pallas_call(
        functools.partial(_pool_kernel, L=L),
        grid=(L // tr,),
        in_specs=_halo_specs(tr, L, POOL_W, B_POOL) + [
            _gate_spec(tr, 2),
            pl.BlockSpec((POOL_GROUPS, POOL_GROUP, POOL_GROUP), lambda i: (0, 0, 0)),
            pl.BlockSpec((1, POOL_W), lambda i: (0, 0))],
        out_specs=pl.BlockSpec((tr, POOL_W), lambda i: (i, 0)),
        out_shape=jax.ShapeDtypeStruct((L, POOL_W), MXU_DTYPE),
        compiler_params=_cparams("parallel"),
        name="pool",
    )(pb, pb, pb, pb, pool_w.astype(MXU_DTYPE), pool_scale.astype(_f32)[None])


def _prep_hy_kernel(v_ref, vp_ref, vn_ref, x0_ref, x0p_ref, x0n_ref, x1_ref, x1p_ref, x1n_ref,
                    g_ref, w_ref, b_ref, wo_ref, x0g_ref):
    i = pl.program_id(0)
    has_prev = (i > 0).astype(_f32)
    has_next = (i < pl.num_programs(0) - 1).astype(_f32)

    def conv(n, x_ref, p_ref, n_ref):
        sl = slice(n * HY_W, (n + 1) * HY_W)
        return _conv3(x_ref, p_ref, n_ref, w_ref.at[:, sl], b_ref.at[:, sl], has_prev, has_next)

    hv = conv(0, v_ref, vp_ref, vn_ref)
    hx0 = conv(1, x0_ref, x0p_ref, x0n_ref)
    hx1 = conv(2, x1_ref, x1p_ref, x1n_ref)
    wo_ref[...] = hx1 * hv
    x0g_ref[...] = hx0 * _silu(g_ref[...].astype(_f32))


def _prep_hy(pb, L, conv_w, conv_b):
    tr = 256
    w = jnp.pad(conv_w.astype(_f32), ((0, SUBLANES - 3), (0, 0)))
    row = pl.BlockSpec((tr, HY_W), lambda i: (i, 0))
    shp = jax.ShapeDtypeStruct((L, HY_W), _f32)
    secs = sum((_halo_specs(tr, L, HY_W, B_HY + n * HY_W) for n in range(3)), [])
    return pl.pallas_call(
        _prep_hy_kernel,
        grid=(L // tr,),
        in_specs=secs + [_gate_spec(tr, 0),
                         pl.BlockSpec((SUBLANES, 3 * HY_W), lambda i: (0, 0)),
                         pl.BlockSpec((1, 3 * HY_W), lambda i: (0, 0))],
        out_specs=[row, row],
        out_shape=[shp, shp],
        compiler_params=_cparams("parallel"),
        name="prep_hy",
    )(*([pb] * 10), w, conv_b.astype(_f32)[None])


def _split(x):
    hi = x.astype(MXU_DTYPE)
    return hi, (x - hi.astype(_f32)).astype(MXU_DTYPE)


def _dot3(a_hi, a_lo, b):
    b_hi, b_lo = _split(b)
    d = lambda p, q: jnp.dot(p, q, preferred_element_type=_f32)
    return d(a_hi, b_hi) + (d(a_hi, b_lo) + d(a_lo, b_hi))


def _dot2(a_hi, a_lo, b):
    b = b.astype(MXU_DTYPE)
    d = lambda p, q: jnp.dot(p, q, preferred_element_type=_f32)
    return d(a_hi, b) + d(a_lo, b)


def _dot2_stacked(a2, b):
    m = a2.shape[0] // 2
    r = jnp.dot(a2, b.astype(MXU_DTYPE), preferred_element_type=_f32)
    return r[:m] + r[m:]


def _const_split(m):
    return _split(jnp.asarray(m, _f32))


def _filter_kernel(z_ref, w1_ref, b1_ref, w2_ref, b2_ref, w3hi_ref, w3lo_ref, freq_ref, delta_ref, o_ref, *, L):
    t = z_ref.shape[1]
    dot = functools.partial(jnp.dot, precision=HIGHEST, preferred_element_type=_f32)
    freq = freq_ref[...]
    hdn = jnp.sin(freq * (dot(w1_ref[...], z_ref[...]) + b1_ref[...]))
    hdn = jnp.sin(freq * (dot(w2_ref[...], hdn) + b2_ref[...]))
    h_hi, h_lo = _split(hdn)
    d = lambda p, q: lax.dot_general(p, q, (((0,), (0,)), ((), ())), preferred_element_type=_f32)
    filt = d(h_hi, w3hi_ref[...]) + (d(h_hi, w3lo_ref[...]) + d(h_lo, w3hi_ref[...]))
    n = pl.program_id(0) * t + lax.broadcasted_iota(jnp.int32, (t, 1), 0)
    lag = jnp.minimum(jnp.where(n < L, n, 2 * L - n), L - 1).astype(_f32)
    o_ref[...] = jnp.where(n == L, 0.0, filt) * jnp.exp(-(lag / (L - 1)) * delta_ref[...])


def _hy_filter(L, lp):
    n = jnp.arange(2 * L)
    lag = jnp.minimum(jnp.where(n < L, n, 2 * L - n), L - 1).astype(_f32)[:, None]
    t = lag / (L - 1)
    w = 2.0 * math.pi * lag / L
    bands = jnp.linspace(1e-4, HY_BANDS - 1, HY_BANDS, dtype=_f32)[None, :]
    z = jnp.concatenate([t, jnp.cos(bands * w), -jnp.sin(bands * w)], axis=-1)
    emb = z.shape[1]
    zt = jnp.pad(z, ((0, 0), (0, LANES - emb))).T
    w1t = jnp.pad(lp['hy_w1'].astype(_f32), ((0, LANES - emb), (0, 0))).T
    deltas = jnp.abs(jnp.linspace(HY_MIN_DECAY, HY_MAX_DECAY, HY_W, dtype=_f32))[None]
    tr = min(512, L)
    w3hi, w3lo = _split(lp['hy_w3'].astype(_f32))
    full = lambda a: pl.BlockSpec(a.shape, lambda i: (0,) * a.ndim)
    half = pl.BlockSpec((w3hi.shape[0], HY_W), lambda i: (0, i // (L // tr)))
    colv = lambda v: v.astype(_f32)[:, None]
    pre = [w1t, colv(lp['hy_b1']), lp['hy_w2'].astype(_f32).T, colv(lp['hy_b2'])]
    post = [colv(lp['hy_freq']), deltas]
    return pl.pallas_call(
        functools.partial(_filter_kernel, L=L),
        grid=(2 * L // tr,),
        in_specs=[pl.BlockSpec((LANES, tr), lambda i: (0, i))] + [full(a) for a in pre] + [half, half]
        + [full(a) for a in post],
        out_specs=pl.BlockSpec((tr, HY_W), lambda i: (i, 0)),
        out_shape=jax.ShapeDtypeStruct((2 * L, HY_W), _f32),
        compiler_params=_cparams("parallel"),
        name="hy_filter",
    )(zt, *pre, w3hi, w3lo, *post)


def _cs(num, den):
    ang = 2.0 * np.pi * (np.asarray(num, np.int64) % den) / den
    return np.cos(ang), np.sin(ang)


FFT_N2 = LANES


def _fft_rows(n1):
    return -(-(n1 // 2 + 1) // SUBLANES) * SUBLANES


FFT_GROUP = 16


def _fft_first_kernel(x_ref, m2_ref, o_ref):
    n2 = FFT_N2
    rows, kb = x_ref.shape[0] // n2, o_ref.shape[1]
    for g0 in range(0, n2, FFT_GROUP):
        x = jnp.concatenate([x_ref[pl.ds(g0 + g, rows, stride=n2), :] for g in range(FFT_GROUP)], axis=1)
        y = _dot2_stacked(m2_ref[...], x)
        for g in range(FFT_GROUP):
            tile = y[:, g * LANES:(g + 1) * LANES].reshape(2, kb, SUBLANES, LANES)
            o_ref[:, :, 0, (g0 + g) * SUBLANES:(g0 + g + 1) * SUBLANES, :] = tile


def _fft_first(x, n1):
    n2 = FFT_N2
    rows, ch = x.shape[0] // n2, x.shape[1]
    kp = _fft_rows(n1)
    c, s = _cs(np.outer(np.arange(kp), np.arange(rows)), n1)
    m2 = jnp.concatenate(_const_split(np.concatenate([c, -s], 0)), axis=0)
    kb, ct = kp // SUBLANES, ch // LANES
    return pl.pallas_call(
        _fft_first_kernel,
        grid=(ct,),
        in_specs=[pl.BlockSpec((rows * n2, LANES), lambda j: (0, j)),
                  pl.BlockSpec((4 * kp, rows), lambda j: (0, 0))],
        out_specs=pl.BlockSpec((2, kb, 1, n2 * SUBLANES, LANES), lambda j: (0, 0, j, 0, 0)),
        out_shape=jax.ShapeDtypeStruct((2, kb, ct, n2 * SUBLANES, LANES), _f32),
        compiler_params=_cparams("parallel"),
        name="fft_first",
    )(x, m2)


def _fft_mid_kernel(a_ref, f_ref, twr_ref, twi_ref, fhi_ref, flo_ref, ghi_ref, glo_ref, o_ref):
    n2 = FFT_N2
    for s0 in range(0, SUBLANES, 2):
        ts, tws = [], []
        for s in (s0, s0 + 1):
            rows = pl.ds(s, n2, stride=SUBLANES)
            twr, twi = twr_ref[s], twi_ref[s]
            tws.append((twr, twi))
            for t_ref in (a_ref, f_ref):
                tr, ti = t_ref[0, 0, 0, rows, :], t_ref[1, 0, 0, rows, :]
                ts.append(jnp.concatenate([tr * twr - ti * twi, tr * twi + ti * twr], axis=0))
        y = _dot2(fhi_ref[...], flo_ref[...], jnp.concatenate(ts, axis=1))
        ps = []
        for p in range(2):
            x, h = y[:, 2 * p * LANES:(2 * p + 1) * LANES], y[:, (2 * p + 1) * LANES:(2 * p + 2) * LANES]
            xr, xi, hr, hi = x[:n2], x[n2:], h[:n2], h[n2:]
            ps.append(jnp.concatenate([xr * hr - xi * hi, xr * hi + xi * hr], axis=0))
        b = _dot2(ghi_ref[...], glo_ref[...], jnp.concatenate(ps, axis=1))
        for p, s in enumerate((s0, s0 + 1)):
            rows = pl.ds(s, n2, stride=SUBLANES)
            twr, twi = tws[p]
            br, bi = b[:n2, p * LANES:(p + 1) * LANES], b[n2:, p * LANES:(p + 1) * LANES]
            o_ref[0, 0, 0, rows, :] = br * twr + bi * twi
            o_ref[1, 0, 0, rows, :] = bi * twr - br * twi


def _fft_mid(a, f, n1):
    n2 = FFT_N2
    _, kb, ct, rows, _ = a.shape
    kp = kb * SUBLANES
    n = n1 * n2
    idx = jnp.arange(kp)[:, None] * jnp.arange(n2)[None, :]
    ang = (2.0 * math.pi / n) * (idx % n).astype(_f32)
    twr = jnp.broadcast_to(jnp.cos(ang)[:, :, None], (kp, n2, LANES))
    twi = jnp.broadcast_to(-jnp.sin(ang)[:, :, None], (kp, n2, LANES))
    c, s = _cs(np.outer(np.arange(n2), np.arange(n2)), n2)
    fhi, flo = _const_split(np.block([[c, s], [-s, c]]))
    ghi, glo = _const_split(np.block([[c, -s], [s, c]]))
    blk = pl.BlockSpec((2, 1, 1, rows, LANES), lambda k, j: (0, k, j, 0, 0))
    tw = pl.BlockSpec((SUBLANES, n2, LANES), lambda k, j: (k, 0, 0))
    mat = pl.BlockSpec((2 * n2, 2 * n2), lambda k, j: (0, 0))
    return pl.pallas_call(
        _fft_mid_kernel,
        grid=(kb, ct),
        in_specs=[blk, blk, tw, tw, mat, mat, mat, mat],
        out_specs=blk,
        out_shape=jax.ShapeDtypeStruct(a.shape, _f32),
        compiler_params=_cparams("parallel", "parallel"),
        name="fft_mid",
    )(a, f, twr, twi, fhi, flo, ghi, glo)


def _fft_last_kernel(c_ref, m2_ref, w_ref, x0g_ref, bias_ref, o_ref):
    n2 = FFT_N2
    rows, kb = w_ref.shape[0] // n2, c_ref.shape[1]
    for g0 in range(0, n2, FFT_GROUP):
        tiles = [c_ref[:, :, 0, (g0 + g) * SUBLANES:(g0 + g + 1) * SUBLANES, :].reshape(2 * kb * SUBLANES, LANES)
                 for g in range(FFT_GROUP)]
        y = _dot2_stacked(m2_ref[...], jnp.concatenate(tiles, axis=1))
        for g in range(FFT_GROUP):
            at = pl.ds(g0 + g, rows, stride=n2)
            o_ref[at, :] = x0g_ref[at, :] * (y[:, g * LANES:(g + 1) * LANES] + w_ref[at, :] * bias_ref[...])


def _fft_last(cc, n1, w, x0g, bias):
    n2 = FFT_N2
    rows, ch = w.shape[0] // n2, w.shape[1]
    _, kb, ct, _, _ = cc.shape
    kp = kb * SUBLANES
    n = n1 * n2
    c, s = _cs(np.outer(np.arange(rows), np.arange(kp)), n1)
    k1 = np.arange(kp)
    mult = np.where((k1 == 0) | (k1 == n1 // 2), 1.0, np.where(k1 < n1 // 2, 2.0, 0.0))
    m2 = jnp.concatenate(_const_split(np.concatenate([c * mult, -s * mult], 1) / n), axis=0)
    blk = pl.BlockSpec((rows * n2, LANES), lambda j: (0, j))
    return pl.pallas_call(
        _fft_last_kernel,
        grid=(ct,),
        in_specs=[pl.BlockSpec((2, kb, 1, n2 * SUBLANES, LANES), lambda j: (0, 0, j, 0, 0)),
                  pl.BlockSpec((2 * rows, 2 * kp), lambda j: (0, 0)),
                  blk, blk, pl.BlockSpec((1, LANES), lambda j: (0, j))],
        out_specs=blk,
        out_shape=jax.ShapeDtypeStruct((rows * n2, ch), _f32),
        compiler_params=_cparams("parallel"),
        name="fft_last",
    )(cc, m2, w, x0g, bias.astype(_f32)[None])


def _hy_small_kernel(w_ref, buf_ref, x0g_ref, bias_ref, fwhi_ref, fwlo_ref, fbhi_ref, fblo_ref,
                     ihi_ref, ilo_ref, o_ref):
    n = buf_ref.shape[0]
    w = w_ref[...]
    wf = _dot3(fwhi_ref[...], fwlo_ref[...], w)
    hf = _dot3(fbhi_ref[...], fblo_ref[...], buf_ref[...])
    wr, wi, hr, hi = wf[:n], wf[n:], hf[:n], hf[n:]
    y = _dot3(ihi_ref[...], ilo_ref[...], jnp.concatenate([wr * hr - wi * hi, wr * hi + wi * hr], axis=0))
    o_ref[...] = (x0g_ref[...] * (y + w * bias_ref[...])).astype(o_ref.dtype)


def _hy_conv_small(w, buf, x0g, bias):
    L, ch = w.shape
    n = 2 * L
    tc = 256
    c, s = _cs(np.outer(np.arange(n), np.arange(n)), n)
    fb = np.concatenate([c, -s], 0)
    mats = [*_const_split(fb[:, :L]), *_const_split(fb),
            *_const_split(np.concatenate([c[:L], -s[:L]], 1) / n)]
    col = lambda r: pl.BlockSpec((r, tc), lambda j: (0, j))
    return pl.pallas_call(
        _hy_small_kernel,
        grid=(ch // tc,),
        in_specs=[col(L), col(n), col(L), col(1)] + [pl.BlockSpec(m.shape, lambda j: (0, 0)) for m in mats],
        out_specs=col(L),
        out_shape=jax.ShapeDtypeStruct((L, ch), MXU_DTYPE),
        compiler_params=_cparams("parallel"),
        name="hy_conv_small",
    )(w, buf, x0g, bias.astype(_f32)[None], *mats)


def _hy_conv(w, buf, x0g, bias):
    L, ch = w.shape
    if L < 512:
        return _hy_conv_small(w, buf, x0g, bias)
    n1 = 2 * L // FFT_N2
    return _fft_last(_fft_mid(_fft_first(w, n1), _fft_first(buf, n1), n1), n1, w, x0g, bias)


def _zero_states():
    return (jnp.zeros((RET_HEADS, RET_DH, RET_DH), _f32), jnp.zeros((RET_HEADS, RET_DH, RET_DH), _f32),
            jnp.zeros((SSM_HEADS, SSM_STATE, SSM_HEADDIM), _f32),
            jnp.zeros((SSM_HEADS, SSM_STATE, SSM_HEADDIM), _f32))


def _recurrent(proj, L, lp, states, latent, states_only=False):
    pa, pb = proj
    q, k, v = _prep_ret(pa, pb, L, latent, states_only)
    ra, rb, ret_f, ret_b = _scan_ret(q, k, v, lp['ret_decay_logit'], states[0], states[1])
    cs, bs, xs, pack = _prep_ssd(pa, pb, L, lp['conv_ssm_w'], lp['conv_ssm_b'], lp['ssm_dt_bias'],
                                 lp['ssm_A_log'], states_only)
    sa, sb_, ssm_f, ssm_b = _scan_ssd(cs, bs, xs, pack, lp['ssm_D'], states[2], states[3])
    return (ra, rb, sa, sb_), (ret_f, ret_b, ssm_f, ssm_b)


def _mix(h, mod, lp, states, latent):
    L = h.shape[0]
    proj = _in_proj(h, mod[0], mod[1], lp)
    pb = proj[1]
    (ra, rb, sa, sb_), fin = _recurrent(proj, L, lp, states, latent)
    y_ret, y_ssm = _merge(ra, rb, sa, sb_, pb, lp['ssm_norm_w'])
    w, x0g = _prep_hy(pb, L, lp['conv_hy_w'], lp['conv_hy_b'])
    y_hy = _hy_conv(w, _hy_filter(L, lp), x0g, lp['hy_bias'])
    y_pool = _pool(pb, L, lp['pool_w'], lp['pool_scale'])
    out = _out_proj([y_hy, y_ret, y_pool, y_ssm], lp['w_out'], h, mod[2], lp['ln_g'], lp['ln_b'])
    return out, fin


def _context_states(hc, mod, lp):
    proj = _in_proj(hc, mod[0], mod[1], lp, SSM_W + SSM_GN)
    _, fin = _recurrent(proj, hc.shape[0], lp, _zero_states(), False, states_only=True)
    return fin


def kernel(x, c, ctx, c_ctx, w_mod, b_mod, w_in, conv_ssm_w, conv_ssm_b, conv_hy_w, conv_hy_b,
           ret_decay_logit, ssm_A_log, ssm_dt_bias, ssm_D, ssm_norm_w, hy_w1, hy_b1, hy_w2, hy_b2,
           hy_w3, hy_freq, hy_bias, pool_w, pool_scale, w_out, ln_g, ln_b):
    assert x.shape[0] == 1
    h, hc = x[0], ctx[0]
    w_in_t = jnp.swapaxes(w_in, 1, 2)
    mods = _adaln(jnp.concatenate([c, c_ctx[None]], axis=0), w_mod, b_mod)
    for l in range(DEPTH):
        lp = {
            'layer': l, 'w_in_t': w_in_t, 'w_a_t': _cast_layer(w_in_t, l, N_A, N_A // 2, 512),
            'w_out': _cast_layer(w_out, l, w_out.shape[1], 1024, 2048)[0],
            'conv_ssm_w': conv_ssm_w[l], 'conv_ssm_b': conv_ssm_b[l],
            'conv_hy_w': conv_hy_w[l], 'conv_hy_b': conv_hy_b[l], 'ret_decay_logit': ret_decay_logit[l],
            'ssm_A_log': ssm_A_log[l], 'ssm_dt_bias': ssm_dt_bias[l], 'ssm_D': ssm_D[l],
            'ssm_norm_w': ssm_norm_w[l], 'hy_w1': hy_w1[l], 'hy_b1': hy_b1[l], 'hy_w2': hy_w2[l],
            'hy_b2': hy_b2[l], 'hy_w3': hy_w3[l], 'hy_freq': hy_freq[l], 'hy_bias': hy_bias[l],
            'pool_w': pool_w[l], 'pool_scale': pool_scale[l], 'ln_g': ln_g[l], 'ln_b': ln_b[l],
        }
        mod = lambda r: tuple(mods[l, r:r + 1, n * D_MODEL:(n + 1) * D_MODEL] for n in range(3))
        if l < DEPTH - 1:
            hc_next, states = _mix(hc, mod(1), lp, _zero_states(), False)
        else:
            states = _context_states(hc, mod(1), lp)
            hc_next = hc
        h, _ = _mix(h, mod(0), lp, states, True)
        hc = hc_next
    return h[None]
```

```python
import functools
import math

import jax
import jax.numpy as jnp
import numpy as np
from jax import lax
from jax.experimental import pallas as pl
from jax.experimental.pallas import tpu as pltpu

D_MODEL = 4096
DEPTH = 2
GRID_W = 64
MIX_W = D_MODEL
BR_W = MIX_W // 4
HY_W = RET_W = POOL_W = SSM_W = BR_W
RET_HEADS = 8
RET_DH = RET_W // RET_HEADS
ROPE_BASE = 10000.0
SSM_HEADDIM = 64
SSM_HEADS = SSM_W // SSM_HEADDIM
SSM_GROUPS = 4
SSM_HPG = SSM_HEADS // SSM_GROUPS
SSM_STATE = 128
SSM_GN = SSM_GROUPS * SSM_STATE
CHUNK = 128
POOL_WINDOWS = (2, 4, 8, 16)
POOL_GROUPS = len(POOL_WINDOWS)
POOL_GROUP = POOL_W // POOL_GROUPS
HY_BANDS = 16
HY_TARGET = 1e-2
HY_FAST = 0.3
HY_SLOW = 1.5
HY_MIN_DECAY = math.log(HY_TARGET) / HY_SLOW
HY_MAX_DECAY = math.log(HY_TARGET) / HY_FAST
ALPHA = (2.0 * DEPTH) ** 0.25
LN_EPS = 1e-5

O_RET_K = 0
O_RET_V = O_RET_K + RET_W
O_SSM_DT = O_RET_V + RET_W
O_SSM_X = O_SSM_DT + 2 * SSM_HEADS
O_SSM_B = O_SSM_X + SSM_W
O_RET_Q = O_SSM_B + SSM_GN
O_SSM_C = O_RET_Q + RET_W
O_HY = O_SSM_C + SSM_GN
O_POOL = O_HY + 3 * HY_W
O_GATE = O_POOL + POOL_W
N_IN = O_GATE + MIX_W

LANES = 128
SUBLANES = 8
N_A = O_SSM_DT + LANES
B_SX = 0
B_SB = O_SSM_B - O_SSM_X
B_RQ = O_RET_Q - O_SSM_X
B_SC = O_SSM_C - O_SSM_X
B_HY = O_HY - O_SSM_X
B_POOL = O_POOL - O_SSM_X
B_GATE = O_GATE - O_SSM_X
N_B = N_IN - O_SSM_X

VMEM_LIMIT_BYTES = 56 * 1024 * 1024
MXU_DTYPE = jnp.bfloat16
HIGHEST = lax.Precision.HIGHEST

_f32 = jnp.float32


def _cparams(*sem, vmem=VMEM_LIMIT_BYTES):
    return pltpu.CompilerParams(dimension_semantics=sem, vmem_limit_bytes=vmem)


def _silu(x):
    return x * jax.nn.sigmoid(x)


def _ln_rows(z):
    mu = jnp.mean(z, -1, keepdims=True)
    zc = z - mu
    var = jnp.mean(zc * zc, -1, keepdims=True)
    return zc * lax.rsqrt(var + LN_EPS)


def _cast_kernel(w_ref, o_ref):
    o_ref[...] = w_ref[...].astype(o_ref.dtype)


def _cast_layer(w, l, r, tr, tc):
    c = w.shape[2]
    assert r % tr == 0 and c % tc == 0
    return pl.pallas_call(
        _cast_kernel,
        grid=(r // tr, c // tc),
        in_specs=[pl.BlockSpec((1, tr, tc), lambda i, j: (l, i, j))],
        out_specs=pl.BlockSpec((1, tr, tc), lambda i, j: (0, i, j)),
        out_shape=jax.ShapeDtypeStruct((1, r, c), MXU_DTYPE),
        compiler_params=_cparams("parallel", "parallel"),
        name="cast_layer",
    )(w)


ADALN_ROWS = 32


def _adaln_kernel(c_ref, w_ref, b_ref, o_ref, xs_ref):
    @pl.when((pl.program_id(0) == 0) & (pl.program_id(1) == 0))
    def _():
        xs_ref[...] = _silu(c_ref[...])

    r, k = c_ref.shape[0], c_ref.shape[1]
    nj = w_ref.shape[-1] // LANES

    def body(t, accs):
        rows = pl.ds(pl.multiple_of(t * ADALN_ROWS, ADALN_ROWS), ADALN_ROWS)
        xs = [xs_ref[m, rows, :] for m in range(r)]
        ws = [w_ref[0, rows, j * LANES:(j + 1) * LANES] for j in range(nj)]
        return tuple(accs[m * nj + j] + xs[m] * ws[j] for m in range(r) for j in range(nj))

    accs = lax.fori_loop(0, k // ADALN_ROWS, body,
                         tuple(jnp.zeros((ADALN_ROWS, LANES), _f32) for _ in range(r * nj)), unroll=4)
    outs = [jnp.concatenate([jnp.sum(accs[m * nj + j], axis=0, keepdims=True) for j in range(nj)], axis=1)
            for m in range(r)]
    outs.append(jnp.zeros((SUBLANES - r, w_ref.shape[-1]), _f32))
    o_ref[0] = jnp.concatenate(outs, axis=0) + b_ref[0]


def _adaln(c_rows, w_mod, b_mod):
    dep, k, n = w_mod.shape
    r = c_rows.shape[0]
    tn = 512
    cb = jnp.broadcast_to(c_rows.astype(_f32)[:, :, None], (r, k, LANES))
    return pl.pallas_call(
        _adaln_kernel,
        grid=(dep, n // tn),
        in_specs=[pl.BlockSpec((r, k, LANES), lambda l, j: (0, 0, 0)),
                  pl.BlockSpec((1, k, tn), lambda l, j: (l, 0, j)),
                  pl.BlockSpec((1, 1, tn), lambda l, j: (l, 0, j))],
        out_specs=pl.BlockSpec((1, SUBLANES, tn), lambda l, j: (l, 0, j)),
        out_shape=jax.ShapeDtypeStruct((dep, SUBLANES, n), _f32),
        scratch_shapes=[pltpu.VMEM((r, k, LANES), _f32)],
        compiler_params=_cparams("arbitrary", "arbitrary"),
        name="adaln",
    )(cb, w_mod, b_mod[:, None, :])


def _matmul_nt_kernel(a_ref, b_ref, o_ref):
    o_ref[...] = lax.dot_general(a_ref[...], b_ref[0].astype(MXU_DTYPE), (((1,), (1,)), ((), ())),
                                 preferred_element_type=_f32).astype(o_ref.dtype)


def _matmul_nt(a, wt, l, row0, n, tm, tn, out_dtype):
    m, k = a.shape
    assert m % tm == 0 and n % tn == 0 and row0 % 32 == 0 and tn % 32 == 0
    return pl.pallas_call(
        _matmul_nt_kernel,
        grid=(m // tm, n // tn),
        in_specs=[pl.BlockSpec((tm, k), lambda i, j: (i, 0)),
                  pl.BlockSpec((pl.Element(1), pl.Element(tn), pl.Element(k)),
                               lambda i, j: (l, pl.multiple_of(row0 + j * tn, 32), 0))],
        out_specs=pl.BlockSpec((tm, tn), lambda i, j: (i, j)),
        out_shape=jax.ShapeDtypeStruct((m, n), out_dtype),
        compiler_params=_cparams("parallel", "parallel"),
        name="matmul_nt",
    )(a, wt)


def _modulate_kernel(h_ref, shift_ref, scale_ref, o_ref):
    o_ref[...] = (_ln_rows(h_ref[...]) * (1.0 + scale_ref[...]) + shift_ref[...]).astype(o_ref.dtype)


def _modulate(h, shift, scale):
    L, d = h.shape
    tr = min(512, L)
    vec = pl.BlockSpec((1, d), lambda i: (0, 0))
    return pl.pallas_call(
        _modulate_kernel,
        grid=(L // tr,),
        in_specs=[pl.BlockSpec((tr, d), lambda i: (i, 0)), vec, vec],
        out_specs=pl.BlockSpec((tr, d), lambda i: (i, 0)),
        out_shape=jax.ShapeDtypeStruct((L, d), MXU_DTYPE),
        compiler_params=_cparams("parallel"),
        name="modulate",
    )(h, shift, scale)


def _in_proj(h, shift, scale, lp, n_b=N_B):
    L = h.shape[0]
    u = _modulate(h, shift, scale)
    pa = _matmul_nt(u, lp['w_a_t'], 0, 0, N_A, 512 if L % 512 == 0 else 256, N_A, _f32)
    pb = _matmul_nt(u, lp['w_in_t'], lp['layer'], O_SSM_X, n_b, 1024 if L % 1024 == 0 else 256, 512, MXU_DTYPE)
    return pa, pb


def _out_proj_kernel(y0_ref, y1_ref, y2_ref, y3_ref, w_ref, h_ref, gate_ref, g_ref, b_ref, o_ref):
    out = None
    for n, y_ref in enumerate((y0_ref, y1_ref, y2_ref, y3_ref)):
        d = jnp.dot(y_ref[...].astype(MXU_DTYPE), w_ref[n * BR_W:(n + 1) * BR_W, :], preferred_element_type=_f32)
        out = d if out is None else out + d
    z = ALPHA * h_ref[...] + gate_ref[...] * out
    o_ref[...] = _ln_rows(z) * g_ref[...] + b_ref[...]


OUT_PROJ_VMEM_BYTES = 60 * 1024 * 1024


def _out_proj(ys, w, h, gate, g, b):
    L, d = h.shape
    tm = 256
    lhs = pl.BlockSpec((tm, BR_W), lambda i: (i, 0))
    vec = pl.BlockSpec((1, d), lambda i: (0, 0))
    row = pl.BlockSpec((tm, d), lambda i: (i, 0))
    return pl.pallas_call(
        _out_proj_kernel,
        grid=(L // tm,),
        in_specs=[lhs] * len(ys) + [
            pl.BlockSpec(w.shape, lambda i: (0, 0), pipeline_mode=pl.Buffered(1)), row, vec, vec, vec],
        out_specs=row,
        out_shape=jax.ShapeDtypeStruct((L, d), _f32),
        compiler_params=_cparams("parallel", vmem=OUT_PROJ_VMEM_BYTES),
        name="out_proj",
    )(*ys, w, h, gate, g[None], b[None])


def _rope_tables(L):
    rows = L // GRID_W
    row = jnp.repeat(jnp.arange(rows), GRID_W).astype(_f32)
    col = jnp.tile(jnp.arange(GRID_W), rows).astype(_f32)
    nq = RET_DH // 4
    inv = ROPE_BASE ** (-jnp.arange(nq, dtype=_f32) / nq)
    ang = jnp.concatenate([row[:, None] * inv, col[:, None] * inv], -1)
    cos, sin = jnp.cos(ang), jnp.sin(ang)
    return jnp.concatenate([cos, cos], -1), jnp.concatenate([-sin, sin], -1)


def _prep_ret_kernel(qlo_ref, qhi_ref, k_ref, v_ref, cos_ref, sin_ref, qo_ref, ko_ref, vo_ref, *, rope):
    def rot(t):
        if not rope:
            return t
        return t * cos_ref[...] + pltpu.roll(t, RET_DH // 2, axis=1) * sin_ref[...]

    half = RET_HEADS // 2
    for h in range(RET_HEADS):
        sl = slice(h * RET_DH, (h + 1) * RET_DH)
        q_ref, qs = (qlo_ref, sl) if h < half else (qhi_ref, slice((h - half) * RET_DH, (h - half + 1) * RET_DH))
        qo_ref[:, sl] = rot(q_ref[:, qs].astype(_f32)).astype(qo_ref.dtype)
        ko_ref[:, sl] = rot(k_ref[:, sl] * (RET_DH ** -0.5)).astype(ko_ref.dtype)
    vo_ref[...] = v_ref[...].astype(vo_ref.dtype)


def _prep_ret(pa, pb, L, rope, states_only=False):
    tr = 256
    cos, sin = _rope_tables(L) if rope else (jnp.ones((L, LANES), _f32), jnp.zeros((L, LANES), _f32))
    hw = RET_W // 2
    q_src, q_col = (pa, O_RET_K) if states_only else (pb, B_RQ)
    qsp = lambda n: pl.BlockSpec((tr, hw), lambda i: (i, q_col // hw + n))
    sec = lambda c: pl.BlockSpec((tr, RET_W), lambda i: (i, c // RET_W))
    tab = pl.BlockSpec((tr, LANES), lambda i: (i, 0))
    out = pl.BlockSpec((tr, RET_W), lambda i: (i, 0))
    shp = jax.ShapeDtypeStruct((L, RET_W), MXU_DTYPE)
    return pl.pallas_call(
        functools.partial(_prep_ret_kernel, rope=rope),
        grid=(L // tr,),
        in_specs=[qsp(0), qsp(1), sec(O_RET_K), sec(O_RET_V), tab, tab],
        out_specs=[out, out, out],
        out_shape=[shp, shp, shp],
        compiler_params=_cparams("parallel"),
        name="prep_ret",
    )(q_src, q_src, pa, pa, cos, sin)


def _scan_ret_kernel(logit_ref, qi_ref, ki_ref, vi_ref, qj_ref, kj_ref, vj_ref, s0f_ref, s0b_ref,
                     ya_ref, yb_ref, finf_ref, finb_ref,
                     sf, sb, dmask, f_out, f_upd, f_all, b_out, b_upd, b_all):
    i = pl.program_id(0)
    c = CHUNK

    @pl.when(i == 0)
    def _():
        sf[...] = s0f_ref[...]
        sb[...] = s0b_ref[...]
        ii = lax.broadcasted_iota(jnp.int32, (c, c), 0).astype(_f32)
        jj = lax.broadcasted_iota(jnp.int32, (c, c), 1).astype(_f32)
        for h in range(RET_HEADS):
            def lg(d):
                x = logit_ref[d, h]
                v = -jnp.log1p(jnp.exp(-x))
                return jnp.broadcast_to(v[0:1, :], (c, c))
            lf, lb = lg(0), lg(1)
            dmask[h] = jnp.where(ii > jj, jnp.exp(lf * (ii - jj)),
                                 jnp.where(jj > ii, jnp.exp(lb * (jj - ii)), 2.0))
            f_out[h] = jnp.exp(lf * (ii + 1.0))
            f_upd[h] = jnp.exp(lf * (c - 1.0 - ii))
            f_all[h] = jnp.exp(lf * float(c))
            b_out[h] = jnp.exp(lb * (c - ii))
            b_upd[h] = jnp.exp(lb * ii)
            b_all[h] = jnp.exp(lb * float(c))

    tn = (((0,), (0,)), ((), ()))
    nt = (((1,), (1,)), ((), ()))
    heads = range(RET_HEADS)
    sls = [slice(h * RET_DH, (h + 1) * RET_DH) for h in heads]
    scores = [lax.dot_general(qi_ref[:, sl], ki_ref[:, sl], nt, preferred_element_type=_f32) for sl in sls]
    upd_f = [lax.dot_general((ki_ref[:, sl].astype(_f32) * f_upd[h]).astype(MXU_DTYPE), vi_ref[:, sl], tn,
                             preferred_element_type=_f32) for h, sl in zip(heads, sls)]
    upd_b = [lax.dot_general((kj_ref[:, sl].astype(_f32) * b_upd[h]).astype(MXU_DTYPE), vj_ref[:, sl], tn,
                             preferred_element_type=_f32) for h, sl in zip(heads, sls)]
    for h, sl in zip(heads, sls):
        lhs = jnp.concatenate([(scores[h] * dmask[h]).astype(MXU_DTYPE),
                               (qi_ref[:, sl].astype(_f32) * f_out[h]).astype(MXU_DTYPE)], axis=1)
        rhs = jnp.concatenate([vi_ref[:, sl], sf[h].astype(MXU_DTYPE)], axis=0)
        ya_ref[:, sl] = jnp.dot(lhs, rhs, preferred_element_type=_f32)
        yb_ref[:, sl] = jnp.dot((qj_ref[:, sl].astype(_f32) * b_out[h]).astype(MXU_DTYPE),
                                sb[h].astype(MXU_DTYPE), preferred_element_type=_f32)
    for h in heads:
        sf[h] = f_all[h] * sf[h] + upd_f[h]
        sb[h] = b_all[h] * sb[h] + upd_b[h]

    @pl.when(i == pl.num_programs(0) - 1)
    def _():
        finf_ref[...] = sf[...]
        finb_ref[...] = sb[...]


def _scan_ret(q, k, v, logit, s0f, s0b):
    L = q.shape[0]
    nc = L // CHUNK
    logit_b = jnp.broadcast_to(logit.astype(_f32)[:, :, None, None], (2, RET_HEADS, SUBLANES, LANES))
    fw = pl.BlockSpec((CHUNK, RET_W), lambda i: (i, 0))
    bw = pl.BlockSpec((CHUNK, RET_W), lambda i: (nc - 1 - i, 0))
    st = pl.BlockSpec((RET_HEADS, RET_DH, RET_DH), lambda i: (0, 0, 0))
    yshape = jax.ShapeDtypeStruct((L, RET_W), _f32)
    sshape = jax.ShapeDtypeStruct((RET_HEADS, RET_DH, RET_DH), _f32)
    tile = pltpu.VMEM((RET_HEADS, CHUNK, CHUNK), _f32)
    return pl.pallas_call(
        _scan_ret_kernel,
        grid=(nc,),
        in_specs=[pl.BlockSpec((2, RET_HEADS, SUBLANES, LANES), lambda i: (0, 0, 0, 0)),
                  fw, fw, fw, bw, bw, bw, st, st],
        out_specs=[fw, bw, st, st],
        out_shape=[yshape, yshape, sshape, sshape],
        scratch_shapes=[pltpu.VMEM((RET_HEADS, RET_DH, RET_DH), _f32)] * 2 + [tile] * 7,
        compiler_params=_cparams("arbitrary"),
        name="scan_ret",
    )(logit_b, q, k, v, q, k, v, s0f, s0b)


def _shift_rows(x, prev_row, next_row):
    r = x.shape[0]
    rid = lax.broadcasted_iota(jnp.int32, x.shape, 0)
    up = jnp.where(rid == 0, prev_row, pltpu.roll(x, 1, axis=0))
    dn = jnp.where(rid == r - 1, next_row, pltpu.roll(x, r - 1, axis=0))
    return up, dn


HALO = 16


def _conv3(x_ref, prev_ref, next_ref, w_ref, b_ref, has_prev, has_next):
    x = x_ref[...].astype(_f32)
    prev_row = prev_ref[...].astype(_f32)[HALO - 1:HALO, :] * has_prev
    next_row = next_ref[...].astype(_f32)[0:1, :] * has_next
    up, dn = _shift_rows(x, prev_row, next_row)
    return up * w_ref[0:1, :] + x * w_ref[1:2, :] + dn * w_ref[2:3, :] + b_ref[...]


def _halo_specs(tr, L, width, col):
    nb = tr // HALO
    last = L // HALO - 1
    cb = col // width
    return [pl.BlockSpec((tr, width), lambda i: (i, cb)),
            pl.BlockSpec((HALO, width), lambda i: (jnp.maximum(i * nb - 1, 0), cb)),
            pl.BlockSpec((HALO, width), lambda i: (jnp.minimum((i + 1) * nb, last), cb))]


def _prep_ssd_kernel(x_ref, xp_ref, xn_ref, b_ref, bp_ref, bn_ref, c_ref, cp_ref, cn_ref, dt_ref,
                     w_ref, cb_ref, dtb_ref, alog_ref, co_ref, bo_ref, xo_ref, pack_ref):
    i = pl.program_id(0)
    has_prev = (i > 0).astype(_f32)
    has_next = (i < pl.num_programs(0) - 1).astype(_f32)

    def conv(lo, hi, t_ref, p_ref, n_ref, o_ref):
        y = _conv3(t_ref, p_ref, n_ref, w_ref.at[:, lo:hi], cb_ref.at[:, lo:hi], has_prev, has_next)
        o_ref[...] = _silu(y).astype(o_ref.dtype)

    conv(0, SSM_W, x_ref, xp_ref, xn_ref, xo_ref)
    conv(SSM_W, SSM_W + SSM_GN, b_ref, bp_ref, bn_ref, bo_ref)
    conv(SSM_W + SSM_GN, SSM_W + 2 * SSM_GN, c_ref, cp_ref, cn_ref, co_ref)
    z = dt_ref[...] + dtb_ref[...]
    dt = jnp.maximum(z, 0.0) + jnp.log1p(jnp.exp(-jnp.abs(z)))
    a = dt * (-jnp.exp(alog_ref[...]))
    c = CHUNK
    ii = lax.broadcasted_iota(jnp.int32, (c, c), 0)
    jj = lax.broadcasted_iota(jnp.int32, (c, c), 1)
    lower = (jj <= ii).astype(_f32)
    upper = (jj >= ii).astype(_f32)
    lane = lax.broadcasted_iota(jnp.int32, (c, LANES), 1)
    dt_sh = pltpu.roll(dt, 2 * SSM_HEADS, axis=1)
    for n in range(x_ref.shape[0] // c):
        rs = slice(n * c, (n + 1) * c)
        pre = jnp.dot(lower, a[rs], precision=HIGHEST, preferred_element_type=_f32)
        suf = jnp.dot(upper, a[rs], precision=HIGHEST, preferred_element_type=_f32)
        pack_ref[rs, :] = jnp.where(lane < SSM_HEADS, pre,
                                    jnp.where(lane < 2 * SSM_HEADS, suf, dt_sh[rs]))


def _prep_ssd(pa, pb, L, conv_w, conv_b, dt_bias, a_log, states_only=False):
    tr = 256
    c_col = B_SB if states_only else B_SC
    w = jnp.pad(conv_w.astype(_f32), ((0, SUBLANES - 3), (0, 0)))
    lanes = lambda t: jnp.pad(t.astype(_f32).reshape(1, 2 * SSM_HEADS), ((0, 0), (0, LANES - 2 * SSM_HEADS)))
    wd = SSM_W + 2 * SSM_GN
    row = lambda c: pl.BlockSpec((tr, c), lambda i: (i, 0))
    return pl.pallas_call(
        _prep_ssd_kernel,
        grid=(L // tr,),
        in_specs=_halo_specs(tr, L, SSM_W, B_SX) + _halo_specs(tr, L, SSM_GN, B_SB)
        + _halo_specs(tr, L, SSM_GN, c_col) + [
            pl.BlockSpec((tr, LANES), lambda i: (i, O_SSM_DT // LANES)),
            pl.BlockSpec((SUBLANES, wd), lambda i: (0, 0)),
            pl.BlockSpec((1, wd), lambda i: (0, 0)),
            pl.BlockSpec((1, LANES), lambda i: (0, 0)),
            pl.BlockSpec((1, LANES), lambda i: (0, 0))],
        out_specs=[row(SSM_GN), row(SSM_GN), row(SSM_W), row(LANES)],
        out_shape=[jax.ShapeDtypeStruct((L, SSM_GN), MXU_DTYPE),
                   jax.ShapeDtypeStruct((L, SSM_GN), MXU_DTYPE),
                   jax.ShapeDtypeStruct((L, SSM_W), MXU_DTYPE),
                   jax.ShapeDtypeStruct((L, LANES), _f32)],
        compiler_params=_cparams("parallel"),
        name="prep_ssd",
    )(*([pb] * 9), pa, w, conv_b.astype(_f32)[None], lanes(dt_bias), lanes(a_log))


def _scan_ssd_kernel(ci_ref, bi_ref, xi_ref, pi_ref, cj_ref, bj_ref, xj_ref, pj_ref, dskip_ref,
                     s0f_ref, s0b_ref, ya_ref, yb_ref, finf_ref, finb_ref, sf, sb):
    i = pl.program_id(0)
    c = CHUNK
    H = SSM_HEADS

    @pl.when(i == 0)
    def _():
        sf[...] = s0f_ref[...]
        sb[...] = s0b_ref[...]

    tn = (((0,), (0,)), ((), ()))
    nt = (((1,), (1,)), ((), ()))
    ii = lax.broadcasted_iota(jnp.int32, (c, c), 0)
    jj = lax.broadcasted_iota(jnp.int32, (c, c), 1)
    low = lax.broadcasted_iota(jnp.int32, (c, LANES), 1) < SSM_HEADDIM
    low2 = lax.broadcasted_iota(jnp.int32, (2 * SSM_STATE, LANES), 1) < SSM_HEADDIM
    diag = (lax.broadcasted_iota(jnp.int32, (2 * SSM_STATE, LANES), 0) < SSM_STATE) == low2
    pi = pi_ref[...]
    pit = pi.T
    pj = pj_ref[...]
    ei = jnp.exp(jnp.minimum(pi, 0.0))
    ej = jnp.exp(jnp.minimum(pj, 0.0))
    tot_i, tot_j = pi[c - 1:c, :], pj[0:1, :]
    wi = jnp.exp(jnp.minimum(tot_i - pi, 0.0)) * pltpu.roll(pi, LANES - 2 * H, axis=1)
    wj = jnp.exp(jnp.minimum(tot_j - pj, 0.0)) * pltpu.roll(pj, LANES - 2 * H, axis=1)
    eti, etj = jnp.exp(jnp.minimum(tot_i, 0.0)), jnp.exp(jnp.minimum(tot_j, 0.0))
    colb = lambda t, k: jnp.broadcast_to(t[:, k:k + 1], (c, LANES))
    for g in range(SSM_GROUPS):
        gs = slice(g * SSM_STATE, (g + 1) * SSM_STATE)
        ci, bi = ci_ref[:, gs], bi_ref[:, gs]
        cj, bj = cj_ref[:, gs], bj_ref[:, gs]
        cb = lax.dot_general(ci, bi, nt, preferred_element_type=_f32)
        ci32, bi32, cj32, bj32 = (t.astype(_f32) for t in (ci, bi, cj, bj))
        for pp in range(SSM_HPG // 2):
            q = g * (SSM_HPG // 2) + pp
            heads = (2 * q, 2 * q + 1)
            xs = slice(q * LANES, (q + 1) * LANES)
            x = xi_ref[:, xs]
            x32 = x.astype(_f32)
            scores, cw, bw = [], [], []
            for h in heads:
                row = lambda o: pit[o + h:o + h + 1, :]
                mf = jnp.where(ii >= jj, jnp.exp(jnp.minimum(colb(pi, h) - row(0), 0.0)), 0.0) * row(2 * H)
                mb = jnp.where(jj >= ii, jnp.exp(jnp.minimum(colb(pi, H + h) - row(H), 0.0)), 0.0) * row(3 * H)
                scores.append((cb * (mf + mb)).astype(MXU_DTYPE))
                cw.append((ci32 * colb(ei, h)).astype(MXU_DTYPE))
                bw.append((bi32 * colb(wi, h)).astype(MXU_DTYPE))
            xa = jnp.where(low, x32, 0.0).astype(MXU_DTYPE)
            xb = jnp.where(low, 0.0, x32).astype(MXU_DTYPE)
            lhs = jnp.concatenate(scores + cw, axis=1)
            rhs = jnp.concatenate([xa, xb, sf[q].astype(MXU_DTYPE)], axis=0)
            y = jnp.dot(lhs, rhs, preferred_element_type=_f32)
            ya_ref[:, xs] = y + dskip_ref[:, xs] * x32
            upd = lax.dot_general(jnp.concatenate(bw, axis=1), x, tn, preferred_element_type=_f32)
            dec = jnp.where(low2, eti[0:1, heads[0]:heads[0] + 1], eti[0:1, heads[1]:heads[1] + 1])
            sf[q] = dec * sf[q] + jnp.where(diag, upd, 0.0)
            x = xj_ref[:, xs]
            cw = [(cj32 * colb(ej, H + h)).astype(MXU_DTYPE) for h in heads]
            bw = [(bj32 * colb(wj, H + h)).astype(MXU_DTYPE) for h in heads]
            yb_ref[:, xs] = jnp.dot(jnp.concatenate(cw, axis=1), sb[q].astype(MXU_DTYPE),
                                    preferred_element_type=_f32)
            upd = lax.dot_general(jnp.concatenate(bw, axis=1), x, tn, preferred_element_type=_f32)
            dec = jnp.where(low2, etj[0:1, H + heads[0]:H + heads[0] + 1], etj[0:1, H + heads[1]:H + heads[1] + 1])
            sb[q] = dec * sb[q] + jnp.where(diag, upd, 0.0)

    @pl.when(i == pl.num_programs(0) - 1)
    def _():
        finf_ref[...] = sf[...]
        finb_ref[...] = sb[...]


def _pair_states(s):
    s = s.reshape(SSM_HEADS // 2, 2, SSM_STATE, SSM_HEADDIM)
    z = jnp.zeros_like(s[:, 0])
    return jnp.concatenate([jnp.concatenate([s[:, 0], z], -1), jnp.concatenate([z, s[:, 1]], -1)], 1)


def _unpair_states(s):
    top, bot = s[:, :SSM_STATE, :SSM_HEADDIM], s[:, SSM_STATE:, SSM_HEADDIM:]
    return jnp.stack([top, bot], 1).reshape(SSM_HEADS, SSM_STATE, SSM_HEADDIM)


def _scan_ssd(cs, bs, xs, pack, d_skip, s0f, s0b):
    L = xs.shape[0]
    nc = L // CHUNK
    dvec = jnp.repeat(d_skip.astype(_f32), SSM_HEADDIM)[None]
    fw = lambda w: pl.BlockSpec((CHUNK, w), lambda i: (i, 0))
    bw = lambda w: pl.BlockSpec((CHUNK, w), lambda i: (nc - 1 - i, 0))
    pshape = (SSM_HEADS // 2, 2 * SSM_STATE, 2 * SSM_HEADDIM)
    st = pl.BlockSpec(pshape, lambda i: (0, 0, 0))
    yshape = jax.ShapeDtypeStruct((L, SSM_W), _f32)
    sshape = jax.ShapeDtypeStruct(pshape, _f32)
    ya, yb, fin_f, fin_b = pl.pallas_call(
        _scan_ssd_kernel,
        grid=(nc,),
        in_specs=[fw(SSM_GN), fw(SSM_GN), fw(SSM_W), fw(LANES), bw(SSM_GN), bw(SSM_GN), bw(SSM_W), bw(LANES),
                  pl.BlockSpec((1, SSM_W), lambda i: (0, 0)), st, st],
        out_specs=[fw(SSM_W), bw(SSM_W), st, st],
        out_shape=[yshape, yshape, sshape, sshape],
        scratch_shapes=[pltpu.VMEM(pshape, _f32)] * 2,
        compiler_params=_cparams("arbitrary"),
        name="scan_ssd",
    )(cs, bs, xs, pack, cs, bs, xs, pack, dvec, _pair_states(s0f), _pair_states(s0b))
    return ya, yb, _unpair_states(fin_f), _unpair_states(fin_b)


def _merge_kernel(ra_ref, rb_ref, sa_ref, sb_ref, gr_ref, gs_ref, nw_ref, yr_ref, ys_ref):
    for h in range(RET_HEADS):
        sl = slice(h * RET_DH, (h + 1) * RET_DH)
        y = _ln_rows(ra_ref[:, sl] + rb_ref[:, sl])
        yr_ref[:, sl] = (y * _silu(gr_ref[:, sl].astype(_f32))).astype(yr_ref.dtype)
    gw = SSM_W // SSM_GROUPS
    for g in range(SSM_GROUPS):
        sl = slice(g * gw, (g + 1) * gw)
        y = (sa_ref[:, sl] + sb_ref[:, sl]) * _silu(gs_ref[:, sl].astype(_f32))
        y = y * lax.rsqrt(jnp.mean(y * y, -1, keepdims=True) + LN_EPS)
        ys_ref[:, sl] = (y * nw_ref[:, sl]).astype(ys_ref.dtype)


def _gate_spec(tr, n):
    return pl.BlockSpec((tr, BR_W), lambda i: (i, B_GATE // BR_W + n))


def _merge(ra, rb, sa, sb_, pb, norm_w):
    L = ra.shape[0]
    tr = min(512, L)
    row = pl.BlockSpec((tr, BR_W), lambda i: (i, 0))
    shp = jax.ShapeDtypeStruct((L, BR_W), MXU_DTYPE)
    return pl.pallas_call(
        _merge_kernel,
        grid=(L // tr,),
        in_specs=[row, row, row, row, _gate_spec(tr, 1), _gate_spec(tr, 3),
                  pl.BlockSpec((1, BR_W), lambda i: (0, 0))],
        out_specs=[row, row],
        out_shape=[shp, shp],
        compiler_params=_cparams("parallel"),
        name="merge",
    )(ra, rb, sa, sb_, pb, pb, norm_w.astype(_f32)[None])


def _pool_kernel(x_ref, prev_ref, next_ref, g_ref, pw_ref, ps_ref, o_ref, *, L):
    i = pl.program_id(0)
    t = x_ref.shape[0]
    halo = HALO
    has_prev = (i > 0).astype(_f32)
    has_next = (i < pl.num_programs(0) - 1).astype(_f32)
    pos = i * t + lax.broadcasted_iota(jnp.int32, (t, 1), 0)
    for g, win in enumerate(POOL_WINDOWS):
        sl = slice(g * POOL_GROUP, (g + 1) * POOL_GROUP)
        x = x_ref[:, sl].astype(_f32)
        s = jnp.concatenate([prev_ref[:, sl].astype(_f32) * has_prev, x,
                             next_ref[:, sl].astype(_f32) * has_next], axis=0)
        rows = t + 2 * halo
        width = 1
        while width < win:
            s = s + pltpu.roll(s, rows - width, axis=0)
            width *= 2
        off = halo - win // 2
        if off:
            s = pltpu.roll(s, rows - off, axis=0)
        cnt = jnp.minimum(pos + win // 2, L) - jnp.maximum(pos - win // 2, 0)
        d = s[:t] / cnt.astype(_f32) - x
        y = jnp.dot(d.astype(MXU_DTYPE), pw_ref[g], preferred_element_type=_f32)
        o_ref[:, sl] = (y * ps_ref[:, sl] * _silu(g_ref[:, sl].astype(_f32))).astype(o_ref.dtype)


def _pool(pb, L, pool_w, pool_scale):
    tr = 256
    return pl.pallas_call(
        functools.partial(_pool_kernel, L=L),
        grid=(L // tr,),
        in_specs=_halo_specs(tr, L, POOL_W, B_POOL) + [
            _gate_spec(tr, 2),
            pl.BlockSpec((POOL_GROUPS, POOL_GROUP, POOL_GROUP), lambda i: (0, 0, 0)),
            pl.BlockSpec((1, POOL_W), lambda i: (0, 0))],
        out_specs=pl.BlockSpec((tr, POOL_W), lambda i: (i, 0)),
        out_shape=jax.ShapeDtypeStruct((L, POOL_W), MXU_DTYPE),
        compiler_params=_cparams("parallel"),
        name="pool",
    )(pb, pb, pb, pb, pool_w.astype(MXU_DTYPE), pool_scale.astype(_f32)[None])


def _prep_hy_kernel(v_ref, vp_ref, vn_ref, x0_ref, x0p_ref, x0n_ref, x1_ref, x1p_ref, x1n_ref,
                    g_ref, w_ref, b_ref, wo_ref, x0g_ref):
    i = pl.program_id(0)
    has_prev = (i > 0).astype(_f32)
    has_next = (i < pl.num_programs(0) - 1).astype(_f32)

    def conv(n, x_ref, p_ref, n_ref):
        sl = slice(n * HY_W, (n + 1) * HY_W)
        return _conv3(x_ref, p_ref, n_ref, w_ref.at[:, sl], b_ref.at[:, sl], has_prev, has_next)

    hv = conv(0, v_ref, vp_ref, vn_ref)
    hx0 = conv(1, x0_ref, x0p_ref, x0n_ref)
    hx1 = conv(2, x1_ref, x1p_ref, x1n_ref)
    wo_ref[...] = hx1 * hv
    x0g_ref[...] = hx0 * _silu(g_ref[...].astype(_f32))


def _prep_hy(pb, L, conv_w, conv_b):
    tr = 256
    w = jnp.pad(conv_w.astype(_f32), ((0, SUBLANES - 3), (0, 0)))
    row = pl.BlockSpec((tr, HY_W), lambda i: (i, 0))
    shp = jax.ShapeDtypeStruct((L, HY_W), _f32)
    secs = sum((_halo_specs(tr, L, HY_W, B_HY + n * HY_W) for n in range(3)), [])
    return pl.pallas_call(
        _prep_hy_kernel,
        grid=(L // tr,),
        in_specs=secs + [_gate_spec(tr, 0),
                         pl.BlockSpec((SUBLANES, 3 * HY_W), lambda i: (0, 0)),
                         pl.BlockSpec((1, 3 * HY_W), lambda i: (0, 0))],
        out_specs=[row, row],
        out_shape=[shp, shp],
        compiler_params=_cparams("parallel"),
        name="prep_hy",
    )(*([pb] * 10), w, conv_b.astype(_f32)[None])


def _split(x):
    hi = x.astype(MXU_DTYPE)
    return hi, (x - hi.astype(_f32)).astype(MXU_DTYPE)


def _dot3(a_hi, a_lo, b):
    b_hi, b_lo = _split(b)
    d = lambda p, q: jnp.dot(p, q, preferred_element_type=_f32)
    return d(a_hi, b_hi) + (d(a_hi, b_lo) + d(a_lo, b_hi))


def _dot2(a_hi, a_lo, b):
    b = b.astype(MXU_DTYPE)
    d = lambda p, q: jnp.dot(p, q, preferred_element_type=_f32)
    return d(a_hi, b) + d(a_lo, b)


def _dot2_stacked(a2, b):
    m = a2.shape[0] // 2
    r = jnp.dot(a2, b.astype(MXU_DTYPE), preferred_element_type=_f32)
    return r[:m] + r[m:]


def _const_split(m):
    return _split(jnp.asarray(m, _f32))


def _filter_kernel(z_ref, w1_ref, b1_ref, w2_ref, b2_ref, w3hi_ref, w3lo_ref, freq_ref, delta_ref, o_ref, *, L):
    t = z_ref.shape[1]
    dot = functools.partial(jnp.dot, precision=HIGHEST, preferred_element_type=_f32)
    freq = freq_ref[...]
    hdn = jnp.sin(freq * (dot(w1_ref[...], z_ref[...]) + b1_ref[...]))
    hdn = jnp.sin(freq * (dot(w2_ref[...], hdn) + b2_ref[...]))
    h_hi, h_lo = _split(hdn)
    d = lambda p, q: lax.dot_general(p, q, (((0,), (0,)), ((), ())), preferred_element_type=_f32)
    filt = d(h_hi, w3hi_ref[...]) + (d(h_hi, w3lo_ref[...]) + d(h_lo, w3hi_ref[...]))
    n = pl.program_id(0) * t + lax.broadcasted_iota(jnp.int32, (t, 1), 0)
    lag = jnp.minimum(jnp.where(n < L, n, 2 * L - n), L - 1).astype(_f32)
    o_ref[...] = jnp.where(n == L, 0.0, filt) * jnp.exp(-(lag / (L - 1)) * delta_ref[...])


def _hy_filter(L, lp):
    n = jnp.arange(2 * L)
    lag = jnp.minimum(jnp.where(n < L, n, 2 * L - n), L - 1).astype(_f32)[:, None]
    t = lag / (L - 1)
    w = 2.0 * math.pi * lag / L
    bands = jnp.linspace(1e-4, HY_BANDS - 1, HY_BANDS, dtype=_f32)[None, :]
    z = jnp.concatenate([t, jnp.cos(bands * w), -jnp.sin(bands * w)], axis=-1)
    emb = z.shape[1]
    zt = jnp.pad(z, ((0, 0), (0, LANES - emb))).T
    w1t = jnp.pad(lp['hy_w1'].astype(_f32), ((0, LANES - emb), (0, 0))).T
    deltas = jnp.abs(jnp.linspace(HY_MIN_DECAY, HY_MAX_DECAY, HY_W, dtype=_f32))[None]
    tr = min(512, L)
    w3hi, w3lo = _split(lp['hy_w3'].astype(_f32))
    full = lambda a: pl.BlockSpec(a.shape, lambda i: (0,) * a.ndim)
    half = pl.BlockSpec((w3hi.shape[0], HY_W), lambda i: (0, i // (L // tr)))
    colv = lambda v: v.astype(_f32)[:, None]
    pre = [w1t, colv(lp['hy_b1']), lp['hy_w2'].astype(_f32).T, colv(lp['hy_b2'])]
    post = [colv(lp['hy_freq']), deltas]
    return pl.pallas_call(
        functools.partial(_filter_kernel, L=L),
        grid=(2 * L // tr,),
        in_specs=[pl.BlockSpec((LANES, tr), lambda i: (0, i))] + [full(a) for a in pre] + [half, half]
        + [full(a) for a in post],
        out_specs=pl.BlockSpec((tr, HY_W), lambda i: (i, 0)),
        out_shape=jax.ShapeDtypeStruct((2 * L, HY_W), _f32),
        compiler_params=_cparams("parallel"),
        name="hy_filter",
    )(zt, *pre, w3hi, w3lo, *post)


def _cs(num, den):
    ang = 2.0 * np.pi * (np.asarray(num, np.int64) % den) / den
    return np.cos(ang), np.sin(ang)


FFT_N2 = LANES


def _fft_rows(n1):
    return -(-(n1 // 2 + 1) // SUBLANES) * SUBLANES


FFT_GROUP = 16
FFT_MID_GROUP = 8


def _fft_first_kernel(x_ref, m2_ref, o_ref):
    n2 = FFT_N2
    rows, kb = x_ref.shape[0] // n2, o_ref.shape[1]
    for g0 in range(0, n2, FFT_GROUP):
        x = jnp.concatenate([x_ref[pl.ds(g0 + g, rows, stride=n2), :] for g in range(FFT_GROUP)], axis=1)
        y = _dot2_stacked(m2_ref[...], x)
        for g in range(FFT_GROUP):
            tile = y[:, g * LANES:(g + 1) * LANES].reshape(2, kb, SUBLANES, LANES)
            o_ref[:, :, 0, (g0 + g) * SUBLANES:(g0 + g + 1) * SUBLANES, :] = tile


def _fft_first(x, n1):
    n2 = FFT_N2
    rows, ch = x.shape[0] // n2, x.shape[1]
    kp = _fft_rows(n1)
    c, s = _cs(np.outer(np.arange(kp), np.arange(rows)), n1)
    m2 = jnp.concatenate(_const_split(np.concatenate([c, -s], 0)), axis=0)
    kb, ct = kp // SUBLANES, ch // LANES
    return pl.pallas_call(
        _fft_first_kernel,
        grid=(ct,),
        in_specs=[pl.BlockSpec((rows * n2, LANES), lambda j: (0, j)),
                  pl.BlockSpec((4 * kp, rows), lambda j: (0, 0))],
        out_specs=pl.BlockSpec((2, kb, 1, n2 * SUBLANES, LANES), lambda j: (0, 0, j, 0, 0)),
        out_shape=jax.ShapeDtypeStruct((2, kb, ct, n2 * SUBLANES, LANES), _f32),
        compiler_params=_cparams("parallel"),
        name="fft_first",
    )(x, m2)


def _fft_mid_kernel(a_ref, f_ref, twr_ref, twi_ref, fhi_ref, flo_ref, ghi_ref, glo_ref, o_ref):
    n2 = FFT_N2
    for s0 in range(0, SUBLANES, FFT_MID_GROUP):
        group = range(s0, s0 + FFT_MID_GROUP)
        ts, tws = [], []
        for s in group:
            rows = pl.ds(s, n2, stride=SUBLANES)
            twr, twi = twr_ref[s], twi_ref[s]
            tws.append((twr, twi))
            for t_ref in (a_ref, f_ref):
                tr, ti = t_ref[0, 0, 0, rows, :], t_ref[1, 0, 0, rows, :]
                ts.append(jnp.concatenate([tr * twr - ti * twi, tr * twi + ti * twr], axis=0))
        y = _dot2(fhi_ref[...], flo_ref[...], jnp.concatenate(ts, axis=1))
        ps = []
        for p in range(FFT_MID_GROUP):
            x, h = y[:, 2 * p * LANES:(2 * p + 1) * LANES], y[:, (2 * p + 1) * LANES:(2 * p + 2) * LANES]
            xr, xi, hr, hi = x[:n2], x[n2:], h[:n2], h[n2:]
            ps.append(jnp.concatenate([xr * hr - xi * hi, xr * hi + xi * hr], axis=0))
        b = _dot2(ghi_ref[...], glo_ref[...], jnp.concatenate(ps, axis=1))
        for p, s in enumerate(group):
            rows = pl.ds(s, n2, stride=SUBLANES)
            twr, twi = tws[p]
            br, bi = b[:n2, p * LANES:(p + 1) * LANES], b[n2:, p * LANES:(p + 1) * LANES]
            o_ref[0, 0, 0, rows, :] = br * twr + bi * twi
            o_ref[1, 0, 0, rows, :] = bi * twr - br * twi


def _fft_mid(a, f, n1):
    n2 = FFT_N2
    _, kb, ct, rows, _ = a.shape
    kp = kb * SUBLANES
    n = n1 * n2
    idx = jnp.arange(kp)[:, None] * jnp.arange(n2)[None, :]
    ang = (2.0 * math.pi / n) * (idx % n).astype(_f32)
    twr = jnp.broadcast_to(jnp.cos(ang)[:, :, None], (kp, n2, LANES))
    twi = jnp.broadcast_to(-jnp.sin(ang)[:, :, None], (kp, n2, LANES))
    c, s = _cs(np.outer(np.arange(n2), np.arange(n2)), n2)
    fhi, flo = _const_split(np.block([[c, s], [-s, c]]))
    ghi, glo = _const_split(np.block([[c, -s], [s, c]]))
    blk = pl.BlockSpec((2, 1, 1, rows, LANES), lambda k, j: (0, k, j, 0, 0))
    tw = pl.BlockSpec((SUBLANES, n2, LANES), lambda k, j: (k, 0, 0))
    mat = pl.BlockSpec((2 * n2, 2 * n2), lambda k, j: (0, 0))
    return pl.pallas_call(
        _fft_mid_kernel,
        grid=(kb, ct),
        in_specs=[blk, blk, tw, tw, mat, mat, mat, mat],
        out_specs=blk,
        out_shape=jax.ShapeDtypeStruct(a.shape, _f32),
        compiler_params=_cparams("parallel", "parallel"),
        name="fft_mid",
    )(a, f, twr, twi, fhi, flo, ghi, glo)


def _fft_last_kernel(c_ref, m2_ref, w_ref, x0g_ref, bias_ref, o_ref):
    n2 = FFT_N2
    rows, kb = w_ref.shape[0] // n2, c_ref.shape[1]
    for g0 in range(0, n2, FFT_GROUP):
        tiles = [c_ref[:, :, 0, (g0 + g) * SUBLANES:(g0 + g + 1) * SUBLANES, :].reshape(2 * kb * SUBLANES, LANES)
                 for g in range(FFT_GROUP)]
        y = _dot2_stacked(m2_ref[...], jnp.concatenate(tiles, axis=1))
        for g in range(FFT_GROUP):
            at = pl.ds(g0 + g, rows, stride=n2)
            o_ref[at, :] = x0g_ref[at, :] * (y[:, g * LANES:(g + 1) * LANES] + w_ref[at, :] * bias_ref[...])


def _fft_last(cc, n1, w, x0g, bias):
    n2 = FFT_N2
    rows, ch = w.shape[0] // n2, w.shape[1]
    _, kb, ct, _, _ = cc.shape
    kp = kb * SUBLANES
    n = n1 * n2
    c, s = _cs(np.outer(np.arange(rows), np.arange(kp)), n1)
    k1 = np.arange(kp)
    mult = np.where((k1 == 0) | (k1 == n1 // 2), 1.0, np.where(k1 < n1 // 2, 2.0, 0.0))
    m2 = jnp.concatenate(_const_split(np.concatenate([c * mult, -s * mult], 1) / n), axis=0)
    blk = pl.BlockSpec((rows * n2, LANES), lambda j: (0, j))
    return pl.pallas_call(
        _fft_last_kernel,
        grid=(ct,),
        in_specs=[pl.BlockSpec((2, kb, 1, n2 * SUBLANES, LANES), lambda j: (0, 0, j, 0, 0)),
                  pl.BlockSpec((2 * rows, 2 * kp), lambda j: (0, 0)),
                  blk, blk, pl.BlockSpec((1, LANES), lambda j: (0, j))],
        out_specs=blk,
        out_shape=jax.ShapeDtypeStruct((rows * n2, ch), _f32),
        compiler_params=_cparams("parallel"),
        name="fft_last",
    )(cc, m2, w, x0g, bias.astype(_f32)[None])


def _hy_small_kernel(w_ref, buf_ref, x0g_ref, bias_ref, fwhi_ref, fwlo_ref, fbhi_ref, fblo_ref,
                     ihi_ref, ilo_ref, o_ref):
    n = buf_ref.shape[0]
    w = w_ref[...]
    wf = _dot3(fwhi_ref[...], fwlo_ref[...], w)
    hf = _dot3(fbhi_ref[...], fblo_ref[...], buf_ref[...])
    wr, wi, hr, hi = wf[:n], wf[n:], hf[:n], hf[n:]
    y = _dot3(ihi_ref[...], ilo_ref[...], jnp.concatenate([wr * hr - wi * hi, wr * hi + wi * hr], axis=0))
    o_ref[...] = (x0g_ref[...] * (y + w * bias_ref[...])).astype(o_ref.dtype)


def _hy_conv_small(w, buf, x0g, bias):
    L, ch = w.shape
    n = 2 * L
    tc = 256
    c, s = _cs(np.outer(np.arange(n), np.arange(n)), n)
    fb = np.concatenate([c, -s], 0)
    mats = [*_const_split(fb[:, :L]), *_const_split(fb),
            *_const_split(np.concatenate([c[:L], -s[:L]], 1) / n)]
    col = lambda r: pl.BlockSpec((r, tc), lambda j: (0, j))
    return pl.pallas_call(
        _hy_small_kernel,
        grid=(ch // tc,),
        in_specs=[col(L), col(n), col(L), col(1)] + [pl.BlockSpec(m.shape, lambda j: (0, 0)) for m in mats],
        out_specs=col(L),
        out_shape=jax.ShapeDtypeStruct((L, ch), MXU_DTYPE),
        compiler_params=_cparams("parallel"),
        name="hy_conv_small",
    )(w, buf, x0g, bias.astype(_f32)[None], *mats)


def _hy_conv(w, buf, x0g, bias):
    L, ch = w.shape
    if L < 512:
        return _hy_conv_small(w, buf, x0g, bias)
    n1 = 2 * L // FFT_N2
    return _fft_last(_fft_mid(_fft_first(w, n1), _fft_first(buf, n1), n1), n1, w, x0g, bias)


def _zero_states():
    return (jnp.zeros((RET_HEADS, RET_DH, RET_DH), _f32), jnp.zeros((RET_HEADS, RET_DH, RET_DH), _f32),
            jnp.zeros((SSM_HEADS, SSM_STATE, SSM_HEADDIM), _f32),
            jnp.zeros((SSM_HEADS, SSM_STATE, SSM_HEADDIM), _f32))


def _recurrent(proj, L, lp, states, latent, states_only=False):
    pa, pb = proj
    q, k, v = _prep_ret(pa, pb, L, latent, states_only)
    ra, rb, ret_f, ret_b = _scan_ret(q, k, v, lp['ret_decay_logit'], states[0], states[1])
    cs, bs, xs, pack = _prep_ssd(pa, pb, L, lp['conv_ssm_w'], lp['conv_ssm_b'], lp['ssm_dt_bias'],
                                 lp['ssm_A_log'], states_only)
    sa, sb_, ssm_f, ssm_b = _scan_ssd(cs, bs, xs, pack, lp['ssm_D'], states[2], states[3])
    return (ra, rb, sa, sb_), (ret_f, ret_b, ssm_f, ssm_b)


def _mix(h, mod, lp, states, latent):
    L = h.shape[0]
    proj = _in_proj(h, mod[0], mod[1], lp)
    pb = proj[1]
    (ra, rb, sa, sb_), fin = _recurrent(proj, L, lp, states, latent)
    y_ret, y_ssm = _merge(ra, rb, sa, sb_, pb, lp['ssm_norm_w'])
    w, x0g = _prep_hy(pb, L, lp['conv_hy_w'], lp['conv_hy_b'])
    y_hy = _hy_conv(w, _hy_filter(L, lp), x0g, lp['hy_bias'])
    y_pool = _pool(pb, L, lp['pool_w'], lp['pool_scale'])
    out = _out_proj([y_hy, y_ret, y_pool, y_ssm], lp['w_out'], h, mod[2], lp['ln_g'], lp['ln_b'])
    return out, fin


def _context_states(hc, mod, lp):
    proj = _in_proj(hc, mod[0], mod[1], lp, SSM_W + SSM_GN)
    _, fin = _recurrent(proj, hc.shape[0], lp, _zero_states(), False, states_only=True)
    return fin


def kernel(x, c, ctx, c_ctx, w_mod, b_mod, w_in, conv_ssm_w, conv_ssm_b, conv_hy_w, conv_hy_b,
           ret_decay_logit, ssm_A_log, ssm_dt_bias, ssm_D, ssm_norm_w, hy_w1, hy_b1, hy_w2, hy_b2,
           hy_w3, hy_freq, hy_bias, pool_w, pool_scale, w_out, ln_g, ln_b):
    assert x.shape[0] == 1
    h, hc = x[0], ctx[0]
    w_in_t = jnp.swapaxes(w_in, 1, 2)
    mods = _adaln(jnp.concatenate([c, c_ctx[None]], axis=0), w_mod, b_mod)
    for l in range(DEPTH):
        lp = {
            'layer': l, 'w_in_t': w_in_t, 'w_a_t': _cast_layer(w_in_t, l, N_A, N_A // 2, 512),
            'w_out': _cast_layer(w_out, l, w_out.shape[1], 1024, 2048)[0],
            'conv_ssm_w': conv_ssm_w[l], 'conv_ssm_b': conv_ssm_b[l],
            'conv_hy_w': conv_hy_w[l], 'conv_hy_b': conv_hy_b[l], 'ret_decay_logit': ret_decay_logit[l],
            'ssm_A_log': ssm_A_log[l], 'ssm_dt_bias': ssm_dt_bias[l], 'ssm_D': ssm_D[l],
            'ssm_norm_w': ssm_norm_w[l], 'hy_w1': hy_w1[l], 'hy_b1': hy_b1[l], 'hy_w2': hy_w2[l],
            'hy_b2': hy_b2[l], 'hy_w3': hy_w3[l], 'hy_freq': hy_freq[l], 'hy_bias': hy_bias[l],
            'pool_w': pool_w[l], 'pool_scale': pool_scale[l], 'ln_g': ln_g[l], 'ln_b': ln_b[l],
        }
        mod = lambda r: tuple(mods[l, r:r + 1, n * D_MODEL:(n + 1) * D_MODEL] for n in range(3))
        if l < DEPTH - 1:
            hc_next, states = _mix(hc, mod(1), lp, _zero_states(), False)
        else:
            states = _context_states(hc, mod(1), lp)
            hc_next = hc
        h, _ = _mix(h, mod(0), lp, states, True)
        hc = hc_next
    return h[None]
```

```python
import functools
import math

import jax
import jax.numpy as jnp
import numpy as np
from jax import lax
from jax.experimental import pallas as pl
from jax.experimental.pallas import tpu as pltpu

D_MODEL = 4096
DEPTH = 2
GRID_W = 64
MIX_W = D_MODEL
BR_W = MIX_W // 4
HY_W = RET_W = POOL_W = SSM_W = BR_W
RET_HEADS = 8
RET_DH = RET_W // RET_HEADS
ROPE_BASE = 10000.0
SSM_HEADDIM = 64
SSM_HEADS = SSM_W // SSM_HEADDIM
SSM_GROUPS = 4
SSM_HPG = SSM_HEADS // SSM_GROUPS
SSM_STATE = 128
SSM_GN = SSM_GROUPS * SSM_STATE
CHUNK = 128
POOL_WINDOWS = (2, 4, 8, 16)
POOL_GROUPS = len(POOL_WINDOWS)
POOL_GROUP = POOL_W // POOL_GROUPS
HY_BANDS = 16
HY_TARGET = 1e-2
HY_FAST = 0.3
HY_SLOW = 1.5
HY_MIN_DECAY = math.log(HY_TARGET) / HY_SLOW
HY_MAX_DECAY = math.log(HY_TARGET) / HY_FAST
ALPHA = (2.0 * DEPTH) ** 0.25
LN_EPS = 1e-5

O_RET_K = 0
O_RET_V = O_RET_K + RET_W
O_SSM_DT = O_RET_V + RET_W
O_SSM_X = O_SSM_DT + 2 * SSM_HEADS
O_SSM_B = O_SSM_X + SSM_W
O_RET_Q = O_SSM_B + SSM_GN
O_SSM_C = O_RET_Q + RET_W
O_HY = O_SSM_C + SSM_GN
O_POOL = O_HY + 3 * HY_W
O_GATE = O_POOL + POOL_W
N_IN = O_GATE + MIX_W

LANES = 128
SUBLANES = 8
N_A = O_SSM_DT + LANES
B_SX = 0
B_SB = O_SSM_B - O_SSM_X
B_RQ = O_RET_Q - O_SSM_X
B_SC = O_SSM_C - O_SSM_X
B_HY = O_HY - O_SSM_X
B_POOL = O_POOL - O_SSM_X
B_GATE = O_GATE - O_SSM_X
N_B = N_IN - O_SSM_X

VMEM_LIMIT_BYTES = 56 * 1024 * 1024
MXU_DTYPE = jnp.bfloat16
HIGHEST = lax.Precision.HIGHEST

_f32 = jnp.float32


def _cparams(*sem, vmem=VMEM_LIMIT_BYTES):
    return pltpu.CompilerParams(dimension_semantics=sem, vmem_limit_bytes=vmem)


def _silu(x):
    return x * jax.nn.sigmoid(x)


def _ln_rows(z):
    mu = jnp.mean(z, -1, keepdims=True)
    zc = z - mu
    var = jnp.mean(zc * zc, -1, keepdims=True)
    return zc * lax.rsqrt(var + LN_EPS)


def _cast_kernel(w_ref, o_ref):
    o_ref[...] = w_ref[...].astype(o_ref.dtype)


def _cast_layer(w, l, r, tr, tc):
    c = w.shape[2]
    assert r % tr == 0 and c % tc == 0
    return pl.pallas_call(
        _cast_kernel,
        grid=(r // tr, c // tc),
        in_specs=[pl.BlockSpec((1, tr, tc), lambda i, j: (l, i, j))],
        out_specs=pl.BlockSpec((1, tr, tc), lambda i, j: (0, i, j)),
        out_shape=jax.ShapeDtypeStruct((1, r, c), MXU_DTYPE),
        compiler_params=_cparams("parallel", "parallel"),
        name="cast_layer",
    )(w)


ADALN_ROWS = 32


def _adaln_kernel(c_ref, w_ref, b_ref, o_ref, xs_ref):
    @pl.when((pl.program_id(0) == 0) & (pl.program_id(1) == 0))
    def _():
        xs_ref[...] = _silu(c_ref[...])

    r, k = c_ref.shape[0], c_ref.shape[1]
    nj = w_ref.shape[-1] // LANES

    def body(t, accs):
        rows = pl.ds(pl.multiple_of(t * ADALN_ROWS, ADALN_ROWS), ADALN_ROWS)
        xs = [xs_ref[m, rows, :] for m in range(r)]
        ws = [w_ref[0, rows, j * LANES:(j + 1) * LANES] for j in range(nj)]
        return tuple(accs[m * nj + j] + xs[m] * ws[j] for m in range(r) for j in range(nj))

    accs = lax.fori_loop(0, k // ADALN_ROWS, body,
                         tuple(jnp.zeros((ADALN_ROWS, LANES), _f32) for _ in range(r * nj)), unroll=4)
    outs = [jnp.concatenate([jnp.sum(accs[m * nj + j], axis=0, keepdims=True) for j in range(nj)], axis=1)
            for m in range(r)]
    outs.append(jnp.zeros((SUBLANES - r, w_ref.shape[-1]), _f32))
    o_ref[0] = jnp.concatenate(outs, axis=0) + b_ref[0]


def _adaln(c_rows, w_mod, b_mod):
    dep, k, n = w_mod.shape
    r = c_rows.shape[0]
    tn = 512
    cb = jnp.broadcast_to(c_rows.astype(_f32)[:, :, None], (r, k, LANES))
    return pl.pallas_call(
        _adaln_kernel,
        grid=(dep, n // tn),
        in_specs=[pl.BlockSpec((r, k, LANES), lambda l, j: (0, 0, 0)),
                  pl.BlockSpec((1, k, tn), lambda l, j: (l, 0, j)),
                  pl.BlockSpec((1, 1, tn), lambda l, j: (l, 0, j))],
        out_specs=pl.BlockSpec((1, SUBLANES, tn), lambda l, j: (l, 0, j)),
        out_shape=jax.ShapeDtypeStruct((dep, SUBLANES, n), _f32),
        scratch_shapes=[pltpu.VMEM((r, k, LANES), _f32)],
        compiler_params=_cparams("arbitrary", "arbitrary"),
        name="adaln",
    )(cb, w_mod, b_mod[:, None, :])


def _matmul_nt_kernel(a_ref, b_ref, o_ref):
    o_ref[...] = lax.dot_general(a_ref[...], b_ref[0].astype(MXU_DTYPE), (((1,), (1,)), ((), ())),
                                 preferred_element_type=_f32).astype(o_ref.dtype)


def _matmul_nt(a, wt, l, row0, n, tm, tn, out_dtype):
    m, k = a.shape
    assert m % tm == 0 and n % tn == 0 and row0 % 32 == 0 and tn % 32 == 0
    return pl.pallas_call(
        _matmul_nt_kernel,
        grid=(m // tm, n // tn),
        in_specs=[pl.BlockSpec((tm, k), lambda i, j: (i, 0)),
                  pl.BlockSpec((pl.Element(1), pl.Element(tn), pl.Element(k)),
                               lambda i, j: (l, pl.multiple_of(row0 + j * tn, 32), 0))],
        out_specs=pl.BlockSpec((tm, tn), lambda i, j: (i, j)),
        out_shape=jax.ShapeDtypeStruct((m, n), out_dtype),
        compiler_params=_cparams("parallel", "parallel"),
        name="matmul_nt",
    )(a, wt)


def _modulate_kernel(h_ref, shift_ref, scale_ref, o_ref):
    o_ref[...] = (_ln_rows(h_ref[...]) * (1.0 + scale_ref[...]) + shift_ref[...]).astype(o_ref.dtype)


def _modulate(h, shift, scale):
    L, d = h.shape
    tr = min(512, L)
    vec = pl.BlockSpec((1, d), lambda i: (0, 0))
    return pl.pallas_call(
        _modulate_kernel,
        grid=(L // tr,),
        in_specs=[pl.BlockSpec((tr, d), lambda i: (i, 0)), vec, vec],
        out_specs=pl.BlockSpec((tr, d), lambda i: (i, 0)),
        out_shape=jax.ShapeDtypeStruct((L, d), MXU_DTYPE),
        compiler_params=_cparams("parallel"),
        name="modulate",
    )(h, shift, scale)


def _in_proj(h, shift, scale, lp, n_b=N_B):
    L = h.shape[0]
    u = _modulate(h, shift, scale)
    pa = _matmul_nt(u, lp['w_a_t'], 0, 0, N_A, 512 if L % 512 == 0 else 256, N_A, _f32)
    pb = _matmul_nt(u, lp['w_in_t'], lp['layer'], O_SSM_X, n_b, 1024 if L % 1024 == 0 else 256, 512, MXU_DTYPE)
    return pa, pb


def _out_proj_kernel(y0_ref, y1_ref, y2_ref, y3_ref, w_ref, h_ref, gate_ref, g_ref, b_ref, o_ref):
    out = None
    for n, y_ref in enumerate((y0_ref, y1_ref, y2_ref, y3_ref)):
        d = jnp.dot(y_ref[...].astype(MXU_DTYPE), w_ref[n * BR_W:(n + 1) * BR_W, :], preferred_element_type=_f32)
        out = d if out is None else out + d
    z = ALPHA * h_ref[...] + gate_ref[...] * out
    o_ref[...] = _ln_rows(z) * g_ref[...] + b_ref[...]


OUT_PROJ_VMEM_BYTES = 60 * 1024 * 1024


def _out_proj(ys, w, h, gate, g, b):
    L, d = h.shape
    tm = 256
    lhs = pl.BlockSpec((tm, BR_W), lambda i: (i, 0))
    vec = pl.BlockSpec((1, d), lambda i: (0, 0))
    row = pl.BlockSpec((tm, d), lambda i: (i, 0))
    return pl.pallas_call(
        _out_proj_kernel,
        grid=(L // tm,),
        in_specs=[lhs] * len(ys) + [
            pl.BlockSpec(w.shape, lambda i: (0, 0), pipeline_mode=pl.Buffered(1)), row, vec, vec, vec],
        out_specs=row,
        out_shape=jax.ShapeDtypeStruct((L, d), _f32),
        compiler_params=_cparams("parallel", vmem=OUT_PROJ_VMEM_BYTES),
        name="out_proj",
    )(*ys, w, h, gate, g[None], b[None])


def _rope_tables(L):
    rows = L // GRID_W
    row = jnp.repeat(jnp.arange(rows), GRID_W).astype(_f32)
    col = jnp.tile(jnp.arange(GRID_W), rows).astype(_f32)
    nq = RET_DH // 4
    inv = ROPE_BASE ** (-jnp.arange(nq, dtype=_f32) / nq)
    ang = jnp.concatenate([row[:, None] * inv, col[:, None] * inv], -1)
    cos, sin = jnp.cos(ang), jnp.sin(ang)
    return jnp.concatenate([cos, cos], -1), jnp.concatenate([-sin, sin], -1)


def _prep_ret_kernel(qlo_ref, qhi_ref, k_ref, v_ref, cos_ref, sin_ref, qo_ref, ko_ref, vo_ref, *, rope):
    def rot(t):
        if not rope:
            return t
        return t * cos_ref[...] + pltpu.roll(t, RET_DH // 2, axis=1) * sin_ref[...]

    half = RET_HEADS // 2
    for h in range(RET_HEADS):
        sl = slice(h * RET_DH, (h + 1) * RET_DH)
        q_ref, qs = (qlo_ref, sl) if h < half else (qhi_ref, slice((h - half) * RET_DH, (h - half + 1) * RET_DH))
        qo_ref[:, sl] = rot(q_ref[:, qs].astype(_f32)).astype(qo_ref.dtype)
        ko_ref[:, sl] = rot(k_ref[:, sl] * (RET_DH ** -0.5)).astype(ko_ref.dtype)
    vo_ref[...] = v_ref[...].astype(vo_ref.dtype)


def _prep_ret(pa, pb, L, rope, states_only=False):
    tr = min(512, L)
    cos, sin = _rope_tables(L) if rope else (jnp.ones((L, LANES), _f32), jnp.zeros((L, LANES), _f32))
    hw = RET_W // 2
    q_src, q_col = (pa, O_RET_K) if states_only else (pb, B_RQ)
    qsp = lambda n: pl.BlockSpec((tr, hw), lambda i: (i, q_col // hw + n))
    sec = lambda c: pl.BlockSpec((tr, RET_W), lambda i: (i, c // RET_W))
    tab = pl.BlockSpec((tr, LANES), lambda i: (i, 0))
    out = pl.BlockSpec((tr, RET_W), lambda i: (i, 0))
    shp = jax.ShapeDtypeStruct((L, RET_W), MXU_DTYPE)
    return pl.pallas_call(
        functools.partial(_prep_ret_kernel, rope=rope),
        grid=(L // tr,),
        in_specs=[qsp(0), qsp(1), sec(O_RET_K), sec(O_RET_V), tab, tab],
        out_specs=[out, out, out],
        out_shape=[shp, shp, shp],
        compiler_params=_cparams("parallel"),
        name="prep_ret",
    )(q_src, q_src, pa, pa, cos, sin)


def _scan_ret_kernel(logit_ref, qi_ref, ki_ref, vi_ref, qj_ref, kj_ref, vj_ref, s0f_ref, s0b_ref,
                     ya_ref, yb_ref, finf_ref, finb_ref,
                     sf, sb, dmask, f_out, f_upd, f_all, b_out, b_upd, b_all):
    i = pl.program_id(0)
    c = CHUNK

    @pl.when(i == 0)
    def _():
        sf[...] = s0f_ref[...]
        sb[...] = s0b_ref[...]
        ii = lax.broadcasted_iota(jnp.int32, (c, c), 0).astype(_f32)
        jj = lax.broadcasted_iota(jnp.int32, (c, c), 1).astype(_f32)
        for h in range(RET_HEADS):
            def lg(d):
                x = logit_ref[d, h]
                v = -jnp.log1p(jnp.exp(-x))
                return jnp.broadcast_to(v[0:1, :], (c, c))
            lf, lb = lg(0), lg(1)
            dmask[h] = jnp.where(ii > jj, jnp.exp(lf * (ii - jj)),
                                 jnp.where(jj > ii, jnp.exp(lb * (jj - ii)), 2.0))
            f_out[h] = jnp.exp(lf * (ii + 1.0))
            f_upd[h] = jnp.exp(lf * (c - 1.0 - ii))
            f_all[h] = jnp.exp(lf * float(c))
            b_out[h] = jnp.exp(lb * (c - ii))
            b_upd[h] = jnp.exp(lb * ii)
            b_all[h] = jnp.exp(lb * float(c))

    tn = (((0,), (0,)), ((), ()))
    nt = (((1,), (1,)), ((), ()))
    heads = range(RET_HEADS)
    sls = [slice(h * RET_DH, (h + 1) * RET_DH) for h in heads]
    scores = [lax.dot_general(qi_ref[:, sl], ki_ref[:, sl], nt, preferred_element_type=_f32) for sl in sls]
    upd_f = [lax.dot_general((ki_ref[:, sl].astype(_f32) * f_upd[h]).astype(MXU_DTYPE), vi_ref[:, sl], tn,
                             preferred_element_type=_f32) for h, sl in zip(heads, sls)]
    upd_b = [lax.dot_general((kj_ref[:, sl].astype(_f32) * b_upd[h]).astype(MXU_DTYPE), vj_ref[:, sl], tn,
                             preferred_element_type=_f32) for h, sl in zip(heads, sls)]
    for h, sl in zip(heads, sls):
        lhs = jnp.concatenate([(scores[h] * dmask[h]).astype(MXU_DTYPE),
                               (qi_ref[:, sl].astype(_f32) * f_out[h]).astype(MXU_DTYPE)], axis=1)
        rhs = jnp.concatenate([vi_ref[:, sl], sf[h].astype(MXU_DTYPE)], axis=0)
        ya_ref[:, sl] = jnp.dot(lhs, rhs, preferred_element_type=_f32)
        yb_ref[:, sl] = jnp.dot((qj_ref[:, sl].astype(_f32) * b_out[h]).astype(MXU_DTYPE),
                                sb[h].astype(MXU_DTYPE), preferred_element_type=_f32)
    for h in heads:
        sf[h] = f_all[h] * sf[h] + upd_f[h]
        sb[h] = b_all[h] * sb[h] + upd_b[h]

    @pl.when(i == pl.num_programs(0) - 1)
    def _():
        finf_ref[...] = sf[...]
        finb_ref[...] = sb[...]


def _scan_ret(q, k, v, logit, s0f, s0b):
    L = q.shape[0]
    nc = L // CHUNK
    logit_b = jnp.broadcast_to(logit.astype(_f32)[:, :, None, None], (2, RET_HEADS, SUBLANES, LANES))
    fw = pl.BlockSpec((CHUNK, RET_W), lambda i: (i, 0))
    bw = pl.BlockSpec((CHUNK, RET_W), lambda i: (nc - 1 - i, 0))
    st = pl.BlockSpec((RET_HEADS, RET_DH, RET_DH), lambda i: (0, 0, 0))
    yshape = jax.ShapeDtypeStruct((L, RET_W), _f32)
    sshape = jax.ShapeDtypeStruct((RET_HEADS, RET_DH, RET_DH), _f32)
    tile = pltpu.VMEM((RET_HEADS, CHUNK, CHUNK), _f32)
    return pl.pallas_call(
        _scan_ret_kernel,
        grid=(nc,),
        in_specs=[pl.BlockSpec((2, RET_HEADS, SUBLANES, LANES), lambda i: (0, 0, 0, 0)),
                  fw, fw, fw, bw, bw, bw, st, st],
        out_specs=[fw, bw, st, st],
        out_shape=[yshape, yshape, sshape, sshape],
        scratch_shapes=[pltpu.VMEM((RET_HEADS, RET_DH, RET_DH), _f32)] * 2 + [tile] * 7,
        compiler_params=_cparams("arbitrary"),
        name="scan_ret",
    )(logit_b, q, k, v, q, k, v, s0f, s0b)


def _shift_rows(x, prev_row, next_row):
    r = x.shape[0]
    rid = lax.broadcasted_iota(jnp.int32, x.shape, 0)
    up = jnp.where(rid == 0, prev_row, pltpu.roll(x, 1, axis=0))
    dn = jnp.where(rid == r - 1, next_row, pltpu.roll(x, r - 1, axis=0))
    return up, dn


HALO = 16


def _conv3(x_ref, prev_ref, next_ref, w_ref, b_ref, has_prev, has_next):
    x = x_ref[...].astype(_f32)
    prev_row = prev_ref[...].astype(_f32)[HALO - 1:HALO, :] * has_prev
    next_row = next_ref[...].astype(_f32)[0:1, :] * has_next
    up, dn = _shift_rows(x, prev_row, next_row)
    return up * w_ref[0:1, :] + x * w_ref[1:2, :] + dn * w_ref[2:3, :] + b_ref[...]


def _halo_specs(tr, L, width, col):
    nb = tr // HALO
    last = L // HALO - 1
    cb = col // width
    return [pl.BlockSpec((tr, width), lambda i: (i, cb)),
            pl.BlockSpec((HALO, width), lambda i: (jnp.maximum(i * nb - 1, 0), cb)),
            pl.BlockSpec((HALO, width), lambda i: (jnp.minimum((i + 1) * nb, last), cb))]


def _prep_ssd_kernel(x_ref, xp_ref, xn_ref, b_ref, bp_ref, bn_ref, c_ref, cp_ref, cn_ref, dt_ref,
                     w_ref, cb_ref, dtb_ref, alog_ref, co_ref, bo_ref, xo_ref, pack_ref):
    i = pl.program_id(0)
    has_prev = (i > 0).astype(_f32)
    has_next = (i < pl.num_programs(0) - 1).astype(_f32)

    def conv(lo, hi, t_ref, p_ref, n_ref, o_ref):
        y = _conv3(t_ref, p_ref, n_ref, w_ref.at[:, lo:hi], cb_ref.at[:, lo:hi], has_prev, has_next)
        o_ref[...] = _silu(y).astype(o_ref.dtype)

    conv(0, SSM_W, x_ref, xp_ref, xn_ref, xo_ref)
    conv(SSM_W, SSM_W + SSM_GN, b_ref, bp_ref, bn_ref, bo_ref)
    conv(SSM_W + SSM_GN, SSM_W + 2 * SSM_GN, c_ref, cp_ref, cn_ref, co_ref)
    z = dt_ref[...] + dtb_ref[...]
    dt = jnp.maximum(z, 0.0) + jnp.log1p(jnp.exp(-jnp.abs(z)))
    a = dt * (-jnp.exp(alog_ref[...]))
    c = CHUNK
    ii = lax.broadcasted_iota(jnp.int32, (c, c), 0)
    jj = lax.broadcasted_iota(jnp.int32, (c, c), 1)
    lower = (jj <= ii).astype(_f32)
    upper = (jj >= ii).astype(_f32)
    lane = lax.broadcasted_iota(jnp.int32, (c, LANES), 1)
    dt_sh = pltpu.roll(dt, 2 * SSM_HEADS, axis=1)
    for n in range(x_ref.shape[0] // c):
        rs = slice(n * c, (n + 1) * c)
        pre = jnp.dot(lower, a[rs], precision=HIGHEST, preferred_element_type=_f32)
        suf = jnp.dot(upper, a[rs], precision=HIGHEST, preferred_element_type=_f32)
        pack_ref[rs, :] = jnp.where(lane < SSM_HEADS, pre,
                                    jnp.where(lane < 2 * SSM_HEADS, suf, dt_sh[rs]))


def _prep_ssd(pa, pb, L, conv_w, conv_b, dt_bias, a_log, states_only=False):
    tr = min(512, L)
    c_col = B_SB if states_only else B_SC
    w = jnp.pad(conv_w.astype(_f32), ((0, SUBLANES - 3), (0, 0)))
    lanes = lambda t: jnp.pad(t.astype(_f32).reshape(1, 2 * SSM_HEADS), ((0, 0), (0, LANES - 2 * SSM_HEADS)))
    wd = SSM_W + 2 * SSM_GN
    row = lambda c: pl.BlockSpec((tr, c), lambda i: (i, 0))
    return pl.pallas_call(
        _prep_ssd_kernel,
        grid=(L // tr,),
        in_specs=_halo_specs(tr, L, SSM_W, B_SX) + _halo_specs(tr, L, SSM_GN, B_SB)
        + _halo_specs(tr, L, SSM_GN, c_col) + [
            pl.BlockSpec((tr, LANES), lambda i: (i, O_SSM_DT // LANES)),
            pl.BlockSpec((SUBLANES, wd), lambda i: (0, 0)),
            pl.BlockSpec((1, wd), lambda i: (0, 0)),
            pl.BlockSpec((1, LANES), lambda i: (0, 0)),
            pl.BlockSpec((1, LANES), lambda i: (0, 0))],
        out_specs=[row(SSM_GN), row(SSM_GN), row(SSM_W), row(LANES)],
        out_shape=[jax.ShapeDtypeStruct((L, SSM_GN), MXU_DTYPE),
                   jax.ShapeDtypeStruct((L, SSM_GN), MXU_DTYPE),
                   jax.ShapeDtypeStruct((L, SSM_W), MXU_DTYPE),
                   jax.ShapeDtypeStruct((L, LANES), _f32)],
        compiler_params=_cparams("parallel"),
        name="prep_ssd",
    )(*([pb] * 9), pa, w, conv_b.astype(_f32)[None], lanes(dt_bias), lanes(a_log))


def _scan_ssd_kernel(ci_ref, bi_ref, xi_ref, pi_ref, cj_ref, bj_ref, xj_ref, pj_ref, dskip_ref,
                     s0f_ref, s0b_ref, ya_ref, yb_ref, finf_ref, finb_ref, sf, sb):
    i = pl.program_id(0)
    c = CHUNK
    H = SSM_HEADS

    @pl.when(i == 0)
    def _():
        sf[...] = s0f_ref[...]
        sb[...] = s0b_ref[...]

    tn = (((0,), (0,)), ((), ()))
    nt = (((1,), (1,)), ((), ()))
    ii = lax.broadcasted_iota(jnp.int32, (c, c), 0)
    jj = lax.broadcasted_iota(jnp.int32, (c, c), 1)
    low = lax.broadcasted_iota(jnp.int32, (c, LANES), 1) < SSM_HEADDIM
    low2 = lax.broadcasted_iota(jnp.int32, (2 * SSM_STATE, LANES), 1) < SSM_HEADDIM
    diag = (lax.broadcasted_iota(jnp.int32, (2 * SSM_STATE, LANES), 0) < SSM_STATE) == low2
    pi = pi_ref[...]
    pit = pi.T
    pj = pj_ref[...]
    ei = jnp.exp(jnp.minimum(pi, 0.0))
    ej = jnp.exp(jnp.minimum(pj, 0.0))
    tot_i, tot_j = pi[c - 1:c, :], pj[0:1, :]
    wi = jnp.exp(jnp.minimum(tot_i - pi, 0.0)) * pltpu.roll(pi, LANES - 2 * H, axis=1)
    wj = jnp.exp(jnp.minimum(tot_j - pj, 0.0)) * pltpu.roll(pj, LANES - 2 * H, axis=1)
    eti, etj = jnp.exp(jnp.minimum(tot_i, 0.0)), jnp.exp(jnp.minimum(tot_j, 0.0))
    colb = lambda t, k: jnp.broadcast_to(t[:, k:k + 1], (c, LANES))
    for g in range(SSM_GROUPS):
        gs = slice(g * SSM_STATE, (g + 1) * SSM_STATE)
        ci, bi = ci_ref[:, gs], bi_ref[:, gs]
        cj, bj = cj_ref[:, gs], bj_ref[:, gs]
        cb = lax.dot_general(ci, bi, nt, preferred_element_type=_f32)
        ci32, bi32, cj32, bj32 = (t.astype(_f32) for t in (ci, bi, cj, bj))
        for pp in range(SSM_HPG // 2):
            q = g * (SSM_HPG // 2) + pp
            heads = (2 * q, 2 * q + 1)
            xs = slice(q * LANES, (q + 1) * LANES)
            x = xi_ref[:, xs]
            x32 = x.astype(_f32)
            scores, cw, bw = [], [], []
            for h in heads:
                row = lambda o: pit[o + h:o + h + 1, :]
                mf = jnp.where(ii >= jj, jnp.exp(jnp.minimum(colb(pi, h) - row(0), 0.0)), 0.0) * row(2 * H)
                mb = jnp.where(jj >= ii, jnp.exp(jnp.minimum(colb(pi, H + h) - row(H), 0.0)), 0.0) * row(3 * H)
                scores.append((cb * (mf + mb)).astype(MXU_DTYPE))
                cw.append((ci32 * colb(ei, h)).astype(MXU_DTYPE))
                bw.append((bi32 * colb(wi, h)).astype(MXU_DTYPE))
            xa = jnp.where(low, x32, 0.0).astype(MXU_DTYPE)
            xb = jnp.where(low, 0.0, x32).astype(MXU_DTYPE)
            lhs = jnp.concatenate(scores + cw, axis=1)
            rhs = jnp.concatenate([xa, xb, sf[q].astype(MXU_DTYPE)], axis=0)
            y = jnp.dot(lhs, rhs, preferred_element_type=_f32)
            ya_ref[:, xs] = y + dskip_ref[:, xs] * x32
            upd = lax.dot_general(jnp.concatenate(bw, axis=1), x, tn, preferred_element_type=_f32)
            dec = jnp.where(low2, eti[0:1, heads[0]:heads[0] + 1], eti[0:1, heads[1]:heads[1] + 1])
            sf[q] = dec * sf[q] + jnp.where(diag, upd, 0.0)
            x = xj_ref[:, xs]
            cw = [(cj32 * colb(ej, H + h)).astype(MXU_DTYPE) for h in heads]
            bw = [(bj32 * colb(wj, H + h)).astype(MXU_DTYPE) for h in heads]
            yb_ref[:, xs] = jnp.dot(jnp.concatenate(cw, axis=1), sb[q].astype(MXU_DTYPE),
                                    preferred_element_type=_f32)
            upd = lax.dot_general(jnp.concatenate(bw, axis=1), x, tn, preferred_element_type=_f32)
            dec = jnp.where(low2, etj[0:1, H + heads[0]:H + heads[0] + 1], etj[0:1, H + heads[1]:H + heads[1] + 1])
            sb[q] = dec * sb[q] + jnp.where(diag, upd, 0.0)

    @pl.when(i == pl.num_programs(0) - 1)
    def _():
        finf_ref[...] = sf[...]
        finb_ref[...] = sb[...]


def _pair_states(s):
    s = s.reshape(SSM_HEADS // 2, 2, SSM_STATE, SSM_HEADDIM)
    z = jnp.zeros_like(s[:, 0])
    return jnp.concatenate([jnp.concatenate([s[:, 0], z], -1), jnp.concatenate([z, s[:, 1]], -1)], 1)


def _unpair_states(s):
    top, bot = s[:, :SSM_STATE, :SSM_HEADDIM], s[:, SSM_STATE:, SSM_HEADDIM:]
    return jnp.stack([top, bot], 1).reshape(SSM_HEADS, SSM_STATE, SSM_HEADDIM)


def _scan_ssd(cs, bs, xs, pack, d_skip, s0f, s0b):
    L = xs.shape[0]
    nc = L // CHUNK
    dvec = jnp.repeat(d_skip.astype(_f32), SSM_HEADDIM)[None]
    fw = lambda w: pl.BlockSpec((CHUNK, w), lambda i: (i, 0))
    bw = lambda w: pl.BlockSpec((CHUNK, w), lambda i: (nc - 1 - i, 0))
    pshape = (SSM_HEADS // 2, 2 * SSM_STATE, 2 * SSM_HEADDIM)
    st = pl.BlockSpec(pshape, lambda i: (0, 0, 0))
    yshape = jax.ShapeDtypeStruct((L, SSM_W), _f32)
    sshape = jax.ShapeDtypeStruct(pshape, _f32)
    ya, yb, fin_f, fin_b = pl.pallas_call(
        _scan_ssd_kernel,
        grid=(nc,),
        in_specs=[fw(SSM_GN), fw(SSM_GN), fw(SSM_W), fw(LANES), bw(SSM_GN), bw(SSM_GN), bw(SSM_W), bw(LANES),
                  pl.BlockSpec((1, SSM_W), lambda i: (0, 0)), st, st],
        out_specs=[fw(SSM_W), bw(SSM_W), st, st],
        out_shape=[yshape, yshape, sshape, sshape],
        scratch_shapes=[pltpu.VMEM(pshape, _f32)] * 2,
        compiler_params=_cparams("arbitrary"),
        name="scan_ssd",
    )(cs, bs, xs, pack, cs, bs, xs, pack, dvec, _pair_states(s0f), _pair_states(s0b))
    return ya, yb, _unpair_states(fin_f), _unpair_states(fin_b)


def _merge_kernel(ra_ref, rb_ref, sa_ref, sb_ref, gr_ref, gs_ref, nw_ref, yr_ref, ys_ref):
    for h in range(RET_HEADS):
        sl = slice(h * RET_DH, (h + 1) * RET_DH)
        y = _ln_rows(ra_ref[:, sl] + rb_ref[:, sl])
        yr_ref[:, sl] = (y * _silu(gr_ref[:, sl].astype(_f32))).astype(yr_ref.dtype)
    gw = SSM_W // SSM_GROUPS
    for g in range(SSM_GROUPS):
        sl = slice(g * gw, (g + 1) * gw)
        y = (sa_ref[:, sl] + sb_ref[:, sl]) * _silu(gs_ref[:, sl].astype(_f32))
        y = y * lax.rsqrt(jnp.mean(y * y, -1, keepdims=True) + LN_EPS)
        ys_ref[:, sl] = (y * nw_ref[:, sl]).astype(ys_ref.dtype)


def _gate_spec(tr, n):
    return pl.BlockSpec((tr, BR_W), lambda i: (i, B_GATE // BR_W + n))


def _merge(ra, rb, sa, sb_, pb, norm_w):
    L = ra.shape[0]
    tr = min(512, L)
    row = pl.BlockSpec((tr, BR_W), lambda i: (i, 0))
    shp = jax.ShapeDtypeStruct((L, BR_W), MXU_DTYPE)
    return pl.pallas_call(
        _merge_kernel,
        grid=(L // tr,),
        in_specs=[row, row, row, row, _gate_spec(tr, 1), _gate_spec(tr, 3),
                  pl.BlockSpec((1, BR_W), lambda i: (0, 0))],
        out_specs=[row, row],
        out_shape=[shp, shp],
        compiler_params=_cparams("parallel"),
        name="merge",
    )(ra, rb, sa, sb_, pb, pb, norm_w.astype(_f32)[None])


def _pool_kernel(x_ref, prev_ref, next_ref, g_ref, pw_ref, ps_ref, o_ref, *, L):
    i = pl.program_id(0)
    t = x_ref.shape[0]
    halo = HALO
    has_prev = (i > 0).astype(_f32)
    has_next = (i < pl.num_programs(0) - 1).astype(_f32)
    pos = i * t + lax.broadcasted_iota(jnp.int32, (t, 1), 0)
    for g, win in enumerate(POOL_WINDOWS):
        sl = slice(g * POOL_GROUP, (g + 1) * POOL_GROUP)
        x = x_ref[:, sl].astype(_f32)
        s = jnp.concatenate([prev_ref[:, sl].astype(_f32) * has_prev, x,
                             next_ref[:, sl].astype(_f32) * has_next], axis=0)
        rows = t + 2 * halo
        width = 1
        while width < win:
            s = s + pltpu.roll(s, rows - width, axis=0)
            width *= 2
        off = halo - win // 2
        if off:
            s = pltpu.roll(s, rows - off, axis=0)
        cnt = jnp.minimum(pos + win // 2, L) - jnp.maximum(pos - win // 2, 0)
        d = s[:t] / cnt.astype(_f32) - x
        y = jnp.dot(d.astype(MXU_DTYPE), pw_ref[g], preferred_element_type=_f32)
        o_ref[:, sl] = (y * ps_ref[:, sl] * _silu(g_ref[:, sl].astype(_f32))).astype(o_ref.dtype)


def _pool(pb, L, pool_w, pool_scale):
    tr = min(512, L)
    return pl.pallas_call(
        functools.partial(_pool_kernel, L=L),
        grid=(L // tr,),
        in_specs=_halo_specs(tr, L, POOL_W, B_POOL) + [
            _gate_spec(tr, 2),
            pl.BlockSpec((POOL_GROUPS, POOL_GROUP, POOL_GROUP), lambda i: (0, 0, 0)),
            pl.BlockSpec((1, POOL_W), lambda i: (0, 0))],
        out_specs=pl.BlockSpec((tr, POOL_W), lambda i: (i, 0)),
        out_shape=jax.ShapeDtypeStruct((L, POOL_W), MXU_DTYPE),
        compiler_params=_cparams("parallel"),
        name="pool",
    )(pb, pb, pb, pb, pool_w.astype(MXU_DTYPE), pool_scale.astype(_f32)[None])


def _prep_hy_kernel(v_ref, vp_ref, vn_ref, x0_ref, x0p_ref, x0n_ref, x1_ref, x1p_ref, x1n_ref,
                    g_ref, w_ref, b_ref, wo_ref, x0g_ref):
    i = pl.program_id(0)
    has_prev = (i > 0).astype(_f32)
    has_next = (i < pl.num_programs(0) - 1).astype(_f32)

    def conv(n, x_ref, p_ref, n_ref):
        sl = slice(n * HY_W, (n + 1) * HY_W)
        return _conv3(x_ref, p_ref, n_ref, w_ref.at[:, sl], b_ref.at[:, sl], has_prev, has_next)

    hv = conv(0, v_ref, vp_ref, vn_ref)
    hx0 = conv(1, x0_ref, x0p_ref, x0n_ref)
    hx1 = conv(2, x1_ref, x1p_ref, x1n_ref)
    wo_ref[...] = hx1 * hv
    x0g_ref[...] = hx0 * _silu(g_ref[...].astype(_f32))


def _prep_hy(pb, L, conv_w, conv_b):
    tr = min(512, L)
    w = jnp.pad(conv_w.astype(_f32), ((0, SUBLANES - 3), (0, 0)))
    row = pl.BlockSpec((tr, HY_W), lambda i: (i, 0))
    shp = jax.ShapeDtypeStruct((L, HY_W), _f32)
    secs = sum((_halo_specs(tr, L, HY_W, B_HY + n * HY_W) for n in range(3)), [])
    return pl.pallas_call(
        _prep_hy_kernel,
        grid=(L // tr,),
        in_specs=secs + [_gate_spec(tr, 0),
                         pl.BlockSpec((SUBLANES, 3 * HY_W), lambda i: (0, 0)),
                         pl.BlockSpec((1, 3 * HY_W), lambda i: (0, 0))],
        out_specs=[row, row],
        out_shape=[shp, shp],
        compiler_params=_cparams("parallel"),
        name="prep_hy",
    )(*([pb] * 10), w, conv_b.astype(_f32)[None])


def _split(x):
    hi = x.astype(MXU_DTYPE)
    return hi, (x - hi.astype(_f32)).astype(MXU_DTYPE)


def _dot3(a_hi, a_lo, b):
    b_hi, b_lo = _split(b)
    d = lambda p, q: jnp.dot(p, q, preferred_element_type=_f32)
    return d(a_hi, b_hi) + (d(a_hi, b_lo) + d(a_lo, b_hi))


def _dot2(a_hi, a_lo, b):
    b = b.astype(MXU_DTYPE)
    d = lambda p, q: jnp.dot(p, q, preferred_element_type=_f32)
    return d(a_hi, b) + d(a_lo, b)


def _dot2_stacked(a2, b):
    m = a2.shape[0] // 2
    r = jnp.dot(a2, b.astype(MXU_DTYPE), preferred_element_type=_f32)
    return r[:m] + r[m:]


def _const_split(m):
    return _split(jnp.asarray(m, _f32))


def _filter_kernel(z_ref, w1_ref, b1_ref, w2_ref, b2_ref, w3hi_ref, w3lo_ref, freq_ref, delta_ref, o_ref, *, L):
    t = z_ref.shape[1]
    dot = functools.partial(jnp.dot, precision=HIGHEST, preferred_element_type=_f32)
    freq = freq_ref[...]
    hdn = jnp.sin(freq * (dot(w1_ref[...], z_ref[...]) + b1_ref[...]))
    hdn = jnp.sin(freq * (dot(w2_ref[...], hdn) + b2_ref[...]))
    h_hi, h_lo = _split(hdn)
    d = lambda p, q: lax.dot_general(p, q, (((0,), (0,)), ((), ())), preferred_element_type=_f32)
    filt = d(h_hi, w3hi_ref[...]) + (d(h_hi, w3lo_ref[...]) + d(h_lo, w3hi_ref[...]))
    n = pl.program_id(0) * t + lax.broadcasted_iota(jnp.int32, (t, 1), 0)
    lag = jnp.minimum(jnp.where(n < L, n, 2 * L - n), L - 1).astype(_f32)
    o_ref[...] = jnp.where(n == L, 0.0, filt) * jnp.exp(-(lag / (L - 1)) * delta_ref[...])


def _hy_filter(L, lp):
    n = jnp.arange(2 * L)
    lag = jnp.minimum(jnp.where(n < L, n, 2 * L - n), L - 1).astype(_f32)[:, None]
    t = lag / (L - 1)
    w = 2.0 * math.pi * lag / L
    bands = jnp.linspace(1e-4, HY_BANDS - 1, HY_BANDS, dtype=_f32)[None, :]
    z = jnp.concatenate([t, jnp.cos(bands * w), -jnp.sin(bands * w)], axis=-1)
    emb = z.shape[1]
    zt = jnp.pad(z, ((0, 0), (0, LANES - emb))).T
    w1t = jnp.pad(lp['hy_w1'].astype(_f32), ((0, LANES - emb), (0, 0))).T
    deltas = jnp.abs(jnp.linspace(HY_MIN_DECAY, HY_MAX_DECAY, HY_W, dtype=_f32))[None]
    tr = min(512, L)
    w3hi, w3lo = _split(lp['hy_w3'].astype(_f32))
    full = lambda a: pl.BlockSpec(a.shape, lambda i: (0,) * a.ndim)
    half = pl.BlockSpec((w3hi.shape[0], HY_W), lambda i: (0, i // (L // tr)))
    colv = lambda v: v.astype(_f32)[:, None]
    pre = [w1t, colv(lp['hy_b1']), lp['hy_w2'].astype(_f32).T, colv(lp['hy_b2'])]
    post = [colv(lp['hy_freq']), deltas]
    return pl.pallas_call(
        functools.partial(_filter_kernel, L=L),
        grid=(2 * L // tr,),
        in_specs=[pl.BlockSpec((LANES, tr), lambda i: (0, i))] + [full(a) for a in pre] + [half, half]
        + [full(a) for a in post],
        out_specs=pl.BlockSpec((tr, HY_W), lambda i: (i, 0)),
        out_shape=jax.ShapeDtypeStruct((2 * L, HY_W), _f32),
        compiler_params=_cparams("parallel"),
        name="hy_filter",
    )(zt, *pre, w3hi, w3lo, *post)


def _cs(num, den):
    ang = 2.0 * np.pi * (np.asarray(num, np.int64) % den) / den
    return np.cos(ang), np.sin(ang)


FFT_N2 = LANES


def _fft_rows(n1):
    return -(-(n1 // 2 + 1) // SUBLANES) * SUBLANES


FFT_GROUP = 16
FFT_MID_GROUP = 8


def _fft_first_kernel(x_ref, m2_ref, o_ref):
    n2 = FFT_N2
    rows, kb = x_ref.shape[0] // n2, o_ref.shape[1]
    for g0 in range(0, n2, FFT_GROUP):
        x = jnp.concatenate([x_ref[pl.ds(g0 + g, rows, stride=n2), :] for g in range(FFT_GROUP)], axis=1)
        y = _dot2_stacked(m2_ref[...], x)
        for g in range(FFT_GROUP):
            tile = y[:, g * LANES:(g + 1) * LANES].reshape(2, kb, SUBLANES, LANES)
            o_ref[:, :, 0, (g0 + g) * SUBLANES:(g0 + g + 1) * SUBLANES, :] = tile


def _fft_first(x, n1):
    n2 = FFT_N2
    rows, ch = x.shape[0] // n2, x.shape[1]
    kp = _fft_rows(n1)
    c, s = _cs(np.outer(np.arange(kp), np.arange(rows)), n1)
    m2 = jnp.concatenate(_const_split(np.concatenate([c, -s], 0)), axis=0)
    kb, ct = kp // SUBLANES, ch // LANES
    return pl.pallas_call(
        _fft_first_kernel,
        grid=(ct,),
        in_specs=[pl.BlockSpec((rows * n2, LANES), lambda j: (0, j)),
                  pl.BlockSpec((4 * kp, rows), lambda j: (0, 0))],
        out_specs=pl.BlockSpec((2, kb, 1, n2 * SUBLANES, LANES), lambda j: (0, 0, j, 0, 0)),
        out_shape=jax.ShapeDtypeStruct((2, kb, ct, n2 * SUBLANES, LANES), _f32),
        compiler_params=_cparams("parallel"),
        name="fft_first",
    )(x, m2)


def _fft_mid_kernel(a_ref, f_ref, twr_ref, twi_ref, fhi_ref, flo_ref, ghi_ref, glo_ref, o_ref):
    n2 = FFT_N2
    for s0 in range(0, SUBLANES, FFT_MID_GROUP):
        group = range(s0, s0 + FFT_MID_GROUP)
        ts, tws = [], []
        for s in group:
            rows = pl.ds(s, n2, stride=SUBLANES)
            twr, twi = twr_ref[s], twi_ref[s]
            tws.append((twr, twi))
            for t_ref in (a_ref, f_ref):
                tr, ti = t_ref[0, 0, 0, rows, :], t_ref[1, 0, 0, rows, :]
                ts.append(jnp.concatenate([tr * twr - ti * twi, tr * twi + ti * twr], axis=0))
        y = _dot2(fhi_ref[...], flo_ref[...], jnp.concatenate(ts, axis=1))
        ps = []
        for p in range(FFT_MID_GROUP):
            x, h = y[:, 2 * p * LANES:(2 * p + 1) * LANES], y[:, (2 * p + 1) * LANES:(2 * p + 2) * LANES]
            xr, xi, hr, hi = x[:n2], x[n2:], h[:n2], h[n2:]
            ps.append(jnp.concatenate([xr * hr - xi * hi, xr * hi + xi * hr], axis=0))
        b = _dot2(ghi_ref[...], glo_ref[...], jnp.concatenate(ps, axis=1))
        for p, s in enumerate(group):
            rows = pl.ds(s, n2, stride=SUBLANES)
            twr, twi = tws[p]
            br, bi = b[:n2, p * LANES:(p + 1) * LANES], b[n2:, p * LANES:(p + 1) * LANES]
            o_ref[0, 0, 0, rows, :] = br * twr + bi * twi
            o_ref[1, 0, 0, rows, :] = bi * twr - br * twi


def _fft_mid(a, f, n1):
    n2 = FFT_N2
    _, kb, ct, rows, _ = a.shape
    kp = kb * SUBLANES
    n = n1 * n2
    idx = jnp.arange(kp)[:, None] * jnp.arange(n2)[None, :]
    ang = (2.0 * math.pi / n) * (idx % n).astype(_f32)
    twr = jnp.broadcast_to(jnp.cos(ang)[:, :, None], (kp, n2, LANES))
    twi = jnp.broadcast_to(-jnp.sin(ang)[:, :, None], (kp, n2, LANES))
    c, s = _cs(np.outer(np.arange(n2), np.arange(n2)), n2)
    fhi, flo = _const_split(np.block([[c, s], [-s, c]]))
    ghi, glo = _const_split(np.block([[c, -s], [s, c]]))
    blk = pl.BlockSpec((2, 1, 1, rows, LANES), lambda k, j: (0, k, j, 0, 0))
    tw = pl.BlockSpec((SUBLANES, n2, LANES), lambda k, j: (k, 0, 0))
    mat = pl.BlockSpec((2 * n2, 2 * n2), lambda k, j: (0, 0))
    return pl.pallas_call(
        _fft_mid_kernel,
        grid=(kb, ct),
        in_specs=[blk, blk, tw, tw, mat, mat, mat, mat],
        out_specs=blk,
        out_shape=jax.ShapeDtypeStruct(a.shape, _f32),
        compiler_params=_cparams("parallel", "parallel"),
        name="fft_mid",
    )(a, f, twr, twi, fhi, flo, ghi, glo)


def _fft_last_kernel(c_ref, m2_ref, w_ref, x0g_ref, bias_ref, o_ref):
    n2 = FFT_N2
    rows, kb = w_ref.shape[0] // n2, c_ref.shape[1]
    for g0 in range(0, n2, FFT_GROUP):
        tiles = [c_ref[:, :, 0, (g0 + g) * SUBLANES:(g0 + g + 1) * SUBLANES, :].reshape(2 * kb * SUBLANES, LANES)
                 for g in range(FFT_GROUP)]
        y = _dot2_stacked(m2_ref[...], jnp.concatenate(tiles, axis=1))
        for g in range(FFT_GROUP):
            at = pl.ds(g0 + g, rows, stride=n2)
            o_ref[at, :] = x0g_ref[at, :] * (y[:, g * LANES:(g + 1) * LANES] + w_ref[at, :] * bias_ref[...])


def _fft_last(cc, n1, w, x0g, bias):
    n2 = FFT_N2
    rows, ch = w.shape[0] // n2, w.shape[1]
    _, kb, ct, _, _ = cc.shape
    kp = kb * SUBLANES
    n = n1 * n2
    c, s = _cs(np.outer(np.arange(rows), np.arange(kp)), n1)
    k1 = np.arange(kp)
    mult = np.where((k1 == 0) | (k1 == n1 // 2), 1.0, np.where(k1 < n1 // 2, 2.0, 0.0))
    m2 = jnp.concatenate(_const_split(np.concatenate([c * mult, -s * mult], 1) / n), axis=0)
    blk = pl.BlockSpec((rows * n2, LANES), lambda j: (0, j))
    return pl.pallas_call(
        _fft_last_kernel,
        grid=(ct,),
        in_specs=[pl.BlockSpec((2, kb, 1, n2 * SUBLANES, LANES), lambda j: (0, 0, j, 0, 0)),
                  pl.BlockSpec((2 * rows, 2 * kp), lambda j: (0, 0)),
                  blk, blk, pl.BlockSpec((1, LANES), lambda j: (0, j))],
        out_specs=blk,
        out_shape=jax.ShapeDtypeStruct((rows * n2, ch), _f32),
        compiler_params=_cparams("parallel"),
        name="fft_last",
    )(cc, m2, w, x0g, bias.astype(_f32)[None])


def _hy_small_kernel(w_ref, buf_ref, x0g_ref, bias_ref, fwhi_ref, fwlo_ref, fbhi_ref, fblo_ref,
                     ihi_ref, ilo_ref, o_ref):
    n = buf_ref.shape[0]
    w = w_ref[...]
    wf = _dot3(fwhi_ref[...], fwlo_ref[...], w)
    hf = _dot3(fbhi_ref[...], fblo_ref[...], buf_ref[...])
    wr, wi, hr, hi = wf[:n], wf[n:], hf[:n], hf[n:]
    y = _dot3(ihi_ref[...], ilo_ref[...], jnp.concatenate([wr * hr - wi * hi, wr * hi + wi * hr], axis=0))
    o_ref[...] = (x0g_ref[...] * (y + w * bias_ref[...])).astype(o_ref.dtype)


def _hy_conv_small(w, buf, x0g, bias):
    L, ch = w.shape
    n = 2 * L
    tc = 256
    c, s = _cs(np.outer(np.arange(n), np.arange(n)), n)
    fb = np.concatenate([c, -s], 0)
    mats = [*_const_split(fb[:, :L]), *_const_split(fb),
            *_const_split(np.concatenate([c[:L], -s[:L]], 1) / n)]
    col = lambda r: pl.BlockSpec((r, tc), lambda j: (0, j))
    return pl.pallas_call(
        _hy_small_kernel,
        grid=(ch // tc,),
        in_specs=[col(L), col(n), col(L), col(1)] + [pl.BlockSpec(m.shape, lambda j: (0, 0)) for m in mats],
        out_specs=col(L),
        out_shape=jax.ShapeDtypeStruct((L, ch), MXU_DTYPE),
        compiler_params=_cparams("parallel"),
        name="hy_conv_small",
    )(w, buf, x0g, bias.astype(_f32)[None], *mats)


def _hy_conv(w, buf, x0g, bias):
    L, ch = w.shape
    if L < 512:
        return _hy_conv_small(w, buf, x0g, bias)
    n1 = 2 * L // FFT_N2
    return _fft_last(_fft_mid(_fft_first(w, n1), _fft_first(buf, n1), n1), n1, w, x0g, bias)


def _zero_states():
    return (jnp.zeros((RET_HEADS, RET_DH, RET_DH), _f32), jnp.zeros((RET_HEADS, RET_DH, RET_DH), _f32),
            jnp.zeros((SSM_HEADS, SSM_STATE, SSM_HEADDIM), _f32),
            jnp.zeros((SSM_HEADS, SSM_STATE, SSM_HEADDIM), _f32))


def _recurrent(proj, L, lp, states, latent, states_only=False):
    pa, pb = proj
    q, k, v = _prep_ret(pa, pb, L, latent, states_only)
    ra, rb, ret_f, ret_b = _scan_ret(q, k, v, lp['ret_decay_logit'], states[0], states[1])
    cs, bs, xs, pack = _prep_ssd(pa, pb, L, lp['conv_ssm_w'], lp['conv_ssm_b'], lp['ssm_dt_bias'],
                                 lp['ssm_A_log'], states_only)
    sa, sb_, ssm_f, ssm_b = _scan_ssd(cs, bs, xs, pack, lp['ssm_D'], states[2], states[3])
    return (ra, rb, sa, sb_), (ret_f, ret_b, ssm_f, ssm_b)


def _mix(h, mod, lp, states, latent):
    L = h.shape[0]
    proj = _in_proj(h, mod[0], mod[1], lp)
    pb = proj[1]
    (ra, rb, sa, sb_), fin = _recurrent(proj, L, lp, states, latent)
    y_ret, y_ssm = _merge(ra, rb, sa, sb_, pb, lp['ssm_norm_w'])
    w, x0g = _prep_hy(pb, L, lp['conv_hy_w'], lp['conv_hy_b'])
    y_hy = _hy_conv(w, _hy_filter(L, lp), x0g, lp['hy_bias'])
    y_pool = _pool(pb, L, lp['pool_w'], lp['pool_scale'])
    out = _out_proj([y_hy, y_ret, y_pool, y_ssm], lp['w_out'], h, mod[2], lp['ln_g'], lp['ln_b'])
    return out, fin


def _context_states(hc, mod, lp):
    proj = _in_proj(hc, mod[0], mod[1], lp, SSM_W + SSM_GN)
    _, fin = _recurrent(proj, hc.shape[0], lp, _zero_states(), False, states_only=True)
    return fin


def kernel(x, c, ctx, c_ctx, w_mod, b_mod, w_in, conv_ssm_w, conv_ssm_b, conv_hy_w, conv_hy_b,
           ret_decay_logit, ssm_A_log, ssm_dt_bias, ssm_D, ssm_norm_w, hy_w1, hy_b1, hy_w2, hy_b2,
           hy_w3, hy_freq, hy_bias, pool_w, pool_scale, w_out, ln_g, ln_b):
    assert x.shape[0] == 1
    h, hc = x[0], ctx[0]
    w_in_t = jnp.swapaxes(w_in, 1, 2)
    mods = _adaln(jnp.concatenate([c, c_ctx[None]], axis=0), w_mod, b_mod)
    for l in range(DEPTH):
        lp = {
            'layer': l, 'w_in_t': w_in_t, 'w_a_t': _cast_layer(w_in_t, l, N_A, N_A // 2, 512),
            'w_out': _cast_layer(w_out, l, w_out.shape[1], 1024, 2048)[0],
            'conv_ssm_w': conv_ssm_w[l], 'conv_ssm_b': conv_ssm_b[l],
            'conv_hy_w': conv_hy_w[l], 'conv_hy_b': conv_hy_b[l], 'ret_decay_logit': ret_decay_logit[l],
            'ssm_A_log': ssm_A_log[l], 'ssm_dt_bias': ssm_dt_bias[l], 'ssm_D': ssm_D[l],
            'ssm_norm_w': ssm_norm_w[l], 'hy_w1': hy_w1[l], 'hy_b1': hy_b1[l], 'hy_w2': hy_w2[l],
            'hy_b2': hy_b2[l], 'hy_w3': hy_w3[l], 'hy_freq': hy_freq[l], 'hy_bias': hy_bias[l],
            'pool_w': pool_w[l], 'pool_scale': pool_scale[l], 'ln_g': ln_g[l], 'ln_b': ln_b[l],
        }
        mod = lambda r: tuple(mods[l, r:r + 1, n * D_MODEL:(n + 1) * D_MODEL] for n in range(3))
        if l < DEPTH - 1:
            hc_next, states = _mix(hc, mod(1), lp, _zero_states(), False)
        else:
            states = _context_states(hc, mod(1), lp)
            hc_next = hc
        h, _ = _mix(h, mod(0), lp, states, True)
        hc = hc_next
    return h[None]
```

```python
import functools
import math

import jax
import jax.numpy as jnp
import numpy as np
from jax import lax
from jax.experimental import pallas as pl
from jax.experimental.pallas import tpu as pltpu

D_MODEL = 4096
DEPTH = 2
GRID_W = 64
MIX_W = D_MODEL
BR_W = MIX_W // 4
HY_W = RET_W = POOL_W = SSM_W = BR_W
RET_HEADS = 8
RET_DH = RET_W // RET_HEADS
ROPE_BASE = 10000.0
SSM_HEADDIM = 64
SSM_HEADS = SSM_W // SSM_HEADDIM
SSM_GROUPS = 4
SSM_HPG = SSM_HEADS // SSM_GROUPS
SSM_STATE = 128
SSM_GN = SSM_GROUPS * SSM_STATE
CHUNK = 128
SCAN_CHUNKS = 2
POOL_WINDOWS = (2, 4, 8, 16)
POOL_GROUPS = len(POOL_WINDOWS)
POOL_GROUP = POOL_W // POOL_GROUPS
HY_BANDS = 16
HY_TARGET = 1e-2
HY_FAST = 0.3
HY_SLOW = 1.5
HY_MIN_DECAY = math.log(HY_TARGET) / HY_SLOW
HY_MAX_DECAY = math.log(HY_TARGET) / HY_FAST
ALPHA = (2.0 * DEPTH) ** 0.25
LN_EPS = 1e-5

O_RET_K = 0
O_RET_V = O_RET_K + RET_W
O_SSM_DT = O_RET_V + RET_W
O_SSM_X = O_SSM_DT + 2 * SSM_HEADS
O_SSM_B = O_SSM_X + SSM_W
O_RET_Q = O_SSM_B + SSM_GN
O_SSM_C = O_RET_Q + RET_W
O_HY = O_SSM_C + SSM_GN
O_POOL = O_HY + 3 * HY_W
O_GATE = O_POOL + POOL_W
N_IN = O_GATE + MIX_W

LANES = 128
SUBLANES = 8
N_A = O_SSM_DT + LANES
B_SX = 0
B_SB = O_SSM_B - O_SSM_X
B_RQ = O_RET_Q - O_SSM_X
B_SC = O_SSM_C - O_SSM_X
B_HY = O_HY - O_SSM_X
B_POOL = O_POOL - O_SSM_X
B_GATE = O_GATE - O_SSM_X
N_B = N_IN - O_SSM_X

VMEM_LIMIT_BYTES = 56 * 1024 * 1024
MXU_DTYPE = jnp.bfloat16
HIGHEST = lax.Precision.HIGHEST

_f32 = jnp.float32


def _cparams(*sem, vmem=VMEM_LIMIT_BYTES):
    return pltpu.CompilerParams(dimension_semantics=sem, vmem_limit_bytes=vmem)


def _silu(x):
    return x * jax.nn.sigmoid(x)


def _ln_rows(z):
    mu = jnp.mean(z, -1, keepdims=True)
    zc = z - mu
    var = jnp.mean(zc * zc, -1, keepdims=True)
    return zc * lax.rsqrt(var + LN_EPS)


def _cast_kernel(w_ref, o_ref):
    o_ref[...] = w_ref[...].astype(o_ref.dtype)


def _cast_layer(w, l, r, tr, tc):
    c = w.shape[2]
    assert r % tr == 0 and c % tc == 0
    return pl.pallas_call(
        _cast_kernel,
        grid=(r // tr, c // tc),
        in_specs=[pl.BlockSpec((1, tr, tc), lambda i, j: (l, i, j))],
        out_specs=pl.BlockSpec((1, tr, tc), lambda i, j: (0, i, j)),
        out_shape=jax.ShapeDtypeStruct((1, r, c), MXU_DTYPE),
        compiler_params=_cparams("parallel", "parallel"),
        name="cast_layer",
    )(w)


ADALN_ROWS = 32


def _adaln_kernel(c_ref, w_ref, b_ref, o_ref, xs_ref):
    @pl.when((pl.program_id(0) == 0) & (pl.program_id(1) == 0))
    def _():
        xs_ref[...] = _silu(c_ref[...])

    r, k = c_ref.shape[0], c_ref.shape[1]
    nj = w_ref.shape[-1] // LANES

    def body(t, accs):
        rows = pl.ds(pl.multiple_of(t * ADALN_ROWS, ADALN_ROWS), ADALN_ROWS)
        xs = [xs_ref[m, rows, :] for m in range(r)]
        ws = [w_ref[0, rows, j * LANES:(j + 1) * LANES] for j in range(nj)]
        return tuple(accs[m * nj + j] + xs[m] * ws[j] for m in range(r) for j in range(nj))

    accs = lax.fori_loop(0, k // ADALN_ROWS, body,
                         tuple(jnp.zeros((ADALN_ROWS, LANES), _f32) for _ in range(r * nj)), unroll=4)
    outs = [jnp.concatenate([jnp.sum(accs[m * nj + j], axis=0, keepdims=True) for j in range(nj)], axis=1)
            for m in range(r)]
    outs.append(jnp.zeros((SUBLANES - r, w_ref.shape[-1]), _f32))
    o_ref[0] = jnp.concatenate(outs, axis=0) + b_ref[0]


def _adaln(c_rows, w_mod, b_mod):
    dep, k, n = w_mod.shape
    r = c_rows.shape[0]
    tn = 512
    cb = jnp.broadcast_to(c_rows.astype(_f32)[:, :, None], (r, k, LANES))
    return pl.pallas_call(
        _adaln_kernel,
        grid=(dep, n // tn),
        in_specs=[pl.BlockSpec((r, k, LANES), lambda l, j: (0, 0, 0)),
                  pl.BlockSpec((1, k, tn), lambda l, j: (l, 0, j)),
                  pl.BlockSpec((1, 1, tn), lambda l, j: (l, 0, j))],
        out_specs=pl.BlockSpec((1, SUBLANES, tn), lambda l, j: (l, 0, j)),
        out_shape=jax.ShapeDtypeStruct((dep, SUBLANES, n), _f32),
        scratch_shapes=[pltpu.VMEM((r, k, LANES), _f32)],
        compiler_params=_cparams("arbitrary", "arbitrary"),
        name="adaln",
    )(cb, w_mod, b_mod[:, None, :])


def _matmul_nt_kernel(a_ref, b_ref, o_ref):
    o_ref[...] = lax.dot_general(a_ref[...], b_ref[0].astype(MXU_DTYPE), (((1,), (1,)), ((), ())),
                                 preferred_element_type=_f32).astype(o_ref.dtype)


def _matmul_nt(a, wt, l, row0, n, tm, tn, out_dtype):
    m, k = a.shape
    assert m % tm == 0 and n % tn == 0 and row0 % 32 == 0 and tn % 32 == 0
    return pl.pallas_call(
        _matmul_nt_kernel,
        grid=(m // tm, n // tn),
        in_specs=[pl.BlockSpec((tm, k), lambda i, j: (i, 0)),
                  pl.BlockSpec((pl.Element(1), pl.Element(tn), pl.Element(k)),
                               lambda i, j: (l, pl.multiple_of(row0 + j * tn, 32), 0))],
        out_specs=pl.BlockSpec((tm, tn), lambda i, j: (i, j)),
        out_shape=jax.ShapeDtypeStruct((m, n), out_dtype),
        compiler_params=_cparams("parallel", "parallel"),
        name="matmul_nt",
    )(a, wt)


def _modulate_kernel(h_ref, shift_ref, scale_ref, o_ref):
    o_ref[...] = (_ln_rows(h_ref[...]) * (1.0 + scale_ref[...]) + shift_ref[...]).astype(o_ref.dtype)


def _modulate(h, shift, scale):
    L, d = h.shape
    tr = min(512, L)
    vec = pl.BlockSpec((1, d), lambda i: (0, 0))
    return pl.pallas_call(
        _modulate_kernel,
        grid=(L // tr,),
        in_specs=[pl.BlockSpec((tr, d), lambda i: (i, 0)), vec, vec],
        out_specs=pl.BlockSpec((tr, d), lambda i: (i, 0)),
        out_shape=jax.ShapeDtypeStruct((L, d), MXU_DTYPE),
        compiler_params=_cparams("parallel"),
        name="modulate",
    )(h, shift, scale)


def _in_proj(h, shift, scale, lp, n_b=N_B):
    L = h.shape[0]
    u = _modulate(h, shift, scale)
    pa = _matmul_nt(u, lp['w_a_t'], 0, 0, N_A, 512 if L % 512 == 0 else 256, N_A, _f32)
    pb = _matmul_nt(u, lp['w_in_t'], lp['layer'], O_SSM_X, n_b, 1024 if L % 1024 == 0 else 256, 512, MXU_DTYPE)
    return pa, pb


def _out_proj_kernel(y0_ref, y1_ref, y2_ref, y3_ref, w_ref, h_ref, gate_ref, g_ref, b_ref, o_ref):
    out = None
    for n, y_ref in enumerate((y0_ref, y1_ref, y2_ref, y3_ref)):
        d = jnp.dot(y_ref[...].astype(MXU_DTYPE), w_ref[n * BR_W:(n + 1) * BR_W, :], preferred_element_type=_f32)
        out = d if out is None else out + d
    z = ALPHA * h_ref[...] + gate_ref[...] * out
    o_ref[...] = _ln_rows(z) * g_ref[...] + b_ref[...]


OUT_PROJ_VMEM_BYTES = 60 * 1024 * 1024


def _out_proj(ys, w, h, gate, g, b):
    L, d = h.shape
    tm = 256
    lhs = pl.BlockSpec((tm, BR_W), lambda i: (i, 0))
    vec = pl.BlockSpec((1, d), lambda i: (0, 0))
    row = pl.BlockSpec((tm, d), lambda i: (i, 0))
    return pl.pallas_call(
        _out_proj_kernel,
        grid=(L // tm,),
        in_specs=[lhs] * len(ys) + [
            pl.BlockSpec(w.shape, lambda i: (0, 0), pipeline_mode=pl.Buffered(1)), row, vec, vec, vec],
        out_specs=row,
        out_shape=jax.ShapeDtypeStruct((L, d), _f32),
        compiler_params=_cparams("parallel", vmem=OUT_PROJ_VMEM_BYTES),
        name="out_proj",
    )(*ys, w, h, gate, g[None], b[None])


def _rope_tables(L):
    rows = L // GRID_W
    row = jnp.repeat(jnp.arange(rows), GRID_W).astype(_f32)
    col = jnp.tile(jnp.arange(GRID_W), rows).astype(_f32)
    nq = RET_DH // 4
    inv = ROPE_BASE ** (-jnp.arange(nq, dtype=_f32) / nq)
    ang = jnp.concatenate([row[:, None] * inv, col[:, None] * inv], -1)
    cos, sin = jnp.cos(ang), jnp.sin(ang)
    return jnp.concatenate([cos, cos], -1), jnp.concatenate([-sin, sin], -1)


def _prep_ret_kernel(qlo_ref, qhi_ref, k_ref, v_ref, cos_ref, sin_ref, qo_ref, ko_ref, vo_ref, *, rope):
    def rot(t):
        if not rope:
            return t
        return t * cos_ref[...] + pltpu.roll(t, RET_DH // 2, axis=1) * sin_ref[...]

    half = RET_HEADS // 2
    for h in range(RET_HEADS):
        sl = slice(h * RET_DH, (h + 1) * RET_DH)
        q_ref, qs = (qlo_ref, sl) if h < half else (qhi_ref, slice((h - half) * RET_DH, (h - half + 1) * RET_DH))
        qo_ref[:, sl] = rot(q_ref[:, qs].astype(_f32)).astype(qo_ref.dtype)
        ko_ref[:, sl] = rot(k_ref[:, sl] * (RET_DH ** -0.5)).astype(ko_ref.dtype)
    vo_ref[...] = v_ref[...].astype(vo_ref.dtype)


def _prep_ret(pa, pb, L, rope, states_only=False):
    tr = min(512, L)
    cos, sin = _rope_tables(L) if rope else (jnp.ones((L, LANES), _f32), jnp.zeros((L, LANES), _f32))
    hw = RET_W // 2
    q_src, q_col = (pa, O_RET_K) if states_only else (pb, B_RQ)
    qsp = lambda n: pl.BlockSpec((tr, hw), lambda i: (i, q_col // hw + n))
    sec = lambda c: pl.BlockSpec((tr, RET_W), lambda i: (i, c // RET_W))
    tab = pl.BlockSpec((tr, LANES), lambda i: (i, 0))
    out = pl.BlockSpec((tr, RET_W), lambda i: (i, 0))
    shp = jax.ShapeDtypeStruct((L, RET_W), MXU_DTYPE)
    return pl.pallas_call(
        functools.partial(_prep_ret_kernel, rope=rope),
        grid=(L // tr,),
        in_specs=[qsp(0), qsp(1), sec(O_RET_K), sec(O_RET_V), tab, tab],
        out_specs=[out, out, out],
        out_shape=[shp, shp, shp],
        compiler_params=_cparams("parallel"),
        name="prep_ret",
    )(q_src, q_src, pa, pa, cos, sin)


def _scan_ret_kernel(logit_ref, qi_ref, ki_ref, vi_ref, qj_ref, kj_ref, vj_ref, s0f_ref, s0b_ref,
                     ya_ref, yb_ref, finf_ref, finb_ref,
                     sf, sb, dmask, f_out, f_upd, f_all, b_out, b_upd, b_all):
    i = pl.program_id(0)
    c = CHUNK

    @pl.when(i == 0)
    def _():
        sf[...] = s0f_ref[...]
        sb[...] = s0b_ref[...]
        ii = lax.broadcasted_iota(jnp.int32, (c, c), 0).astype(_f32)
        jj = lax.broadcasted_iota(jnp.int32, (c, c), 1).astype(_f32)
        for h in range(RET_HEADS):
            def lg(d):
                x = logit_ref[d, h]
                v = -jnp.log1p(jnp.exp(-x))
                return jnp.broadcast_to(v[0:1, :], (c, c))
            lf, lb = lg(0), lg(1)
            dmask[h] = jnp.where(ii > jj, jnp.exp(lf * (ii - jj)),
                                 jnp.where(jj > ii, jnp.exp(lb * (jj - ii)), 2.0))
            f_out[h] = jnp.exp(lf * (ii + 1.0))
            f_upd[h] = jnp.exp(lf * (c - 1.0 - ii))
            f_all[h] = jnp.exp(lf * float(c))
            b_out[h] = jnp.exp(lb * (c - ii))
            b_upd[h] = jnp.exp(lb * ii)
            b_all[h] = jnp.exp(lb * float(c))

    tn = (((0,), (0,)), ((), ()))
    nt = (((1,), (1,)), ((), ()))
    heads = range(RET_HEADS)
    sls = [slice(h * RET_DH, (h + 1) * RET_DH) for h in heads]
    for u in range(SCAN_CHUNKS):
        ri = slice(u * c, (u + 1) * c)
        rj = slice((SCAN_CHUNKS - 1 - u) * c, (SCAN_CHUNKS - u) * c)
        scores = [lax.dot_general(qi_ref[ri, sl], ki_ref[ri, sl], nt, preferred_element_type=_f32) for sl in sls]
        upd_f = [lax.dot_general((ki_ref[ri, sl].astype(_f32) * f_upd[h]).astype(MXU_DTYPE), vi_ref[ri, sl], tn,
                                 preferred_element_type=_f32) for h, sl in zip(heads, sls)]
        upd_b = [lax.dot_general((kj_ref[rj, sl].astype(_f32) * b_upd[h]).astype(MXU_DTYPE), vj_ref[rj, sl], tn,
                                 preferred_element_type=_f32) for h, sl in zip(heads, sls)]
        for h, sl in zip(heads, sls):
            lhs = jnp.concatenate([(scores[h] * dmask[h]).astype(MXU_DTYPE),
                                   (qi_ref[ri, sl].astype(_f32) * f_out[h]).astype(MXU_DTYPE)], axis=1)
            rhs = jnp.concatenate([vi_ref[ri, sl], sf[h].astype(MXU_DTYPE)], axis=0)
            ya_ref[ri, sl] = jnp.dot(lhs, rhs, preferred_element_type=_f32)
            yb_ref[rj, sl] = jnp.dot((qj_ref[rj, sl].astype(_f32) * b_out[h]).astype(MXU_DTYPE),
                                     sb[h].astype(MXU_DTYPE), preferred_element_type=_f32)
        for h in heads:
            sf[h] = f_all[h] * sf[h] + upd_f[h]
            sb[h] = b_all[h] * sb[h] + upd_b[h]

    @pl.when(i == pl.num_programs(0) - 1)
    def _():
        finf_ref[...] = sf[...]
        finb_ref[...] = sb[...]


def _scan_ret(q, k, v, logit, s0f, s0b):
    L = q.shape[0]
    nc = L // (SCAN_CHUNKS * CHUNK)
    assert nc * SCAN_CHUNKS * CHUNK == L
    logit_b = jnp.broadcast_to(logit.astype(_f32)[:, :, None, None], (2, RET_HEADS, SUBLANES, LANES))
    fw = pl.BlockSpec((SCAN_CHUNKS * CHUNK, RET_W), lambda i: (i, 0))
    bw = pl.BlockSpec((SCAN_CHUNKS * CHUNK, RET_W), lambda i: (nc - 1 - i, 0))
    st = pl.BlockSpec((RET_HEADS, RET_DH, RET_DH), lambda i: (0, 0, 0))
    yshape = jax.ShapeDtypeStruct((L, RET_W), _f32)
    sshape = jax.ShapeDtypeStruct((RET_HEADS, RET_DH, RET_DH), _f32)
    tile = pltpu.VMEM((RET_HEADS, CHUNK, CHUNK), _f32)
    return pl.pallas_call(
        _scan_ret_kernel,
        grid=(nc,),
        in_specs=[pl.BlockSpec((2, RET_HEADS, SUBLANES, LANES), lambda i: (0, 0, 0, 0)),
                  fw, fw, fw, bw, bw, bw, st, st],
        out_specs=[fw, bw, st, st],
        out_shape=[yshape, yshape, sshape, sshape],
        scratch_shapes=[pltpu.VMEM((RET_HEADS, RET_DH, RET_DH), _f32)] * 2 + [tile] * 7,
        compiler_params=_cparams("arbitrary"),
        name="scan_ret",
    )(logit_b, q, k, v, q, k, v, s0f, s0b)


def _shift_rows(x, prev_row, next_row):
    r = x.shape[0]
    rid = lax.broadcasted_iota(jnp.int32, x.shape, 0)
    up = jnp.where(rid == 0, prev_row, pltpu.roll(x, 1, axis=0))
    dn = jnp.where(rid == r - 1, next_row, pltpu.roll(x, r - 1, axis=0))
    return up, dn


HALO = 16


def _conv3(x_ref, prev_ref, next_ref, w_ref, b_ref, has_prev, has_next):
    x = x_ref[...].astype(_f32)
    prev_row = prev_ref[...].astype(_f32)[HALO - 1:HALO, :] * has_prev
    next_row = next_ref[...].astype(_f32)[0:1, :] * has_next
    up, dn = _shift_rows(x, prev_row, next_row)
    return up * w_ref[0:1, :] + x * w_ref[1:2, :] + dn * w_ref[2:3, :] + b_ref[...]


def _halo_specs(tr, L, width, col):
    nb = tr // HALO
    last = L // HALO - 1
    cb = col // width
    return [pl.BlockSpec((tr, width), lambda i: (i, cb)),
            pl.BlockSpec((HALO, width), lambda i: (jnp.maximum(i * nb - 1, 0), cb)),
            pl.BlockSpec((HALO, width), lambda i: (jnp.minimum((i + 1) * nb, last), cb))]


def _prep_ssd_kernel(x_ref, xp_ref, xn_ref, b_ref, bp_ref, bn_ref, c_ref, cp_ref, cn_ref, dt_ref,
                     w_ref, cb_ref, dtb_ref, alog_ref, co_ref, bo_ref, xo_ref, pack_ref):
    i = pl.program_id(0)
    has_prev = (i > 0).astype(_f32)
    has_next = (i < pl.num_programs(0) - 1).astype(_f32)

    def conv(lo, hi, t_ref, p_ref, n_ref, o_ref):
        y = _conv3(t_ref, p_ref, n_ref, w_ref.at[:, lo:hi], cb_ref.at[:, lo:hi], has_prev, has_next)
        o_ref[...] = _silu(y).astype(o_ref.dtype)

    conv(0, SSM_W, x_ref, xp_ref, xn_ref, xo_ref)
    conv(SSM_W, SSM_W + SSM_GN, b_ref, bp_ref, bn_ref, bo_ref)
    conv(SSM_W + SSM_GN, SSM_W + 2 * SSM_GN, c_ref, cp_ref, cn_ref, co_ref)
    z = dt_ref[...] + dtb_ref[...]
    dt = jnp.maximum(z, 0.0) + jnp.log1p(jnp.exp(-jnp.abs(z)))
    a = dt * (-jnp.exp(alog_ref[...]))
    c = CHUNK
    ii = lax.broadcasted_iota(jnp.int32, (c, c), 0)
    jj = lax.broadcasted_iota(jnp.int32, (c, c), 1)
    lower = (jj <= ii).astype(_f32)
    upper = (jj >= ii).astype(_f32)
    lane = lax.broadcasted_iota(jnp.int32, (c, LANES), 1)
    dt_sh = pltpu.roll(dt, 2 * SSM_HEADS, axis=1)
    for n in range(x_ref.shape[0] // c):
        rs = slice(n * c, (n + 1) * c)
        pre = jnp.dot(lower, a[rs], precision=HIGHEST, preferred_element_type=_f32)
        suf = jnp.dot(upper, a[rs], precision=HIGHEST, preferred_element_type=_f32)
        pack_ref[rs, :] = jnp.where(lane < SSM_HEADS, pre,
                                    jnp.where(lane < 2 * SSM_HEADS, suf, dt_sh[rs]))


def _prep_ssd(pa, pb, L, conv_w, conv_b, dt_bias, a_log, states_only=False):
    tr = min(512, L)
    c_col = B_SB if states_only else B_SC
    w = jnp.pad(conv_w.astype(_f32), ((0, SUBLANES - 3), (0, 0)))
    lanes = lambda t: jnp.pad(t.astype(_f32).reshape(1, 2 * SSM_HEADS), ((0, 0), (0, LANES - 2 * SSM_HEADS)))
    wd = SSM_W + 2 * SSM_GN
    row = lambda c: pl.BlockSpec((tr, c), lambda i: (i, 0))
    return pl.pallas_call(
        _prep_ssd_kernel,
        grid=(L // tr,),
        in_specs=_halo_specs(tr, L, SSM_W, B_SX) + _halo_specs(tr, L, SSM_GN, B_SB)
        + _halo_specs(tr, L, SSM_GN, c_col) + [
            pl.BlockSpec((tr, LANES), lambda i: (i, O_SSM_DT // LANES)),
            pl.BlockSpec((SUBLANES, wd), lambda i: (0, 0)),
            pl.BlockSpec((1, wd), lambda i: (0, 0)),
            pl.BlockSpec((1, LANES), lambda i: (0, 0)),
            pl.BlockSpec((1, LANES), lambda i: (0, 0))],
        out_specs=[row(SSM_GN), row(SSM_GN), row(SSM_W), row(LANES)],
        out_shape=[jax.ShapeDtypeStruct((L, SSM_GN), MXU_DTYPE),
                   jax.ShapeDtypeStruct((L, SSM_GN), MXU_DTYPE),
                   jax.ShapeDtypeStruct((L, SSM_W), MXU_DTYPE),
                   jax.ShapeDtypeStruct((L, LANES), _f32)],
        compiler_params=_cparams("parallel"),
        name="prep_ssd",
    )(*([pb] * 9), pa, w, conv_b.astype(_f32)[None], lanes(dt_bias), lanes(a_log))


def _scan_ssd_kernel(ci_ref, bi_ref, xi_ref, pi_ref, cj_ref, bj_ref, xj_ref, pj_ref, dskip_ref,
                     s0f_ref, s0b_ref, ya_ref, yb_ref, finf_ref, finb_ref, sf, sb):
    i = pl.program_id(0)
    c = CHUNK
    H = SSM_HEADS

    @pl.when(i == 0)
    def _():
        sf[...] = s0f_ref[...]
        sb[...] = s0b_ref[...]

    tn = (((0,), (0,)), ((), ()))
    nt = (((1,), (1,)), ((), ()))
    ii = lax.broadcasted_iota(jnp.int32, (c, c), 0)
    jj = lax.broadcasted_iota(jnp.int32, (c, c), 1)
    low = lax.broadcasted_iota(jnp.int32, (c, LANES), 1) < SSM_HEADDIM
    low2 = lax.broadcasted_iota(jnp.int32, (2 * SSM_STATE, LANES), 1) < SSM_HEADDIM
    diag = (lax.broadcasted_iota(jnp.int32, (2 * SSM_STATE, LANES), 0) < SSM_STATE) == low2
    for u in range(SCAN_CHUNKS):
        ri = slice(u * c, (u + 1) * c)
        rj = slice((SCAN_CHUNKS - 1 - u) * c, (SCAN_CHUNKS - u) * c)
        pi = pi_ref[ri, :]
        pit = pi.T
        pj = pj_ref[rj, :]
        ei = jnp.exp(jnp.minimum(pi, 0.0))
        ej = jnp.exp(jnp.minimum(pj, 0.0))
        tot_i, tot_j = pi[c - 1:c, :], pj[0:1, :]
        dt_i, dt_j = pltpu.roll(pi, LANES - 2 * H, axis=1), pltpu.roll(pj, LANES - 2 * H, axis=1)
        wi = jnp.exp(jnp.minimum(tot_i - pi, 0.0)) * dt_i
        wj = jnp.exp(jnp.minimum(tot_j - pj, 0.0)) * dt_j
        eti, etj = jnp.exp(jnp.minimum(tot_i, 0.0)), jnp.exp(jnp.minimum(tot_j, 0.0))
        colb = lambda t, k: jnp.broadcast_to(t[:, k:k + 1], (c, LANES))
        for g in range(SSM_GROUPS):
            gs = slice(g * SSM_STATE, (g + 1) * SSM_STATE)
            ci, bi = ci_ref[ri, gs], bi_ref[ri, gs]
            cj, bj = cj_ref[rj, gs], bj_ref[rj, gs]
            cb = lax.dot_general(ci, bi, nt, preferred_element_type=_f32)
            ci32, bi32, cj32, bj32 = (t.astype(_f32) for t in (ci, bi, cj, bj))
            for pp in range(SSM_HPG // 2):
                q = g * (SSM_HPG // 2) + pp
                heads = (2 * q, 2 * q + 1)
                xs = slice(q * LANES, (q + 1) * LANES)
                x = xi_ref[ri, xs]
                x32 = x.astype(_f32)
                scores, cw, bw = [], [], []
                for h in heads:
                    row = lambda o: pit[o + h:o + h + 1, :]
                    mf = jnp.where(ii >= jj, jnp.exp(jnp.minimum(colb(pi, h) - row(0), 0.0)), 0.0) * row(2 * H)
                    mb = jnp.where(jj >= ii, jnp.exp(jnp.minimum(colb(pi, H + h) - row(H), 0.0)), 0.0) * row(3 * H)
                    scores.append((cb * (mf + mb)).astype(MXU_DTYPE))
                    cw.append((ci32 * colb(ei, h)).astype(MXU_DTYPE))
                    bw.append((bi32 * colb(wi, h)).astype(MXU_DTYPE))
                xa = jnp.where(low, x32, 0.0).astype(MXU_DTYPE)
                xb = jnp.where(low, 0.0, x32).astype(MXU_DTYPE)
                lhs = jnp.concatenate(scores + cw, axis=1)
                rhs = jnp.concatenate([xa, xb, sf[q].astype(MXU_DTYPE)], axis=0)
                y = jnp.dot(lhs, rhs, preferred_element_type=_f32)
                ya_ref[ri, xs] = y + dskip_ref[:, xs] * x32
                upd = lax.dot_general(jnp.concatenate(bw, axis=1), x, tn, preferred_element_type=_f32)
                dec = jnp.where(low2, eti[0:1, heads[0]:heads[0] + 1], eti[0:1, heads[1]:heads[1] + 1])
                sf[q] = dec * sf[q] + jnp.where(diag, upd, 0.0)
                x = xj_ref[rj, xs]
                cw = [(cj32 * colb(ej, H + h)).astype(MXU_DTYPE) for h in heads]
                bw = [(bj32 * colb(wj, H + h)).astype(MXU_DTYPE) for h in heads]
                yb_ref[rj, xs] = jnp.dot(jnp.concatenate(cw, axis=1), sb[q].astype(MXU_DTYPE),
                                         preferred_element_type=_f32)
                upd = lax.dot_general(jnp.concatenate(bw, axis=1), x, tn, preferred_element_type=_f32)
                hb = (H + heads[0], H + heads[1])
                dec = jnp.where(low2, etj[0:1, hb[0]:hb[0] + 1], etj[0:1, hb[1]:hb[1] + 1])
                sb[q] = dec * sb[q] + jnp.where(diag, upd, 0.0)

    @pl.when(i == pl.num_programs(0) - 1)
    def _():
        finf_ref[...] = sf[...]
        finb_ref[...] = sb[...]


def _pair_states(s):
    s = s.reshape(SSM_HEADS // 2, 2, SSM_STATE, SSM_HEADDIM)
    z = jnp.zeros_like(s[:, 0])
    return jnp.concatenate([jnp.concatenate([s[:, 0], z], -1), jnp.concatenate([z, s[:, 1]], -1)], 1)


def _unpair_states(s):
    top, bot = s[:, :SSM_STATE, :SSM_HEADDIM], s[:, SSM_STATE:, SSM_HEADDIM:]
    return jnp.stack([top, bot], 1).reshape(SSM_HEADS, SSM_STATE, SSM_HEADDIM)


def _scan_ssd(cs, bs, xs, pack, d_skip, s0f, s0b):
    L = xs.shape[0]
    nc = L // (SCAN_CHUNKS * CHUNK)
    assert nc * SCAN_CHUNKS * CHUNK == L
    dvec = jnp.repeat(d_skip.astype(_f32), SSM_HEADDIM)[None]
    fw = lambda w: pl.BlockSpec((SCAN_CHUNKS * CHUNK, w), lambda i: (i, 0))
    bw = lambda w: pl.BlockSpec((SCAN_CHUNKS * CHUNK, w), lambda i: (nc - 1 - i, 0))
    pshape = (SSM_HEADS // 2, 2 * SSM_STATE, 2 * SSM_HEADDIM)
    st = pl.BlockSpec(pshape, lambda i: (0, 0, 0))
    yshape = jax.ShapeDtypeStruct((L, SSM_W), _f32)
    sshape = jax.ShapeDtypeStruct(pshape, _f32)
    ya, yb, fin_f, fin_b = pl.pallas_call(
        _scan_ssd_kernel,
        grid=(nc,),
        in_specs=[fw(SSM_GN), fw(SSM_GN), fw(SSM_W), fw(LANES), bw(SSM_GN), bw(SSM_GN), bw(SSM_W), bw(LANES),
                  pl.BlockSpec((1, SSM_W), lambda i: (0, 0)), st, st],
        out_specs=[fw(SSM_W), bw(SSM_W), st, st],
        out_shape=[yshape, yshape, sshape, sshape],
        scratch_shapes=[pltpu.VMEM(pshape, _f32)] * 2,
        compiler_params=_cparams("arbitrary"),
        name="scan_ssd",
    )(cs, bs, xs, pack, cs, bs, xs, pack, dvec, _pair_states(s0f), _pair_states(s0b))
    return ya, yb, _unpair_states(fin_f), _unpair_states(fin_b)


def _merge_kernel(ra_ref, rb_ref, sa_ref, sb_ref, gr_ref, gs_ref, nw_ref, yr_ref, ys_ref):
    for h in range(RET_HEADS):
        sl = slice(h * RET_DH, (h + 1) * RET_DH)
        y = _ln_rows(ra_ref[:, sl] + rb_ref[:, sl])
        yr_ref[:, sl] = (y * _silu(gr_ref[:, sl].astype(_f32))).astype(yr_ref.dtype)
    gw = SSM_W // SSM_GROUPS
    for g in range(SSM_GROUPS):
        sl = slice(g * gw, (g + 1) * gw)
        y = (sa_ref[:, sl] + sb_ref[:, sl]) * _silu(gs_ref[:, sl].astype(_f32))
        y = y * lax.rsqrt(jnp.mean(y * y, -1, keepdims=True) + LN_EPS)
        ys_ref[:, sl] = (y * nw_ref[:, sl]).astype(ys_ref.dtype)


def _gate_spec(tr, n):
    return pl.BlockSpec((tr, BR_W), lambda i: (i, B_GATE // BR_W + n))


def _merge(ra, rb, sa, sb_, pb, norm_w):
    L = ra.shape[0]
    tr = min(512, L)
    row = pl.BlockSpec((tr, BR_W), lambda i: (i, 0))
    shp = jax.ShapeDtypeStruct((L, BR_W), MXU_DTYPE)
    return pl.pallas_call(
        _merge_kernel,
        grid=(L // tr,),
        in_specs=[row, row, row, row, _gate_spec(tr, 1), _gate_spec(tr, 3),
                  pl.BlockSpec((1, BR_W), lambda i: (0, 0))],
        out_specs=[row, row],
        out_shape=[shp, shp],
        compiler_params=_cparams("parallel"),
        name="merge",
    )(ra, rb, sa, sb_, pb, pb, norm_w.astype(_f32)[None])


def _pool_kernel(x_ref, prev_ref, next_ref, g_ref, pw_ref, ps_ref, o_ref, *, L):
    i = pl.program_id(0)
    t = x_ref.shape[0]
    halo = HALO
    has_prev = (i > 0).astype(_f32)
    has_next = (i < pl.num_programs(0) - 1).astype(_f32)
    pos = i * t + lax.broadcasted_iota(jnp.int32, (t, 1), 0)
    for g, win in enumerate(POOL_WINDOWS):
        sl = slice(g * POOL_GROUP, (g + 1) * POOL_GROUP)
        x = x_ref[:, sl].astype(_f32)
        s = jnp.concatenate([prev_ref[:, sl].astype(_f32) * has_prev, x,
                             next_ref[:, sl].astype(_f32) * has_next], axis=0)
        rows = t + 2 * halo
        width = 1
        while width < win:
            s = s + pltpu.roll(s, rows - width, axis=0)
            width *= 2
        off = halo - win // 2
        if off:
            s = pltpu.roll(s, rows - off, axis=0)
        cnt = jnp.minimum(pos + win // 2, L) - jnp.maximum(pos - win // 2, 0)
        d = s[:t] / cnt.astype(_f32) - x
        y = jnp.dot(d.astype(MXU_DTYPE), pw_ref[g], preferred_element_type=_f32)
        o_ref[:, sl] = (y * ps_ref[:, sl] * _silu(g_ref[:, sl].astype(_f32))).astype(o_ref.dtype)


def _pool(pb, L, pool_w, pool_scale):
    tr = min(512, L)
    return pl.pallas_call(
        functools.partial(_pool_kernel, L=L),
        grid=(L // tr,),
        in_specs=_halo_specs(tr, L, POOL_W, B_POOL) + [
            _gate_spec(tr, 2),
            pl.BlockSpec((POOL_GROUPS, POOL_GROUP, POOL_GROUP), lambda i: (0, 0, 0)),
            pl.BlockSpec((1, POOL_W), lambda i: (0, 0))],
        out_specs=pl.BlockSpec((tr, POOL_W), lambda i: (i, 0)),
        out_shape=jax.ShapeDtypeStruct((L, POOL_W), MXU_DTYPE),
        compiler_params=_cparams("parallel"),
        name="pool",
    )(pb, pb, pb, pb, pool_w.astype(MXU_DTYPE), pool_scale.astype(_f32)[None])


def _prep_hy_kernel(v_ref, vp_ref, vn_ref, x0_ref, x0p_ref, x0n_ref, x1_ref, x1p_ref, x1n_ref,
                    g_ref, w_ref, b_ref, wo_ref, x0g_ref):
    i = pl.program_id(0)
    has_prev = (i > 0).astype(_f32)
    has_next = (i < pl.num_programs(0) - 1).astype(_f32)

    def conv(n, x_ref, p_ref, n_ref):
        sl = slice(n * HY_W, (n + 1) * HY_W)
        return _conv3(x_ref, p_ref, n_ref, w_ref.at[:, sl], b_ref.at[:, sl], has_prev, has_next)

    hv = conv(0, v_ref, vp_ref, vn_ref)
    hx0 = conv(1, x0_ref, x0p_ref, x0n_ref)
    hx1 = conv(2, x1_ref, x1p_ref, x1n_ref)
    wo_ref[...] = hx1 * hv
    x0g_ref[...] = hx0 * _silu(g_ref[...].astype(_f32))


def _prep_hy(pb, L, conv_w, conv_b):
    tr = min(512, L)
    w = jnp.pad(conv_w.astype(_f32), ((0, SUBLANES - 3), (0, 0)))
    row = pl.BlockSpec((tr, HY_W), lambda i: (i, 0))
    shp = jax.ShapeDtypeStruct((L, HY_W), _f32)
    secs = sum((_halo_specs(tr, L, HY_W, B_HY + n * HY_W) for n in range(3)), [])
    return pl.pallas_call(
        _prep_hy_kernel,
        grid=(L // tr,),
        in_specs=secs + [_gate_spec(tr, 0),
                         pl.BlockSpec((SUBLANES, 3 * HY_W), lambda i: (0, 0)),
                         pl.BlockSpec((1, 3 * HY_W), lambda i: (0, 0))],
        out_specs=[row, row],
        out_shape=[shp, shp],
        compiler_params=_cparams("parallel"),
        name="prep_hy",
    )(*([pb] * 10), w, conv_b.astype(_f32)[None])


def _split(x):
    hi = x.astype(MXU_DTYPE)
    return hi, (x - hi.astype(_f32)).astype(MXU_DTYPE)


def _dot3(a_hi, a_lo, b):
    b_hi, b_lo = _split(b)
    d = lambda p, q: jnp.dot(p, q, preferred_element_type=_f32)
    return d(a_hi, b_hi) + (d(a_hi, b_lo) + d(a_lo, b_hi))


def _dot2(a_hi, a_lo, b):
    b = b.astype(MXU_DTYPE)
    d = lambda p, q: jnp.dot(p, q, preferred_element_type=_f32)
    return d(a_hi, b) + d(a_lo, b)


def _dot2_stacked(a2, b):
    m = a2.shape[0] // 2
    r = jnp.dot(a2, b.astype(MXU_DTYPE), preferred_element_type=_f32)
    return r[:m] + r[m:]


def _const_split(m):
    return _split(jnp.asarray(m, _f32))


def _filter_kernel(z_ref, w1_ref, b1_ref, w2_ref, b2_ref, w3hi_ref, w3lo_ref, freq_ref, delta_ref, o_ref, *, L):
    t = z_ref.shape[1]
    dot = functools.partial(jnp.dot, precision=HIGHEST, preferred_element_type=_f32)
    freq = freq_ref[...]
    hdn = jnp.sin(freq * (dot(w1_ref[...], z_ref[...]) + b1_ref[...]))
    hdn = jnp.sin(freq * (dot(w2_ref[...], hdn) + b2_ref[...]))
    h_hi, h_lo = _split(hdn)
    d = lambda p, q: lax.dot_general(p, q, (((0,), (0,)), ((), ())), preferred_element_type=_f32)
    filt = d(h_hi, w3hi_ref[...]) + (d(h_hi, w3lo_ref[...]) + d(h_lo, w3hi_ref[...]))
    n = pl.program_id(0) * t + lax.broadcasted_iota(jnp.int32, (t, 1), 0)
    lag = jnp.minimum(jnp.where(n < L, n, 2 * L - n), L - 1).astype(_f32)
    o_ref[...] = jnp.where(n == L, 0.0, filt) * jnp.exp(-(lag / (L - 1)) * delta_ref[...])


def _hy_filter(L, lp):
    n = jnp.arange(2 * L)
    lag = jnp.minimum(jnp.where(n < L, n, 2 * L - n), L - 1).astype(_f32)[:, None]
    t = lag / (L - 1)
    w = 2.0 * math.pi * lag / L
    bands = jnp.linspace(1e-4, HY_BANDS - 1, HY_BANDS, dtype=_f32)[None, :]
    z = jnp.concatenate([t, jnp.cos(bands * w), -jnp.sin(bands * w)], axis=-1)
    emb = z.shape[1]
    zt = jnp.pad(z, ((0, 0), (0, LANES - emb))).T
    w1t = jnp.pad(lp['hy_w1'].astype(_f32), ((0, LANES - emb), (0, 0))).T
    deltas = jnp.abs(jnp.linspace(HY_MIN_DECAY, HY_MAX_DECAY, HY_W, dtype=_f32))[None]
    tr = min(512, L)
    w3hi, w3lo = _split(lp['hy_w3'].astype(_f32))
    full = lambda a: pl.BlockSpec(a.shape, lambda i: (0,) * a.ndim)
    half = pl.BlockSpec((w3hi.shape[0], HY_W), lambda i: (0, i // (L // tr)))
    colv = lambda v: v.astype(_f32)[:, None]
    pre = [w1t, colv(lp['hy_b1']), lp['hy_w2'].astype(_f32).T, colv(lp['hy_b2'])]
    post = [colv(lp['hy_freq']), deltas]
    return pl.pallas_call(
        functools.partial(_filter_kernel, L=L),
        grid=(2 * L // tr,),
        in_specs=[pl.BlockSpec((LANES, tr), lambda i: (0, i))] + [full(a) for a in pre] + [half, half]
        + [full(a) for a in post],
        out_specs=pl.BlockSpec((tr, HY_W), lambda i: (i, 0)),
        out_shape=jax.ShapeDtypeStruct((2 * L, HY_W), _f32),
        compiler_params=_cparams("parallel"),
        name="hy_filter",
    )(zt, *pre, w3hi, w3lo, *post)


def _cs(num, den):
    ang = 2.0 * np.pi * (np.asarray(num, np.int64) % den) / den
    return np.cos(ang), np.sin(ang)


FFT_N2 = LANES


def _fft_rows(n1):
    return -(-(n1 // 2 + 1) // SUBLANES) * SUBLANES


FFT_GROUP = 16
FFT_MID_GROUP = 8


def _fft_first_kernel(x_ref, m2_ref, o_ref):
    n2 = FFT_N2
    rows, kb = x_ref.shape[0] // n2, o_ref.shape[1]
    for g0 in range(0, n2, FFT_GROUP):
        x = jnp.concatenate([x_ref[pl.ds(g0 + g, rows, stride=n2), :] for g in range(FFT_GROUP)], axis=1)
        y = _dot2_stacked(m2_ref[...], x)
        for g in range(FFT_GROUP):
            tile = y[:, g * LANES:(g + 1) * LANES].reshape(2, kb, SUBLANES, LANES)
            o_ref[:, :, 0, (g0 + g) * SUBLANES:(g0 + g + 1) * SUBLANES, :] = tile


def _fft_first(x, n1):
    n2 = FFT_N2
    rows, ch = x.shape[0] // n2, x.shape[1]
    kp = _fft_rows(n1)
    c, s = _cs(np.outer(np.arange(kp), np.arange(rows)), n1)
    m2 = jnp.concatenate(_const_split(np.concatenate([c, -s], 0)), axis=0)
    kb, ct = kp // SUBLANES, ch // LANES
    return pl.pallas_call(
        _fft_first_kernel,
        grid=(ct,),
        in_specs=[pl.BlockSpec((rows * n2, LANES), lambda j: (0, j)),
                  pl.BlockSpec((4 * kp, rows), lambda j: (0, 0))],
        out_specs=pl.BlockSpec((2, kb, 1, n2 * SUBLANES, LANES), lambda j: (0, 0, j, 0, 0)),
        out_shape=jax.ShapeDtypeStruct((2, kb, ct, n2 * SUBLANES, LANES), _f32),
        compiler_params=_cparams("parallel"),
        name="fft_first",
    )(x, m2)


def _fft_mid_kernel(a_ref, f_ref, twr_ref, twi_ref, fhi_ref, flo_ref, ghi_ref, glo_ref, o_ref):
    n2 = FFT_N2
    for s0 in range(0, SUBLANES, FFT_MID_GROUP):
        group = range(s0, s0 + FFT_MID_GROUP)
        ts, tws = [], []
        for s in group:
            rows = pl.ds(s, n2, stride=SUBLANES)
            twr, twi = twr_ref[s], twi_ref[s]
            tws.append((twr, twi))
            for t_ref in (a_ref, f_ref):
                tr, ti = t_ref[0, 0, 0, rows, :], t_ref[1, 0, 0, rows, :]
                ts.append(jnp.concatenate([tr * twr - ti * twi, tr * twi + ti * twr], axis=0))
        y = _dot2(fhi_ref[...], flo_ref[...], jnp.concatenate(ts, axis=1))
        ps = []
        for p in range(FFT_MID_GROUP):
            x, h = y[:, 2 * p * LANES:(2 * p + 1) * LANES], y[:, (2 * p + 1) * LANES:(2 * p + 2) * LANES]
            xr, xi, hr, hi = x[:n2], x[n2:], h[:n2], h[n2:]
            ps.append(jnp.concatenate([xr * hr - xi * hi, xr * hi + xi * hr], axis=0))
        b = _dot2(ghi_ref[...], glo_ref[...], jnp.concatenate(ps, axis=1))
        for p, s in enumerate(group):
            rows = pl.ds(s, n2, stride=SUBLANES)
            twr, twi = tws[p]
            br, bi = b[:n2, p * LANES:(p + 1) * LANES], b[n2:, p * LANES:(p + 1) * LANES]
            o_ref[0, 0, 0, rows, :] = br * twr + bi * twi
            o_ref[1, 0, 0, rows, :] = bi * twr - br * twi


def _fft_mid(a, f, n1):
    n2 = FFT_N2
    _, kb, ct, rows, _ = a.shape
    kp = kb * SUBLANES
    n = n1 * n2
    idx = jnp.arange(kp)[:, None] * jnp.arange(n2)[None, :]
    ang = (2.0 * math.pi / n) * (idx % n).astype(_f32)
    twr = jnp.broadcast_to(jnp.cos(ang)[:, :, None], (kp, n2, LANES))
    twi = jnp.broadcast_to(-jnp.sin(ang)[:, :, None], (kp, n2, LANES))
    c, s = _cs(np.outer(np.arange(n2), np.arange(n2)), n2)
    fhi, flo = _const_split(np.block([[c, s], [-s, c]]))
    ghi, glo = _const_split(np.block([[c, -s], [s, c]]))
    blk = pl.BlockSpec((2, 1, 1, rows, LANES), lambda k, j: (0, k, j, 0, 0))
    tw = pl.BlockSpec((SUBLANES, n2, LANES), lambda k, j: (k, 0, 0))
    mat = pl.BlockSpec((2 * n2, 2 * n2), lambda k, j: (0, 0))
    return pl.pallas_call(
        _fft_mid_kernel,
        grid=(kb, ct),
        in_specs=[blk, blk, tw, tw, mat, mat, mat, mat],
        out_specs=blk,
        out_shape=jax.ShapeDtypeStruct(a.shape, _f32),
        compiler_params=_cparams("parallel", "parallel"),
        name="fft_mid",
    )(a, f, twr, twi, fhi, flo, ghi, glo)


def _fft_last_kernel(c_ref, m2_ref, w_ref, x0g_ref, bias_ref, o_ref):
    n2 = FFT_N2
    rows, kb = w_ref.shape[0] // n2, c_ref.shape[1]
    for g0 in range(0, n2, FFT_GROUP):
        tiles = [c_ref[:, :, 0, (g0 + g) * SUBLANES:(g0 + g + 1) * SUBLANES, :].reshape(2 * kb * SUBLANES, LANES)
                 for g in range(FFT_GROUP)]
        y = _dot2_stacked(m2_ref[...], jnp.concatenate(tiles, axis=1))
        for g in range(FFT_GROUP):
            at = pl.ds(g0 + g, rows, stride=n2)
            o_ref[at, :] = x0g_ref[at, :] * (y[:, g * LANES:(g + 1) * LANES] + w_ref[at, :] * bias_ref[...])


def _fft_last(cc, n1, w, x0g, bias):
    n2 = FFT_N2
    rows, ch = w.shape[0] // n2, w.shape[1]
    _, kb, ct, _, _ = cc.shape
    kp = kb * SUBLANES
    n = n1 * n2
    c, s = _cs(np.outer(np.arange(rows), np.arange(kp)), n1)
    k1 = np.arange(kp)
    mult = np.where((k1 == 0) | (k1 == n1 // 2), 1.0, np.where(k1 < n1 // 2, 2.0, 0.0))
    m2 = jnp.concatenate(_const_split(np.concatenate([c * mult, -s * mult], 1) / n), axis=0)
    blk = pl.BlockSpec((rows * n2, LANES), lambda j: (0, j))
    return pl.pallas_call(
        _fft_last_kernel,
        grid=(ct,),
        in_specs=[pl.BlockSpec((2, kb, 1, n2 * SUBLANES, LANES), lambda j: (0, 0, j, 0, 0)),
                  pl.BlockSpec((2 * rows, 2 * kp), lambda j: (0, 0)),
                  blk, blk, pl.BlockSpec((1, LANES), lambda j: (0, j))],
        out_specs=blk,
        out_shape=jax.ShapeDtypeStruct((rows * n2, ch), _f32),
        compiler_params=_cparams("parallel"),
        name="fft_last",
    )(cc, m2, w, x0g, bias.astype(_f32)[None])


def _hy_small_kernel(w_ref, buf_ref, x0g_ref, bias_ref, fwhi_ref, fwlo_ref, fbhi_ref, fblo_ref,
                     ihi_ref, ilo_ref, o_ref):
    n = buf_ref.shape[0]
    w = w_ref[...]
    wf = _dot3(fwhi_ref[...], fwlo_ref[...], w)
    hf = _dot3(fbhi_ref[...], fblo_ref[...], buf_ref[...])
    wr, wi, hr, hi = wf[:n], wf[n:], hf[:n], hf[n:]
    y = _dot3(ihi_ref[...], ilo_ref[...], jnp.concatenate([wr * hr - wi * hi, wr * hi + wi * hr], axis=0))
    o_ref[...] = (x0g_ref[...] * (y + w * bias_ref[...])).astype(o_ref.dtype)


def _hy_conv_small(w, buf, x0g, bias):
    L, ch = w.shape
    n = 2 * L
    tc = 256
    c, s = _cs(np.outer(np.arange(n), np.arange(n)), n)
    fb = np.concatenate([c, -s], 0)
    mats = [*_const_split(fb[:, :L]), *_const_split(fb),
            *_const_split(np.concatenate([c[:L], -s[:L]], 1) / n)]
    col = lambda r: pl.BlockSpec((r, tc), lambda j: (0, j))
    return pl.pallas_call(
        _hy_small_kernel,
        grid=(ch // tc,),
        in_specs=[col(L), col(n), col(L), col(1)] + [pl.BlockSpec(m.shape, lambda j: (0, 0)) for m in mats],
        out_specs=col(L),
        out_shape=jax.ShapeDtypeStruct((L, ch), MXU_DTYPE),
        compiler_params=_cparams("parallel"),
        name="hy_conv_small",
    )(w, buf, x0g, bias.astype(_f32)[None], *mats)


def _hy_conv(w, buf, x0g, bias):
    L, ch = w.shape
    if L < 512:
        return _hy_conv_small(w, buf, x0g, bias)
    n1 = 2 * L // FFT_N2
    return _fft_last(_fft_mid(_fft_first(w, n1), _fft_first(buf, n1), n1), n1, w, x0g, bias)


def _zero_states():
    return (jnp.zeros((RET_HEADS, RET_DH, RET_DH), _f32), jnp.zeros((RET_HEADS, RET_DH, RET_DH), _f32),
            jnp.zeros((SSM_HEADS, SSM_STATE, SSM_HEADDIM), _f32),
            jnp.zeros((SSM_HEADS, SSM_STATE, SSM_HEADDIM), _f32))


def _recurrent(proj, L, lp, states, latent, states_only=False):
    pa, pb = proj
    q, k, v = _prep_ret(pa, pb, L, latent, states_only)
    ra, rb, ret_f, ret_b = _scan_ret(q, k, v, lp['ret_decay_logit'], states[0], states[1])
    cs, bs, xs, pack = _prep_ssd(pa, pb, L, lp['conv_ssm_w'], lp['conv_ssm_b'], lp['ssm_dt_bias'],
                                 lp['ssm_A_log'], states_only)
    sa, sb_, ssm_f, ssm_b = _scan_ssd(cs, bs, xs, pack, lp['ssm_D'], states[2], states[3])
    return (ra, rb, sa, sb_), (ret_f, ret_b, ssm_f, ssm_b)


def _mix(h, mod, lp, states, latent):
    L = h.shape[0]
    proj = _in_proj(h, mod[0], mod[1], lp)
    pb = proj[1]
    (ra, rb, sa, sb_), fin = _recurrent(proj, L, lp, states, latent)
    y_ret, y_ssm = _merge(ra, rb, sa, sb_, pb, lp['ssm_norm_w'])
    w, x0g = _prep_hy(pb, L, lp['conv_hy_w'], lp['conv_hy_b'])
    y_hy = _hy_conv(w, _hy_filter(L, lp), x0g, lp['hy_bias'])
    y_pool = _pool(pb, L, lp['pool_w'], lp['pool_scale'])
    out = _out_proj([y_hy, y_ret, y_pool, y_ssm], lp['w_out'], h, mod[2], lp['ln_g'], lp['ln_b'])
    return out, fin


def _context_states(hc, mod, lp):
    proj = _in_proj(hc, mod[0], mod[1], lp, SSM_W + SSM_GN)
    _, fin = _recurrent(proj, hc.shape[0], lp, _zero_states(), False, states_only=True)
    return fin


def kernel(x, c, ctx, c_ctx, w_mod, b_mod, w_in, conv_ssm_w, conv_ssm_b, conv_hy_w, conv_hy_b,
           ret_decay_logit, ssm_A_log, ssm_dt_bias, ssm_D, ssm_norm_w, hy_w1, hy_b1, hy_w2, hy_b2,
           hy_w3, hy_freq, hy_bias, pool_w, pool_scale, w_out, ln_g, ln_b):
    assert x.shape[0] == 1
    h, hc = x[0], ctx[0]
    w_in_t = jnp.swapaxes(w_in, 1, 2)
    mods = _adaln(jnp.concatenate([c, c_ctx[None]], axis=0), w_mod, b_mod)
    for l in range(DEPTH):
        lp = {
            'layer': l, 'w_in_t': w_in_t, 'w_a_t': _cast_layer(w_in_t, l, N_A, N_A // 2, 512),
            'w_out': _cast_layer(w_out, l, w_out.shape[1], 1024, 2048)[0],
            'conv_ssm_w': conv_ssm_w[l], 'conv_ssm_b': conv_ssm_b[l],
            'conv_hy_w': conv_hy_w[l], 'conv_hy_b': conv_hy_b[l], 'ret_decay_logit': ret_decay_logit[l],
            'ssm_A_log': ssm_A_log[l], 'ssm_dt_bias': ssm_dt_bias[l], 'ssm_D': ssm_D[l],
            'ssm_norm_w': ssm_norm_w[l], 'hy_w1': hy_w1[l], 'hy_b1': hy_b1[l], 'hy_w2': hy_w2[l],
            'hy_b2': hy_b2[l], 'hy_w3': hy_w3[l], 'hy_freq': hy_freq[l], 'hy_bias': hy_bias[l],
            'pool_w': pool_w[l], 'pool_scale': pool_scale[l], 'ln_g': ln_g[l], 'ln_b': ln_b[l],
        }
        mod = lambda r: tuple(mods[l, r:r + 1, n * D_MODEL:(n + 1) * D_MODEL] for n in range(3))
        if l < DEPTH - 1:
            hc_next, states = _mix(hc, mod(1), lp, _zero_states(), False)
        else:
            states = _context_states(hc, mod(1), lp)
            hc_next = hc
        h, _ = _mix(h, mod(0), lp, states, True)
        hc = hc_next
    return h[None]
```

```python
import functools
import math

import jax
import jax.numpy as jnp
import numpy as np
from jax import lax
from jax.experimental import pallas as pl
from jax.experimental.pallas import tpu as pltpu

D_MODEL = 4096
DEPTH = 2
GRID_W = 64
MIX_W = D_MODEL
BR_W = MIX_W // 4
HY_W = RET_W = POOL_W = SSM_W = BR_W
RET_HEADS = 8
RET_DH = RET_W // RET_HEADS
ROPE_BASE = 10000.0
SSM_HEADDIM = 64
SSM_HEADS = SSM_W // SSM_HEADDIM
SSM_GROUPS = 4
SSM_HPG = SSM_HEADS // SSM_GROUPS
SSM_STATE = 128
SSM_GN = SSM_GROUPS * SSM_STATE
CHUNK = 128
SCAN_CHUNKS = 4
POOL_WINDOWS = (2, 4, 8, 16)
POOL_GROUPS = len(POOL_WINDOWS)
POOL_GROUP = POOL_W // POOL_GROUPS
HY_BANDS = 16
HY_TARGET = 1e-2
HY_FAST = 0.3
HY_SLOW = 1.5
HY_MIN_DECAY = math.log(HY_TARGET) / HY_SLOW
HY_MAX_DECAY = math.log(HY_TARGET) / HY_FAST
ALPHA = (2.0 * DEPTH) ** 0.25
LN_EPS = 1e-5

O_RET_K = 0
O_RET_V = O_RET_K + RET_W
O_SSM_DT = O_RET_V + RET_W
O_SSM_X = O_SSM_DT + 2 * SSM_HEADS
O_SSM_B = O_SSM_X + SSM_W
O_RET_Q = O_SSM_B + SSM_GN
O_SSM_C = O_RET_Q + RET_W
O_HY = O_SSM_C + SSM_GN
O_POOL = O_HY + 3 * HY_W
O_GATE = O_POOL + POOL_W
N_IN = O_GATE + MIX_W

LANES = 128
SUBLANES = 8
N_A = O_SSM_DT + LANES
B_SX = 0
B_SB = O_SSM_B - O_SSM_X
B_RQ = O_RET_Q - O_SSM_X
B_SC = O_SSM_C - O_SSM_X
B_HY = O_HY - O_SSM_X
B_POOL = O_POOL - O_SSM_X
B_GATE = O_GATE - O_SSM_X
N_B = N_IN - O_SSM_X

VMEM_LIMIT_BYTES = 56 * 1024 * 1024
MXU_DTYPE = jnp.bfloat16
HIGHEST = lax.Precision.HIGHEST

_f32 = jnp.float32


def _cparams(*sem, vmem=VMEM_LIMIT_BYTES):
    return pltpu.CompilerParams(dimension_semantics=sem, vmem_limit_bytes=vmem)


def _silu(x):
    return x * jax.nn.sigmoid(x)


def _ln_rows(z):
    mu = jnp.mean(z, -1, keepdims=True)
    zc = z - mu
    var = jnp.mean(zc * zc, -1, keepdims=True)
    return zc * lax.rsqrt(var + LN_EPS)


def _cast_kernel(w_ref, o_ref):
    o_ref[...] = w_ref[...].astype(o_ref.dtype)


def _cast_layer(w, l, r, tr, tc):
    c = w.shape[2]
    assert r % tr == 0 and c % tc == 0
    return pl.pallas_call(
        _cast_kernel,
        grid=(r // tr, c // tc),
        in_specs=[pl.BlockSpec((1, tr, tc), lambda i, j: (l, i, j))],
        out_specs=pl.BlockSpec((1, tr, tc), lambda i, j: (0, i, j)),
        out_shape=jax.ShapeDtypeStruct((1, r, c), MXU_DTYPE),
        compiler_params=_cparams("parallel", "parallel"),
        name="cast_layer",
    )(w)


ADALN_ROWS = 32


def _adaln_kernel(c_ref, w_ref, b_ref, o_ref, xs_ref):
    @pl.when((pl.program_id(0) == 0) & (pl.program_id(1) == 0))
    def _():
        xs_ref[...] = _silu(c_ref[...])

    r, k = c_ref.shape[0], c_ref.shape[1]
    nj = w_ref.shape[-1] // LANES

    def body(t, accs):
        rows = pl.ds(pl.multiple_of(t * ADALN_ROWS, ADALN_ROWS), ADALN_ROWS)
        xs = [xs_ref[m, rows, :] for m in range(r)]
        ws = [w_ref[0, rows, j * LANES:(j + 1) * LANES] for j in range(nj)]
        return tuple(accs[m * nj + j] + xs[m] * ws[j] for m in range(r) for j in range(nj))

    accs = lax.fori_loop(0, k // ADALN_ROWS, body,
                         tuple(jnp.zeros((ADALN_ROWS, LANES), _f32) for _ in range(r * nj)), unroll=4)
    outs = [jnp.concatenate([jnp.sum(accs[m * nj + j], axis=0, keepdims=True) for j in range(nj)], axis=1)
            for m in range(r)]
    outs.append(jnp.zeros((SUBLANES - r, w_ref.shape[-1]), _f32))
    o_ref[0] = jnp.concatenate(outs, axis=0) + b_ref[0]


def _adaln(c_rows, w_mod, b_mod):
    dep, k, n = w_mod.shape
    r = c_rows.shape[0]
    tn = 512
    cb = jnp.broadcast_to(c_rows.astype(_f32)[:, :, None], (r, k, LANES))
    return pl.pallas_call(
        _adaln_kernel,
        grid=(dep, n // tn),
        in_specs=[pl.BlockSpec((r, k, LANES), lambda l, j: (0, 0, 0)),
                  pl.BlockSpec((1, k, tn), lambda l, j: (l, 0, j)),
                  pl.BlockSpec((1, 1, tn), lambda l, j: (l, 0, j))],
        out_specs=pl.BlockSpec((1, SUBLANES, tn), lambda l, j: (l, 0, j)),
        out_shape=jax.ShapeDtypeStruct((dep, SUBLANES, n), _f32),
        scratch_shapes=[pltpu.VMEM((r, k, LANES), _f32)],
        compiler_params=_cparams("arbitrary", "arbitrary"),
        name="adaln",
    )(cb, w_mod, b_mod[:, None, :])


def _matmul_nt_kernel(a_ref, b_ref, o_ref):
    o_ref[...] = lax.dot_general(a_ref[...], b_ref[0].astype(MXU_DTYPE), (((1,), (1,)), ((), ())),
                                 preferred_element_type=_f32).astype(o_ref.dtype)


def _matmul_nt(a, wt, l, row0, n, tm, tn, out_dtype):
    m, k = a.shape
    assert m % tm == 0 and n % tn == 0 and row0 % 32 == 0 and tn % 32 == 0
    return pl.pallas_call(
        _matmul_nt_kernel,
        grid=(m // tm, n // tn),
        in_specs=[pl.BlockSpec((tm, k), lambda i, j: (i, 0)),
                  pl.BlockSpec((pl.Element(1), pl.Element(tn), pl.Element(k)),
                               lambda i, j: (l, pl.multiple_of(row0 + j * tn, 32), 0))],
        out_specs=pl.BlockSpec((tm, tn), lambda i, j: (i, j)),
        out_shape=jax.ShapeDtypeStruct((m, n), out_dtype),
        compiler_params=_cparams("parallel", "parallel"),
        name="matmul_nt",
    )(a, wt)


def _modulate_kernel(h_ref, shift_ref, scale_ref, o_ref):
    o_ref[...] = (_ln_rows(h_ref[...]) * (1.0 + scale_ref[...]) + shift_ref[...]).astype(o_ref.dtype)


def _modulate(h, shift, scale):
    L, d = h.shape
    tr = min(512, L)
    vec = pl.BlockSpec((1, d), lambda i: (0, 0))
    return pl.pallas_call(
        _modulate_kernel,
        grid=(L // tr,),
        in_specs=[pl.BlockSpec((tr, d), lambda i: (i, 0)), vec, vec],
        out_specs=pl.BlockSpec((tr, d), lambda i: (i, 0)),
        out_shape=jax.ShapeDtypeStruct((L, d), MXU_DTYPE),
        compiler_params=_cparams("parallel"),
        name="modulate",
    )(h, shift, scale)


def _in_proj(h, shift, scale, lp, n_b=N_B):
    L = h.shape[0]
    u = _modulate(h, shift, scale)
    pa = _matmul_nt(u, lp['w_a_t'], 0, 0, N_A, 512 if L % 512 == 0 else 256, N_A, _f32)
    pb = _matmul_nt(u, lp['w_in_t'], lp['layer'], O_SSM_X, n_b, 1024 if L % 1024 == 0 else 256, 512, MXU_DTYPE)
    return pa, pb


def _out_proj_kernel(y0_ref, y1_ref, y2_ref, y3_ref, w_ref, h_ref, gate_ref, g_ref, b_ref, o_ref):
    out = None
    for n, y_ref in enumerate((y0_ref, y1_ref, y2_ref, y3_ref)):
        d = jnp.dot(y_ref[...].astype(MXU_DTYPE), w_ref[n * BR_W:(n + 1) * BR_W, :], preferred_element_type=_f32)
        out = d if out is None else out + d
    z = ALPHA * h_ref[...] + gate_ref[...] * out
    o_ref[...] = _ln_rows(z) * g_ref[...] + b_ref[...]


OUT_PROJ_VMEM_BYTES = 60 * 1024 * 1024


def _out_proj(ys, w, h, gate, g, b):
    L, d = h.shape
    tm = 256
    lhs = pl.BlockSpec((tm, BR_W), lambda i: (i, 0))
    vec = pl.BlockSpec((1, d), lambda i: (0, 0))
    row = pl.BlockSpec((tm, d), lambda i: (i, 0))
    return pl.pallas_call(
        _out_proj_kernel,
        grid=(L // tm,),
        in_specs=[lhs] * len(ys) + [
            pl.BlockSpec(w.shape, lambda i: (0, 0), pipeline_mode=pl.Buffered(1)), row, vec, vec, vec],
        out_specs=row,
        out_shape=jax.ShapeDtypeStruct((L, d), _f32),
        compiler_params=_cparams("parallel", vmem=OUT_PROJ_VMEM_BYTES),
        name="out_proj",
    )(*ys, w, h, gate, g[None], b[None])


def _rope_tables(L):
    rows = L // GRID_W
    row = jnp.repeat(jnp.arange(rows), GRID_W).astype(_f32)
    col = jnp.tile(jnp.arange(GRID_W), rows).astype(_f32)
    nq = RET_DH // 4
    inv = ROPE_BASE ** (-jnp.arange(nq, dtype=_f32) / nq)
    ang = jnp.concatenate([row[:, None] * inv, col[:, None] * inv], -1)
    cos, sin = jnp.cos(ang), jnp.sin(ang)
    return jnp.concatenate([cos, cos], -1), jnp.concatenate([-sin, sin], -1)


def _prep_ret_kernel(qlo_ref, qhi_ref, k_ref, v_ref, cos_ref, sin_ref, qo_ref, ko_ref, vo_ref, *, rope):
    def rot(t):
        if not rope:
            return t
        return t * cos_ref[...] + pltpu.roll(t, RET_DH // 2, axis=1) * sin_ref[...]

    half = RET_HEADS // 2
    for h in range(RET_HEADS):
        sl = slice(h * RET_DH, (h + 1) * RET_DH)
        q_ref, qs = (qlo_ref, sl) if h < half else (qhi_ref, slice((h - half) * RET_DH, (h - half + 1) * RET_DH))
        qo_ref[:, sl] = rot(q_ref[:, qs].astype(_f32)).astype(qo_ref.dtype)
        ko_ref[:, sl] = rot(k_ref[:, sl] * (RET_DH ** -0.5)).astype(ko_ref.dtype)
    vo_ref[...] = v_ref[...].astype(vo_ref.dtype)


def _prep_ret(pa, pb, L, rope, states_only=False):
    tr = min(512, L)
    cos, sin = _rope_tables(L) if rope else (jnp.ones((L, LANES), _f32), jnp.zeros((L, LANES), _f32))
    hw = RET_W // 2
    q_src, q_col = (pa, O_RET_K) if states_only else (pb, B_RQ)
    qsp = lambda n: pl.BlockSpec((tr, hw), lambda i: (i, q_col // hw + n))
    sec = lambda c: pl.BlockSpec((tr, RET_W), lambda i: (i, c // RET_W))
    tab = pl.BlockSpec((tr, LANES), lambda i: (i, 0))
    out = pl.BlockSpec((tr, RET_W), lambda i: (i, 0))
    shp = jax.ShapeDtypeStruct((L, RET_W), MXU_DTYPE)
    return pl.pallas_call(
        functools.partial(_prep_ret_kernel, rope=rope),
        grid=(L // tr,),
        in_specs=[qsp(0), qsp(1), sec(O_RET_K), sec(O_RET_V), tab, tab],
        out_specs=[out, out, out],
        out_shape=[shp, shp, shp],
        compiler_params=_cparams("parallel"),
        name="prep_ret",
    )(q_src, q_src, pa, pa, cos, sin)


def _scan_ret_kernel(logit_ref, qi_ref, ki_ref, vi_ref, qj_ref, kj_ref, vj_ref, s0f_ref, s0b_ref,
                     ya_ref, yb_ref, finf_ref, finb_ref,
                     sf, sb, dmask, f_out, f_upd, f_all, b_out, b_upd, b_all):
    i = pl.program_id(0)
    c = CHUNK

    @pl.when(i == 0)
    def _():
        sf[...] = s0f_ref[...]
        sb[...] = s0b_ref[...]
        ii = lax.broadcasted_iota(jnp.int32, (c, c), 0).astype(_f32)
        jj = lax.broadcasted_iota(jnp.int32, (c, c), 1).astype(_f32)
        for h in range(RET_HEADS):
            def lg(d):
                x = logit_ref[d, h]
                v = -jnp.log1p(jnp.exp(-x))
                return jnp.broadcast_to(v[0:1, :], (c, c))
            lf, lb = lg(0), lg(1)
            dmask[h] = jnp.where(ii > jj, jnp.exp(lf * (ii - jj)),
                                 jnp.where(jj > ii, jnp.exp(lb * (jj - ii)), 2.0))
            f_out[h] = jnp.exp(lf * (ii + 1.0))
            f_upd[h] = jnp.exp(lf * (c - 1.0 - ii))
            f_all[h] = jnp.exp(lf * float(c))
            b_out[h] = jnp.exp(lb * (c - ii))
            b_upd[h] = jnp.exp(lb * ii)
            b_all[h] = jnp.exp(lb * float(c))

    tn = (((0,), (0,)), ((), ()))
    nt = (((1,), (1,)), ((), ()))
    heads = range(RET_HEADS)
    sls = [slice(h * RET_DH, (h + 1) * RET_DH) for h in heads]
    ns = qi_ref.shape[0] // c
    for u in range(ns):
        ri = slice(u * c, (u + 1) * c)
        rj = slice((ns - 1 - u) * c, (ns - u) * c)
        scores = [lax.dot_general(qi_ref[ri, sl], ki_ref[ri, sl], nt, preferred_element_type=_f32) for sl in sls]
        upd_f = [lax.dot_general((ki_ref[ri, sl].astype(_f32) * f_upd[h]).astype(MXU_DTYPE), vi_ref[ri, sl], tn,
                                 preferred_element_type=_f32) for h, sl in zip(heads, sls)]
        upd_b = [lax.dot_general((kj_ref[rj, sl].astype(_f32) * b_upd[h]).astype(MXU_DTYPE), vj_ref[rj, sl], tn,
                                 preferred_element_type=_f32) for h, sl in zip(heads, sls)]
        for h, sl in zip(heads, sls):
            lhs = jnp.concatenate([(scores[h] * dmask[h]).astype(MXU_DTYPE),
                                   (qi_ref[ri, sl].astype(_f32) * f_out[h]).astype(MXU_DTYPE)], axis=1)
            rhs = jnp.concatenate([vi_ref[ri, sl], sf[h].astype(MXU_DTYPE)], axis=0)
            ya_ref[ri, sl] = jnp.dot(lhs, rhs, preferred_element_type=_f32)
            yb_ref[rj, sl] = jnp.dot((qj_ref[rj, sl].astype(_f32) * b_out[h]).astype(MXU_DTYPE),
                                     sb[h].astype(MXU_DTYPE), preferred_element_type=_f32)
        for h in heads:
            sf[h] = f_all[h] * sf[h] + upd_f[h]
            sb[h] = b_all[h] * sb[h] + upd_b[h]

    @pl.when(i == pl.num_programs(0) - 1)
    def _():
        finf_ref[...] = sf[...]
        finb_ref[...] = sb[...]


def _scan_ret(q, k, v, logit, s0f, s0b):
    L = q.shape[0]
    rows = min(SCAN_CHUNKS * CHUNK, L)
    nc = L // rows
    assert nc * rows == L
    logit_b = jnp.broadcast_to(logit.astype(_f32)[:, :, None, None], (2, RET_HEADS, SUBLANES, LANES))
    fw = pl.BlockSpec((rows, RET_W), lambda i: (i, 0))
    bw = pl.BlockSpec((rows, RET_W), lambda i: (nc - 1 - i, 0))
    st = pl.BlockSpec((RET_HEADS, RET_DH, RET_DH), lambda i: (0, 0, 0))
    yshape = jax.ShapeDtypeStruct((L, RET_W), _f32)
    sshape = jax.ShapeDtypeStruct((RET_HEADS, RET_DH, RET_DH), _f32)
    tile = pltpu.VMEM((RET_HEADS, CHUNK, CHUNK), _f32)
    return pl.pallas_call(
        _scan_ret_kernel,
        grid=(nc,),
        in_specs=[pl.BlockSpec((2, RET_HEADS, SUBLANES, LANES), lambda i: (0, 0, 0, 0)),
                  fw, fw, fw, bw, bw, bw, st, st],
        out_specs=[fw, bw, st, st],
        out_shape=[yshape, yshape, sshape, sshape],
        scratch_shapes=[pltpu.VMEM((RET_HEADS, RET_DH, RET_DH), _f32)] * 2 + [tile] * 7,
        compiler_params=_cparams("arbitrary"),
        name="scan_ret",
    )(logit_b, q, k, v, q, k, v, s0f, s0b)


def _shift_rows(x, prev_row, next_row):
    r = x.shape[0]
    rid = lax.broadcasted_iota(jnp.int32, x.shape, 0)
    up = jnp.where(rid == 0, prev_row, pltpu.roll(x, 1, axis=0))
    dn = jnp.where(rid == r - 1, next_row, pltpu.roll(x, r - 1, axis=0))
    return up, dn


HALO = 16


def _conv3(x_ref, prev_ref, next_ref, w_ref, b_ref, has_prev, has_next):
    x = x_ref[...].astype(_f32)
    prev_row = prev_ref[...].astype(_f32)[HALO - 1:HALO, :] * has_prev
    next_row = next_ref[...].astype(_f32)[0:1, :] * has_next
    up, dn = _shift_rows(x, prev_row, next_row)
    return up * w_ref[0:1, :] + x * w_ref[1:2, :] + dn * w_ref[2:3, :] + b_ref[...]


def _halo_specs(tr, L, width, col):
    nb = tr // HALO
    last = L // HALO - 1
    cb = col // width
    return [pl.BlockSpec((tr, width), lambda i: (i, cb)),
            pl.BlockSpec((HALO, width), lambda i: (jnp.maximum(i * nb - 1, 0), cb)),
            pl.BlockSpec((HALO, width), lambda i: (jnp.minimum((i + 1) * nb, last), cb))]


def _prep_ssd_kernel(x_ref, xp_ref, xn_ref, b_ref, bp_ref, bn_ref, c_ref, cp_ref, cn_ref, dt_ref,
                     w_ref, cb_ref, dtb_ref, alog_ref, co_ref, bo_ref, xo_ref, pack_ref):
    i = pl.program_id(0)
    has_prev = (i > 0).astype(_f32)
    has_next = (i < pl.num_programs(0) - 1).astype(_f32)

    def conv(lo, hi, t_ref, p_ref, n_ref, o_ref):
        y = _conv3(t_ref, p_ref, n_ref, w_ref.at[:, lo:hi], cb_ref.at[:, lo:hi], has_prev, has_next)
        o_ref[...] = _silu(y).astype(o_ref.dtype)

    conv(0, SSM_W, x_ref, xp_ref, xn_ref, xo_ref)
    conv(SSM_W, SSM_W + SSM_GN, b_ref, bp_ref, bn_ref, bo_ref)
    conv(SSM_W + SSM_GN, SSM_W + 2 * SSM_GN, c_ref, cp_ref, cn_ref, co_ref)
    z = dt_ref[...] + dtb_ref[...]
    dt = jnp.maximum(z, 0.0) + jnp.log1p(jnp.exp(-jnp.abs(z)))
    a = dt * (-jnp.exp(alog_ref[...]))
    c = CHUNK
    ii = lax.broadcasted_iota(jnp.int32, (c, c), 0)
    jj = lax.broadcasted_iota(jnp.int32, (c, c), 1)
    lower = (jj <= ii).astype(_f32)
    upper = (jj >= ii).astype(_f32)
    lane = lax.broadcasted_iota(jnp.int32, (c, LANES), 1)
    dt_sh = pltpu.roll(dt, 2 * SSM_HEADS, axis=1)
    for n in range(x_ref.shape[0] // c):
        rs = slice(n * c, (n + 1) * c)
        pre = jnp.dot(lower, a[rs], precision=HIGHEST, preferred_element_type=_f32)
        suf = jnp.dot(upper, a[rs], precision=HIGHEST, preferred_element_type=_f32)
        pack_ref[rs, :] = jnp.where(lane < SSM_HEADS, pre,
                                    jnp.where(lane < 2 * SSM_HEADS, suf, dt_sh[rs]))


def _prep_ssd(pa, pb, L, conv_w, conv_b, dt_bias, a_log, states_only=False):
    tr = min(512, L)
    c_col = B_SB if states_only else B_SC
    w = jnp.pad(conv_w.astype(_f32), ((0, SUBLANES - 3), (0, 0)))
    lanes = lambda t: jnp.pad(t.astype(_f32).reshape(1, 2 * SSM_HEADS), ((0, 0), (0, LANES - 2 * SSM_HEADS)))
    wd = SSM_W + 2 * SSM_GN
    row = lambda c: pl.BlockSpec((tr, c), lambda i: (i, 0))
    return pl.pallas_call(
        _prep_ssd_kernel,
        grid=(L // tr,),
        in_specs=_halo_specs(tr, L, SSM_W, B_SX) + _halo_specs(tr, L, SSM_GN, B_SB)
        + _halo_specs(tr, L, SSM_GN, c_col) + [
            pl.BlockSpec((tr, LANES), lambda i: (i, O_SSM_DT // LANES)),
            pl.BlockSpec((SUBLANES, wd), lambda i: (0, 0)),
            pl.BlockSpec((1, wd), lambda i: (0, 0)),
            pl.BlockSpec((1, LANES), lambda i: (0, 0)),
            pl.BlockSpec((1, LANES), lambda i: (0, 0))],
        out_specs=[row(SSM_GN), row(SSM_GN), row(SSM_W), row(LANES)],
        out_shape=[jax.ShapeDtypeStruct((L, SSM_GN), MXU_DTYPE),
                   jax.ShapeDtypeStruct((L, SSM_GN), MXU_DTYPE),
                   jax.ShapeDtypeStruct((L, SSM_W), MXU_DTYPE),
                   jax.ShapeDtypeStruct((L, LANES), _f32)],
        compiler_params=_cparams("parallel"),
        name="prep_ssd",
    )(*([pb] * 9), pa, w, conv_b.astype(_f32)[None], lanes(dt_bias), lanes(a_log))


def _scan_ssd_kernel(ci_ref, bi_ref, xi_ref, pi_ref, cj_ref, bj_ref, xj_ref, pj_ref, dskip_ref,
                     s0f_ref, s0b_ref, ya_ref, yb_ref, finf_ref, finb_ref, sf, sb):
    i = pl.program_id(0)
    c = CHUNK
    H = SSM_HEADS

    @pl.when(i == 0)
    def _():
        sf[...] = s0f_ref[...]
        sb[...] = s0b_ref[...]

    tn = (((0,), (0,)), ((), ()))
    nt = (((1,), (1,)), ((), ()))
    ii = lax.broadcasted_iota(jnp.int32, (c, c), 0)
    jj = lax.broadcasted_iota(jnp.int32, (c, c), 1)
    low = lax.broadcasted_iota(jnp.int32, (c, LANES), 1) < SSM_HEADDIM
    low2 = lax.broadcasted_iota(jnp.int32, (2 * SSM_STATE, LANES), 1) < SSM_HEADDIM
    diag = (lax.broadcasted_iota(jnp.int32, (2 * SSM_STATE, LANES), 0) < SSM_STATE) == low2
    ns = pi_ref.shape[0] // c
    for u in range(ns):
        ri = slice(u * c, (u + 1) * c)
        rj = slice((ns - 1 - u) * c, (ns - u) * c)
        pi = pi_ref[ri, :]
        pit = pi.T
        pj = pj_ref[rj, :]
        ei = jnp.exp(jnp.minimum(pi, 0.0))
        ej = jnp.exp(jnp.minimum(pj, 0.0))
        tot_i, tot_j = pi[c - 1:c, :], pj[0:1, :]
        dt_i, dt_j = pltpu.roll(pi, LANES - 2 * H, axis=1), pltpu.roll(pj, LANES - 2 * H, axis=1)
        wi = jnp.exp(jnp.minimum(tot_i - pi, 0.0)) * dt_i
        wj = jnp.exp(jnp.minimum(tot_j - pj, 0.0)) * dt_j
        eti, etj = jnp.exp(jnp.minimum(tot_i, 0.0)), jnp.exp(jnp.minimum(tot_j, 0.0))
        colb = lambda t, k: jnp.broadcast_to(t[:, k:k + 1], (c, LANES))
        for g in range(SSM_GROUPS):
            gs = slice(g * SSM_STATE, (g + 1) * SSM_STATE)
            ci, bi = ci_ref[ri, gs], bi_ref[ri, gs]
            cj, bj = cj_ref[rj, gs], bj_ref[rj, gs]
            cb = lax.dot_general(ci, bi, nt, preferred_element_type=_f32)
            ci32, bi32, cj32, bj32 = (t.astype(_f32) for t in (ci, bi, cj, bj))
            for pp in range(SSM_HPG // 2):
                q = g * (SSM_HPG // 2) + pp
                heads = (2 * q, 2 * q + 1)
                xs = slice(q * LANES, (q + 1) * LANES)
                x = xi_ref[ri, xs]
                x32 = x.astype(_f32)
                scores, cw, bw = [], [], []
                for h in heads:
                    row = lambda o: pit[o + h:o + h + 1, :]
                    mf = jnp.where(ii >= jj, jnp.exp(jnp.minimum(colb(pi, h) - row(0), 0.0)), 0.0) * row(2 * H)
                    mb = jnp.where(jj >= ii, jnp.exp(jnp.minimum(colb(pi, H + h) - row(H), 0.0)), 0.0) * row(3 * H)
                    scores.append((cb * (mf + mb)).astype(MXU_DTYPE))
                    cw.append((ci32 * colb(ei, h)).astype(MXU_DTYPE))
                    bw.append((bi32 * colb(wi, h)).astype(MXU_DTYPE))
                xa = jnp.where(low, x32, 0.0).astype(MXU_DTYPE)
                xb = jnp.where(low, 0.0, x32).astype(MXU_DTYPE)
                lhs = jnp.concatenate(scores + cw, axis=1)
                rhs = jnp.concatenate([xa, xb, sf[q].astype(MXU_DTYPE)], axis=0)
                y = jnp.dot(lhs, rhs, preferred_element_type=_f32)
                ya_ref[ri, xs] = y + dskip_ref[:, xs] * x32
                upd = lax.dot_general(jnp.concatenate(bw, axis=1), x, tn, preferred_element_type=_f32)
                dec = jnp.where(low2, eti[0:1, heads[0]:heads[0] + 1], eti[0:1, heads[1]:heads[1] + 1])
                sf[q] = dec * sf[q] + jnp.where(diag, upd, 0.0)
                x = xj_ref[rj, xs]
                cw = [(cj32 * colb(ej, H + h)).astype(MXU_DTYPE) for h in heads]
                bw = [(bj32 * colb(wj, H + h)).astype(MXU_DTYPE) for h in heads]
                yb_ref[rj, xs] = jnp.dot(jnp.concatenate(cw, axis=1), sb[q].astype(MXU_DTYPE),
                                         preferred_element_type=_f32)
                upd = lax.dot_general(jnp.concatenate(bw, axis=1), x, tn, preferred_element_type=_f32)
                hb = (H + heads[0], H + heads[1])
                dec = jnp.where(low2, etj[0:1, hb[0]:hb[0] + 1], etj[0:1, hb[1]:hb[1] + 1])
                sb[q] = dec * sb[q] + jnp.where(diag, upd, 0.0)

    @pl.when(i == pl.num_programs(0) - 1)
    def _():
        finf_ref[...] = sf[...]
        finb_ref[...] = sb[...]


def _pair_states(s):
    s = s.reshape(SSM_HEADS // 2, 2, SSM_STATE, SSM_HEADDIM)
    z = jnp.zeros_like(s[:, 0])
    return jnp.concatenate([jnp.concatenate([s[:, 0], z], -1), jnp.concatenate([z, s[:, 1]], -1)], 1)


def _unpair_states(s):
    top, bot = s[:, :SSM_STATE, :SSM_HEADDIM], s[:, SSM_STATE:, SSM_HEADDIM:]
    return jnp.stack([top, bot], 1).reshape(SSM_HEADS, SSM_STATE, SSM_HEADDIM)


def _scan_ssd(cs, bs, xs, pack, d_skip, s0f, s0b):
    L = xs.shape[0]
    rows = min(SCAN_CHUNKS * CHUNK, L)
    nc = L // rows
    assert nc * rows == L
    dvec = jnp.repeat(d_skip.astype(_f32), SSM_HEADDIM)[None]
    fw = lambda w: pl.BlockSpec((rows, w), lambda i: (i, 0))
    bw = lambda w: pl.BlockSpec((rows, w), lambda i: (nc - 1 - i, 0))
    pshape = (SSM_HEADS // 2, 2 * SSM_STATE, 2 * SSM_HEADDIM)
    st = pl.BlockSpec(pshape, lambda i: (0, 0, 0))
    yshape = jax.ShapeDtypeStruct((L, SSM_W), _f32)
    sshape = jax.ShapeDtypeStruct(pshape, _f32)
    ya, yb, fin_f, fin_b = pl.pallas_call(
        _scan_ssd_kernel,
        grid=(nc,),
        in_specs=[fw(SSM_GN), fw(SSM_GN), fw(SSM_W), fw(LANES), bw(SSM_GN), bw(SSM_GN), bw(SSM_W), bw(LANES),
                  pl.BlockSpec((1, SSM_W), lambda i: (0, 0)), st, st],
        out_specs=[fw(SSM_W), bw(SSM_W), st, st],
        out_shape=[yshape, yshape, sshape, sshape],
        scratch_shapes=[pltpu.VMEM(pshape, _f32)] * 2,
        compiler_params=_cparams("arbitrary"),
        name="scan_ssd",
    )(cs, bs, xs, pack, cs, bs, xs, pack, dvec, _pair_states(s0f), _pair_states(s0b))
    return ya, yb, _unpair_states(fin_f), _unpair_states(fin_b)


def _merge_kernel(ra_ref, rb_ref, sa_ref, sb_ref, gr_ref, gs_ref, nw_ref, yr_ref, ys_ref):
    for h in range(RET_HEADS):
        sl = slice(h * RET_DH, (h + 1) * RET_DH)
        y = _ln_rows(ra_ref[:, sl] + rb_ref[:, sl])
        yr_ref[:, sl] = (y * _silu(gr_ref[:, sl].astype(_f32))).astype(yr_ref.dtype)
    gw = SSM_W // SSM_GROUPS
    for g in range(SSM_GROUPS):
        sl = slice(g * gw, (g + 1) * gw)
        y = (sa_ref[:, sl] + sb_ref[:, sl]) * _silu(gs_ref[:, sl].astype(_f32))
        y = y * lax.rsqrt(jnp.mean(y * y, -1, keepdims=True) + LN_EPS)
        ys_ref[:, sl] = (y * nw_ref[:, sl]).astype(ys_ref.dtype)


def _gate_spec(tr, n):
    return pl.BlockSpec((tr, BR_W), lambda i: (i, B_GATE // BR_W + n))


def _merge(ra, rb, sa, sb_, pb, norm_w):
    L = ra.shape[0]
    tr = min(512, L)
    row = pl.BlockSpec((tr, BR_W), lambda i: (i, 0))
    shp = jax.ShapeDtypeStruct((L, BR_W), MXU_DTYPE)
    return pl.pallas_call(
        _merge_kernel,
        grid=(L // tr,),
        in_specs=[row, row, row, row, _gate_spec(tr, 1), _gate_spec(tr, 3),
                  pl.BlockSpec((1, BR_W), lambda i: (0, 0))],
        out_specs=[row, row],
        out_shape=[shp, shp],
        compiler_params=_cparams("parallel"),
        name="merge",
    )(ra, rb, sa, sb_, pb, pb, norm_w.astype(_f32)[None])


def _pool_kernel(x_ref, prev_ref, next_ref, g_ref, pw_ref, ps_ref, o_ref, *, L):
    i = pl.program_id(0)
    t = x_ref.shape[0]
    halo = HALO
    has_prev = (i > 0).astype(_f32)
    has_next = (i < pl.num_programs(0) - 1).astype(_f32)
    pos = i * t + lax.broadcasted_iota(jnp.int32, (t, 1), 0)
    for g, win in enumerate(POOL_WINDOWS):
        sl = slice(g * POOL_GROUP, (g + 1) * POOL_GROUP)
        x = x_ref[:, sl].astype(_f32)
        s = jnp.concatenate([prev_ref[:, sl].astype(_f32) * has_prev, x,
                             next_ref[:, sl].astype(_f32) * has_next], axis=0)
        rows = t + 2 * halo
        width = 1
        while width < win:
            s = s + pltpu.roll(s, rows - width, axis=0)
            width *= 2
        off = halo - win // 2
        if off:
            s = pltpu.roll(s, rows - off, axis=0)
        cnt = jnp.minimum(pos + win // 2, L) - jnp.maximum(pos - win // 2, 0)
        d = s[:t] / cnt.astype(_f32) - x
        y = jnp.dot(d.astype(MXU_DTYPE), pw_ref[g], preferred_element_type=_f32)
        o_ref[:, sl] = (y * ps_ref[:, sl] * _silu(g_ref[:, sl].astype(_f32))).astype(o_ref.dtype)


def _pool(pb, L, pool_w, pool_scale):
    tr = min(512, L)
    return pl.pallas_call(
        functools.partial(_pool_kernel, L=L),
        grid=(L // tr,),
        in_specs=_halo_specs(tr, L, POOL_W, B_POOL) + [
            _gate_spec(tr, 2),
            pl.BlockSpec((POOL_GROUPS, POOL_GROUP, POOL_GROUP), lambda i: (0, 0, 0)),
            pl.BlockSpec((1, POOL_W), lambda i: (0, 0))],
        out_specs=pl.BlockSpec((tr, POOL_W), lambda i: (i, 0)),
        out_shape=jax.ShapeDtypeStruct((L, POOL_W), MXU_DTYPE),
        compiler_params=_cparams("parallel"),
        name="pool",
    )(pb, pb, pb, pb, pool_w.astype(MXU_DTYPE), pool_scale.astype(_f32)[None])


def _prep_hy_kernel(v_ref, vp_ref, vn_ref, x0_ref, x0p_ref, x0n_ref, x1_ref, x1p_ref, x1n_ref,
                    g_ref, w_ref, b_ref, wo_ref, x0g_ref):
    i = pl.program_id(0)
    has_prev = (i > 0).astype(_f32)
    has_next = (i < pl.num_programs(0) - 1).astype(_f32)

    def conv(n, x_ref, p_ref, n_ref):
        sl = slice(n * HY_W, (n + 1) * HY_W)
        return _conv3(x_ref, p_ref, n_ref, w_ref.at[:, sl], b_ref.at[:, sl], has_prev, has_next)

    hv = conv(0, v_ref, vp_ref, vn_ref)
    hx0 = conv(1, x0_ref, x0p_ref, x0n_ref)
    hx1 = conv(2, x1_ref, x1p_ref, x1n_ref)
    wo_ref[...] = hx1 * hv
    x0g_ref[...] = hx0 * _silu(g_ref[...].astype(_f32))


def _prep_hy(pb, L, conv_w, conv_b):
    tr = min(512, L)
    w = jnp.pad(conv_w.astype(_f32), ((0, SUBLANES - 3), (0, 0)))
    row = pl.BlockSpec((tr, HY_W), lambda i: (i, 0))
    shp = jax.ShapeDtypeStruct((L, HY_W), _f32)
    secs = sum((_halo_specs(tr, L, HY_W, B_HY + n * HY_W) for n in range(3)), [])
    return pl.pallas_call(
        _prep_hy_kernel,
        grid=(L // tr,),
        in_specs=secs + [_gate_spec(tr, 0),
                         pl.BlockSpec((SUBLANES, 3 * HY_W), lambda i: (0, 0)),
                         pl.BlockSpec((1, 3 * HY_W), lambda i: (0, 0))],
        out_specs=[row, row],
        out_shape=[shp, shp],
        compiler_params=_cparams("parallel"),
        name="prep_hy",
    )(*([pb] * 10), w, conv_b.astype(_f32)[None])


def _split(x):
    hi = x.astype(MXU_DTYPE)
    return hi, (x - hi.astype(_f32)).astype(MXU_DTYPE)


def _dot3(a_hi, a_lo, b):
    b_hi, b_lo = _split(b)
    d = lambda p, q: jnp.dot(p, q, preferred_element_type=_f32)
    return d(a_hi, b_hi) + (d(a_hi, b_lo) + d(a_lo, b_hi))


def _dot2(a_hi, a_lo, b):
    b = b.astype(MXU_DTYPE)
    d = lambda p, q: jnp.dot(p, q, preferred_element_type=_f32)
    return d(a_hi, b) + d(a_lo, b)


def _dot2_stacked(a2, b):
    m = a2.shape[0] // 2
    r = jnp.dot(a2, b.astype(MXU_DTYPE), preferred_element_type=_f32)
    return r[:m] + r[m:]


def _const_split(m):
    return _split(jnp.asarray(m, _f32))


def _filter_kernel(z_ref, w1_ref, b1_ref, w2_ref, b2_ref, w3hi_ref, w3lo_ref, freq_ref, delta_ref, o_ref, *, L):
    t = z_ref.shape[1]
    dot = functools.partial(jnp.dot, precision=HIGHEST, preferred_element_type=_f32)
    freq = freq_ref[...]
    hdn = jnp.sin(freq * (dot(w1_ref[...], z_ref[...]) + b1_ref[...]))
    hdn = jnp.sin(freq * (dot(w2_ref[...], hdn) + b2_ref[...]))
    h_hi, h_lo = _split(hdn)
    d = lambda p, q: lax.dot_general(p, q, (((0,), (0,)), ((), ())), preferred_element_type=_f32)
    filt = d(h_hi, w3hi_ref[...]) + (d(h_hi, w3lo_ref[...]) + d(h_lo, w3hi_ref[...]))
    n = pl.program_id(0) * t + lax.broadcasted_iota(jnp.int32, (t, 1), 0)
    lag = jnp.minimum(jnp.where(n < L, n, 2 * L - n), L - 1).astype(_f32)
    o_ref[...] = jnp.where(n == L, 0.0, filt) * jnp.exp(-(lag / (L - 1)) * delta_ref[...])


def _hy_filter(L, lp):
    n = jnp.arange(2 * L)
    lag = jnp.minimum(jnp.where(n < L, n, 2 * L - n), L - 1).astype(_f32)[:, None]
    t = lag / (L - 1)
    w = 2.0 * math.pi * lag / L
    bands = jnp.linspace(1e-4, HY_BANDS - 1, HY_BANDS, dtype=_f32)[None, :]
    z = jnp.concatenate([t, jnp.cos(bands * w), -jnp.sin(bands * w)], axis=-1)
    emb = z.shape[1]
    zt = jnp.pad(z, ((0, 0), (0, LANES - emb))).T
    w1t = jnp.pad(lp['hy_w1'].astype(_f32), ((0, LANES - emb), (0, 0))).T
    deltas = jnp.abs(jnp.linspace(HY_MIN_DECAY, HY_MAX_DECAY, HY_W, dtype=_f32))[None]
    tr = min(512, L)
    w3hi, w3lo = _split(lp['hy_w3'].astype(_f32))
    full = lambda a: pl.BlockSpec(a.shape, lambda i: (0,) * a.ndim)
    half = pl.BlockSpec((w3hi.shape[0], HY_W), lambda i: (0, i // (L // tr)))
    colv = lambda v: v.astype(_f32)[:, None]
    pre = [w1t, colv(lp['hy_b1']), lp['hy_w2'].astype(_f32).T, colv(lp['hy_b2'])]
    post = [colv(lp['hy_freq']), deltas]
    return pl.pallas_call(
        functools.partial(_filter_kernel, L=L),
        grid=(2 * L // tr,),
        in_specs=[pl.BlockSpec((LANES, tr), lambda i: (0, i))] + [full(a) for a in pre] + [half, half]
        + [full(a) for a in post],
        out_specs=pl.BlockSpec((tr, HY_W), lambda i: (i, 0)),
        out_shape=jax.ShapeDtypeStruct((2 * L, HY_W), _f32),
        compiler_params=_cparams("parallel"),
        name="hy_filter",
    )(zt, *pre, w3hi, w3lo, *post)


def _cs(num, den):
    ang = 2.0 * np.pi * (np.asarray(num, np.int64) % den) / den
    return np.cos(ang), np.sin(ang)


FFT_N2 = LANES


def _fft_rows(n1):
    return -(-(n1 // 2 + 1) // SUBLANES) * SUBLANES


FFT_GROUP = 16
FFT_MID_GROUP = 8


def _fft_first_kernel(x_ref, m2_ref, o_ref):
    n2 = FFT_N2
    rows, kb = x_ref.shape[0] // n2, o_ref.shape[1]
    for g0 in range(0, n2, FFT_GROUP):
        x = jnp.concatenate([x_ref[pl.ds(g0 + g, rows, stride=n2), :] for g in range(FFT_GROUP)], axis=1)
        y = _dot2_stacked(m2_ref[...], x)
        for g in range(FFT_GROUP):
            tile = y[:, g * LANES:(g + 1) * LANES].reshape(2, kb, SUBLANES, LANES)
            o_ref[:, :, 0, (g0 + g) * SUBLANES:(g0 + g + 1) * SUBLANES, :] = tile


def _fft_first(x, n1):
    n2 = FFT_N2
    rows, ch = x.shape[0] // n2, x.shape[1]
    kp = _fft_rows(n1)
    c, s = _cs(np.outer(np.arange(kp), np.arange(rows)), n1)
    m2 = jnp.concatenate(_const_split(np.concatenate([c, -s], 0)), axis=0)
    kb, ct = kp // SUBLANES, ch // LANES
    return pl.pallas_call(
        _fft_first_kernel,
        grid=(ct,),
        in_specs=[pl.BlockSpec((rows * n2, LANES), lambda j: (0, j)),
                  pl.BlockSpec((4 * kp, rows), lambda j: (0, 0))],
        out_specs=pl.BlockSpec((2, kb, 1, n2 * SUBLANES, LANES), lambda j: (0, 0, j, 0, 0)),
        out_shape=jax.ShapeDtypeStruct((2, kb, ct, n2 * SUBLANES, LANES), _f32),
        compiler_params=_cparams("parallel"),
        name="fft_first",
    )(x, m2)


def _fft_mid_kernel(a_ref, f_ref, twr_ref, twi_ref, fhi_ref, flo_ref, ghi_ref, glo_ref, o_ref):
    n2 = FFT_N2
    for s0 in range(0, SUBLANES, FFT_MID_GROUP):
        group = range(s0, s0 + FFT_MID_GROUP)
        ts, tws = [], []
        for s in group:
            rows = pl.ds(s, n2, stride=SUBLANES)
            twr, twi = twr_ref[s], twi_ref[s]
            tws.append((twr, twi))
            for t_ref in (a_ref, f_ref):
                tr, ti = t_ref[0, 0, 0, rows, :], t_ref[1, 0, 0, rows, :]
                ts.append(jnp.concatenate([tr * twr - ti * twi, tr * twi + ti * twr], axis=0))
        y = _dot2(fhi_ref[...], flo_ref[...], jnp.concatenate(ts, axis=1))
        ps = []
        for p in range(FFT_MID_GROUP):
            x, h = y[:, 2 * p * LANES:(2 * p + 1) * LANES], y[:, (2 * p + 1) * LANES:(2 * p + 2) * LANES]
            xr, xi, hr, hi = x[:n2], x[n2:], h[:n2], h[n2:]
            ps.append(jnp.concatenate([xr * hr - xi * hi, xr * hi + xi * hr], axis=0))
        b = _dot2(ghi_ref[...], glo_ref[...], jnp.concatenate(ps, axis=1))
        for p, s in enumerate(group):
            rows = pl.ds(s, n2, stride=SUBLANES)
            twr, twi = tws[p]
            br, bi = b[:n2, p * LANES:(p + 1) * LANES], b[n2:, p * LANES:(p + 1) * LANES]
            o_ref[0, 0, 0, rows, :] = br * twr + bi * twi
            o_ref[1, 0, 0, rows, :] = bi * twr - br * twi


def _fft_mid(a, f, n1):
    n2 = FFT_N2
    _, kb, ct, rows, _ = a.shape
    kp = kb * SUBLANES
    n = n1 * n2
    idx = jnp.arange(kp)[:, None] * jnp.arange(n2)[None, :]
    ang = (2.0 * math.pi / n) * (idx % n).astype(_f32)
    twr = jnp.broadcast_to(jnp.cos(ang)[:, :, None], (kp, n2, LANES))
    twi = jnp.broadcast_to(-jnp.sin(ang)[:, :, None], (kp, n2, LANES))
    c, s = _cs(np.outer(np.arange(n2), np.arange(n2)), n2)
    fhi, flo = _const_split(np.block([[c, s], [-s, c]]))
    ghi, glo = _const_split(np.block([[c, -s], [s, c]]))
    blk = pl.BlockSpec((2, 1, 1, rows, LANES), lambda k, j: (0, k, j, 0, 0))
    tw = pl.BlockSpec((SUBLANES, n2, LANES), lambda k, j: (k, 0, 0))
    mat = pl.BlockSpec((2 * n2, 2 * n2), lambda k, j: (0, 0))
    return pl.pallas_call(
        _fft_mid_kernel,
        grid=(kb, ct),
        in_specs=[blk, blk, tw, tw, mat, mat, mat, mat],
        out_specs=blk,
        out_shape=jax.ShapeDtypeStruct(a.shape, _f32),
        compiler_params=_cparams("parallel", "parallel"),
        name="fft_mid",
    )(a, f, twr, twi, fhi, flo, ghi, glo)


def _fft_last_kernel(c_ref, m2_ref, w_ref, x0g_ref, bias_ref, o_ref):
    n2 = FFT_N2
    rows, kb = w_ref.shape[0] // n2, c_ref.shape[1]
    for g0 in range(0, n2, FFT_GROUP):
        tiles = [c_ref[:, :, 0, (g0 + g) * SUBLANES:(g0 + g + 1) * SUBLANES, :].reshape(2 * kb * SUBLANES, LANES)
                 for g in range(FFT_GROUP)]
        y = _dot2_stacked(m2_ref[...], jnp.concatenate(tiles, axis=1))
        for g in range(FFT_GROUP):
            at = pl.ds(g0 + g, rows, stride=n2)
            o_ref[at, :] = x0g_ref[at, :] * (y[:, g * LANES:(g + 1) * LANES] + w_ref[at, :] * bias_ref[...])


def _fft_last(cc, n1, w, x0g, bias):
    n2 = FFT_N2
    rows, ch = w.shape[0] // n2, w.shape[1]
    _, kb, ct, _, _ = cc.shape
    kp = kb * SUBLANES
    n = n1 * n2
    c, s = _cs(np.outer(np.arange(rows), np.arange(kp)), n1)
    k1 = np.arange(kp)
    mult = np.where((k1 == 0) | (k1 == n1 // 2), 1.0, np.where(k1 < n1 // 2, 2.0, 0.0))
    m2 = jnp.concatenate(_const_split(np.concatenate([c * mult, -s * mult], 1) / n), axis=0)
    blk = pl.BlockSpec((rows * n2, LANES), lambda j: (0, j))
    return pl.pallas_call(
        _fft_last_kernel,
        grid=(ct,),
        in_specs=[pl.BlockSpec((2, kb, 1, n2 * SUBLANES, LANES), lambda j: (0, 0, j, 0, 0)),
                  pl.BlockSpec((2 * rows, 2 * kp), lambda j: (0, 0)),
                  blk, blk, pl.BlockSpec((1, LANES), lambda j: (0, j))],
        out_specs=blk,
        out_shape=jax.ShapeDtypeStruct((rows * n2, ch), _f32),
        compiler_params=_cparams("parallel"),
        name="fft_last",
    )(cc, m2, w, x0g, bias.astype(_f32)[None])


def _hy_small_kernel(w_ref, buf_ref, x0g_ref, bias_ref, fwhi_ref, fwlo_ref, fbhi_ref, fblo_ref,
                     ihi_ref, ilo_ref, o_ref):
    n = buf_ref.shape[0]
    w = w_ref[...]
    wf = _dot3(fwhi_ref[...], fwlo_ref[...], w)
    hf = _dot3(fbhi_ref[...], fblo_ref[...], buf_ref[...])
    wr, wi, hr, hi = wf[:n], wf[n:], hf[:n], hf[n:]
    y = _dot3(ihi_ref[...], ilo_ref[...], jnp.concatenate([wr * hr - wi * hi, wr * hi + wi * hr], axis=0))
    o_ref[...] = (x0g_ref[...] * (y + w * bias_ref[...])).astype(o_ref.dtype)


def _hy_conv_small(w, buf, x0g, bias):
    L, ch = w.shape
    n = 2 * L
    tc = 256
    c, s = _cs(np.outer(np.arange(n), np.arange(n)), n)
    fb = np.concatenate([c, -s], 0)
    mats = [*_const_split(fb[:, :L]), *_const_split(fb),
            *_const_split(np.concatenate([c[:L], -s[:L]], 1) / n)]
    col = lambda r: pl.BlockSpec((r, tc), lambda j: (0, j))
    return pl.pallas_call(
        _hy_small_kernel,
        grid=(ch // tc,),
        in_specs=[col(L), col(n), col(L), col(1)] + [pl.BlockSpec(m.shape, lambda j: (0, 0)) for m in mats],
        out_specs=col(L),
        out_shape=jax.ShapeDtypeStruct((L, ch), MXU_DTYPE),
        compiler_params=_cparams("parallel"),
        name="hy_conv_small",
    )(w, buf, x0g, bias.astype(_f32)[None], *mats)


def _hy_conv(w, buf, x0g, bias):
    L, ch = w.shape
    if L < 512:
        return _hy_conv_small(w, buf, x0g, bias)
    n1 = 2 * L // FFT_N2
    return _fft_last(_fft_mid(_fft_first(w, n1), _fft_first(buf, n1), n1), n1, w, x0g, bias)


def _zero_states():
    return (jnp.zeros((RET_HEADS, RET_DH, RET_DH), _f32), jnp.zeros((RET_HEADS, RET_DH, RET_DH), _f32),
            jnp.zeros((SSM_HEADS, SSM_STATE, SSM_HEADDIM), _f32),
            jnp.zeros((SSM_HEADS, SSM_STATE, SSM_HEADDIM), _f32))


def _recurrent(proj, L, lp, states, latent, states_only=False):
    pa, pb = proj
    q, k, v = _prep_ret(pa, pb, L, latent, states_only)
    ra, rb, ret_f, ret_b = _scan_ret(q, k, v, lp['ret_decay_logit'], states[0], states[1])
    cs, bs, xs, pack = _prep_ssd(pa, pb, L, lp['conv_ssm_w'], lp['conv_ssm_b'], lp['ssm_dt_bias'],
                                 lp['ssm_A_log'], states_only)
    sa, sb_, ssm_f, ssm_b = _scan_ssd(cs, bs, xs, pack, lp['ssm_D'], states[2], states[3])
    return (ra, rb, sa, sb_), (ret_f, ret_b, ssm_f, ssm_b)


def _mix(h, mod, lp, states, latent):
    L = h.shape[0]
    proj = _in_proj(h, mod[0], mod[1], lp)
    pb = proj[1]
    (ra, rb, sa, sb_), fin = _recurrent(proj, L, lp, states, latent)
    y_ret, y_ssm = _merge(ra, rb, sa, sb_, pb, lp['ssm_norm_w'])
    w, x0g = _prep_hy(pb, L, lp['conv_hy_w'], lp['conv_hy_b'])
    y_hy = _hy_conv(w, _hy_filter(L, lp), x0g, lp['hy_bias'])
    y_pool = _pool(pb, L, lp['pool_w'], lp['pool_scale'])
    out = _out_proj([y_hy, y_ret, y_pool, y_ssm], lp['w_out'], h, mod[2], lp['ln_g'], lp['ln_b'])
    return out, fin


def _context_states(hc, mod, lp):
    proj = _in_proj(hc, mod[0], mod[1], lp, SSM_W + SSM_GN)
    _, fin = _recurrent(proj, hc.shape[0], lp, _zero_states(), False, states_only=True)
    return fin


def kernel(x, c, ctx, c_ctx, w_mod, b_mod, w_in, conv_ssm_w, conv_ssm_b, conv_hy_w, conv_hy_b,
           ret_decay_logit, ssm_A_log, ssm_dt_bias, ssm_D, ssm_norm_w, hy_w1, hy_b1, hy_w2, hy_b2,
           hy_w3, hy_freq, hy_bias, pool_w, pool_scale, w_out, ln_g, ln_b):
    assert x.shape[0] == 1
    h, hc = x[0], ctx[0]
    w_in_t = jnp.swapaxes(w_in, 1, 2)
    mods = _adaln(jnp.concatenate([c, c_ctx[None]], axis=0), w_mod, b_mod)
    for l in range(DEPTH):
        lp = {
            'layer': l, 'w_in_t': w_in_t, 'w_a_t': _cast_layer(w_in_t, l, N_A, N_A // 2, 512),
            'w_out': _cast_layer(w_out, l, w_out.shape[1], 1024, 2048)[0],
            'conv_ssm_w': conv_ssm_w[l], 'conv_ssm_b': conv_ssm_b[l],
            'conv_hy_w': conv_hy_w[l], 'conv_hy_b': conv_hy_b[l], 'ret_decay_logit': ret_decay_logit[l],
            'ssm_A_log': ssm_A_log[l], 'ssm_dt_bias': ssm_dt_bias[l], 'ssm_D': ssm_D[l],
            'ssm_norm_w': ssm_norm_w[l], 'hy_w1': hy_w1[l], 'hy_b1': hy_b1[l], 'hy_w2': hy_w2[l],
            'hy_b2': hy_b2[l], 'hy_w3': hy_w3[l], 'hy_freq': hy_freq[l], 'hy_bias': hy_bias[l],
            'pool_w': pool_w[l], 'pool_scale': pool_scale[l], 'ln_g': ln_g[l], 'ln_b': ln_b[l],
        }
        mod = lambda r: tuple(mods[l, r:r + 1, n * D_MODEL:(n + 1) * D_MODEL] for n in range(3))
        if l < DEPTH - 1:
            hc_next, states = _mix(hc, mod(1), lp, _zero_states(), False)
        else:
            states = _context_states(hc, mod(1), lp)
            hc_next = hc
        h, _ = _mix(h, mod(0), lp, states, True)
        hc = hc_next
    return h[None]
```

```python
import functools
import math

import jax
import jax.numpy as jnp
import numpy as np
from jax import lax
from jax.experimental import pallas as pl
from jax.experimental.pallas import tpu as pltpu

D_MODEL = 4096
DEPTH = 2
GRID_W = 64
MIX_W = D_MODEL
BR_W = MIX_W // 4
HY_W = RET_W = POOL_W = SSM_W = BR_W
RET_HEADS = 8
RET_DH = RET_W // RET_HEADS
ROPE_BASE = 10000.0
SSM_HEADDIM = 64
SSM_HEADS = SSM_W // SSM_HEADDIM
SSM_GROUPS = 4
SSM_HPG = SSM_HEADS // SSM_GROUPS
SSM_STATE = 128
SSM_GN = SSM_GROUPS * SSM_STATE
CHUNK = 128
SCAN_CHUNKS = 4
POOL_WINDOWS = (2, 4, 8, 16)
POOL_GROUPS = len(POOL_WINDOWS)
POOL_GROUP = POOL_W // POOL_GROUPS
HY_BANDS = 16
HY_TARGET = 1e-2
HY_FAST = 0.3
HY_SLOW = 1.5
HY_MIN_DECAY = math.log(HY_TARGET) / HY_SLOW
HY_MAX_DECAY = math.log(HY_TARGET) / HY_FAST
ALPHA = (2.0 * DEPTH) ** 0.25
LN_EPS = 1e-5

O_RET_K = 0
O_RET_V = O_RET_K + RET_W
O_SSM_DT = O_RET_V + RET_W
O_SSM_X = O_SSM_DT + 2 * SSM_HEADS
O_SSM_B = O_SSM_X + SSM_W
O_RET_Q = O_SSM_B + SSM_GN
O_SSM_C = O_RET_Q + RET_W
O_HY = O_SSM_C + SSM_GN
O_POOL = O_HY + 3 * HY_W
O_GATE = O_POOL + POOL_W
N_IN = O_GATE + MIX_W

LANES = 128
SUBLANES = 8
N_A = O_SSM_DT + LANES
B_SX = 0
B_SB = O_SSM_B - O_SSM_X
B_RQ = O_RET_Q - O_SSM_X
B_SC = O_SSM_C - O_SSM_X
B_HY = O_HY - O_SSM_X
B_POOL = O_POOL - O_SSM_X
B_GATE = O_GATE - O_SSM_X
N_B = N_IN - O_SSM_X

VMEM_LIMIT_BYTES = 56 * 1024 * 1024
MXU_DTYPE = jnp.bfloat16
HIGHEST = lax.Precision.HIGHEST

_f32 = jnp.float32


def _cparams(*sem, vmem=VMEM_LIMIT_BYTES):
    return pltpu.CompilerParams(dimension_semantics=sem, vmem_limit_bytes=vmem)


def _silu(x):
    return x * jax.nn.sigmoid(x)


def _ln_rows(z):
    mu = jnp.mean(z, -1, keepdims=True)
    zc = z - mu
    var = jnp.mean(zc * zc, -1, keepdims=True)
    return zc * lax.rsqrt(var + LN_EPS)


def _cast_kernel(w_ref, o_ref):
    o_ref[...] = w_ref[...].astype(o_ref.dtype)


def _cast_layer(w, l, r, tr, tc):
    c = w.shape[2]
    assert r % tr == 0 and c % tc == 0
    return pl.pallas_call(
        _cast_kernel,
        grid=(r // tr, c // tc),
        in_specs=[pl.BlockSpec((1, tr, tc), lambda i, j: (l, i, j))],
        out_specs=pl.BlockSpec((1, tr, tc), lambda i, j: (0, i, j)),
        out_shape=jax.ShapeDtypeStruct((1, r, c), MXU_DTYPE),
        compiler_params=_cparams("parallel", "parallel"),
        name="cast_layer",
    )(w)


ADALN_ROWS = 32


def _adaln_kernel(c_ref, w_ref, b_ref, o_ref, xs_ref):
    @pl.when((pl.program_id(0) == 0) & (pl.program_id(1) == 0))
    def _():
        xs_ref[...] = _silu(c_ref[...])

    r, k = c_ref.shape[0], c_ref.shape[1]
    nj = w_ref.shape[-1] // LANES

    def body(t, accs):
        rows = pl.ds(pl.multiple_of(t * ADALN_ROWS, ADALN_ROWS), ADALN_ROWS)
        xs = [xs_ref[m, rows, :] for m in range(r)]
        ws = [w_ref[0, rows, j * LANES:(j + 1) * LANES] for j in range(nj)]
        return tuple(accs[m * nj + j] + xs[m] * ws[j] for m in range(r) for j in range(nj))

    accs = lax.fori_loop(0, k // ADALN_ROWS, body,
                         tuple(jnp.zeros((ADALN_ROWS, LANES), _f32) for _ in range(r * nj)), unroll=4)
    outs = [jnp.concatenate([jnp.sum(accs[m * nj + j], axis=0, keepdims=True) for j in range(nj)], axis=1)
            for m in range(r)]
    outs.append(jnp.zeros((SUBLANES - r, w_ref.shape[-1]), _f32))
    o_ref[0] = jnp.concatenate(outs, axis=0) + b_ref[0]


def _adaln(c_rows, w_mod, b_mod):
    dep, k, n = w_mod.shape
    r = c_rows.shape[0]
    tn = 512
    cb = jnp.broadcast_to(c_rows.astype(_f32)[:, :, None], (r, k, LANES))
    return pl.pallas_call(
        _adaln_kernel,
        grid=(dep, n // tn),
        in_specs=[pl.BlockSpec((r, k, LANES), lambda l, j: (0, 0, 0)),
                  pl.BlockSpec((1, k, tn), lambda l, j: (l, 0, j)),
                  pl.BlockSpec((1, 1, tn), lambda l, j: (l, 0, j))],
        out_specs=pl.BlockSpec((1, SUBLANES, tn), lambda l, j: (l, 0, j)),
        out_shape=jax.ShapeDtypeStruct((dep, SUBLANES, n), _f32),
        scratch_shapes=[pltpu.VMEM((r, k, LANES), _f32)],
        compiler_params=_cparams("arbitrary", "arbitrary"),
        name="adaln",
    )(cb, w_mod, b_mod[:, None, :])


def _matmul_nt_kernel(a_ref, b_ref, o_ref):
    o_ref[...] = lax.dot_general(a_ref[...], b_ref[0].astype(MXU_DTYPE), (((1,), (1,)), ((), ())),
                                 preferred_element_type=_f32).astype(o_ref.dtype)


def _matmul_nt(a, wt, l, row0, n, tm, tn, out_dtype):
    m, k = a.shape
    assert m % tm == 0 and n % tn == 0 and row0 % 32 == 0 and tn % 32 == 0
    return pl.pallas_call(
        _matmul_nt_kernel,
        grid=(m // tm, n // tn),
        in_specs=[pl.BlockSpec((tm, k), lambda i, j: (i, 0)),
                  pl.BlockSpec((pl.Element(1), pl.Element(tn), pl.Element(k)),
                               lambda i, j: (l, pl.multiple_of(row0 + j * tn, 32), 0))],
        out_specs=pl.BlockSpec((tm, tn), lambda i, j: (i, j)),
        out_shape=jax.ShapeDtypeStruct((m, n), out_dtype),
        compiler_params=_cparams("parallel", "parallel"),
        name="matmul_nt",
    )(a, wt)


def _modulate_kernel(h_ref, shift_ref, scale_ref, o_ref):
    o_ref[...] = (_ln_rows(h_ref[...]) * (1.0 + scale_ref[...]) + shift_ref[...]).astype(o_ref.dtype)


def _modulate(h, shift, scale):
    L, d = h.shape
    tr = min(512, L)
    vec = pl.BlockSpec((1, d), lambda i: (0, 0))
    return pl.pallas_call(
        _modulate_kernel,
        grid=(L // tr,),
        in_specs=[pl.BlockSpec((tr, d), lambda i: (i, 0)), vec, vec],
        out_specs=pl.BlockSpec((tr, d), lambda i: (i, 0)),
        out_shape=jax.ShapeDtypeStruct((L, d), MXU_DTYPE),
        compiler_params=_cparams("parallel"),
        name="modulate",
    )(h, shift, scale)


def _in_proj(h, shift, scale, lp, n_b=N_B):
    L = h.shape[0]
    u = _modulate(h, shift, scale)
    pa = _matmul_nt(u, lp['w_a_t'], 0, 0, N_A, 512 if L % 512 == 0 else 256, N_A, _f32)
    pb = _matmul_nt(u, lp['w_in_t'], lp['layer'], O_SSM_X, n_b, 1024 if L % 1024 == 0 else 256, 512, MXU_DTYPE)
    return pa, pb


def _out_proj_kernel(y0_ref, y1_ref, y2_ref, y3_ref, w_ref, h_ref, gate_ref, g_ref, b_ref, o_ref):
    y = jnp.concatenate([r[...].astype(MXU_DTYPE) for r in (y0_ref, y1_ref, y2_ref, y3_ref)], axis=1)
    out = jnp.dot(y, w_ref[...], preferred_element_type=_f32)
    z = ALPHA * h_ref[...] + gate_ref[...] * out
    o_ref[...] = _ln_rows(z) * g_ref[...] + b_ref[...]


OUT_PROJ_VMEM_BYTES = 60 * 1024 * 1024


def _out_proj(ys, w, h, gate, g, b):
    L, d = h.shape
    tm = 256
    lhs = pl.BlockSpec((tm, BR_W), lambda i: (i, 0))
    vec = pl.BlockSpec((1, d), lambda i: (0, 0))
    row = pl.BlockSpec((tm, d), lambda i: (i, 0))
    return pl.pallas_call(
        _out_proj_kernel,
        grid=(L // tm,),
        in_specs=[lhs] * len(ys) + [
            pl.BlockSpec(w.shape, lambda i: (0, 0), pipeline_mode=pl.Buffered(1)), row, vec, vec, vec],
        out_specs=row,
        out_shape=jax.ShapeDtypeStruct((L, d), _f32),
        compiler_params=_cparams("parallel", vmem=OUT_PROJ_VMEM_BYTES),
        name="out_proj",
    )(*ys, w, h, gate, g[None], b[None])


def _rope_tables(L):
    rows = L // GRID_W
    row = jnp.repeat(jnp.arange(rows), GRID_W).astype(_f32)
    col = jnp.tile(jnp.arange(GRID_W), rows).astype(_f32)
    nq = RET_DH // 4
    inv = ROPE_BASE ** (-jnp.arange(nq, dtype=_f32) / nq)
    ang = jnp.concatenate([row[:, None] * inv, col[:, None] * inv], -1)
    cos, sin = jnp.cos(ang), jnp.sin(ang)
    return jnp.concatenate([cos, cos], -1), jnp.concatenate([-sin, sin], -1)


def _prep_ret_kernel(qlo_ref, qhi_ref, k_ref, v_ref, cos_ref, sin_ref, qo_ref, ko_ref, vo_ref, *, rope):
    def rot(t):
        if not rope:
            return t
        return t * cos_ref[...] + pltpu.roll(t, RET_DH // 2, axis=1) * sin_ref[...]

    half = RET_HEADS // 2
    for h in range(RET_HEADS):
        sl = slice(h * RET_DH, (h + 1) * RET_DH)
        q_ref, qs = (qlo_ref, sl) if h < half else (qhi_ref, slice((h - half) * RET_DH, (h - half + 1) * RET_DH))
        qo_ref[:, sl] = rot(q_ref[:, qs].astype(_f32)).astype(qo_ref.dtype)
        ko_ref[:, sl] = rot(k_ref[:, sl] * (RET_DH ** -0.5)).astype(ko_ref.dtype)
    vo_ref[...] = v_ref[...].astype(vo_ref.dtype)


def _prep_ret(pa, pb, L, rope, states_only=False):
    tr = min(512, L)
    cos, sin = _rope_tables(L) if rope else (jnp.ones((L, LANES), _f32), jnp.zeros((L, LANES), _f32))
    hw = RET_W // 2
    q_src, q_col = (pa, O_RET_K) if states_only else (pb, B_RQ)
    qsp = lambda n: pl.BlockSpec((tr, hw), lambda i: (i, q_col // hw + n))
    sec = lambda c: pl.BlockSpec((tr, RET_W), lambda i: (i, c // RET_W))
    tab = pl.BlockSpec((tr, LANES), lambda i: (i, 0))
    out = pl.BlockSpec((tr, RET_W), lambda i: (i, 0))
    shp = jax.ShapeDtypeStruct((L, RET_W), MXU_DTYPE)
    return pl.pallas_call(
        functools.partial(_prep_ret_kernel, rope=rope),
        grid=(L // tr,),
        in_specs=[qsp(0), qsp(1), sec(O_RET_K), sec(O_RET_V), tab, tab],
        out_specs=[out, out, out],
        out_shape=[shp, shp, shp],
        compiler_params=_cparams("parallel"),
        name="prep_ret",
    )(q_src, q_src, pa, pa, cos, sin)


def _scan_ret_kernel(logit_ref, qi_ref, ki_ref, vi_ref, qj_ref, kj_ref, vj_ref, s0f_ref, s0b_ref,
                     ya_ref, yb_ref, finf_ref, finb_ref,
                     sf, sb, dmask, f_out, f_upd, f_all, b_out, b_upd, b_all):
    i = pl.program_id(0)
    c = CHUNK

    @pl.when(i == 0)
    def _():
        sf[...] = s0f_ref[...]
        sb[...] = s0b_ref[...]
        ii = lax.broadcasted_iota(jnp.int32, (c, c), 0).astype(_f32)
        jj = lax.broadcasted_iota(jnp.int32, (c, c), 1).astype(_f32)
        for h in range(RET_HEADS):
            def lg(d):
                x = logit_ref[d, h]
                v = -jnp.log1p(jnp.exp(-x))
                return jnp.broadcast_to(v[0:1, :], (c, c))
            lf, lb = lg(0), lg(1)
            dmask[h] = jnp.where(ii > jj, jnp.exp(lf * (ii - jj)),
                                 jnp.where(jj > ii, jnp.exp(lb * (jj - ii)), 2.0))
            f_out[h] = jnp.exp(lf * (ii + 1.0))
            f_upd[h] = jnp.exp(lf * (c - 1.0 - ii))
            f_all[h] = jnp.exp(lf * float(c))
            b_out[h] = jnp.exp(lb * (c - ii))
            b_upd[h] = jnp.exp(lb * ii)
            b_all[h] = jnp.exp(lb * float(c))

    tn = (((0,), (0,)), ((), ()))
    nt = (((1,), (1,)), ((), ()))
    heads = range(RET_HEADS)
    sls = [slice(h * RET_DH, (h + 1) * RET_DH) for h in heads]
    ns = qi_ref.shape[0] // c
    for u in range(ns):
        ri = slice(u * c, (u + 1) * c)
        rj = slice((ns - 1 - u) * c, (ns - u) * c)
        scores = [lax.dot_general(qi_ref[ri, sl], ki_ref[ri, sl], nt, preferred_element_type=_f32) for sl in sls]
        upd_f = [lax.dot_general((ki_ref[ri, sl].astype(_f32) * f_upd[h]).astype(MXU_DTYPE), vi_ref[ri, sl], tn,
                                 preferred_element_type=_f32) for h, sl in zip(heads, sls)]
        upd_b = [lax.dot_general((kj_ref[rj, sl].astype(_f32) * b_upd[h]).astype(MXU_DTYPE), vj_ref[rj, sl], tn,
                                 preferred_element_type=_f32) for h, sl in zip(heads, sls)]
        for h, sl in zip(heads, sls):
            lhs = jnp.concatenate([(scores[h] * dmask[h]).astype(MXU_DTYPE),
                                   (qi_ref[ri, sl].astype(_f32) * f_out[h]).astype(MXU_DTYPE)], axis=1)
            rhs = jnp.concatenate([vi_ref[ri, sl], sf[h].astype(MXU_DTYPE)], axis=0)
            ya_ref[ri, sl] = jnp.dot(lhs, rhs, preferred_element_type=_f32)
            yb_ref[rj, sl] = jnp.dot((qj_ref[rj, sl].astype(_f32) * b_out[h]).astype(MXU_DTYPE),
                                     sb[h].astype(MXU_DTYPE), preferred_element_type=_f32)
        for h in heads:
            sf[h] = f_all[h] * sf[h] + upd_f[h]
            sb[h] = b_all[h] * sb[h] + upd_b[h]

    @pl.when(i == pl.num_programs(0) - 1)
    def _():
        finf_ref[...] = sf[...]
        finb_ref[...] = sb[...]


def _scan_ret(q, k, v, logit, s0f, s0b):
    L = q.shape[0]
    rows = min(SCAN_CHUNKS * CHUNK, L)
    nc = L // rows
    assert nc * rows == L
    logit_b = jnp.broadcast_to(logit.astype(_f32)[:, :, None, None], (2, RET_HEADS, SUBLANES, LANES))
    fw = pl.BlockSpec((rows, RET_W), lambda i: (i, 0))
    bw = pl.BlockSpec((rows, RET_W), lambda i: (nc - 1 - i, 0))
    st = pl.BlockSpec((RET_HEADS, RET_DH, RET_DH), lambda i: (0, 0, 0))
    yshape = jax.ShapeDtypeStruct((L, RET_W), _f32)
    sshape = jax.ShapeDtypeStruct((RET_HEADS, RET_DH, RET_DH), _f32)
    tile = pltpu.VMEM((RET_HEADS, CHUNK, CHUNK), _f32)
    return pl.pallas_call(
        _scan_ret_kernel,
        grid=(nc,),
        in_specs=[pl.BlockSpec((2, RET_HEADS, SUBLANES, LANES), lambda i: (0, 0, 0, 0)),
                  fw, fw, fw, bw, bw, bw, st, st],
        out_specs=[fw, bw, st, st],
        out_shape=[yshape, yshape, sshape, sshape],
        scratch_shapes=[pltpu.VMEM((RET_HEADS, RET_DH, RET_DH), _f32)] * 2 + [tile] * 7,
        compiler_params=_cparams("arbitrary"),
        name="scan_ret",
    )(logit_b, q, k, v, q, k, v, s0f, s0b)


def _shift_rows(x, prev_row, next_row):
    r = x.shape[0]
    rid = lax.broadcasted_iota(jnp.int32, x.shape, 0)
    up = jnp.where(rid == 0, prev_row, pltpu.roll(x, 1, axis=0))
    dn = jnp.where(rid == r - 1, next_row, pltpu.roll(x, r - 1, axis=0))
    return up, dn


HALO = 16


def _conv3(x_ref, prev_ref, next_ref, w_ref, b_ref, has_prev, has_next):
    x = x_ref[...].astype(_f32)
    prev_row = prev_ref[...].astype(_f32)[HALO - 1:HALO, :] * has_prev
    next_row = next_ref[...].astype(_f32)[0:1, :] * has_next
    up, dn = _shift_rows(x, prev_row, next_row)
    return up * w_ref[0:1, :] + x * w_ref[1:2, :] + dn * w_ref[2:3, :] + b_ref[...]


def _halo_specs(tr, L, width, col):
    nb = tr // HALO
    last = L // HALO - 1
    cb = col // width
    return [pl.BlockSpec((tr, width), lambda i: (i, cb)),
            pl.BlockSpec((HALO, width), lambda i: (jnp.maximum(i * nb - 1, 0), cb)),
            pl.BlockSpec((HALO, width), lambda i: (jnp.minimum((i + 1) * nb, last), cb))]


def _prep_ssd_kernel(x_ref, xp_ref, xn_ref, b_ref, bp_ref, bn_ref, c_ref, cp_ref, cn_ref, dt_ref,
                     w_ref, cb_ref, dtb_ref, alog_ref, co_ref, bo_ref, xo_ref, pack_ref):
    i = pl.program_id(0)
    has_prev = (i > 0).astype(_f32)
    has_next = (i < pl.num_programs(0) - 1).astype(_f32)

    def conv(lo, hi, t_ref, p_ref, n_ref, o_ref):
        y = _conv3(t_ref, p_ref, n_ref, w_ref.at[:, lo:hi], cb_ref.at[:, lo:hi], has_prev, has_next)
        o_ref[...] = _silu(y).astype(o_ref.dtype)

    conv(0, SSM_W, x_ref, xp_ref, xn_ref, xo_ref)
    conv(SSM_W, SSM_W + SSM_GN, b_ref, bp_ref, bn_ref, bo_ref)
    conv(SSM_W + SSM_GN, SSM_W + 2 * SSM_GN, c_ref, cp_ref, cn_ref, co_ref)
    z = dt_ref[...] + dtb_ref[...]
    dt = jnp.maximum(z, 0.0) + jnp.log1p(jnp.exp(-jnp.abs(z)))
    a = dt * (-jnp.exp(alog_ref[...]))
    c = CHUNK
    ii = lax.broadcasted_iota(jnp.int32, (c, c), 0)
    jj = lax.broadcasted_iota(jnp.int32, (c, c), 1)
    lower = (jj <= ii).astype(_f32)
    upper = (jj >= ii).astype(_f32)
    lane = lax.broadcasted_iota(jnp.int32, (c, LANES), 1)
    dt_sh = pltpu.roll(dt, 2 * SSM_HEADS, axis=1)
    for n in range(x_ref.shape[0] // c):
        rs = slice(n * c, (n + 1) * c)
        pre = jnp.dot(lower, a[rs], precision=HIGHEST, preferred_element_type=_f32)
        suf = jnp.dot(upper, a[rs], precision=HIGHEST, preferred_element_type=_f32)
        pack_ref[rs, :] = jnp.where(lane < SSM_HEADS, pre,
                                    jnp.where(lane < 2 * SSM_HEADS, suf, dt_sh[rs]))


def _prep_ssd(pa, pb, L, conv_w, conv_b, dt_bias, a_log, states_only=False):
    tr = min(512, L)
    c_col = B_SB if states_only else B_SC
    w = jnp.pad(conv_w.astype(_f32), ((0, SUBLANES - 3), (0, 0)))
    lanes = lambda t: jnp.pad(t.astype(_f32).reshape(1, 2 * SSM_HEADS), ((0, 0), (0, LANES - 2 * SSM_HEADS)))
    wd = SSM_W + 2 * SSM_GN
    row = lambda c: pl.BlockSpec((tr, c), lambda i: (i, 0))
    return pl.pallas_call(
        _prep_ssd_kernel,
        grid=(L // tr,),
        in_specs=_halo_specs(tr, L, SSM_W, B_SX) + _halo_specs(tr, L, SSM_GN, B_SB)
        + _halo_specs(tr, L, SSM_GN, c_col) + [
            pl.BlockSpec((tr, LANES), lambda i: (i, O_SSM_DT // LANES)),
            pl.BlockSpec((SUBLANES, wd), lambda i: (0, 0)),
            pl.BlockSpec((1, wd), lambda i: (0, 0)),
            pl.BlockSpec((1, LANES), lambda i: (0, 0)),
            pl.BlockSpec((1, LANES), lambda i: (0, 0))],
        out_specs=[row(SSM_GN), row(SSM_GN), row(SSM_W), row(LANES)],
        out_shape=[jax.ShapeDtypeStruct((L, SSM_GN), MXU_DTYPE),
                   jax.ShapeDtypeStruct((L, SSM_GN), MXU_DTYPE),
                   jax.ShapeDtypeStruct((L, SSM_W), MXU_DTYPE),
                   jax.ShapeDtypeStruct((L, LANES), _f32)],
        compiler_params=_cparams("parallel"),
        name="prep_ssd",
    )(*([pb] * 9), pa, w, conv_b.astype(_f32)[None], lanes(dt_bias), lanes(a_log))


def _scan_ssd_kernel(ci_ref, bi_ref, xi_ref, pi_ref, cj_ref, bj_ref, xj_ref, pj_ref, dskip_ref,
                     s0f_ref, s0b_ref, ya_ref, yb_ref, finf_ref, finb_ref, sf, sb):
    i = pl.program_id(0)
    c = CHUNK
    H = SSM_HEADS

    @pl.when(i == 0)
    def _():
        sf[...] = s0f_ref[...]
        sb[...] = s0b_ref[...]

    tn = (((0,), (0,)), ((), ()))
    nt = (((1,), (1,)), ((), ()))
    ii = lax.broadcasted_iota(jnp.int32, (c, c), 0)
    jj = lax.broadcasted_iota(jnp.int32, (c, c), 1)
    low = lax.broadcasted_iota(jnp.int32, (c, LANES), 1) < SSM_HEADDIM
    low2 = lax.broadcasted_iota(jnp.int32, (2 * SSM_STATE, LANES), 1) < SSM_HEADDIM
    diag = (lax.broadcasted_iota(jnp.int32, (2 * SSM_STATE, LANES), 0) < SSM_STATE) == low2
    ns = pi_ref.shape[0] // c
    for u in range(ns):
        ri = slice(u * c, (u + 1) * c)
        rj = slice((ns - 1 - u) * c, (ns - u) * c)
        pi = pi_ref[ri, :]
        pit = pi.T
        pj = pj_ref[rj, :]
        ei = jnp.exp(jnp.minimum(pi, 0.0))
        ej = jnp.exp(jnp.minimum(pj, 0.0))
        tot_i, tot_j = pi[c - 1:c, :], pj[0:1, :]
        dt_i, dt_j = pltpu.roll(pi, LANES - 2 * H, axis=1), pltpu.roll(pj, LANES - 2 * H, axis=1)
        wi = jnp.exp(jnp.minimum(tot_i - pi, 0.0)) * dt_i
        wj = jnp.exp(jnp.minimum(tot_j - pj, 0.0)) * dt_j
        eti, etj = jnp.exp(jnp.minimum(tot_i, 0.0)), jnp.exp(jnp.minimum(tot_j, 0.0))
        colb = lambda t, k: jnp.broadcast_to(t[:, k:k + 1], (c, LANES))
        for g in range(SSM_GROUPS):
            gs = slice(g * SSM_STATE, (g + 1) * SSM_STATE)
            ci, bi = ci_ref[ri, gs], bi_ref[ri, gs]
            cj, bj = cj_ref[rj, gs], bj_ref[rj, gs]
            cb = lax.dot_general(ci, bi, nt, preferred_element_type=_f32)
            ci32, bi32, cj32, bj32 = (t.astype(_f32) for t in (ci, bi, cj, bj))
            for pp in range(SSM_HPG // 2):
                q = g * (SSM_HPG // 2) + pp
                heads = (2 * q, 2 * q + 1)
                xs = slice(q * LANES, (q + 1) * LANES)
                x = xi_ref[ri, xs]
                x32 = x.astype(_f32)
                scores, cw, bw = [], [], []
                for h in heads:
                    row = lambda o: pit[o + h:o + h + 1, :]
                    mf = jnp.where(ii >= jj, jnp.exp(jnp.minimum(colb(pi, h) - row(0), 0.0)), 0.0) * row(2 * H)
                    mb = jnp.where(jj >= ii, jnp.exp(jnp.minimum(colb(pi, H + h) - row(H), 0.0)), 0.0) * row(3 * H)
                    scores.append((cb * (mf + mb)).astype(MXU_DTYPE))
                    cw.append((ci32 * colb(ei, h)).astype(MXU_DTYPE))
                    bw.append((bi32 * colb(wi, h)).astype(MXU_DTYPE))
                xa = jnp.where(low, x32, 0.0).astype(MXU_DTYPE)
                xb = jnp.where(low, 0.0, x32).astype(MXU_DTYPE)
                lhs = jnp.concatenate(scores + cw, axis=1)
                rhs = jnp.concatenate([xa, xb, sf[q].astype(MXU_DTYPE)], axis=0)
                y = jnp.dot(lhs, rhs, preferred_element_type=_f32)
                ya_ref[ri, xs] = y + dskip_ref[:, xs] * x32
                upd = lax.dot_general(jnp.concatenate(bw, axis=1), x, tn, preferred_element_type=_f32)
                dec = jnp.where(low2, eti[0:1, heads[0]:heads[0] + 1], eti[0:1, heads[1]:heads[1] + 1])
                sf[q] = dec * sf[q] + jnp.where(diag, upd, 0.0)
                x = xj_ref[rj, xs]
                cw = [(cj32 * colb(ej, H + h)).astype(MXU_DTYPE) for h in heads]
                bw = [(bj32 * colb(wj, H + h)).astype(MXU_DTYPE) for h in heads]
                yb_ref[rj, xs] = jnp.dot(jnp.concatenate(cw, axis=1), sb[q].astype(MXU_DTYPE),
                                         preferred_element_type=_f32)
                upd = lax.dot_general(jnp.concatenate(bw, axis=1), x, tn, preferred_element_type=_f32)
                hb = (H + heads[0], H + heads[1])
                dec = jnp.where(low2, etj[0:1, hb[0]:hb[0] + 1], etj[0:1, hb[1]:hb[1] + 1])
                sb[q] = dec * sb[q] + jnp.where(diag, upd, 0.0)

    @pl.when(i == pl.num_programs(0) - 1)
    def _():
        finf_ref[...] = sf[...]
        finb_ref[...] = sb[...]


def _pair_states(s):
    s = s.reshape(SSM_HEADS // 2, 2, SSM_STATE, SSM_HEADDIM)
    z = jnp.zeros_like(s[:, 0])
    return jnp.concatenate([jnp.concatenate([s[:, 0], z], -1), jnp.concatenate([z, s[:, 1]], -1)], 1)


def _unpair_states(s):
    top, bot = s[:, :SSM_STATE, :SSM_HEADDIM], s[:, SSM_STATE:, SSM_HEADDIM:]
    return jnp.stack([top, bot], 1).reshape(SSM_HEADS, SSM_STATE, SSM_HEADDIM)


def _scan_ssd(cs, bs, xs, pack, d_skip, s0f, s0b):
    L = xs.shape[0]
    rows = min(SCAN_CHUNKS * CHUNK, L)
    nc = L // rows
    assert nc * rows == L
    dvec = jnp.repeat(d_skip.astype(_f32), SSM_HEADDIM)[None]
    fw = lambda w: pl.BlockSpec((rows, w), lambda i: (i, 0))
    bw = lambda w: pl.BlockSpec((rows, w), lambda i: (nc - 1 - i, 0))
    pshape = (SSM_HEADS // 2, 2 * SSM_STATE, 2 * SSM_HEADDIM)
    st = pl.BlockSpec(pshape, lambda i: (0, 0, 0))
    yshape = jax.ShapeDtypeStruct((L, SSM_W), _f32)
    sshape = jax.ShapeDtypeStruct(pshape, _f32)
    ya, yb, fin_f, fin_b = pl.pallas_call(
        _scan_ssd_kernel,
        grid=(nc,),
        in_specs=[fw(SSM_GN), fw(SSM_GN), fw(SSM_W), fw(LANES), bw(SSM_GN), bw(SSM_GN), bw(SSM_W), bw(LANES),
                  pl.BlockSpec((1, SSM_W), lambda i: (0, 0)), st, st],
        out_specs=[fw(SSM_W), bw(SSM_W), st, st],
        out_shape=[yshape, yshape, sshape, sshape],
        scratch_shapes=[pltpu.VMEM(pshape, _f32)] * 2,
        compiler_params=_cparams("arbitrary"),
        name="scan_ssd",
    )(cs, bs, xs, pack, cs, bs, xs, pack, dvec, _pair_states(s0f), _pair_states(s0b))
    return ya, yb, _unpair_states(fin_f), _unpair_states(fin_b)


def _merge_kernel(ra_ref, rb_ref, sa_ref, sb_ref, gr_ref, gs_ref, nw_ref, yr_ref, ys_ref):
    for h in range(RET_HEADS):
        sl = slice(h * RET_DH, (h + 1) * RET_DH)
        y = _ln_rows(ra_ref[:, sl] + rb_ref[:, sl])
        yr_ref[:, sl] = (y * _silu(gr_ref[:, sl].astype(_f32))).astype(yr_ref.dtype)
    gw = SSM_W // SSM_GROUPS
    for g in range(SSM_GROUPS):
        sl = slice(g * gw, (g + 1) * gw)
        y = (sa_ref[:, sl] + sb_ref[:, sl]) * _silu(gs_ref[:, sl].astype(_f32))
        y = y * lax.rsqrt(jnp.mean(y * y, -1, keepdims=True) + LN_EPS)
        ys_ref[:, sl] = (y * nw_ref[:, sl]).astype(ys_ref.dtype)


def _gate_spec(tr, n):
    return pl.BlockSpec((tr, BR_W), lambda i: (i, B_GATE // BR_W + n))


def _merge(ra, rb, sa, sb_, pb, norm_w):
    L = ra.shape[0]
    tr = min(512, L)
    row = pl.BlockSpec((tr, BR_W), lambda i: (i, 0))
    shp = jax.ShapeDtypeStruct((L, BR_W), MXU_DTYPE)
    return pl.pallas_call(
        _merge_kernel,
        grid=(L // tr,),
        in_specs=[row, row, row, row, _gate_spec(tr, 1), _gate_spec(tr, 3),
                  pl.BlockSpec((1, BR_W), lambda i: (0, 0))],
        out_specs=[row, row],
        out_shape=[shp, shp],
        compiler_params=_cparams("parallel"),
        name="merge",
    )(ra, rb, sa, sb_, pb, pb, norm_w.astype(_f32)[None])


def _pool_kernel(x_ref, prev_ref, next_ref, g_ref, pw_ref, ps_ref, o_ref, *, L):
    i = pl.program_id(0)
    t = x_ref.shape[0]
    halo = HALO
    has_prev = (i > 0).astype(_f32)
    has_next = (i < pl.num_programs(0) - 1).astype(_f32)
    pos = i * t + lax.broadcasted_iota(jnp.int32, (t, 1), 0)
    for g, win in enumerate(POOL_WINDOWS):
        sl = slice(g * POOL_GROUP, (g + 1) * POOL_GROUP)
        x = x_ref[:, sl].astype(_f32)
        s = jnp.concatenate([prev_ref[:, sl].astype(_f32) * has_prev, x,
                             next_ref[:, sl].astype(_f32) * has_next], axis=0)
        rows = t + 2 * halo
        width = 1
        while width < win:
            s = s + pltpu.roll(s, rows - width, axis=0)
            width *= 2
        off = halo - win // 2
        if off:
            s = pltpu.roll(s, rows - off, axis=0)
        cnt = jnp.minimum(pos + win // 2, L) - jnp.maximum(pos - win // 2, 0)
        d = s[:t] / cnt.astype(_f32) - x
        y = jnp.dot(d.astype(MXU_DTYPE), pw_ref[g], preferred_element_type=_f32)
        o_ref[:, sl] = (y * ps_ref[:, sl] * _silu(g_ref[:, sl].astype(_f32))).astype(o_ref.dtype)


def _pool(pb, L, pool_w, pool_scale):
    tr = min(512, L)
    return pl.pallas_call(
        functools.partial(_pool_kernel, L=L),
        grid=(L // tr,),
        in_specs=_halo_specs(tr, L, POOL_W, B_POOL) + [
            _gate_spec(tr, 2),
            pl.BlockSpec((POOL_GROUPS, POOL_GROUP, POOL_GROUP), lambda i: (0, 0, 0)),
            pl.BlockSpec((1, POOL_W), lambda i: (0, 0))],
        out_specs=pl.BlockSpec((tr, POOL_W), lambda i: (i, 0)),
        out_shape=jax.ShapeDtypeStruct((L, POOL_W), MXU_DTYPE),
        compiler_params=_cparams("parallel"),
        name="pool",
    )(pb, pb, pb, pb, pool_w.astype(MXU_DTYPE), pool_scale.astype(_f32)[None])


def _prep_hy_kernel(v_ref, vp_ref, vn_ref, x0_ref, x0p_ref, x0n_ref, x1_ref, x1p_ref, x1n_ref,
                    g_ref, w_ref, b_ref, wo_ref, x0g_ref):
    i = pl.program_id(0)
    has_prev = (i > 0).astype(_f32)
    has_next = (i < pl.num_programs(0) - 1).astype(_f32)

    def conv(n, x_ref, p_ref, n_ref):
        sl = slice(n * HY_W, (n + 1) * HY_W)
        return _conv3(x_ref, p_ref, n_ref, w_ref.at[:, sl], b_ref.at[:, sl], has_prev, has_next)

    hv = conv(0, v_ref, vp_ref, vn_ref)
    hx0 = conv(1, x0_ref, x0p_ref, x0n_ref)
    hx1 = conv(2, x1_ref, x1p_ref, x1n_ref)
    wo_ref[...] = hx1 * hv
    x0g_ref[...] = hx0 * _silu(g_ref[...].astype(_f32))


def _prep_hy(pb, L, conv_w, conv_b):
    tr = min(512, L)
    w = jnp.pad(conv_w.astype(_f32), ((0, SUBLANES - 3), (0, 0)))
    row = pl.BlockSpec((tr, HY_W), lambda i: (i, 0))
    shp = jax.ShapeDtypeStruct((L, HY_W), _f32)
    secs = sum((_halo_specs(tr, L, HY_W, B_HY + n * HY_W) for n in range(3)), [])
    return pl.pallas_call(
        _prep_hy_kernel,
        grid=(L // tr,),
        in_specs=secs + [_gate_spec(tr, 0),
                         pl.BlockSpec((SUBLANES, 3 * HY_W), lambda i: (0, 0)),
                         pl.BlockSpec((1, 3 * HY_W), lambda i: (0, 0))],
        out_specs=[row, row],
        out_shape=[shp, shp],
        compiler_params=_cparams("parallel"),
        name="prep_hy",
    )(*([pb] * 10), w, conv_b.astype(_f32)[None])


def _split(x):
    hi = x.astype(MXU_DTYPE)
    return hi, (x - hi.astype(_f32)).astype(MXU_DTYPE)


def _dot3(a_hi, a_lo, b):
    b_hi, b_lo = _split(b)
    d = lambda p, q: jnp.dot(p, q, preferred_element_type=_f32)
    return d(a_hi, b_hi) + (d(a_hi, b_lo) + d(a_lo, b_hi))


def _dot2(a_hi, a_lo, b):
    b = b.astype(MXU_DTYPE)
    d = lambda p, q: jnp.dot(p, q, preferred_element_type=_f32)
    return d(a_hi, b) + d(a_lo, b)


def _dot2_stacked(a2, b):
    m = a2.shape[0] // 2
    r = jnp.dot(a2, b.astype(MXU_DTYPE), preferred_element_type=_f32)
    return r[:m] + r[m:]


def _const_split(m):
    return _split(jnp.asarray(m, _f32))


def _filter_kernel(z_ref, w1_ref, b1_ref, w2_ref, b2_ref, w3hi_ref, w3lo_ref, freq_ref, delta_ref, o_ref, *, L):
    t = z_ref.shape[1]
    dot = functools.partial(jnp.dot, precision=HIGHEST, preferred_element_type=_f32)
    freq = freq_ref[...]
    hdn = jnp.sin(freq * (dot(w1_ref[...], z_ref[...]) + b1_ref[...]))
    hdn = jnp.sin(freq * (dot(w2_ref[...], hdn) + b2_ref[...]))
    h_hi, h_lo = _split(hdn)
    d = lambda p, q: lax.dot_general(p, q, (((0,), (0,)), ((), ())), preferred_element_type=_f32)
    filt = d(h_hi, w3hi_ref[...]) + (d(h_hi, w3lo_ref[...]) + d(h_lo, w3hi_ref[...]))
    n = pl.program_id(0) * t + lax.broadcasted_iota(jnp.int32, (t, 1), 0)
    lag = jnp.minimum(jnp.where(n < L, n, 2 * L - n), L - 1).astype(_f32)
    o_ref[...] = jnp.where(n == L, 0.0, filt) * jnp.exp(-(lag / (L - 1)) * delta_ref[...])


def _hy_filter(L, lp):
    n = jnp.arange(2 * L)
    lag = jnp.minimum(jnp.where(n < L, n, 2 * L - n), L - 1).astype(_f32)[:, None]
    t = lag / (L - 1)
    w = 2.0 * math.pi * lag / L
    bands = jnp.linspace(1e-4, HY_BANDS - 1, HY_BANDS, dtype=_f32)[None, :]
    z = jnp.concatenate([t, jnp.cos(bands * w), -jnp.sin(bands * w)], axis=-1)
    emb = z.shape[1]
    zt = jnp.pad(z, ((0, 0), (0, LANES - emb))).T
    w1t = jnp.pad(lp['hy_w1'].astype(_f32), ((0, LANES - emb), (0, 0))).T
    deltas = jnp.abs(jnp.linspace(HY_MIN_DECAY, HY_MAX_DECAY, HY_W, dtype=_f32))[None]
    tr = min(512, L)
    w3hi, w3lo = _split(lp['hy_w3'].astype(_f32))
    full = lambda a: pl.BlockSpec(a.shape, lambda i: (0,) * a.ndim)
    half = pl.BlockSpec((w3hi.shape[0], HY_W), lambda i: (0, i // (L // tr)))
    colv = lambda v: v.astype(_f32)[:, None]
    pre = [w1t, colv(lp['hy_b1']), lp['hy_w2'].astype(_f32).T, colv(lp['hy_b2'])]
    post = [colv(lp['hy_freq']), deltas]
    return pl.pallas_call(
        functools.partial(_filter_kernel, L=L),
        grid=(2 * L // tr,),
        in_specs=[pl.BlockSpec((LANES, tr), lambda i: (0, i))] + [full(a) for a in pre] + [half, half]
        + [full(a) for a in post],
        out_specs=pl.BlockSpec((tr, HY_W), lambda i: (i, 0)),
        out_shape=jax.ShapeDtypeStruct((2 * L, HY_W), _f32),
        compiler_params=_cparams("parallel"),
        name="hy_filter",
    )(zt, *pre, w3hi, w3lo, *post)


def _cs(num, den):
    ang = 2.0 * np.pi * (np.asarray(num, np.int64) % den) / den
    return np.cos(ang), np.sin(ang)


FFT_N2 = LANES


def _fft_rows(n1):
    return -(-(n1 // 2 + 1) // SUBLANES) * SUBLANES


FFT_GROUP = 16
FFT_MID_GROUP = 8


def _fft_first_kernel(x_ref, m2_ref, o_ref):
    n2 = FFT_N2
    rows, kb = x_ref.shape[0] // n2, o_ref.shape[1]
    for g0 in range(0, n2, FFT_GROUP):
        x = jnp.concatenate([x_ref[pl.ds(g0 + g, rows, stride=n2), :] for g in range(FFT_GROUP)], axis=1)
        y = _dot2_stacked(m2_ref[...], x)
        for g in range(FFT_GROUP):
            tile = y[:, g * LANES:(g + 1) * LANES].reshape(2, kb, SUBLANES, LANES)
            o_ref[:, :, 0, (g0 + g) * SUBLANES:(g0 + g + 1) * SUBLANES, :] = tile


def _fft_first(x, n1):
    n2 = FFT_N2
    rows, ch = x.shape[0] // n2, x.shape[1]
    kp = _fft_rows(n1)
    c, s = _cs(np.outer(np.arange(kp), np.arange(rows)), n1)
    m2 = jnp.concatenate(_const_split(np.concatenate([c, -s], 0)), axis=0)
    kb, ct = kp // SUBLANES, ch // LANES
    return pl.pallas_call(
        _fft_first_kernel,
        grid=(ct,),
        in_specs=[pl.BlockSpec((rows * n2, LANES), lambda j: (0, j)),
                  pl.BlockSpec((4 * kp, rows), lambda j: (0, 0))],
        out_specs=pl.BlockSpec((2, kb, 1, n2 * SUBLANES, LANES), lambda j: (0, 0, j, 0, 0)),
        out_shape=jax.ShapeDtypeStruct((2, kb, ct, n2 * SUBLANES, LANES), _f32),
        compiler_params=_cparams("parallel"),
        name="fft_first",
    )(x, m2)


def _fft_mid_kernel(a_ref, f_ref, twr_ref, twi_ref, fhi_ref, flo_ref, ghi_ref, glo_ref, o_ref):
    n2 = FFT_N2
    for s0 in range(0, SUBLANES, FFT_MID_GROUP):
        group = range(s0, s0 + FFT_MID_GROUP)
        ts, tws = [], []
        for s in group:
            rows = pl.ds(s, n2, stride=SUBLANES)
            twr, twi = twr_ref[s], twi_ref[s]
            tws.append((twr, twi))
            for t_ref in (a_ref, f_ref):
                tr, ti = t_ref[0, 0, 0, rows, :], t_ref[1, 0, 0, rows, :]
                ts.append(jnp.concatenate([tr * twr - ti * twi, tr * twi + ti * twr], axis=0))
        y = _dot2(fhi_ref[...], flo_ref[...], jnp.concatenate(ts, axis=1))
        ps = []
        for p in range(FFT_MID_GROUP):
            x, h = y[:, 2 * p * LANES:(2 * p + 1) * LANES], y[:, (2 * p + 1) * LANES:(2 * p + 2) * LANES]
            xr, xi, hr, hi = x[:n2], x[n2:], h[:n2], h[n2:]
            ps.append(jnp.concatenate([xr * hr - xi * hi, xr * hi + xi * hr], axis=0))
        b = _dot2(ghi_ref[...], glo_ref[...], jnp.concatenate(ps, axis=1))
        for p, s in enumerate(group):
            rows = pl.ds(s, n2, stride=SUBLANES)
            twr, twi = tws[p]
            br, bi = b[:n2, p * LANES:(p + 1) * LANES], b[n2:, p * LANES:(p + 1) * LANES]
            o_ref[0, 0, 0, rows, :] = br * twr + bi * twi
            o_ref[1, 0, 0, rows, :] = bi * twr - br * twi


def _fft_mid(a, f, n1):
    n2 = FFT_N2
    _, kb, ct, rows, _ = a.shape
    kp = kb * SUBLANES
    n = n1 * n2
    idx = jnp.arange(kp)[:, None] * jnp.arange(n2)[None, :]
    ang = (2.0 * math.pi / n) * (idx % n).astype(_f32)
    twr = jnp.broadcast_to(jnp.cos(ang)[:, :, None], (kp, n2, LANES))
    twi = jnp.broadcast_to(-jnp.sin(ang)[:, :, None], (kp, n2, LANES))
    c, s = _cs(np.outer(np.arange(n2), np.arange(n2)), n2)
    fhi, flo = _const_split(np.block([[c, s], [-s, c]]))
    ghi, glo = _const_split(np.block([[c, -s], [s, c]]))
    blk = pl.BlockSpec((2, 1, 1, rows, LANES), lambda k, j: (0, k, j, 0, 0))
    tw = pl.BlockSpec((SUBLANES, n2, LANES), lambda k, j: (k, 0, 0))
    mat = pl.BlockSpec((2 * n2, 2 * n2), lambda k, j: (0, 0))
    return pl.pallas_call(
        _fft_mid_kernel,
        grid=(kb, ct),
        in_specs=[blk, blk, tw, tw, mat, mat, mat, mat],
        out_specs=blk,
        out_shape=jax.ShapeDtypeStruct(a.shape, _f32),
        compiler_params=_cparams("parallel", "parallel"),
        name="fft_mid",
    )(a, f, twr, twi, fhi, flo, ghi, glo)


def _fft_last_kernel(c_ref, m2_ref, w_ref, x0g_ref, bias_ref, o_ref):
    n2 = FFT_N2
    rows, kb = w_ref.shape[0] // n2, c_ref.shape[1]
    for g0 in range(0, n2, FFT_GROUP):
        tiles = [c_ref[:, :, 0, (g0 + g) * SUBLANES:(g0 + g + 1) * SUBLANES, :].reshape(2 * kb * SUBLANES, LANES)
                 for g in range(FFT_GROUP)]
        y = _dot2_stacked(m2_ref[...], jnp.concatenate(tiles, axis=1))
        for g in range(FFT_GROUP):
            at = pl.ds(g0 + g, rows, stride=n2)
            o_ref[at, :] = x0g_ref[at, :] * (y[:, g * LANES:(g + 1) * LANES] + w_ref[at, :] * bias_ref[...])


def _fft_last(cc, n1, w, x0g, bias):
    n2 = FFT_N2
    rows, ch = w.shape[0] // n2, w.shape[1]
    _, kb, ct, _, _ = cc.shape
    kp = kb * SUBLANES
    n = n1 * n2
    c, s = _cs(np.outer(np.arange(rows), np.arange(kp)), n1)
    k1 = np.arange(kp)
    mult = np.where((k1 == 0) | (k1 == n1 // 2), 1.0, np.where(k1 < n1 // 2, 2.0, 0.0))
    m2 = jnp.concatenate(_const_split(np.concatenate([c * mult, -s * mult], 1) / n), axis=0)
    blk = pl.BlockSpec((rows * n2, LANES), lambda j: (0, j))
    return pl.pallas_call(
        _fft_last_kernel,
        grid=(ct,),
        in_specs=[pl.BlockSpec((2, kb, 1, n2 * SUBLANES, LANES), lambda j: (0, 0, j, 0, 0)),
                  pl.BlockSpec((2 * rows, 2 * kp), lambda j: (0, 0)),
                  blk, blk, pl.BlockSpec((1, LANES), lambda j: (0, j))],
        out_specs=blk,
        out_shape=jax.ShapeDtypeStruct((rows * n2, ch), _f32),
        compiler_params=_cparams("parallel"),
        name="fft_last",
    )(cc, m2, w, x0g, bias.astype(_f32)[None])


def _hy_small_kernel(w_ref, buf_ref, x0g_ref, bias_ref, fwhi_ref, fwlo_ref, fbhi_ref, fblo_ref,
                     ihi_ref, ilo_ref, o_ref):
    n = buf_ref.shape[0]
    w = w_ref[...]
    wf = _dot3(fwhi_ref[...], fwlo_ref[...], w)
    hf = _dot3(fbhi_ref[...], fblo_ref[...], buf_ref[...])
    wr, wi, hr, hi = wf[:n], wf[n:], hf[:n], hf[n:]
    y = _dot3(ihi_ref[...], ilo_ref[...], jnp.concatenate([wr * hr - wi * hi, wr * hi + wi * hr], axis=0))
    o_ref[...] = (x0g_ref[...] * (y + w * bias_ref[...])).astype(o_ref.dtype)


def _hy_conv_small(w, buf, x0g, bias):
    L, ch = w.shape
    n = 2 * L
    tc = 256
    c, s = _cs(np.outer(np.arange(n), np.arange(n)), n)
    fb = np.concatenate([c, -s], 0)
    mats = [*_const_split(fb[:, :L]), *_const_split(fb),
            *_const_split(np.concatenate([c[:L], -s[:L]], 1) / n)]
    col = lambda r: pl.BlockSpec((r, tc), lambda j: (0, j))
    return pl.pallas_call(
        _hy_small_kernel,
        grid=(ch // tc,),
        in_specs=[col(L), col(n), col(L), col(1)] + [pl.BlockSpec(m.shape, lambda j: (0, 0)) for m in mats],
        out_specs=col(L),
        out_shape=jax.ShapeDtypeStruct((L, ch), MXU_DTYPE),
        compiler_params=_cparams("parallel"),
        name="hy_conv_small",
    )(w, buf, x0g, bias.astype(_f32)[None], *mats)


def _hy_conv(w, buf, x0g, bias):
    L, ch = w.shape
    if L < 512:
        return _hy_conv_small(w, buf, x0g, bias)
    n1 = 2 * L // FFT_N2
    return _fft_last(_fft_mid(_fft_first(w, n1), _fft_first(buf, n1), n1), n1, w, x0g, bias)


def _zero_states():
    return (jnp.zeros((RET_HEADS, RET_DH, RET_DH), _f32), jnp.zeros((RET_HEADS, RET_DH, RET_DH), _f32),
            jnp.zeros((SSM_HEADS, SSM_STATE, SSM_HEADDIM), _f32),
            jnp.zeros((SSM_HEADS, SSM_STATE, SSM_HEADDIM), _f32))


def _recurrent(proj, L, lp, states, latent, states_only=False):
    pa, pb = proj
    q, k, v = _prep_ret(pa, pb, L, latent, states_only)
    ra, rb, ret_f, ret_b = _scan_ret(q, k, v, lp['ret_decay_logit'], states[0], states[1])
    cs, bs, xs, pack = _prep_ssd(pa, pb, L, lp['conv_ssm_w'], lp['conv_ssm_b'], lp['ssm_dt_bias'],
                                 lp['ssm_A_log'], states_only)
    sa, sb_, ssm_f, ssm_b = _scan_ssd(cs, bs, xs, pack, lp['ssm_D'], states[2], states[3])
    return (ra, rb, sa, sb_), (ret_f, ret_b, ssm_f, ssm_b)


def _mix(h, mod, lp, states, latent):
    L = h.shape[0]
    proj = _in_proj(h, mod[0], mod[1], lp)
    pb = proj[1]
    (ra, rb, sa, sb_), fin = _recurrent(proj, L, lp, states, latent)
    y_ret, y_ssm = _merge(ra, rb, sa, sb_, pb, lp['ssm_norm_w'])
    w, x0g = _prep_hy(pb, L, lp['conv_hy_w'], lp['conv_hy_b'])
    y_hy = _hy_conv(w, _hy_filter(L, lp), x0g, lp['hy_bias'])
    y_pool = _pool(pb, L, lp['pool_w'], lp['pool_scale'])
    out = _out_proj([y_hy, y_ret, y_pool, y_ssm], lp['w_out'], h, mod[2], lp['ln_g'], lp['ln_b'])
    return out, fin


def _context_states(hc, mod, lp):
    proj = _in_proj(hc, mod[0], mod[1], lp, SSM_W + SSM_GN)
    _, fin = _recurrent(proj, hc.shape[0], lp, _zero_states(), False, states_only=True)
    return fin


def kernel(x, c, ctx, c_ctx, w_mod, b_mod, w_in, conv_ssm_w, conv_ssm_b, conv_hy_w, conv_hy_b,
           ret_decay_logit, ssm_A_log, ssm_dt_bias, ssm_D, ssm_norm_w, hy_w1, hy_b1, hy_w2, hy_b2,
           hy_w3, hy_freq, hy_bias, pool_w, pool_scale, w_out, ln_g, ln_b):
    assert x.shape[0] == 1
    h, hc = x[0], ctx[0]
    w_in_t = jnp.swapaxes(w_in, 1, 2)
    mods = _adaln(jnp.concatenate([c, c_ctx[None]], axis=0), w_mod, b_mod)
    for l in range(DEPTH):
        lp = {
            'layer': l, 'w_in_t': w_in_t, 'w_a_t': _cast_layer(w_in_t, l, N_A, N_A // 2, 512),
            'w_out': _cast_layer(w_out, l, w_out.shape[1], 1024, 2048)[0],
            'conv_ssm_w': conv_ssm_w[l], 'conv_ssm_b': conv_ssm_b[l],
            'conv_hy_w': conv_hy_w[l], 'conv_hy_b': conv_hy_b[l], 'ret_decay_logit': ret_decay_logit[l],
            'ssm_A_log': ssm_A_log[l], 'ssm_dt_bias': ssm_dt_bias[l], 'ssm_D': ssm_D[l],
            'ssm_norm_w': ssm_norm_w[l], 'hy_w1': hy_w1[l], 'hy_b1': hy_b1[l], 'hy_w2': hy_w2[l],
            'hy_b2': hy_b2[l], 'hy_w3': hy_w3[l], 'hy_freq': hy_freq[l], 'hy_bias': hy_bias[l],
            'pool_w': pool_w[l], 'pool_scale': pool_scale[l], 'ln_g': ln_g[l], 'ln_b': ln_b[l],
        }
        mod = lambda r: tuple(mods[l, r:r + 1, n * D_MODEL:(n + 1) * D_MODEL] for n in range(3))
        if l < DEPTH - 1:
            hc_next, states = _mix(hc, mod(1), lp, _zero_states(), False)
        else:
            states = _context_states(hc, mod(1), lp)
            hc_next = hc
        h, _ = _mix(h, mod(0), lp, states, True)
        hc = hc_next
    return h[None]
```
